```python
import jax, jax.numpy as jnp
from jax import lax
import numpy as np

D_MODEL = 2048
BATCH = 8
SEQ = 8192
DEPTH = 4

HEAD_DIM = 128
NORM_EPS = 1e-6
SB_HEADS = 8
SB_WIDTH = SB_HEADS * HEAD_DIM
SB_BLOCK = 128
CONV_WIDTH = 1024
CONV_KERNEL = 31
SGU_WIDTH = 1024
SGU_GROUPS = 8
SGU_CHUNK = 128
DIL_PATTERNS = ((128, 1), (512, 4), (2048, 16))
DIL_NGROUPS = 3
DIL_SLOTS = 8
DIL_WIDTH = DIL_SLOTS * HEAD_DIM
DIL_BLOCK = 128
N_BRANCH = 4
BRANCH_WIDTH = 1024
IN_SIZES = (SB_WIDTH, SB_WIDTH, SB_WIDTH, SB_WIDTH,
            CONV_WIDTH, CONV_WIDTH, CONV_WIDTH,
            SGU_WIDTH, SGU_WIDTH, SGU_WIDTH,
            DIL_NGROUPS * DIL_WIDTH, DIL_NGROUPS * DIL_WIDTH,
            DIL_WIDTH, DIL_WIDTH)
IN_WIDTH = 18432

kernel_name = "hybrid_sb_conv_sgu_dilated_block"


def rmsnorm(x, g):
    xf = x.astype(jnp.float32)
    y = xf * lax.rsqrt(jnp.mean(xf * xf, axis=-1, keepdims=True) + NORM_EPS)
    return (y * g.astype(jnp.float32)).astype(x.dtype)


def layernorm(x, g, b):
    xf = x.astype(jnp.float32)
    mu = jnp.mean(xf, axis=-1, keepdims=True)
    var = jnp.mean(jnp.square(xf - mu), axis=-1, keepdims=True)
    y = (xf - mu) * lax.rsqrt(var + NORM_EPS)
    return (y * g.astype(jnp.float32) + b.astype(jnp.float32)).astype(x.dtype)


def stick_breaking_attention(q, k, v):
    B, S, H, Dh = q.shape
    nb = S // SB_BLOCK
    scale = Dh ** -0.5
    qb = q.astype(jnp.float32).reshape(B, nb, SB_BLOCK, H, Dh).transpose(1, 0, 3, 2, 4)
    kf = k.astype(jnp.float32)
    vf = v.astype(jnp.float32)
    key_pos = jnp.arange(S)

    def block(args):
        qi, n = args
        z = jnp.einsum('bhqd,bkhd->bhqk', qi, kf) * scale
        q_pos = n * SB_BLOCK + jnp.arange(SB_BLOCK)
        causal = key_pos[None, :] < q_pos[:, None]
        log_beta = jax.nn.log_sigmoid(z)
        log_1mb = jnp.where(causal, jax.nn.log_sigmoid(-z), 0.0)
        after = lax.cumsum(log_1mb, axis=3, reverse=True) - log_1mb
        w = jnp.where(causal, jnp.exp(log_beta + after), 0.0)
        return jnp.einsum('bhqk,bkhd->bqhd', w, vf)

    out = lax.map(block, (qb, jnp.arange(nb)))
    return out.transpose(1, 0, 2, 3, 4).reshape(B, S, H * Dh)


def causal_depthwise_conv(x, w, b):
    K, C = w.shape
    xp = jnp.pad(x, ((0, 0), (K - 1, 0), (0, 0)))
    y = lax.conv_general_dilated(xp, w[:, None, :], window_strides=(1,), padding='VALID',
                                 dimension_numbers=('NWC', 'WIO', 'NWC'), feature_group_count=C)
    return y + b


def conformer_conv(glu_a, glu_b, conv_w, conv_b, ln_g, ln_b):
    g = glu_a * jax.nn.sigmoid(glu_b)
    c = causal_depthwise_conv(g, conv_w, conv_b)
    return jax.nn.silu(layernorm(c, ln_g, ln_b))


def spatial_gating(u, v, w_s, b_s, ln_g, ln_b):
    B, S, C = u.shape
    n = S // SGU_CHUNK
    G = SGU_GROUPS
    vn = layernorm(v, ln_g, ln_b).reshape(B, n, SGU_CHUNK, G, C // G)
    mask = jnp.tril(jnp.ones((SGU_CHUNK, SGU_CHUNK), dtype=bool))
    w = jnp.where(mask[None], w_s, jnp.zeros_like(w_s))
    z = jnp.einsum('gts,bnsgc->bntgc', w, vn) + b_s.T[None, None, :, :, None]
    return u * z.reshape(B, S, C)


def dilated_group_attention(q, k, v, window, dil):
    B, S, H, Dh = q.shape
    n_keys = window // dil
    L = S // dil
    N = B * dil
    T = DIL_BLOCK
    nb = -(-L // T)
    Lp = nb * T

    def to_classes(t):
        t = t.astype(jnp.float32).reshape(B, L, dil, H, Dh).transpose(0, 2, 1, 3, 4).reshape(N, L, H, Dh)
        t = jnp.pad(t, ((0, 0), (0, Lp - L), (0, 0), (0, 0)))
        return t.reshape(N, nb, T, H, Dh)

    def with_prev(t):
        prev = jnp.pad(t[:, :-1], ((0, 0), (1, 0), (0, 0), (0, 0), (0, 0)))
        return jnp.concatenate([prev, t], axis=2)

    qb = to_classes(q)
    kw = with_prev(to_classes(k))
    vw = with_prev(to_classes(v))
    s = jnp.einsum('nbqhd,nbkhd->nbhqk', qb, kw) * (Dh ** -0.5)
    a = jnp.arange(T)[:, None]
    c = jnp.arange(2 * T)[None, :]
    rel = T + a - c
    blk = jnp.arange(nb)[:, None, None]
    valid = (rel >= 0) & (rel <= n_keys) & ((blk > 0) | (c >= T))
    s = jnp.where(valid[None, :, None], s, -jnp.inf)
    m = jnp.max(s, axis=-1, keepdims=True)
    p = jnp.exp(s - m)
    den = jnp.sum(p, axis=-1, keepdims=True)
    o = jnp.einsum('nbhqk,nbkhd->nbqhd', p, vw) / den.transpose(0, 1, 3, 2, 4)
    lse = (m + jnp.log(den))[..., 0].transpose(0, 1, 3, 2)

    def from_classes(t):
        tail = t.shape[3:]
        t = t.reshape((N, Lp) + tail)[:, :L]
        t = jnp.moveaxis(t.reshape((B, dil, L) + tail), 1, 2)
        return t.reshape((B, S) + tail)

    return from_classes(o), from_classes(lse)


def dilated_mixture(q, k, v):
    outs, lses = [], []
    for g, (window, dil) in enumerate(DIL_PATTERNS):
        o, lse = dilated_group_attention(q[:, :, g], k[:, :, g], v, window, dil)
        outs.append(o)
        lses.append(lse)
    w = jax.nn.softmax(jnp.stack(lses, 0), axis=0)
    return jnp.einsum('gbsh,gbshd->bshd', w, jnp.stack(outs, 0))


def hybrid_layer(x, norm_g, w_in, conv_w, conv_b, conv_ln_g, conv_ln_b, sgu_ln_g, sgu_ln_b,
                 sgu_w, sgu_b, w_branch, w_gate, b_gate, w_out):
    B, S, D = x.shape
    h = rmsnorm(x, norm_g)
    proj = jnp.einsum('bsd,df->bsf', h, w_in)
    splits = [int(i) for i in np.cumsum(IN_SIZES)[:-1]]
    (a_q, a_k, a_v, a_g, b_a, b_b, b_g, c_u, c_v, c_g, d_q, d_k, d_v, d_g) = jnp.split(proj, splits, axis=-1)

    shp = (B, S, SB_HEADS, HEAD_DIM)
    ya = stick_breaking_attention(a_q.reshape(shp), a_k.reshape(shp), a_v.reshape(shp)).astype(x.dtype)
    ya = ya * jax.nn.silu(a_g)
    yb = conformer_conv(b_a, b_b, conv_w, conv_b, conv_ln_g, conv_ln_b) * jax.nn.silu(b_g)
    yc = spatial_gating(jax.nn.gelu(c_u), jax.nn.gelu(c_v), sgu_w, sgu_b, sgu_ln_g, sgu_ln_b) * jax.nn.silu(c_g)
    gshp = (B, S, DIL_NGROUPS, DIL_SLOTS, HEAD_DIM)
    yd = dilated_mixture(d_q.reshape(gshp), d_k.reshape(gshp), d_v.reshape(B, S, DIL_SLOTS, HEAD_DIM))
    yd = yd.reshape(B, S, DIL_WIDTH).astype(x.dtype) * jax.nn.silu(d_g)

    branches = jnp.stack([ya, yb, yc, yd], axis=2)
    yproj = jnp.einsum('bsnc,ncd->bsnd', branches, w_branch)
    gates = jax.nn.sigmoid(jnp.einsum('bsd,df->bsf', h, w_gate) + b_gate).reshape(B, S, N_BRANCH, D)
    merged = jnp.sum(gates * yproj, axis=2)
    return x + jnp.einsum('bsd,de->bse', merged, w_out)


def _fwd_setup_inputs(seed: int = 0) -> dict:
    key = jax.random.key(seed)
    ks = jax.random.split(key, 17)
    f32 = jnp.float32
    nrm = lambda k, shape, scale: jax.random.normal(k, shape, f32) * scale
    return {
        "x": nrm(ks[0], (BATCH, SEQ, D_MODEL), 1.0),
        "norm_g": 1.0 + nrm(ks[1], (DEPTH, D_MODEL), 0.01),
        "w_in": nrm(ks[2], (DEPTH, D_MODEL, IN_WIDTH), D_MODEL ** -0.5),
        "conv_w": nrm(ks[3], (DEPTH, CONV_KERNEL, CONV_WIDTH), CONV_KERNEL ** -0.5),
        "conv_b": nrm(ks[4], (DEPTH, CONV_WIDTH), 0.01),
        "conv_ln_g": 1.0 + nrm(ks[5], (DEPTH, CONV_WIDTH), 0.01),
        "conv_ln_b": nrm(ks[6], (DEPTH, CONV_WIDTH), 0.01),
        "sgu_ln_g": 1.0 + nrm(ks[7], (DEPTH, SGU_WIDTH), 0.01),
        "sgu_ln_b": nrm(ks[8], (DEPTH, SGU_WIDTH), 0.01),
        "sgu_w": nrm(ks[9], (DEPTH, SGU_GROUPS, SGU_CHUNK, SGU_CHUNK), SGU_CHUNK ** -0.5),
        "sgu_b": 1.0 + nrm(ks[10], (DEPTH, SGU_GROUPS, SGU_CHUNK), 0.01),
        "w_branch": nrm(ks[11], (DEPTH, N_BRANCH, BRANCH_WIDTH, D_MODEL), BRANCH_WIDTH ** -0.5),
        "w_gate": nrm(ks[12], (DEPTH, D_MODEL, N_BRANCH * D_MODEL), D_MODEL ** -0.5),
        "b_gate": nrm(ks[13], (DEPTH, N_BRANCH * D_MODEL), 0.01),
        "w_out": nrm(ks[14], (DEPTH, D_MODEL, D_MODEL), D_MODEL ** -0.5),
        "final_g": 1.0 + nrm(ks[15], (D_MODEL,), 0.01),
    }


def _fwd_reference(x, norm_g, w_in, conv_w, conv_b, conv_ln_g, conv_ln_b, sgu_ln_g, sgu_ln_b,
              sgu_w, sgu_b, w_branch, w_gate, b_gate, w_out, final_g):
    for l in range(DEPTH):
        x = hybrid_layer(x, norm_g[l], w_in[l], conv_w[l], conv_b[l], conv_ln_g[l], conv_ln_b[l],
                         sgu_ln_g[l], sgu_ln_b[l], sgu_w[l], sgu_b[l], w_branch[l], w_gate[l],
                         b_gate[l], w_out[l])
    return rmsnorm(x, final_g)


import jax as _jax
import jax.numpy as _jnp

TWIN_FORMAT = 'train_step'
FWD_PARAMS = ['x', 'norm_g', 'w_in', 'conv_w', 'conv_b', 'conv_ln_g', 'conv_ln_b', 'sgu_ln_g', 'sgu_ln_b', 'sgu_w', 'sgu_b', 'w_branch', 'w_gate', 'b_gate', 'w_out', 'final_g']
TWIN_WEIGHTS = ['norm_g', 'w_in', 'conv_w', 'conv_b', 'conv_ln_g', 'conv_ln_b', 'sgu_ln_g', 'sgu_ln_b', 'sgu_w', 'sgu_b', 'w_branch', 'w_gate', 'b_gate', 'w_out', 'final_g']
TWIN_DIFF_INPUT = 'x'
TWIN_INPUTS = ['x', 'norm_g', 'w_in', 'conv_w', 'conv_b', 'conv_ln_g', 'conv_ln_b', 'sgu_ln_g', 'sgu_ln_b', 'sgu_w', 'sgu_b', 'w_branch', 'w_gate', 'b_gate', 'w_out', 'final_g', 'loss_target', 'm_norm_g', 'm_w_in', 'm_conv_w', 'm_conv_b', 'm_conv_ln_g', 'm_conv_ln_b', 'm_sgu_ln_g', 'm_sgu_ln_b', 'm_sgu_w', 'm_sgu_b', 'm_w_branch', 'm_w_gate', 'm_b_gate', 'm_w_out', 'm_final_g', 'v_norm_g', 'v_w_in', 'v_conv_w', 'v_conv_b', 'v_conv_ln_g', 'v_conv_ln_b', 'v_sgu_ln_g', 'v_sgu_ln_b', 'v_sgu_w', 'v_sgu_b', 'v_w_branch', 'v_w_gate', 'v_b_gate', 'v_w_out', 'v_final_g']
TWIN_OUTPUTS = ['loss', 'grad_x', 'grad_norm_g', 'grad_w_in', 'grad_conv_w', 'grad_conv_b', 'grad_conv_ln_g', 'grad_conv_ln_b', 'grad_sgu_ln_g', 'grad_sgu_ln_b', 'grad_sgu_w', 'grad_sgu_b', 'grad_w_branch', 'grad_w_gate', 'grad_b_gate', 'grad_w_out', 'grad_final_g', 'delta_norm_g', 'delta_w_in', 'delta_conv_w', 'delta_conv_b', 'delta_conv_ln_g', 'delta_conv_ln_b', 'delta_sgu_ln_g', 'delta_sgu_ln_b', 'delta_sgu_w', 'delta_sgu_b', 'delta_w_branch', 'delta_w_gate', 'delta_b_gate', 'delta_w_out', 'delta_final_g', 'new_m_norm_g', 'new_m_w_in', 'new_m_conv_w', 'new_m_conv_b', 'new_m_conv_ln_g', 'new_m_conv_ln_b', 'new_m_sgu_ln_g', 'new_m_sgu_ln_b', 'new_m_sgu_w', 'new_m_sgu_b', 'new_m_w_branch', 'new_m_w_gate', 'new_m_b_gate', 'new_m_w_out', 'new_m_final_g', 'new_v_norm_g', 'new_v_w_in', 'new_v_conv_w', 'new_v_conv_b', 'new_v_conv_ln_g', 'new_v_conv_ln_b', 'new_v_sgu_ln_g', 'new_v_sgu_ln_b', 'new_v_sgu_w', 'new_v_sgu_b', 'new_v_w_branch', 'new_v_w_gate', 'new_v_b_gate', 'new_v_w_out', 'new_v_final_g']
TWIN_LEAF_KINDS = {'loss': 'loss', 'grad_x': 'grad_x', 'grad_norm_g': 'grad_w', 'grad_w_in': 'grad_w', 'grad_conv_w': 'grad_w', 'grad_conv_b': 'grad_w', 'grad_conv_ln_g': 'grad_w', 'grad_conv_ln_b': 'grad_w', 'grad_sgu_ln_g': 'grad_w', 'grad_sgu_ln_b': 'grad_w', 'grad_sgu_w': 'grad_w', 'grad_sgu_b': 'grad_w', 'grad_w_branch': 'grad_w', 'grad_w_gate': 'grad_w', 'grad_b_gate': 'grad_w', 'grad_w_out': 'grad_w', 'grad_final_g': 'grad_w', 'delta_norm_g': 'delta_w', 'delta_w_in': 'delta_w', 'delta_conv_w': 'delta_w', 'delta_conv_b': 'delta_w', 'delta_conv_ln_g': 'delta_w', 'delta_conv_ln_b': 'delta_w', 'delta_sgu_ln_g': 'delta_w', 'delta_sgu_ln_b': 'delta_w', 'delta_sgu_w': 'delta_w', 'delta_sgu_b': 'delta_w', 'delta_w_branch': 'delta_w', 'delta_w_gate': 'delta_w', 'delta_b_gate': 'delta_w', 'delta_w_out': 'delta_w', 'delta_final_g': 'delta_w', 'new_m_norm_g': 'new_m', 'new_m_w_in': 'new_m', 'new_m_conv_w': 'new_m', 'new_m_conv_b': 'new_m', 'new_m_conv_ln_g': 'new_m', 'new_m_conv_ln_b': 'new_m', 'new_m_sgu_ln_g': 'new_m', 'new_m_sgu_ln_b': 'new_m', 'new_m_sgu_w': 'new_m', 'new_m_sgu_b': 'new_m', 'new_m_w_branch': 'new_m', 'new_m_w_gate': 'new_m', 'new_m_b_gate': 'new_m', 'new_m_w_out': 'new_m', 'new_m_final_g': 'new_m', 'new_v_norm_g': 'new_v', 'new_v_w_in': 'new_v', 'new_v_conv_w': 'new_v', 'new_v_conv_b': 'new_v', 'new_v_conv_ln_g': 'new_v', 'new_v_conv_ln_b': 'new_v', 'new_v_sgu_ln_g': 'new_v', 'new_v_sgu_ln_b': 'new_v', 'new_v_sgu_w': 'new_v', 'new_v_sgu_b': 'new_v', 'new_v_w_branch': 'new_v', 'new_v_w_gate': 'new_v', 'new_v_b_gate': 'new_v', 'new_v_w_out': 'new_v', 'new_v_final_g': 'new_v'}


def _forward(args):
    return _fwd_reference(*[args[k] for k in FWD_PARAMS])


def _output_shape():
    def fwd():
        inp = _fwd_setup_inputs(0)
        return _fwd_reference(*[inp[k] for k in FWD_PARAMS])
    out = _jax.eval_shape(fwd)
    return out.shape, out.dtype

N_MICROBATCH = 1
ADAM_LR = 0.001
ADAM_B1 = 0.9
ADAM_B2 = 0.999
ADAM_EPS = 1e-08
ADAM_WD = 0.01
ADAM_STEP = 10
PER_EXAMPLE_BATCH_AXIS = {'x': 0, 'loss_target': 0}
SHARED_INPUTS = []
_WEIGHT_DTYPES = {'norm_g': _jnp.float32, 'w_in': _jnp.float32, 'conv_w': _jnp.float32, 'conv_b': _jnp.float32, 'conv_ln_g': _jnp.float32, 'conv_ln_b': _jnp.float32, 'sgu_ln_g': _jnp.float32, 'sgu_ln_b': _jnp.float32, 'sgu_w': _jnp.float32, 'sgu_b': _jnp.float32, 'w_branch': _jnp.float32, 'w_gate': _jnp.float32, 'b_gate': _jnp.float32, 'w_out': _jnp.float32, 'final_g': _jnp.float32}
MOMENT_SCALE = {'norm_g': 7.763468e-02, 'w_in': 2.523584e-02, 'conv_w': 3.480157e-02, 'conv_b': 7.551621e-02, 'conv_ln_g': 4.032255e-02, 'conv_ln_b': 3.505959e-02, 'sgu_ln_g': 2.548082e-02, 'sgu_ln_b': 2.575616e-02, 'sgu_w': 2.595392e-02, 'sgu_b': 3.676343e-02, 'w_branch': 2.390198e-02, 'w_gate': 9.340959e-03, 'b_gate': 9.265815e-03, 'w_out': 4.779566e-02, 'final_g': 3.194572e+01}


def _to_microbatches(a, axis):
    t = _jnp.moveaxis(a, axis, 0)
    t = t.reshape((N_MICROBATCH, t.shape[0] // N_MICROBATCH) + t.shape[1:])
    return _jnp.moveaxis(t, 1, axis + 1)


def setup_inputs(seed: int = 0) -> dict:
    inp = _fwd_setup_inputs(seed)
    key = _jax.random.fold_in(_jax.random.key(seed), 7919)
    shape, _ = _output_shape()
    out = dict(inp)
    out["loss_target"] = _jax.random.normal(_jax.random.fold_in(key, 0), shape, _jnp.float32)
    for i, name in enumerate(TWIN_WEIGHTS):
        w = inp[name].astype(_jnp.float32)
        if MOMENT_SCALE is None:
            s = _jnp.sqrt(_jnp.mean(_jnp.square(w)) + 1e-30)
        else:
            s = MOMENT_SCALE[name]
        km, kv = _jax.random.split(_jax.random.fold_in(key, i + 1))
        out[name] = w
        out["m_" + name] = s * _jax.random.normal(km, w.shape, _jnp.float32)
        out["v_" + name] = (s * s) * _jax.random.uniform(kv, w.shape, _jnp.float32, 0.5, 1.5)
    if N_MICROBATCH > 1:
        for name, axis in PER_EXAMPLE_BATCH_AXIS.items():
            out[name] = _to_microbatches(out[name], axis)
    return {'x': out['x'], 'norm_g': out['norm_g'], 'w_in': out['w_in'], 'conv_w': out['conv_w'], 'conv_b': out['conv_b'], 'conv_ln_g': out['conv_ln_g'], 'conv_ln_b': out['conv_ln_b'], 'sgu_ln_g': out['sgu_ln_g'], 'sgu_ln_b': out['sgu_ln_b'], 'sgu_w': out['sgu_w'], 'sgu_b': out['sgu_b'], 'w_branch': out['w_branch'], 'w_gate': out['w_gate'], 'b_gate': out['b_gate'], 'w_out': out['w_out'], 'final_g': out['final_g'], 'loss_target': out['loss_target'], 'm_norm_g': out['m_norm_g'], 'm_w_in': out['m_w_in'], 'm_conv_w': out['m_conv_w'], 'm_conv_b': out['m_conv_b'], 'm_conv_ln_g': out['m_conv_ln_g'], 'm_conv_ln_b': out['m_conv_ln_b'], 'm_sgu_ln_g': out['m_sgu_ln_g'], 'm_sgu_ln_b': out['m_sgu_ln_b'], 'm_sgu_w': out['m_sgu_w'], 'm_sgu_b': out['m_sgu_b'], 'm_w_branch': out['m_w_branch'], 'm_w_gate': out['m_w_gate'], 'm_b_gate': out['m_b_gate'], 'm_w_out': out['m_w_out'], 'm_final_g': out['m_final_g'], 'v_norm_g': out['v_norm_g'], 'v_w_in': out['v_w_in'], 'v_conv_w': out['v_conv_w'], 'v_conv_b': out['v_conv_b'], 'v_conv_ln_g': out['v_conv_ln_g'], 'v_conv_ln_b': out['v_conv_ln_b'], 'v_sgu_ln_g': out['v_sgu_ln_g'], 'v_sgu_ln_b': out['v_sgu_ln_b'], 'v_sgu_w': out['v_sgu_w'], 'v_sgu_b': out['v_sgu_b'], 'v_w_branch': out['v_w_branch'], 'v_w_gate': out['v_w_gate'], 'v_b_gate': out['v_b_gate'], 'v_w_out': out['v_w_out'], 'v_final_g': out['v_final_g']}


def _loss(weights, diff, rest, loss_target):
    with _jax.named_scope("forward"):
        args = {**rest, TWIN_DIFF_INPUT: diff, **{k: w.astype(_WEIGHT_DTYPES[k]) for k, w in weights.items()}}
        y = _forward(args)
    with _jax.named_scope("loss_head"):
        err = _jnp.square(y.astype(_jnp.float32) - loss_target)
        return 0.5 * _jnp.sum(_jnp.mean(err, axis=-1)) if err.ndim else 0.5 * err


def _adamw(w, g, m, v):
    m = ADAM_B1 * m + (1.0 - ADAM_B1) * g
    v = ADAM_B2 * v + (1.0 - ADAM_B2) * _jnp.square(g)
    m_hat = m / (1.0 - ADAM_B1 ** ADAM_STEP)
    v_hat = v / (1.0 - ADAM_B2 ** ADAM_STEP)
    delta = -ADAM_LR * (m_hat / (_jnp.sqrt(v_hat) + ADAM_EPS) + ADAM_WD * w)
    return delta, m, v


def reference(x, norm_g, w_in, conv_w, conv_b, conv_ln_g, conv_ln_b, sgu_ln_g, sgu_ln_b, sgu_w, sgu_b, w_branch, w_gate, b_gate, w_out, final_g, loss_target, m_norm_g, m_w_in, m_conv_w, m_conv_b, m_conv_ln_g, m_conv_ln_b, m_sgu_ln_g, m_sgu_ln_b, m_sgu_w, m_sgu_b, m_w_branch, m_w_gate, m_b_gate, m_w_out, m_final_g, v_norm_g, v_w_in, v_conv_w, v_conv_b, v_conv_ln_g, v_conv_ln_b, v_sgu_ln_g, v_sgu_ln_b, v_sgu_w, v_sgu_b, v_w_branch, v_w_gate, v_b_gate, v_w_out, v_final_g):
    given = dict(x=x, norm_g=norm_g, w_in=w_in, conv_w=conv_w, conv_b=conv_b, conv_ln_g=conv_ln_g, conv_ln_b=conv_ln_b, sgu_ln_g=sgu_ln_g, sgu_ln_b=sgu_ln_b, sgu_w=sgu_w, sgu_b=sgu_b, w_branch=w_branch, w_gate=w_gate, b_gate=b_gate, w_out=w_out, final_g=final_g, loss_target=loss_target, m_norm_g=m_norm_g, m_w_in=m_w_in, m_conv_w=m_conv_w, m_conv_b=m_conv_b, m_conv_ln_g=m_conv_ln_g, m_conv_ln_b=m_conv_ln_b, m_sgu_ln_g=m_sgu_ln_g, m_sgu_ln_b=m_sgu_ln_b, m_sgu_w=m_sgu_w, m_sgu_b=m_sgu_b, m_w_branch=m_w_branch, m_w_gate=m_w_gate, m_b_gate=m_b_gate, m_w_out=m_w_out, m_final_g=m_final_g, v_norm_g=v_norm_g, v_w_in=v_w_in, v_conv_w=v_conv_w, v_conv_b=v_conv_b, v_conv_ln_g=v_conv_ln_g, v_conv_ln_b=v_conv_ln_b, v_sgu_ln_g=v_sgu_ln_g, v_sgu_ln_b=v_sgu_ln_b, v_sgu_w=v_sgu_w, v_sgu_b=v_sgu_b, v_w_branch=v_w_branch, v_w_gate=v_w_gate, v_b_gate=v_b_gate, v_w_out=v_w_out, v_final_g=v_final_g)
    weights = {n: given[n] for n in TWIN_WEIGHTS}
    shared = {n: given[n] for n in SHARED_INPUTS}
    per_example = {n: given[n] for n in ['x']}
    grad_fn = _jax.value_and_grad(_loss, argnums=(0, 1))

    def one_microbatch(ex, loss_target):
        ex = dict(ex)
        diff = ex.pop(TWIN_DIFF_INPUT)
        return grad_fn(weights, diff, {**shared, **ex}, loss_target)

    if N_MICROBATCH == 1:
        loss, (grad_w, grad_x) = one_microbatch(per_example, given["loss_target"])
    else:
        def body(carry, xs):
            loss_sum, grad_sum = carry
            l_k, (gw_k, gx_k) = one_microbatch(xs[0], xs[1])
            with _jax.named_scope("update"):
                return (loss_sum + l_k, _jax.tree.map(_jnp.add, grad_sum, gw_k)), gx_k

        init = (_jnp.zeros((), _jnp.float32), _jax.tree.map(_jnp.zeros_like, weights))
        (loss, grad_w), grad_x = _jax.lax.scan(body, init, (per_example, given["loss_target"]))
    with _jax.named_scope("update"):
        delta_w, new_m, new_v = {}, {}, {}
        for n in TWIN_WEIGHTS:
            delta_w[n], new_m[n], new_v[n] = _adamw(weights[n], grad_w[n], given["m_" + n], given["v_" + n])
    return (loss, grad_x, *[grad_w[n] for n in TWIN_WEIGHTS], *[delta_w[n] for n in TWIN_WEIGHTS],
            *[new_m[n] for n in TWIN_WEIGHTS], *[new_v[n] for n in TWIN_WEIGHTS])
```

```python
import functools
import math

import jax
import jax.numpy as jnp
from jax import lax
from jax.experimental import pallas as pl
from jax.experimental.pallas import tpu as pltpu

F32 = jnp.float32
BF16 = jnp.bfloat16
MESH = pl.DeviceIdType.MESH

DEPTH = 4
D_MODEL = 2048
HEAD = 128
NHEAD = 8
BW = 1024
IN_WIDTH = 18432
GATE_W = 4 * D_MODEL
NORM_EPS = 1e-6
CONV_K = 31
HALO = 32
DIL_PATTERNS = ((128, 1), (512, 4), (2048, 16))
C_AQ, C_AK, C_AV, C_AG, C_BA, C_BB, C_BG, C_CU, C_CV, C_CG, C_DQ, C_DK, C_DV, C_DG = (
    0, 1, 2, 3, 4, 5, 6, 7, 8, 9, 10, 13, 16, 17)
ADAM_LR, ADAM_B1, ADAM_B2, ADAM_EPS, ADAM_WD, ADAM_STEP = 0.001, 0.9, 0.999, 1e-08, 0.01, 10
VMEM_LIMIT = 56 * 1024 * 1024

NN = (((1,), (0,)), ((), ()))
NT = (((1,), (1,)), ((), ()))
TN = (((0,), (0,)), ((), ()))


def _cparams(sem):
    return pltpu.CompilerParams(dimension_semantics=sem, vmem_limit_bytes=VMEM_LIMIT)


def _dot(a, b, dims):
    return lax.dot_general(a.astype(BF16), b.astype(BF16), dims, preferred_element_type=F32)


def _sigmoid(x):
    return 1.0 / (1.0 + jnp.exp(-x))


def _silu_and_grad(x):
    s = _sigmoid(x)
    return x * s, s * (1.0 + x * (1.0 - s))


_GELU_C = math.sqrt(2.0 / math.pi)


def _gelu_and_grad(x):
    t = jnp.tanh(_GELU_C * (x + 0.044715 * x * x * x))
    y = 0.5 * x * (1.0 + t)
    dy = 0.5 * (1.0 + t) + 0.5 * x * (1.0 - t * t) * _GELU_C * (1.0 + 3.0 * 0.044715 * x * x)
    return y, dy


def _ln_fwd(x, g, b):
    mu = jnp.mean(x, axis=-1, keepdims=True)
    xc = x - mu
    var = jnp.mean(xc * xc, axis=-1, keepdims=True)
    rstd = lax.rsqrt(var + NORM_EPS)
    xh = xc * rstd
    return xh * g + b, xh, rstd


def _ln_bwd(dy, xh, rstd, g):
    dxh = dy * g
    dx = rstd * (dxh - jnp.mean(dxh, axis=-1, keepdims=True) - xh * jnp.mean(dxh * xh, axis=-1, keepdims=True))
    return dx, jnp.sum(dy * xh, axis=0, keepdims=True), jnp.sum(dy, axis=0, keepdims=True)


def _rowwise(name, fn, rows, bcast, outs, accs, *, nrows, tr, nj=1):
    ni = nrows // tr
    nr, nb, no = len(rows), len(bcast), len(outs)

    def rmap(colfn, shift):
        if callable(shift):
            return lambda j, i: (shift(i), colfn(j))
        if shift == 0:
            return lambda j, i: (i, colfn(j))
        return lambda j, i: (jnp.clip(i + shift, 0, ni - 1), colfn(j))

    in_specs = [pl.BlockSpec((tr, w), rmap(cf, sh)) for (_, w, cf, sh) in rows]
    in_specs += [pl.BlockSpec(b.shape, lambda j, i, nd=b.ndim: (0,) * nd) for b in bcast]
    out_specs = [pl.BlockSpec((tr, w), rmap(cf, 0)) for (_, w, cf, _) in outs]
    out_specs += [pl.BlockSpec((r, w), lambda j, i, cf=cf: (0, cf(j))) for (r, _, w, cf) in accs]
    out_shape = [jax.ShapeDtypeStruct((nrows, nc), dt) for (nc, _, _, dt) in outs]
    out_shape += [jax.ShapeDtypeStruct((r, nc), F32) for (r, nc, _, _) in accs]

    def body(*refs):
        j = pl.program_id(0)
        i = pl.program_id(1)
        rv = [r[...] for r in refs[:nr]]
        bv = [r[...] for r in refs[nr:nr + nb]]
        o_refs = refs[nr + nb:nr + nb + no]
        a_refs = refs[nr + nb + no:]
        o_vals, a_vals = fn(i, j, rv, bv)
        for ref, val in zip(o_refs, o_vals):
            ref[...] = val.astype(ref.dtype)
        if a_refs:
            @pl.when(i == 0)
            def _():
                for ref in a_refs:
                    ref[...] = jnp.zeros(ref.shape, F32)
            for ref, val in zip(a_refs, a_vals):
                ref[...] += val

    res = pl.pallas_call(
        body, name=name, grid=(nj, ni), in_specs=in_specs, out_specs=out_specs, out_shape=out_shape,
        compiler_params=_cparams(("arbitrary", "arbitrary")),
    )(*[r[0] for r in rows], *bcast)
    return res


def _c(k):
    return lambda j: k


def _mm(name, a, b, dims, *, a_blk, a_map, b_blk, b_map, o_shape, o_blk, o_map, grid, o_dtype,
        bias=None, bias_blk=None, bias_map=None, res=None):
    nk = grid[3]
    extra, extra_specs = [], []
    if bias is not None:
        extra.append(bias)
        extra_specs.append(pl.BlockSpec(bias_blk, bias_map))
    if res is not None:
        extra.append(res)
        extra_specs.append(pl.BlockSpec(o_blk, o_map))
    nbias = bias is not None
    nres = res is not None

    def body(*refs):
        a_ref, b_ref = refs[0], refs[1]
        idx = 2
        bias_ref = refs[idx] if nbias else None
        idx += nbias
        res_ref = refs[idx] if nres else None
        idx += nres
        o_ref = refs[idx]
        acc_ref = refs[idx + 1]
        k = pl.program_id(3)
        part = _dot(a_ref[...], b_ref[...], dims)

        def finish(r):
            if nbias:
                r = r + bias_ref[...]
            if nres:
                r = r + res_ref[...]
            o_ref[...] = r.astype(o_ref.dtype)

        if nk == 1:
            finish(part)
        else:
            @pl.when(k == 0)
            def _():
                acc_ref[...] = part

            @pl.when(k > 0)
            def _():
                acc_ref[...] += part

            @pl.when(k == nk - 1)
            def _():
                finish(acc_ref[...])

    acc_shape = o_blk if nk > 1 else (8, 128)
    return pl.pallas_call(
        body, name=name, grid=grid,
        in_specs=[pl.BlockSpec(a_blk, a_map), pl.BlockSpec(b_blk, b_map)] + extra_specs,
        out_specs=pl.BlockSpec(o_blk, o_map),
        out_shape=jax.ShapeDtypeStruct(o_shape, o_dtype),
        scratch_shapes=[pltpu.VMEM(acc_shape, F32)],
        compiler_params=_cparams(("arbitrary", "arbitrary", "arbitrary", "arbitrary")),
    )(a, b, *extra)


def _mm_nn(name, a, b, o_dtype, *, bias=None, res=None, groups=1, tm=1024, tn=1024):
    M = a.shape[0]
    K = a.shape[1] // groups
    N = b.shape[1]
    tm, tn = min(tm, M), min(tn, N)
    njn = N // tn
    return _mm(name, a, b, NN,
               a_blk=(tm, K), a_map=lambda g, j, i, k: (i, g),
               b_blk=(K, tn), b_map=lambda g, j, i, k: (g, j),
               o_shape=(M, groups * N), o_blk=(tm, tn), o_map=lambda g, j, i, k: (i, g * njn + j),
               grid=(groups, njn, M // tm, 1), o_dtype=o_dtype,
               bias=bias, bias_blk=(1, tn), bias_map=lambda g, j, i, k: (0, g * njn + j), res=res)


def _mm_nt(name, a, b, o_dtype, *, res=None, groups=1, tm=512, tk=1024):
    M = a.shape[0]
    K = a.shape[1] // groups
    N = b.shape[0] // groups
    tm, tk = min(tm, M), min(tk, K)
    nk = K // tk
    return _mm(name, a, b, NT,
               a_blk=(tm, tk), a_map=lambda g, j, i, k: (i, g * nk + k),
               b_blk=(N, tk), b_map=lambda g, j, i, k: (g, k),
               o_shape=(M, groups * N), o_blk=(tm, N), o_map=lambda g, j, i, k: (i, g),
               grid=(groups, 1, M // tm, nk), o_dtype=o_dtype, res=res)


def _mm_tn(name, a, b, *, groups=1, tn=1024, tk=1024):
    K = a.shape[0]
    M = a.shape[1] // groups
    N = b.shape[1] // groups
    tn, tk = min(tn, N), min(tk, K)
    njn = N // tn
    return _mm(name, a, b, TN,
               a_blk=(tk, M), a_map=lambda g, j, i, k: (k, g),
               b_blk=(tk, tn), b_map=lambda g, j, i, k: (k, g * njn + j),
               o_shape=(groups * M, N), o_blk=(M, tn), o_map=lambda g, j, i, k: (g, j),
               grid=(groups, njn, 1, K // tk), o_dtype=F32)


def _rms_fwd(x, g):
    S, D = x.shape

    def fn(i, j, rv, bv):
        xv, gv = rv[0], bv[0]
        r = lax.rsqrt(jnp.mean(xv * xv, axis=-1, keepdims=True) + NORM_EPS)
        return [xv * r * gv], []

    return _rowwise("rms_fwd", fn, [(x, D, _c(0), 0)], [g.reshape(1, D)], [(D, D, _c(0), BF16)], [],
                    nrows=S, tr=512)[0]


def _rms_bwd_math(xv, dyv, gv):
    r = lax.rsqrt(jnp.mean(xv * xv, axis=-1, keepdims=True) + NORM_EPS)
    xh = xv * r
    dg = jnp.sum(dyv * xh, axis=0, keepdims=True)
    dxh = dyv * gv
    dx = r * (dxh - xh * jnp.mean(dxh * xh, axis=-1, keepdims=True))
    return dx, dg


def _rms_bwd(x, dh, dout, g):
    S, D = x.shape

    def fn(i, j, rv, bv):
        dx, dg = _rms_bwd_math(rv[0], rv[1], bv[0])
        return [rv[2] + dx], [dg]

    return _rowwise("rms_bwd", fn, [(x, D, _c(0), 0), (dh, D, _c(0), 0), (dout, D, _c(0), 0)], [g.reshape(1, D)],
                    [(D, D, _c(0), F32)], [(1, D, D, _c(0))], nrows=S, tr=256)


def _final_loss(x, target, g):
    S, D = x.shape

    def fn(i, j, rv, bv):
        xv, tv, gv = rv[0], rv[1], bv[0]
        r = lax.rsqrt(jnp.mean(xv * xv, axis=-1, keepdims=True) + NORM_EPS)
        err = xv * r * gv - tv
        loss = 0.5 * jnp.sum(jnp.mean(err * err, axis=-1, keepdims=True), axis=0, keepdims=True)
        dx, dg = _rms_bwd_math(xv, err * (1.0 / D), gv)
        return [dx], [dg, jnp.broadcast_to(loss, (1, 128))]

    return _rowwise("final_loss", fn, [(x, D, _c(0), 0), (target, D, _c(0), 0)], [g.reshape(1, D)],
                    [(D, D, _c(0), F32)], [(1, D, D, _c(0)), (1, 128, 128, _c(0))], nrows=S, tr=256)


MERGE_W = 512


def _merge_fwd(gpre, yproj):
    S = gpre.shape[0]
    nj = D_MODEL // MERGE_W

    def fn(i, j, rv, bv):
        acc = _sigmoid(rv[0]) * rv[4]
        for n in range(1, 4):
            acc = acc + _sigmoid(rv[n]) * rv[4 + n]
        return [acc], []

    rows = [(gpre, MERGE_W, (lambda j, n=n: n * nj + j), 0) for n in range(4)]
    rows += [(yproj, MERGE_W, (lambda j, n=n: n * nj + j), 0) for n in range(4)]
    return _rowwise("merge_fwd", fn, rows, [], [(D_MODEL, MERGE_W, lambda j: j, BF16)], [],
                    nrows=S, tr=512, nj=nj)[0]


def _merge_bwd(dmerged, gpre, yproj):
    S = gpre.shape[0]
    nj4 = D_MODEL // MERGE_W

    def fn(i, j, rv, bv):
        dm, gp, yp = rv
        sg = _sigmoid(gp)
        dgp = dm * yp * sg * (1.0 - sg)
        return [dm * sg, dgp], [jnp.sum(dgp, axis=0, keepdims=True)]

    rows = [(dmerged, MERGE_W, lambda j: j % nj4, 0), (gpre, MERGE_W, lambda j: j, 0), (yproj, MERGE_W, lambda j: j, 0)]
    return _rowwise("merge_bwd", fn, rows, [],
                    [(GATE_W, MERGE_W, lambda j: j, BF16), (GATE_W, MERGE_W, lambda j: j, BF16)],
                    [(1, GATE_W, MERGE_W, lambda j: j)], nrows=S, tr=512, nj=4 * nj4)


SB_TQ = 512


def _split2(v):
    hi = v.astype(BF16)
    lo = (v - hi.astype(F32)).astype(BF16)
    return hi, lo


def _sb_scores(qb, kblk, off, qpos, cidx):
    z = _dot(qb, kblk, NT) * (HEAD ** -0.5)
    mask = (off + cidx) < qpos
    sp = jnp.log(1.0 + jnp.exp(-jnp.abs(z)))
    lb = jnp.minimum(z, 0.0) - sp
    l1 = jnp.where(mask, -jnp.maximum(z, 0.0) - sp, 0.0)
    return mask, lb, l1


def _sb_fwd(proj):
    S = proj.shape[0]
    assert S // HEAD <= HEAD
    tq = min(SB_TQ, S)
    nq = S // tq
    nkb = tq // HEAD

    def body(q_ref, k_ref, v_ref, g_ref, o_ref, y_ref, r_ref, kb_ref, vb_ref):
        i = pl.program_id(1)

        @pl.when(i == 0)
        def _():
            kb_ref[...] = k_ref[...].astype(BF16)
            vb_ref[...] = v_ref[...].astype(BF16)

        qb = q_ref[...].astype(BF16)
        qpos = i * tq + lax.broadcasted_iota(jnp.int32, (tq, HEAD), 0)
        cidx = lax.broadcasted_iota(jnp.int32, (tq, HEAD), 1)
        rr = lax.broadcasted_iota(jnp.int32, (HEAD, HEAD), 0)
        cc = lax.broadcasted_iota(jnp.int32, (HEAD, HEAD), 1)
        upper = (rr > cc).astype(BF16)
        nblk = (i + 1) * nkb

        def step(n, carry):
            acc, run, runs = carry
            kblock = nblk - 1 - n
            off = pl.multiple_of(kblock * HEAD, HEAD)
            kblk = kb_ref[pl.ds(off, HEAD), :]
            vblk = vb_ref[pl.ds(off, HEAD), :]
            mask, lb, l1 = _sb_scores(qb, kblk, off, qpos, cidx)
            hi, lo = _split2(l1)
            after = _dot(hi, upper, NN) + _dot(lo, upper, NN) + run
            w = jnp.where(mask, jnp.exp(lb + after), 0.0)
            acc = acc + _dot(w, vblk, NN)
            runs = jnp.where(cidx == kblock, run, runs)
            run = run + jnp.sum(l1, axis=1, keepdims=True)
            return acc, run, runs

        runs0 = jnp.where(qpos + cidx < 0, 1.0, 0.0)
        acc, _, runs = lax.fori_loop(0, nblk, step, (jnp.zeros((tq, HEAD), F32), jnp.zeros((tq, 1), F32), runs0))
        o_ref[...] = acc
        r_ref[...] = runs
        gate, _ = _silu_and_grad(g_ref[...])
        y_ref[...] = (acc * gate).astype(BF16)

    return pl.pallas_call(
        body, name="sb_fwd", grid=(NHEAD, nq),
        in_specs=[pl.BlockSpec((tq, HEAD), lambda h, i: (i, C_AQ * NHEAD + h)),
                  pl.BlockSpec((S, HEAD), lambda h, i: (0, C_AK * NHEAD + h)),
                  pl.BlockSpec((S, HEAD), lambda h, i: (0, C_AV * NHEAD + h)),
                  pl.BlockSpec((tq, HEAD), lambda h, i: (i, C_AG * NHEAD + h))],
        out_specs=[pl.BlockSpec((tq, HEAD), lambda h, i: (i, h))] * 3,
        out_shape=[jax.ShapeDtypeStruct((S, BW), F32), jax.ShapeDtypeStruct((S, BW), BF16),
                   jax.ShapeDtypeStruct((S, BW), F32)],
        scratch_shapes=[pltpu.VMEM((S, HEAD), BF16), pltpu.VMEM((S, HEAD), BF16)],
        compiler_params=_cparams(("arbitrary", "arbitrary")),
    )(proj, proj, proj, proj)


def _sb_bwd(proj, att, runs, dy):
    S = proj.shape[0]
    tq = min(SB_TQ, S)
    nq = S // tq
    nkb = tq // HEAD
    scale = HEAD ** -0.5

    def body(q_ref, k_ref, v_ref, g_ref, o_ref, r_ref, dy_ref, dq_ref, dk_ref, dv_ref, dg_ref, kb_ref, vb_ref):
        i = pl.program_id(1)

        @pl.when(i == 0)
        def _():
            kb_ref[...] = k_ref[...].astype(BF16)
            vb_ref[...] = v_ref[...].astype(BF16)
            dk_ref[...] = jnp.zeros(dk_ref.shape, F32)
            dv_ref[...] = jnp.zeros(dv_ref.shape, F32)

        gate, dgate = _silu_and_grad(g_ref[...])
        dyv = dy_ref[...]
        att_v = o_ref[...]
        dg_ref[...] = dyv * att_v * dgate
        dob = (dyv * gate).astype(BF16)
        runs = r_ref[...]
        qb = q_ref[...].astype(BF16)
        qpos = i * tq + lax.broadcasted_iota(jnp.int32, (tq, HEAD), 0)
        cidx = lax.broadcasted_iota(jnp.int32, (tq, HEAD), 1)
        rr = lax.broadcasted_iota(jnp.int32, (HEAD, HEAD), 0)
        cc = lax.broadcasted_iota(jnp.int32, (HEAD, HEAD), 1)
        upper = (rr > cc).astype(BF16)
        lower = (rr < cc).astype(BF16)
        nblk = (i + 1) * nkb

        def step(n, carry):
            dq, grun = carry
            off = pl.multiple_of(n * HEAD, HEAD)
            kblk = kb_ref[pl.ds(off, HEAD), :]
            vblk = vb_ref[pl.ds(off, HEAD), :]
            mask, lb, l1 = _sb_scores(qb, kblk, off, qpos, cidx)
            hi, lo = _split2(l1)
            run = jnp.sum(jnp.where(cidx == n, runs, 0.0), axis=1, keepdims=True)
            after = _dot(hi, upper, NN) + _dot(lo, upper, NN) + run
            w = jnp.where(mask, jnp.exp(lb + after), 0.0)
            gw = _dot(dob, vblk, NT) * w
            ghi, glo = _split2(gw)
            gsum = grun + _dot(ghi, lower, NN) + _dot(glo, lower, NN)
            sig = jnp.exp(lb)
            dz = jnp.where(mask, gw * (1.0 - sig) - gsum * sig, 0.0) * scale
            dzb = dz.astype(BF16)
            dq = dq + _dot(dzb, kblk, NN)
            dk_ref[pl.ds(off, HEAD), :] += _dot(dzb, qb, TN)
            dv_ref[pl.ds(off, HEAD), :] += _dot(w, dob, TN)
            grun = grun + jnp.sum(gw, axis=1, keepdims=True)
            return dq, grun

        dq, _ = lax.fori_loop(0, nblk, step, (jnp.zeros((tq, HEAD), F32), jnp.zeros((tq, 1), F32)))
        dq_ref[...] = dq

    blk = lambda h, i: (i, h)
    head = lambda h, i: (0, h)
    return pl.pallas_call(
        body, name="sb_bwd", grid=(NHEAD, nq),
        in_specs=[pl.BlockSpec((tq, HEAD), lambda h, i: (i, C_AQ * NHEAD + h)),
                  pl.BlockSpec((S, HEAD), lambda h, i: (0, C_AK * NHEAD + h)),
                  pl.BlockSpec((S, HEAD), lambda h, i: (0, C_AV * NHEAD + h)),
                  pl.BlockSpec((tq, HEAD), lambda h, i: (i, C_AG * NHEAD + h)),
                  pl.BlockSpec((tq, HEAD), blk), pl.BlockSpec((tq, HEAD), blk), pl.BlockSpec((tq, HEAD), blk)],
        out_specs=[pl.BlockSpec((tq, HEAD), blk), pl.BlockSpec((S, HEAD), head), pl.BlockSpec((S, HEAD), head),
                   pl.BlockSpec((tq, HEAD), blk)],
        out_shape=[jax.ShapeDtypeStruct((S, BW), F32)] * 4,
        scratch_shapes=[pltpu.VMEM((S, HEAD), BF16), pltpu.VMEM((S, HEAD), BF16)],
        compiler_params=_cparams(("arbitrary", "arbitrary")),
    )(proj, proj, proj, proj, att, runs, dy)


CONV_TR = 256


def _conv_taps(buf_ref, w_ref, base, tr):
    acc = w_ref[0:1, :] * buf_ref[pl.ds(base, tr), :]
    for k in range(1, CONV_K):
        acc = acc + w_ref[k:k + 1, :] * buf_ref[pl.ds(base + k, tr), :]
    return acc


def _conv_fwd(proj, conv_w, conv_b, ln_g, ln_b):
    S = proj.shape[0]
    tr = min(CONV_TR, S)
    ni = S // tr

    def body(a_ref, b_ref, ap_ref, bp_ref, g_ref, w_ref, cb_ref, lg_ref, lb_ref, c_ref, y_ref, buf_ref):
        i = pl.program_id(0)
        prev = ap_ref[tr - HALO:, :] * _sigmoid(bp_ref[tr - HALO:, :])
        buf_ref[0:HALO, :] = jnp.where(i > 0, prev, 0.0)
        buf_ref[HALO:, :] = a_ref[...] * _sigmoid(b_ref[...])
        c = _conv_taps(buf_ref, w_ref, HALO - (CONV_K - 1), tr) + cb_ref[...]
        c_ref[...] = c
        yn, _, _ = _ln_fwd(c, lg_ref[...], lb_ref[...])
        act, _ = _silu_and_grad(yn)
        gate, _ = _silu_and_grad(g_ref[...])
        y_ref[...] = (act * gate).astype(BF16)

    cur = lambda k: (lambda i: (i, k))
    prv = lambda k: (lambda i: (jnp.maximum(i - 1, 0), k))
    full = lambda a: pl.BlockSpec(a.shape, lambda i: (0, 0))
    wpad = jnp.pad(conv_w, ((0, HALO - CONV_K), (0, 0)))
    small = [wpad, conv_b.reshape(1, BW), ln_g.reshape(1, BW), ln_b.reshape(1, BW)]
    return pl.pallas_call(
        body, name="conv_fwd", grid=(ni,),
        in_specs=[pl.BlockSpec((tr, BW), cur(C_BA)), pl.BlockSpec((tr, BW), cur(C_BB)),
                  pl.BlockSpec((tr, BW), prv(C_BA)), pl.BlockSpec((tr, BW), prv(C_BB)),
                  pl.BlockSpec((tr, BW), cur(C_BG))] + [full(a) for a in small],
        out_specs=[pl.BlockSpec((tr, BW), lambda i: (i, 0)), pl.BlockSpec((tr, BW), lambda i: (i, 0))],
        out_shape=[jax.ShapeDtypeStruct((S, BW), F32), jax.ShapeDtypeStruct((S, BW), BF16)],
        scratch_shapes=[pltpu.VMEM((tr + HALO, BW), F32)],
        compiler_params=_cparams(("arbitrary",)),
    )(proj, proj, proj, proj, proj, *small)


def _conv_bwd_norm(proj, cpre, dy, ln_g, ln_b):
    S = proj.shape[0]

    def fn(i, j, rv, bv):
        c, bg, dyv = rv
        lg, lb = bv
        yn, xh, rstd = _ln_fwd(c, lg, lb)
        act, dact = _silu_and_grad(yn)
        gate, dgate = _silu_and_grad(bg)
        dyn = dyv * gate * dact
        dc, dlg, dlb = _ln_bwd(dyn, xh, rstd, lg)
        return [dc, dyv * act * dgate], [dlg, dlb, jnp.sum(dc, axis=0, keepdims=True)]

    rows = [(cpre, BW, _c(0), 0), (proj, BW, _c(C_BG), 0), (dy, BW, _c(1), 0)]
    return _rowwise("conv_bwd_norm", fn, rows, [ln_g.reshape(1, BW), ln_b.reshape(1, BW)],
                    [(BW, BW, _c(0), F32), (BW, BW, _c(0), BF16)],
                    [(1, BW, BW, _c(0))] * 3, nrows=S, tr=256)


def _conv_bwd_taps(proj, dc, conv_w):
    S = proj.shape[0]
    tr = min(CONV_TR, S)
    ni = S // tr

    def body(a_ref, b_ref, ap_ref, bp_ref, dc_ref, dcn_ref, w_ref, da_ref, db_ref, dw_ref, gbuf, dbuf, wrev):
        i = pl.program_id(0)

        @pl.when(i == 0)
        def _():
            dw_ref[...] = jnp.zeros(dw_ref.shape, F32)
            for k in range(CONV_K):
                wrev[k:k + 1, :] = w_ref[CONV_K - 1 - k:CONV_K - k, :]

        sb = _sigmoid(b_ref[...])
        av = a_ref[...]
        prev = ap_ref[tr - HALO:, :] * _sigmoid(bp_ref[tr - HALO:, :])
        gbuf[0:HALO, :] = jnp.where(i > 0, prev, 0.0)
        gbuf[HALO:, :] = av * sb
        dcv = dc_ref[...]
        dbuf[0:tr, :] = dcv
        dbuf[tr:, :] = jnp.where(i < ni - 1, dcn_ref[0:HALO, :], 0.0)
        dg = _conv_taps(dbuf, wrev, 0, tr)
        da_ref[...] = (dg * sb).astype(BF16)
        db_ref[...] = (dg * av * sb * (1.0 - sb)).astype(BF16)
        base = HALO - (CONV_K - 1)
        for k in range(CONV_K):
            dw_ref[k:k + 1, :] += jnp.sum(dcv * gbuf[pl.ds(base + k, tr), :], axis=0, keepdims=True)

    cur = lambda k: (lambda i: (i, k))
    prv = lambda k: (lambda i: (jnp.maximum(i - 1, 0), k))
    wpad = jnp.pad(conv_w, ((0, HALO - CONV_K), (0, 0)))
    return pl.pallas_call(
        body, name="conv_bwd_taps", grid=(ni,),
        in_specs=[pl.BlockSpec((tr, BW), cur(C_BA)), pl.BlockSpec((tr, BW), cur(C_BB)),
                  pl.BlockSpec((tr, BW), prv(C_BA)), pl.BlockSpec((tr, BW), prv(C_BB)),
                  pl.BlockSpec((tr, BW), lambda i: (i, 0)),
                  pl.BlockSpec((tr, BW), lambda i: (jnp.minimum(i + 1, ni - 1), 0)),
                  pl.BlockSpec((HALO, BW), lambda i: (0, 0))],
        out_specs=[pl.BlockSpec((tr, BW), lambda i: (i, 0)), pl.BlockSpec((tr, BW), lambda i: (i, 0)),
                   pl.BlockSpec((HALO, BW), lambda i: (0, 0))],
        out_shape=[jax.ShapeDtypeStruct((S, BW), BF16), jax.ShapeDtypeStruct((S, BW), BF16),
                   jax.ShapeDtypeStruct((HALO, BW), F32)],
        scratch_shapes=[pltpu.VMEM((tr + HALO, BW), F32), pltpu.VMEM((tr + HALO, BW), F32),
                        pltpu.VMEM((HALO, BW), F32)],
        compiler_params=_cparams(("arbitrary",)),
    )(proj, proj, proj, proj, dc, dc, wpad)


SGU_TR = 256


def _sgu_mix(wm, vn, bias):
    tr = vn.shape[0]
    rows = []
    for n in range(tr // HEAD):
        cols = []
        for g in range(NHEAD):
            blk = vn[n * HEAD:(n + 1) * HEAD, g * HEAD:(g + 1) * HEAD]
            cols.append(_dot(wm[g], blk, NN) + bias[g * HEAD:(g + 1) * HEAD, :])
        rows.append(jnp.concatenate(cols, axis=1))
    return jnp.concatenate(rows, axis=0)


def _sgu_masked_w(w2d):
    rr = lax.broadcasted_iota(jnp.int32, (HEAD, HEAD), 0)
    cc = lax.broadcasted_iota(jnp.int32, (HEAD, HEAD), 1)
    tril = rr >= cc
    return [jnp.where(tril, w2d[g * HEAD:(g + 1) * HEAD, :], 0.0).astype(BF16) for g in range(NHEAD)], tril


def _sgu_inputs(sgu_w, sgu_b, ln_g, ln_b):
    w2d = sgu_w.reshape(NHEAD * HEAD, HEAD)
    bias = jnp.broadcast_to(sgu_b[:, :, None], (NHEAD, HEAD, HEAD)).reshape(NHEAD * HEAD, HEAD)
    return [w2d, bias, ln_g.reshape(1, BW), ln_b.reshape(1, BW)]


def _sgu_fwd(proj, sgu_w, sgu_b, ln_g, ln_b):
    S = proj.shape[0]

    def fn(i, j, rv, bv):
        cu, cv, cg = rv
        w2d, bias, lg, lb = bv
        wm, _ = _sgu_masked_w(w2d)
        u, _ = _gelu_and_grad(cu)
        v, _ = _gelu_and_grad(cv)
        vn, _, _ = _ln_fwd(v, lg, lb)
        z = _sgu_mix(wm, vn, bias)
        gate, _ = _silu_and_grad(cg)
        return [u * z * gate], []

    rows = [(proj, BW, _c(C_CU), 0), (proj, BW, _c(C_CV), 0), (proj, BW, _c(C_CG), 0)]
    return _rowwise("sgu_fwd", fn, rows, _sgu_inputs(sgu_w, sgu_b, ln_g, ln_b), [(BW, BW, _c(0), BF16)], [],
                    nrows=S, tr=min(SGU_TR, S))[0]


def _sgu_bwd(proj, dy, sgu_w, sgu_b, ln_g, ln_b):
    S = proj.shape[0]

    def fn(i, j, rv, bv):
        cu, cv, cg, dyv = rv
        w2d, bias, lg, lb = bv
        wm, tril = _sgu_masked_w(w2d)
        u, du_dcu = _gelu_and_grad(cu)
        v, dv_dcv = _gelu_and_grad(cv)
        vn, xh, rstd = _ln_fwd(v, lg, lb)
        z = _sgu_mix(wm, vn, bias)
        gate, dgate = _silu_and_grad(cg)
        dz = dyv * u * gate
        dcu = dyv * z * gate * du_dcu
        dcg = dyv * u * z * dgate
        tr = dz.shape[0]
        dvn_rows = []
        dw = [jnp.zeros((HEAD, HEAD), F32) for _ in range(NHEAD)]
        dbias = [jnp.zeros((HEAD, HEAD), F32) for _ in range(NHEAD)]
        for n in range(tr // HEAD):
            cols = []
            for g in range(NHEAD):
                dzb = dz[n * HEAD:(n + 1) * HEAD, g * HEAD:(g + 1) * HEAD]
                vnb = vn[n * HEAD:(n + 1) * HEAD, g * HEAD:(g + 1) * HEAD]
                cols.append(_dot(wm[g], dzb, TN))
                dw[g] = dw[g] + _dot(dzb, vnb, NT)
                dbias[g] = dbias[g] + dzb
            dvn_rows.append(jnp.concatenate(cols, axis=1))
        dvn = jnp.concatenate(dvn_rows, axis=0)
        dv, dlg, dlb = _ln_bwd(dvn, xh, rstd, lg)
        dw2d = jnp.concatenate([jnp.where(tril, d, 0.0) for d in dw], axis=0)
        return [dcu, dv * dv_dcv, dcg], [dw2d, jnp.concatenate(dbias, axis=0), dlg, dlb]

    rows = [(proj, BW, _c(C_CU), 0), (proj, BW, _c(C_CV), 0), (proj, BW, _c(C_CG), 0), (dy, BW, _c(2), 0)]
    return _rowwise("sgu_bwd", fn, rows, _sgu_inputs(sgu_w, sgu_b, ln_g, ln_b),
                    [(BW, BW, _c(0), BF16)] * 3,
                    [(NHEAD * HEAD, HEAD, HEAD, _c(0)), (NHEAD * HEAD, HEAD, HEAD, _c(0)), (1, BW, BW, _c(0)),
                     (1, BW, BW, _c(0))], nrows=S, tr=min(SGU_TR, S))


NEG = -1e30
NCOL = IN_WIDTH // BW


def _band_masks():
    a = lax.broadcasted_iota(jnp.int32, (HEAD, HEAD), 0)
    c = lax.broadcasted_iota(jnp.int32, (HEAD, HEAD), 1)
    return a >= c, a <= c


def _dil_fwd(proj, grp):
    S = proj.shape[0]
    dil = DIL_PATTERNS[grp][1]
    L = S // dil
    nb = L // HEAD
    proj2 = proj.reshape(L, dil * IN_WIDTH)
    scale = HEAD ** -0.5

    def body(q_ref, kc_ref, kp_ref, vc_ref, vp_ref, o_ref, l_ref):
        b = pl.program_id(1)
        m_cur, m_prev = _band_masks()
        m_prev = m_prev & (b > 0)
        for h in range(NHEAD):
            sl = slice(h * HEAD, (h + 1) * HEAD)
            q = q_ref[:, sl].astype(BF16)
            s_c = jnp.where(m_cur, _dot(q, kc_ref[:, sl], NT) * scale, NEG)
            s_p = jnp.where(m_prev, _dot(q, kp_ref[:, sl], NT) * scale, NEG)
            m = jnp.maximum(jnp.max(s_c, axis=1, keepdims=True), jnp.max(s_p, axis=1, keepdims=True))
            p_c = jnp.exp(s_c - m)
            p_p = jnp.exp(s_p - m)
            den = jnp.sum(p_c, axis=1, keepdims=True) + jnp.sum(p_p, axis=1, keepdims=True)
            o = (_dot(p_c, vc_ref[:, sl], NN) + _dot(p_p, vp_ref[:, sl], NN)) / den
            o_ref[:, sl] = o
            l_ref[:, sl] = jnp.broadcast_to(m + jnp.log(den), (HEAD, HEAD))

    cur = lambda col: (lambda r, b: (b, r * NCOL + col))
    prv = lambda col: (lambda r, b: (jnp.maximum(b - 1, 0), r * NCOL + col))
    out_map = lambda r, b: (b, r)
    o, lse = pl.pallas_call(
        body, name=f"dil_fwd{grp}", grid=(dil, nb),
        in_specs=[pl.BlockSpec((HEAD, BW), cur(C_DQ + grp)),
                  pl.BlockSpec((HEAD, BW), cur(C_DK + grp)), pl.BlockSpec((HEAD, BW), prv(C_DK + grp)),
                  pl.BlockSpec((HEAD, BW), cur(C_DV)), pl.BlockSpec((HEAD, BW), prv(C_DV))],
        out_specs=[pl.BlockSpec((HEAD, BW), out_map), pl.BlockSpec((HEAD, BW), out_map)],
        out_shape=[jax.ShapeDtypeStruct((L, dil * BW), F32), jax.ShapeDtypeStruct((L, dil * BW), F32)],
        compiler_params=_cparams(("arbitrary", "arbitrary")),
    )(proj2, proj2, proj2, proj2, proj2)
    return o.reshape(S, BW), lse.reshape(S, BW)


def _dil_combine(proj, os_, lses):
    S = proj.shape[0]

    def fn(i, j, rv, bv):
        o1, o2, o3, l1, l2, l3, dg = rv
        m = jnp.maximum(jnp.maximum(l1, l2), l3)
        e1, e2, e3 = jnp.exp(l1 - m), jnp.exp(l2 - m), jnp.exp(l3 - m)
        den = e1 + e2 + e3
        out = (e1 * o1 + e2 * o2 + e3 * o3) / den
        gate, _ = _silu_and_grad(dg)
        return [out, m + jnp.log(den), out * gate], []

    rows = [(a, BW, _c(0), 0) for a in (*os_, *lses)] + [(proj, BW, _c(C_DG), 0)]
    return _rowwise("dil_combine", fn, rows, [], [(BW, BW, _c(0), F32), (BW, BW, _c(0), F32), (BW, BW, _c(0), BF16)],
                    [], nrows=S, tr=256)


def _dil_gate_bwd(proj, att, dy):
    S = proj.shape[0]

    def fn(i, j, rv, bv):
        dg, av, dyv = rv
        gate, dgate = _silu_and_grad(dg)
        dout = dyv * gate
        prod = dout * av
        tr = prod.shape[0]
        delta = jnp.concatenate(
            [jnp.broadcast_to(jnp.sum(prod[:, h * HEAD:(h + 1) * HEAD], axis=1, keepdims=True), (tr, HEAD))
             for h in range(NHEAD)], axis=1)
        return [dout, dyv * av * dgate, delta], []

    rows = [(proj, BW, _c(C_DG), 0), (att, BW, _c(0), 0), (dy, BW, _c(3), 0)]
    return _rowwise("dil_gate_bwd", fn, rows, [], [(BW, BW, _c(0), F32), (BW, BW, _c(0), BF16), (BW, BW, _c(0), F32)],
                    [], nrows=S, tr=256)


def _dil_bwd(proj, dout, lse, delta, grp):
    S = proj.shape[0]
    dil = DIL_PATTERNS[grp][1]
    L = S // dil
    nb = L // HEAD
    proj2 = proj.reshape(L, dil * IN_WIDTH)
    view = lambda a: a.reshape(L, dil * BW)
    scale = HEAD ** -0.5

    def body(q_ref, qn_ref, kc_ref, kp_ref, vc_ref, vp_ref, do_ref, don_ref, l_ref, ln_ref, d_ref, dn_ref,
             dq_ref, dk_ref, dv_ref):
        b = pl.program_id(1)
        m_cur, m_band = _band_masks()
        m_prev = m_band & (b > 0)
        m_next = m_band & (b < nb - 1)
        for h in range(NHEAD):
            sl = slice(h * HEAD, (h + 1) * HEAD)
            q, qn = q_ref[:, sl].astype(BF16), qn_ref[:, sl].astype(BF16)
            kc, kp = kc_ref[:, sl].astype(BF16), kp_ref[:, sl].astype(BF16)
            vc, vp = vc_ref[:, sl].astype(BF16), vp_ref[:, sl].astype(BF16)
            do, don = do_ref[:, sl].astype(BF16), don_ref[:, sl].astype(BF16)
            lse_c, lse_n = l_ref[:, sl], ln_ref[:, sl]
            dl_c, dl_n = d_ref[:, sl], dn_ref[:, sl]
            p_cc = jnp.exp(jnp.where(m_cur, _dot(q, kc, NT) * scale - lse_c, NEG))
            p_cp = jnp.exp(jnp.where(m_prev, _dot(q, kp, NT) * scale - lse_c, NEG))
            p_nc = jnp.exp(jnp.where(m_next, _dot(qn, kc, NT) * scale - lse_n, NEG))
            ds_cc = p_cc * (_dot(do, vc, NT) - dl_c)
            ds_cp = p_cp * (_dot(do, vp, NT) - dl_c)
            ds_nc = p_nc * (_dot(don, vc, NT) - dl_n)
            dq_ref[:, sl] = ((_dot(ds_cc, kc, NN) + _dot(ds_cp, kp, NN)) * scale).astype(BF16)
            dk_ref[:, sl] = ((_dot(ds_cc, q, TN) + _dot(ds_nc, qn, TN)) * scale).astype(BF16)
            dv_ref[:, sl] = _dot(p_cc, do, TN) + _dot(p_nc, don, TN)

    cur = lambda col: (lambda r, b: (b, r * NCOL + col))
    prv = lambda col: (lambda r, b: (jnp.maximum(b - 1, 0), r * NCOL + col))
    nxt = lambda col: (lambda r, b: (jnp.minimum(b + 1, nb - 1), r * NCOL + col))
    o_cur = lambda r, b: (b, r)
    o_nxt = lambda r, b: (jnp.minimum(b + 1, nb - 1), r)
    blk = (HEAD, BW)
    dq, dk, dv = pl.pallas_call(
        body, name=f"dil_bwd{grp}", grid=(dil, nb),
        in_specs=[pl.BlockSpec(blk, cur(C_DQ + grp)), pl.BlockSpec(blk, nxt(C_DQ + grp)),
                  pl.BlockSpec(blk, cur(C_DK + grp)), pl.BlockSpec(blk, prv(C_DK + grp)),
                  pl.BlockSpec(blk, cur(C_DV)), pl.BlockSpec(blk, prv(C_DV)),
                  pl.BlockSpec(blk, o_cur), pl.BlockSpec(blk, o_nxt),
                  pl.BlockSpec(blk, o_cur), pl.BlockSpec(blk, o_nxt),
                  pl.BlockSpec(blk, o_cur), pl.BlockSpec(blk, o_nxt)],
        out_specs=[pl.BlockSpec(blk, o_cur)] * 3,
        out_shape=[jax.ShapeDtypeStruct((L, dil * BW), BF16), jax.ShapeDtypeStruct((L, dil * BW), BF16),
                   jax.ShapeDtypeStruct((L, dil * BW), F32)],
        compiler_params=_cparams(("arbitrary", "arbitrary")),
    )(proj2, proj2, proj2, proj2, proj2, proj2, view(dout), view(dout), view(lse), view(lse), view(delta), view(delta))
    return dq.reshape(S, BW), dk.reshape(S, BW), dv.reshape(S, BW)


def _add3(a, b, c):
    S, W = a.shape

    def fn(i, j, rv, bv):
        return [rv[0] + rv[1] + rv[2]], []

    return _rowwise("add3", fn, [(a, W, _c(0), 0), (b, W, _c(0), 0), (c, W, _c(0), 0)], [], [(W, W, _c(0), BF16)], [],
                    nrows=S, tr=512)[0]


def _layer_fwd(x, p):
    h = _rms_fwd(x, p["norm_g"])
    proj = _mm_nn("in_proj", h, p["w_in"], F32)
    gpre = _mm_nn("gate_proj", h, p["w_gate"], F32, bias=p["b_gate"].reshape(1, GATE_W))
    att_a, ya, runs_a = _sb_fwd(proj)
    cpre, yb = _conv_fwd(proj, p["conv_w"], p["conv_b"], p["conv_ln_g"], p["conv_ln_b"])
    yc = _sgu_fwd(proj, p["sgu_w"], p["sgu_b"], p["sgu_ln_g"], p["sgu_ln_b"])
    dil = [_dil_fwd(proj, g) for g in range(3)]
    att_d, lse_d, yd = _dil_combine(proj, [d[0] for d in dil], [d[1] for d in dil])
    y = jnp.concatenate([ya, yb, yc, yd], axis=1)
    yproj = _mm_nn("branch_proj", y, p["w_branch"], F32, groups=4)
    merged = _merge_fwd(gpre, yproj)
    x_next = _mm_nn("out_proj", merged, p["w_out"], F32, res=x)
    saved = dict(x=x, h=h, proj=proj, gpre=gpre, att_a=att_a, runs_a=runs_a, cpre=cpre, att_d=att_d, lse_d=lse_d, y=y,
                 yproj=yproj, merged=merged)
    return x_next, saved


def _layer_bwd(dout, s, p):
    proj = s["proj"]
    dmerged = _mm_nt("out_proj_dx", dout, p["w_out"], F32)
    g_w_out = _mm_tn("out_proj_dw", s["merged"], dout)
    dyproj, dgpre, g_b_gate = _merge_bwd(dmerged, s["gpre"], s["yproj"])
    dy = _mm_nt("branch_proj_dx", dyproj, p["w_branch"], F32, groups=4)
    g_w_branch = _mm_tn("branch_proj_dw", s["y"], dyproj, groups=4)
    d_aq, d_ak, d_av, d_ag = _sb_bwd(proj, s["att_a"], s["runs_a"], dy)
    dc, d_bg, g_cln_g, g_cln_b, g_conv_b = _conv_bwd_norm(proj, s["cpre"], dy, p["conv_ln_g"], p["conv_ln_b"])
    d_ba, d_bb, g_conv_w = _conv_bwd_taps(proj, dc, p["conv_w"])
    d_cu, d_cv, d_cg, g_sgu_w, g_sgu_bias, g_sln_g, g_sln_b = _sgu_bwd(
        proj, dy, p["sgu_w"], p["sgu_b"], p["sgu_ln_g"], p["sgu_ln_b"])
    dout_d, d_dg, delta = _dil_gate_bwd(proj, s["att_d"], dy)
    dil = [_dil_bwd(proj, dout_d, s["lse_d"], delta, g) for g in range(3)]
    d_dv = _add3(dil[0][2], dil[1][2], dil[2][2])
    pieces = [d_aq, d_ak, d_av, d_ag, d_ba, d_bb, d_bg, d_cu, d_cv, d_cg,
              dil[0][0], dil[1][0], dil[2][0], dil[0][1], dil[1][1], dil[2][1], d_dv, d_dg]
    dproj = jnp.concatenate([t.astype(BF16) for t in pieces], axis=1)
    dh_gate = _mm_nt("gate_proj_dx", dgpre, p["w_gate"], F32)
    dh = _mm_nt("in_proj_dx", dproj, p["w_in"], F32, res=dh_gate)
    g_w_in = _mm_tn("in_proj_dw", s["h"], dproj)
    g_w_gate = _mm_tn("gate_proj_dw", s["h"], dgpre)
    dx, g_norm_g = _rms_bwd(s["x"], dh, dout, p["norm_g"])
    grads = dict(
        norm_g=g_norm_g.reshape(D_MODEL), w_in=g_w_in, conv_w=g_conv_w[:CONV_K], conv_b=g_conv_b.reshape(BW),
        conv_ln_g=g_cln_g.reshape(BW), conv_ln_b=g_cln_b.reshape(BW), sgu_ln_g=g_sln_g.reshape(BW),
        sgu_ln_b=g_sln_b.reshape(BW), sgu_w=g_sgu_w.reshape(NHEAD, HEAD, HEAD),
        sgu_b=jnp.sum(g_sgu_bias.reshape(NHEAD, HEAD, HEAD), axis=-1), w_branch=g_w_branch.reshape(4, BW, D_MODEL),
        w_gate=g_w_gate, b_gate=g_b_gate.reshape(GATE_W), w_out=g_w_out)
    return dx, grads


ANY = pl.BlockSpec(memory_space=pl.ANY)
CHIP_FLIPS = ((1, 0), (0, 1), (1, 1))


def _at(ref, axis, start, size):
    idx = [slice(None)] * len(ref.shape)
    idx[axis] = pl.ds(start, size)
    return ref.at[tuple(idx)]


def _position():
    return lax.axis_index("x"), lax.axis_index("y"), lax.axis_index("c")


def _gather(name, w, axis):
    n = w.shape[axis]
    full = list(w.shape)
    full[axis] = 4 * n

    def body(w_ref, out_ref, send_sems, recv_sems, local_sem):
        x, y, c = _position()
        mine = pltpu.make_async_copy(w_ref, _at(out_ref, axis, (2 * x + y) * n, n), local_sem)
        mine.start()
        sends = []
        for k, (fx, fy) in enumerate(CHIP_FLIPS):
            cp = pltpu.make_async_remote_copy(
                src_ref=w_ref, dst_ref=_at(out_ref, axis, (2 * x + y) * n, n), send_sem=send_sems.at[k],
                recv_sem=recv_sems.at[k], device_id=(x ^ fx, y ^ fy, c), device_id_type=MESH)
            cp.start()
            sends.append(cp)
        for k, (fx, fy) in enumerate(CHIP_FLIPS):
            px, py = x ^ fx, y ^ fy
            pltpu.make_async_remote_copy(
                src_ref=w_ref, dst_ref=_at(out_ref, axis, (2 * px + py) * n, n), send_sem=send_sems.at[k],
                recv_sem=recv_sems.at[k], device_id=(px, py, c), device_id_type=MESH).wait_recv()
        for cp in sends:
            cp.wait_send()
        mine.wait()

    return pl.pallas_call(
        body, name=name, in_specs=[ANY], out_specs=ANY, out_shape=jax.ShapeDtypeStruct(tuple(full), w.dtype),
        scratch_shapes=[pltpu.SemaphoreType.DMA((3,)), pltpu.SemaphoreType.DMA((3,)), pltpu.SemaphoreType.DMA],
    )(w)


def _exchange_halves(name, g, hx):
    L = g.shape[0]
    nh = g.shape[hx] // 2
    out = list(g.shape)
    out[hx] = nh

    def body(g_ref, out_ref, send_sems, recv_sems):
        x, y, c = _position()
        copies = []
        for l in range(L):
            cp = pltpu.make_async_remote_copy(
                src_ref=_at(g_ref.at[l], hx - 1, (1 - c) * nh, nh), dst_ref=out_ref.at[l], send_sem=send_sems.at[l],
                recv_sem=recv_sems.at[l], device_id=(x, y, 1 - c), device_id_type=MESH)
            cp.start()
            copies.append(cp)
        for cp in copies:
            cp.wait()

    return pl.pallas_call(
        body, name=name, in_specs=[ANY], out_specs=ANY, out_shape=jax.ShapeDtypeStruct(tuple(out), g.dtype),
        scratch_shapes=[pltpu.SemaphoreType.DMA((L,)), pltpu.SemaphoreType.DMA((L,))],
    )(g)


def _scatter_shards(name, s1, ax):
    n = s1.shape[ax] // 4
    piece = list(s1.shape)
    piece[ax] = n

    def body(s_ref, out_ref, send_sems, recv_sems):
        x, y, c = _position()
        copies = []
        for k, (fx, fy) in enumerate(CHIP_FLIPS):
            px, py = x ^ fx, y ^ fy
            cp = pltpu.make_async_remote_copy(
                src_ref=_at(s_ref, ax, (2 * px + py) * n, n), dst_ref=out_ref.at[k], send_sem=send_sems.at[k],
                recv_sem=recv_sems.at[k], device_id=(px, py, c), device_id_type=MESH)
            cp.start()
            copies.append(cp)
        for cp in copies:
            cp.wait()

    return pl.pallas_call(
        body, name=name, in_specs=[ANY], out_specs=ANY, out_shape=jax.ShapeDtypeStruct((3, *piece), s1.dtype),
        scratch_shapes=[pltpu.SemaphoreType.DMA((3,)), pltpu.SemaphoreType.DMA((3,))],
    )(s1)


def _join_halves(name, s2, hx):
    nh = s2.shape[hx]
    out = list(s2.shape)
    out[hx] = 2 * nh

    def body(s_ref, out_ref, send_sem, recv_sem, local_sem):
        x, y, c = _position()
        mine = pltpu.make_async_copy(s_ref, _at(out_ref, hx, c * nh, nh), local_sem)
        mine.start()
        cp = pltpu.make_async_remote_copy(
            src_ref=s_ref, dst_ref=_at(out_ref, hx, c * nh, nh), send_sem=send_sem, recv_sem=recv_sem,
            device_id=(x, y, 1 - c), device_id_type=MESH)
        cp.start()
        pltpu.make_async_remote_copy(
            src_ref=s_ref, dst_ref=_at(out_ref, hx, (1 - c) * nh, nh), send_sem=send_sem, recv_sem=recv_sem,
            device_id=(x, y, 1 - c), device_id_type=MESH).wait_recv()
        cp.wait_send()
        mine.wait()

    return pl.pallas_call(
        body, name=name, in_specs=[ANY], out_specs=ANY, out_shape=jax.ShapeDtypeStruct(tuple(out), s2.dtype),
        scratch_shapes=[pltpu.SemaphoreType.DMA, pltpu.SemaphoreType.DMA, pltpu.SemaphoreType.DMA],
    )(s2)


def _sum_terms(name, terms, out_rows, out_cols, blk, grid, scalars):
    tr, tc = blk

    def body(s_ref, *refs):
        acc = refs[0][...]
        for r in refs[1:-1]:
            acc = acc + r[...]
        refs[-1][...] = acc

    in_specs = [pl.BlockSpec((None, tr, tc) if a.ndim == 3 else (tr, tc), m) for a, m in terms]
    nrb = grid[1]
    return pl.pallas_call(
        body, name=name,
        grid_spec=pltpu.PrefetchScalarGridSpec(
            num_scalar_prefetch=1, grid=grid, in_specs=in_specs,
            out_specs=pl.BlockSpec((tr, tc), lambda l, i, s: (l * nrb + i, 0))),
        out_shape=jax.ShapeDtypeStruct((out_rows, out_cols), F32),
        compiler_params=_cparams(("arbitrary", "arbitrary")),
    )(scalars, *[a for a, _ in terms])


def _rows_for(cols, rows, budget=1 << 19):
    tr = 8
    while tr * 2 * cols <= budget and tr * 2 <= min(rows, 256) and rows % (tr * 2) == 0:
        tr *= 2
    assert rows % tr == 0
    return tr


def _reduce_scatter(tag, g, ax):
    hx = 3 - ax
    L, R, C = g.shape
    x, y, c = _position()
    scalars = jnp.stack([c, 2 * x + y]).astype(jnp.int32)
    got = _exchange_halves(f"rs_halves_{tag}", g, hx)
    Rh, Ch = got.shape[1], got.shape[2]
    tr1 = _rows_for(Ch, Rh)
    nb1 = Rh // tr1
    if hx == 1:
        g_map = lambda l, i, s: ((2 * l + s[0]) * nb1 + i, 0)
    else:
        g_map = lambda l, i, s: (l * nb1 + i, s[0])
    s1 = _sum_terms(f"rs_sum2_{tag}",
                    [(g.reshape(L * R, C), g_map), (got.reshape(L * Rh, Ch), lambda l, i, s: (l * nb1 + i, 0))],
                    L * Rh, Ch, (tr1, Ch), (L, nb1), scalars).reshape(L, Rh, Ch)
    parts = _scatter_shards(f"rs_shards_{tag}", s1, ax)
    Rs, Cs = parts.shape[2], parts.shape[3]
    tr2 = _rows_for(Cs, Rs)
    nb2 = Rs // tr2
    if ax == 2:
        s_map = lambda l, i, s: (l * nb2 + i, s[1])
    else:
        s_map = lambda l, i, s: ((4 * l + s[1]) * nb2 + i, 0)
    terms = [(s1.reshape(L * Rh, Ch), s_map)]
    terms += [(parts.reshape(3, L * Rs, Cs), (lambda l, i, s, k=k: (k, l * nb2 + i, 0))) for k in range(3)]
    s2 = _sum_terms(f"rs_sum4_{tag}", terms, L * Rs, Cs, (tr2, Cs), (L, nb2), scalars).reshape(L, Rs, Cs)
    return _join_halves(f"rs_join_{tag}", s2, hx)


def _all_to_all_small(buf):
    nr = buf.shape[0]

    def body(b_ref, out_ref, send_sems, recv_sems, local_sem):
        x, y, c = _position()
        me = 4 * x + 2 * y + c
        mine = pltpu.make_async_copy(b_ref, out_ref.at[me], local_sem)
        mine.start()
        sends = []
        for r in range(1, 8):
            peer = (x ^ (r >> 2), y ^ ((r >> 1) & 1), c ^ (r & 1))
            cp = pltpu.make_async_remote_copy(
                src_ref=b_ref, dst_ref=out_ref.at[me], send_sem=send_sems.at[r - 1], recv_sem=recv_sems.at[r - 1],
                device_id=peer, device_id_type=MESH)
            cp.start()
            sends.append(cp)
        for r in range(1, 8):
            px, py, pc = x ^ (r >> 2), y ^ ((r >> 1) & 1), c ^ (r & 1)
            pltpu.make_async_remote_copy(
                src_ref=b_ref, dst_ref=out_ref.at[4 * px + 2 * py + pc], send_sem=send_sems.at[r - 1],
                recv_sem=recv_sems.at[r - 1], device_id=(px, py, pc), device_id_type=MESH).wait_recv()
        for cp in sends:
            cp.wait_send()
        mine.wait()

    return pl.pallas_call(
        body, name="small_exchange", in_specs=[ANY], out_specs=ANY,
        out_shape=jax.ShapeDtypeStruct((8, nr, 128), buf.dtype),
        scratch_shapes=[pltpu.SemaphoreType.DMA((7,)), pltpu.SemaphoreType.DMA((7,)), pltpu.SemaphoreType.DMA],
    )(buf)


SMALL_TR = 256


def _all_reduce_small(buf):
    nr = buf.shape[0]
    slots = _all_to_all_small(buf).reshape(8 * nr, 128)
    nblk = nr // SMALL_TR

    def fn(i, j, rv, bv):
        acc = rv[0]
        for v in rv[1:]:
            acc = acc + v
        return [acc], []

    rows = [(slots, 128, _c(0), (lambda i, d=d: d * nblk + i)) for d in range(8)]
    return _rowwise("small_sum", fn, rows, [], [(128, 128, _c(0), F32)], [], nrows=nr, tr=SMALL_TR)[0]


def _adamw(name, w, g, m, v):
    rows, cols = w.shape
    tr = _rows_for(cols, rows, budget=1 << 18)

    def fn(i, j, rv, bv):
        wv, gv, mv, vv = rv
        m2 = ADAM_B1 * mv + (1.0 - ADAM_B1) * gv
        v2 = ADAM_B2 * vv + (1.0 - ADAM_B2) * (gv * gv)
        m_hat = m2 / (1.0 - ADAM_B1 ** ADAM_STEP)
        v_hat = v2 / (1.0 - ADAM_B2 ** ADAM_STEP)
        delta = -ADAM_LR * (m_hat / (jnp.sqrt(v_hat) + ADAM_EPS) + ADAM_WD * wv)
        return [delta, m2, v2], []

    return _rowwise(name, fn, [(a, cols, _c(0), 0) for a in (w, g, m, v)], [], [(cols, cols, _c(0), F32)] * 3, [],
                    nrows=rows, tr=tr)


WEIGHTS = ("norm_g", "w_in", "conv_w", "conv_b", "conv_ln_g", "conv_ln_b", "sgu_ln_g", "sgu_ln_b", "sgu_w", "sgu_b",
           "w_branch", "w_gate", "b_gate", "w_out", "final_g")
BIG = ("w_in", "w_branch", "w_gate", "w_out")
SMALL = tuple(n for n in WEIGHTS if n not in BIG)


def _pack(arrays, pad_rows):
    flat = jnp.concatenate([a.reshape(-1).astype(F32) for a in arrays])
    unit = 128 * pad_rows
    total = -(-flat.shape[0] // unit) * unit
    return jnp.pad(flat, (0, total - flat.shape[0])).reshape(total // 128, 128)


def _unpack(buf, shapes):
    flat = buf.reshape(-1)
    out, off = [], 0
    for shp in shapes:
        size = math.prod(shp)
        out.append(flat[off:off + size].reshape(shp))
        off += size
    return out


def kernel(x, norm_g, w_in, conv_w, conv_b, conv_ln_g, conv_ln_b, sgu_ln_g, sgu_ln_b, sgu_w, sgu_b, w_branch, w_gate, b_gate, w_out, final_g, loss_target, m_norm_g, m_w_in, m_conv_w, m_conv_b, m_conv_ln_g, m_conv_ln_b, m_sgu_ln_g, m_sgu_ln_b, m_sgu_w, m_sgu_b, m_w_branch, m_w_gate, m_b_gate, m_w_out, m_final_g, v_norm_g, v_w_in, v_conv_w, v_conv_b, v_conv_ln_g, v_conv_ln_b, v_sgu_ln_g, v_sgu_ln_b, v_sgu_w, v_sgu_b, v_w_branch, v_w_gate, v_b_gate, v_w_out, v_final_g):
    w = dict(norm_g=norm_g, w_in=w_in, conv_w=conv_w, conv_b=conv_b, conv_ln_g=conv_ln_g, conv_ln_b=conv_ln_b,
             sgu_ln_g=sgu_ln_g, sgu_ln_b=sgu_ln_b, sgu_w=sgu_w, sgu_b=sgu_b, w_branch=w_branch, w_gate=w_gate,
             b_gate=b_gate, w_out=w_out, final_g=final_g)
    m = dict(norm_g=m_norm_g, w_in=m_w_in, conv_w=m_conv_w, conv_b=m_conv_b, conv_ln_g=m_conv_ln_g,
             conv_ln_b=m_conv_ln_b, sgu_ln_g=m_sgu_ln_g, sgu_ln_b=m_sgu_ln_b, sgu_w=m_sgu_w, sgu_b=m_sgu_b,
             w_branch=m_w_branch, w_gate=m_w_gate, b_gate=m_b_gate, w_out=m_w_out, final_g=m_final_g)
    v = dict(norm_g=v_norm_g, w_in=v_w_in, conv_w=v_conv_w, conv_b=v_conv_b, conv_ln_g=v_conv_ln_g,
             conv_ln_b=v_conv_ln_b, sgu_ln_g=v_sgu_ln_g, sgu_ln_b=v_sgu_ln_b, sgu_w=v_sgu_w, sgu_b=v_sgu_b,
             w_branch=v_w_branch, w_gate=v_w_gate, b_gate=v_b_gate, w_out=v_w_out, final_g=v_final_g)
    depth = w_in.shape[0]
    chip = 2 * lax.axis_index("x") + lax.axis_index("y")

    full = dict(
        w_in=_gather("gather_w_in", w_in.astype(BF16), 2),
        w_gate=_gather("gather_w_gate", w_gate.astype(BF16), 2),
        w_branch=_gather("gather_w_branch", w_branch.astype(BF16).reshape(depth * 4, BW, -1), 2).reshape(
            depth, 4 * BW, D_MODEL),
        w_out=_gather("gather_w_out", w_out.astype(BF16), 1),
        conv_w=_gather("gather_conv_w", conv_w, 2))
    layer_params = []
    for l in range(depth):
        p = {n: w[n][l] for n in SMALL if n not in ("final_g", "conv_w")}
        p.update({n: full[n][l] for n in full})
        layer_params.append(p)

    h = x[0]
    saved = []
    for l in range(depth):
        h, s = _layer_fwd(h, layer_params[l])
        saved.append(s)
    dh, g_final, loss = _final_loss(h, loss_target[0], final_g)
    layer_grads = [None] * depth
    for l in reversed(range(depth)):
        dh, layer_grads[l] = _layer_bwd(dh, saved[l], layer_params[l])
    grad_x = dh[None]
    local = {n: jnp.stack([layer_grads[l][n] for l in range(depth)]) for n in WEIGHTS if n != "final_g"}
    local["final_g"] = g_final.reshape(-1)

    grads = dict(
        w_in=_reduce_scatter("w_in", local["w_in"], 2),
        w_gate=_reduce_scatter("w_gate", local["w_gate"], 2),
        w_branch=_reduce_scatter("w_branch", local["w_branch"].reshape(depth * 4, BW, D_MODEL), 2).reshape(
            w_branch.shape),
        w_out=_reduce_scatter("w_out", local["w_out"], 1))
    small_shapes = [local[n].shape for n in SMALL] + [(128,)]
    reduced = _unpack(_all_reduce_small(_pack([local[n] for n in SMALL] + [loss.reshape(128)], SMALL_TR)), small_shapes)
    for n, r in zip(SMALL, reduced[:-1]):
        grads[n] = r
    loss_out = reduced[-1][0]
    ncw = conv_w.shape[2]
    grads["conv_w"] = lax.dynamic_slice_in_dim(grads["conv_w"], chip * ncw, ncw, axis=2)

    delta, new_m, new_v = {}, {}, {}
    for n in BIG:
        cols = w[n].shape[-1]
        d2, m2, v2 = _adamw(f"adamw_{n}", *[a.reshape(-1, cols) for a in (w[n], grads[n], m[n], v[n])])
        delta[n], new_m[n], new_v[n] = d2.reshape(w[n].shape), m2.reshape(w[n].shape), v2.reshape(w[n].shape)
    shapes = [w[n].shape for n in SMALL]
    packed = [_pack([t[n] for n in SMALL], SMALL_TR) for t in (w, grads, m, v)]
    outs = _adamw("adamw_small", *packed)
    for res, o in zip((delta, new_m, new_v), outs):
        for n, a in zip(SMALL, _unpack(o, shapes)):
            res[n] = a
    return (loss_out, grad_x, *[grads[n] for n in WEIGHTS], *[delta[n] for n in WEIGHTS],
            *[new_m[n] for n in WEIGHTS], *[new_v[n] for n in WEIGHTS])
```

```python
import functools
import math

import jax
import jax.numpy as jnp
from jax import lax
from jax.experimental import pallas as pl
from jax.experimental.pallas import tpu as pltpu

F32 = jnp.float32
BF16 = jnp.bfloat16
MESH = pl.DeviceIdType.MESH

DEPTH = 4
D_MODEL = 2048
HEAD = 128
NHEAD = 8
BW = 1024
IN_WIDTH = 18432
GATE_W = 4 * D_MODEL
NORM_EPS = 1e-6
CONV_K = 31
HALO = 32
DIL_PATTERNS = ((128, 1), (512, 4), (2048, 16))
C_AQ, C_AK, C_AV, C_AG, C_BA, C_BB, C_BG, C_CU, C_CV, C_CG, C_DQ, C_DK, C_DV, C_DG = (
    0, 1, 2, 3, 4, 5, 6, 7, 8, 9, 10, 13, 16, 17)
ADAM_LR, ADAM_B1, ADAM_B2, ADAM_EPS, ADAM_WD, ADAM_STEP = 0.001, 0.9, 0.999, 1e-08, 0.01, 10
VMEM_LIMIT = 56 * 1024 * 1024

NN = (((1,), (0,)), ((), ()))
NT = (((1,), (1,)), ((), ()))
TN = (((0,), (0,)), ((), ()))


def _cparams(sem):
    return pltpu.CompilerParams(dimension_semantics=sem, vmem_limit_bytes=VMEM_LIMIT)


def _dot(a, b, dims):
    return lax.dot_general(a.astype(BF16), b.astype(BF16), dims, preferred_element_type=F32)


def _sigmoid(x):
    return 1.0 / (1.0 + jnp.exp(-x))


def _silu_and_grad(x):
    s = _sigmoid(x)
    return x * s, s * (1.0 + x * (1.0 - s))


_GELU_C = math.sqrt(2.0 / math.pi)


def _gelu_and_grad(x):
    t = jnp.tanh(_GELU_C * (x + 0.044715 * x * x * x))
    y = 0.5 * x * (1.0 + t)
    dy = 0.5 * (1.0 + t) + 0.5 * x * (1.0 - t * t) * _GELU_C * (1.0 + 3.0 * 0.044715 * x * x)
    return y, dy


def _ln_fwd(x, g, b):
    mu = jnp.mean(x, axis=-1, keepdims=True)
    xc = x - mu
    var = jnp.mean(xc * xc, axis=-1, keepdims=True)
    rstd = lax.rsqrt(var + NORM_EPS)
    xh = xc * rstd
    return xh * g + b, xh, rstd


def _ln_bwd(dy, xh, rstd, g):
    dxh = dy * g
    dx = rstd * (dxh - jnp.mean(dxh, axis=-1, keepdims=True) - xh * jnp.mean(dxh * xh, axis=-1, keepdims=True))
    return dx, jnp.sum(dy * xh, axis=0, keepdims=True), jnp.sum(dy, axis=0, keepdims=True)


def _rowwise(name, fn, rows, bcast, outs, accs, *, nrows, tr, nj=1):
    ni = nrows // tr
    nr, nb, no = len(rows), len(bcast), len(outs)

    def rmap(colfn, shift):
        if callable(shift):
            return lambda j, i: (shift(i), colfn(j))
        if shift == 0:
            return lambda j, i: (i, colfn(j))
        return lambda j, i: (jnp.clip(i + shift, 0, ni - 1), colfn(j))

    in_specs = [pl.BlockSpec((tr, w), rmap(cf, sh)) for (_, w, cf, sh) in rows]
    in_specs += [pl.BlockSpec(b.shape, lambda j, i, nd=b.ndim: (0,) * nd) for b in bcast]
    out_specs = [pl.BlockSpec((tr, w), rmap(cf, 0)) for (_, w, cf, _) in outs]
    out_specs += [pl.BlockSpec((r, w), lambda j, i, cf=cf: (0, cf(j))) for (r, _, w, cf) in accs]
    out_shape = [jax.ShapeDtypeStruct((nrows, nc), dt) for (nc, _, _, dt) in outs]
    out_shape += [jax.ShapeDtypeStruct((r, nc), F32) for (r, nc, _, _) in accs]

    def body(*refs):
        j = pl.program_id(0)
        i = pl.program_id(1)
        rv = [r[...] for r in refs[:nr]]
        bv = [r[...] for r in refs[nr:nr + nb]]
        o_refs = refs[nr + nb:nr + nb + no]
        a_refs = refs[nr + nb + no:]
        o_vals, a_vals = fn(i, j, rv, bv)
        for ref, val in zip(o_refs, o_vals):
            ref[...] = val.astype(ref.dtype)
        if a_refs:
            @pl.when(i == 0)
            def _():
                for ref in a_refs:
                    ref[...] = jnp.zeros(ref.shape, F32)
            for ref, val in zip(a_refs, a_vals):
                ref[...] += val

    res = pl.pallas_call(
        body, name=name, grid=(nj, ni), in_specs=in_specs, out_specs=out_specs, out_shape=out_shape,
        compiler_params=_cparams(("arbitrary", "arbitrary")),
    )(*[r[0] for r in rows], *bcast)
    return res


def _c(k):
    return lambda j: k


def _mm(name, a, b, dims, *, a_blk, a_map, b_blk, b_map, o_shape, o_blk, o_map, grid, o_dtype,
        bias=None, bias_blk=None, bias_map=None, res=None):
    nk = grid[3]
    extra, extra_specs = [], []
    if bias is not None:
        extra.append(bias)
        extra_specs.append(pl.BlockSpec(bias_blk, bias_map))
    if res is not None:
        extra.append(res)
        extra_specs.append(pl.BlockSpec(o_blk, o_map))
    nbias = bias is not None
    nres = res is not None

    def body(*refs):
        a_ref, b_ref = refs[0], refs[1]
        idx = 2
        bias_ref = refs[idx] if nbias else None
        idx += nbias
        res_ref = refs[idx] if nres else None
        idx += nres
        o_ref = refs[idx]
        acc_ref = refs[idx + 1]
        k = pl.program_id(3)
        part = _dot(a_ref[...], b_ref[...], dims)

        def finish(r):
            if nbias:
                r = r + bias_ref[...]
            if nres:
                r = r + res_ref[...]
            o_ref[...] = r.astype(o_ref.dtype)

        if nk == 1:
            finish(part)
        else:
            @pl.when(k == 0)
            def _():
                acc_ref[...] = part

            @pl.when(k > 0)
            def _():
                acc_ref[...] += part

            @pl.when(k == nk - 1)
            def _():
                finish(acc_ref[...])

    acc_shape = o_blk if nk > 1 else (8, 128)
    return pl.pallas_call(
        body, name=name, grid=grid,
        in_specs=[pl.BlockSpec(a_blk, a_map), pl.BlockSpec(b_blk, b_map)] + extra_specs,
        out_specs=pl.BlockSpec(o_blk, o_map),
        out_shape=jax.ShapeDtypeStruct(o_shape, o_dtype),
        scratch_shapes=[pltpu.VMEM(acc_shape, F32)],
        compiler_params=_cparams(("arbitrary", "arbitrary", "arbitrary", "arbitrary")),
    )(a, b, *extra)


def _mm_nn(name, a, b, o_dtype, *, bias=None, res=None, groups=1, tm=1024, tn=1024):
    M = a.shape[0]
    K = a.shape[1] // groups
    N = b.shape[1]
    tm, tn = min(tm, M), min(tn, N)
    njn = N // tn
    return _mm(name, a, b, NN,
               a_blk=(tm, K), a_map=lambda g, j, i, k: (i, g),
               b_blk=(K, tn), b_map=lambda g, j, i, k: (g, j),
               o_shape=(M, groups * N), o_blk=(tm, tn), o_map=lambda g, j, i, k: (i, g * njn + j),
               grid=(groups, njn, M // tm, 1), o_dtype=o_dtype,
               bias=bias, bias_blk=(1, tn), bias_map=lambda g, j, i, k: (0, g * njn + j), res=res)


def _mm_nt(name, a, b, o_dtype, *, res=None, groups=1, tm=512, tk=1024):
    M = a.shape[0]
    K = a.shape[1] // groups
    N = b.shape[0] // groups
    tm, tk = min(tm, M), min(tk, K)
    nk = K // tk
    return _mm(name, a, b, NT,
               a_blk=(tm, tk), a_map=lambda g, j, i, k: (i, g * nk + k),
               b_blk=(N, tk), b_map=lambda g, j, i, k: (g, k),
               o_shape=(M, groups * N), o_blk=(tm, N), o_map=lambda g, j, i, k: (i, g),
               grid=(groups, 1, M // tm, nk), o_dtype=o_dtype, res=res)


def _mm_tn(name, a, b, *, groups=1, tn=1024, tk=1024):
    K = a.shape[0]
    M = a.shape[1] // groups
    N = b.shape[1] // groups
    tn, tk = min(tn, N), min(tk, K)
    njn = N // tn
    return _mm(name, a, b, TN,
               a_blk=(tk, M), a_map=lambda g, j, i, k: (k, g),
               b_blk=(tk, tn), b_map=lambda g, j, i, k: (k, g * njn + j),
               o_shape=(groups * M, N), o_blk=(M, tn), o_map=lambda g, j, i, k: (g, j),
               grid=(groups, njn, 1, K // tk), o_dtype=F32)


def _rms_fwd(x, g):
    S, D = x.shape

    def fn(i, j, rv, bv):
        xv, gv = rv[0], bv[0]
        r = lax.rsqrt(jnp.mean(xv * xv, axis=-1, keepdims=True) + NORM_EPS)
        return [xv * r * gv], []

    return _rowwise("rms_fwd", fn, [(x, D, _c(0), 0)], [g.reshape(1, D)], [(D, D, _c(0), BF16)], [],
                    nrows=S, tr=512)[0]


def _rms_bwd_math(xv, dyv, gv):
    r = lax.rsqrt(jnp.mean(xv * xv, axis=-1, keepdims=True) + NORM_EPS)
    xh = xv * r
    dg = jnp.sum(dyv * xh, axis=0, keepdims=True)
    dxh = dyv * gv
    dx = r * (dxh - xh * jnp.mean(dxh * xh, axis=-1, keepdims=True))
    return dx, dg


def _rms_bwd(x, dh, dout, g):
    S, D = x.shape

    def fn(i, j, rv, bv):
        dx, dg = _rms_bwd_math(rv[0], rv[1], bv[0])
        return [rv[2] + dx], [dg]

    return _rowwise("rms_bwd", fn, [(x, D, _c(0), 0), (dh, D, _c(0), 0), (dout, D, _c(0), 0)], [g.reshape(1, D)],
                    [(D, D, _c(0), F32)], [(1, D, D, _c(0))], nrows=S, tr=256)


def _final_loss(x, target, g):
    S, D = x.shape

    def fn(i, j, rv, bv):
        xv, tv, gv = rv[0], rv[1], bv[0]
        r = lax.rsqrt(jnp.mean(xv * xv, axis=-1, keepdims=True) + NORM_EPS)
        err = xv * r * gv - tv
        loss = 0.5 * jnp.sum(jnp.mean(err * err, axis=-1, keepdims=True), axis=0, keepdims=True)
        dx, dg = _rms_bwd_math(xv, err * (1.0 / D), gv)
        return [dx], [dg, jnp.broadcast_to(loss, (1, 128))]

    return _rowwise("final_loss", fn, [(x, D, _c(0), 0), (target, D, _c(0), 0)], [g.reshape(1, D)],
                    [(D, D, _c(0), F32)], [(1, D, D, _c(0)), (1, 128, 128, _c(0))], nrows=S, tr=256)


MERGE_W = 512


def _merge_fwd(gpre, yproj):
    S = gpre.shape[0]
    nj = D_MODEL // MERGE_W

    def fn(i, j, rv, bv):
        acc = _sigmoid(rv[0]) * rv[4]
        for n in range(1, 4):
            acc = acc + _sigmoid(rv[n]) * rv[4 + n]
        return [acc], []

    rows = [(gpre, MERGE_W, (lambda j, n=n: n * nj + j), 0) for n in range(4)]
    rows += [(yproj, MERGE_W, (lambda j, n=n: n * nj + j), 0) for n in range(4)]
    return _rowwise("merge_fwd", fn, rows, [], [(D_MODEL, MERGE_W, lambda j: j, BF16)], [],
                    nrows=S, tr=512, nj=nj)[0]


def _merge_bwd(dmerged, gpre, yproj):
    S = gpre.shape[0]
    nj4 = D_MODEL // MERGE_W

    def fn(i, j, rv, bv):
        dm, gp, yp = rv
        sg = _sigmoid(gp)
        dgp = dm * yp * sg * (1.0 - sg)
        return [dm * sg, dgp], [jnp.sum(dgp, axis=0, keepdims=True)]

    rows = [(dmerged, MERGE_W, lambda j: j % nj4, 0), (gpre, MERGE_W, lambda j: j, 0), (yproj, MERGE_W, lambda j: j, 0)]
    return _rowwise("merge_bwd", fn, rows, [],
                    [(GATE_W, MERGE_W, lambda j: j, BF16), (GATE_W, MERGE_W, lambda j: j, BF16)],
                    [(1, GATE_W, MERGE_W, lambda j: j)], nrows=S, tr=512, nj=4 * nj4)


SB_TQ = 1024
SB_TK = 256
LOG2E = math.log2(math.e)


def _split2(v):
    hi = v.astype(BF16)
    lo = (v - hi.astype(F32)).astype(BF16)
    return hi, lo


def _stacked_triangle(n, pred):
    r = lax.broadcasted_iota(jnp.int32, (n, n), 0)
    c = lax.broadcasted_iota(jnp.int32, (n, n), 1)
    t = pred(r, c).astype(BF16)
    return jnp.concatenate([t, t], axis=0)


def _sb_fwd(proj):
    S = proj.shape[0]
    tq, tk = min(SB_TQ, S), SB_TK
    assert S // tk <= HEAD
    nq = S // tq
    nkb = tq // tk
    scale = HEAD ** -0.5

    def body(q_ref, k_ref, v_ref, g_ref, o_ref, y_ref, r_ref, kb_ref, vb_ref):
        i = pl.program_id(1)

        @pl.when(i == 0)
        def _():
            kb_ref[...] = k_ref[...].astype(BF16)
            vb_ref[...] = v_ref[...].astype(BF16)

        qb = q_ref[...].astype(BF16)
        qpos = i * tq + lax.broadcasted_iota(jnp.int32, (tq, tk), 0)
        kidx = lax.broadcasted_iota(jnp.int32, (tq, tk), 1)
        lane = lax.broadcasted_iota(jnp.int32, (tq, HEAD), 1)
        upper2 = _stacked_triangle(tk, lambda r, c: r > c)

        def block(kblock, masked, carry):
            acc, run, runs = carry
            off = pl.multiple_of(kblock * tk, tk)
            zz = _dot(qb, kb_ref[pl.ds(off, tk), :], NT) * (scale * LOG2E)
            sp = jnp.maximum(zz, 0.0) + jnp.log2(1.0 + jnp.exp2(-jnp.abs(zz)))
            if masked:
                keep = (off + kidx) < qpos
                sp_sum = jnp.where(keep, sp, 0.0)
            else:
                sp_sum = sp
            hi, lo = _split2(sp_sum)
            w = jnp.exp2((zz - sp) - (_dot(jnp.concatenate([hi, lo], axis=1), upper2, NN) + run))
            if masked:
                w = jnp.where(keep, w, 0.0)
            acc = acc + _dot(w, vb_ref[pl.ds(off, tk), :], NN)
            runs = jnp.where(lane == kblock, run, runs)
            return acc, run + jnp.sum(sp_sum, axis=1, keepdims=True), runs

        runs0 = jnp.where(qpos[:, :HEAD] + lane < 0, 1.0, 0.0)
        carry = (jnp.zeros((tq, HEAD), F32), jnp.zeros((tq, 1), F32), runs0)
        for kbl in reversed(range(nkb)):
            carry = block(i * nkb + kbl, True, carry)
        acc, _, runs = lax.fori_loop(0, i * nkb, lambda n, c: block(i * nkb - 1 - n, False, c), carry)
        o_ref[...] = acc
        r_ref[...] = runs
        gate, _ = _silu_and_grad(g_ref[...])
        y_ref[...] = (acc * gate).astype(BF16)

    return pl.pallas_call(
        body, name="sb_fwd", grid=(NHEAD, nq),
        in_specs=[pl.BlockSpec((tq, HEAD), lambda h, i: (i, C_AQ * NHEAD + h)),
                  pl.BlockSpec((S, HEAD), lambda h, i: (0, C_AK * NHEAD + h)),
                  pl.BlockSpec((S, HEAD), lambda h, i: (0, C_AV * NHEAD + h)),
                  pl.BlockSpec((tq, HEAD), lambda h, i: (i, C_AG * NHEAD + h))],
        out_specs=[pl.BlockSpec((tq, HEAD), lambda h, i: (i, h))] * 3,
        out_shape=[jax.ShapeDtypeStruct((S, BW), F32), jax.ShapeDtypeStruct((S, BW), BF16),
                   jax.ShapeDtypeStruct((S, BW), F32)],
        scratch_shapes=[pltpu.VMEM((S, HEAD), BF16), pltpu.VMEM((S, HEAD), BF16)],
        compiler_params=_cparams(("arbitrary", "arbitrary")),
    )(proj, proj, proj, proj)


def _sb_bwd(proj, att, runs, dy):
    S = proj.shape[0]
    tq, tk = min(SB_TQ, S), SB_TK
    nq = S // tq
    nkb = tq // tk
    scale = HEAD ** -0.5

    def body(q_ref, k_ref, v_ref, g_ref, o_ref, r_ref, dy_ref, dq_ref, dk_ref, dv_ref, dg_ref, kb_ref, vb_ref):
        i = pl.program_id(1)

        @pl.when(i == 0)
        def _():
            kb_ref[...] = k_ref[...].astype(BF16)
            vb_ref[...] = v_ref[...].astype(BF16)
            dk_ref[...] = jnp.zeros(dk_ref.shape, F32)
            dv_ref[...] = jnp.zeros(dv_ref.shape, F32)

        gate, dgate = _silu_and_grad(g_ref[...])
        dyv = dy_ref[...]
        dg_ref[...] = dyv * o_ref[...] * dgate
        dob = (dyv * gate).astype(BF16)
        qb = q_ref[...].astype(BF16)
        runs = r_ref[...]
        qpos = i * tq + lax.broadcasted_iota(jnp.int32, (tq, tk), 0)
        kidx = lax.broadcasted_iota(jnp.int32, (tq, tk), 1)
        lane = lax.broadcasted_iota(jnp.int32, (tq, HEAD), 1)
        upper2 = _stacked_triangle(tk, lambda r, c: r > c)
        lower2 = _stacked_triangle(tk, lambda r, c: r < c)

        def block(kblock, masked, carry):
            dq, grun = carry
            off = pl.multiple_of(kblock * tk, tk)
            kblk = kb_ref[pl.ds(off, tk), :]
            zz = _dot(qb, kblk, NT) * (scale * LOG2E)
            sp = jnp.maximum(zz, 0.0) + jnp.log2(1.0 + jnp.exp2(-jnp.abs(zz)))
            lb = zz - sp
            if masked:
                keep = (off + kidx) < qpos
                sp_sum = jnp.where(keep, sp, 0.0)
            else:
                sp_sum = sp
            hi, lo = _split2(sp_sum)
            run = jnp.sum(jnp.where(lane == kblock, runs, 0.0), axis=1, keepdims=True)
            w = jnp.exp2(lb - (_dot(jnp.concatenate([hi, lo], axis=1), upper2, NN) + run))
            if masked:
                w = jnp.where(keep, w, 0.0)
            gw = _dot(dob, vb_ref[pl.ds(off, tk), :], NT) * w
            ghi, glo = _split2(gw)
            gsum = grun + _dot(jnp.concatenate([ghi, glo], axis=1), lower2, NN)
            dz = (gw - (gw + gsum) * jnp.exp2(lb)) * scale
            if masked:
                dz = jnp.where(keep, dz, 0.0)
            dzb = dz.astype(BF16)
            dk_ref[pl.ds(off, tk), :] += _dot(dzb, qb, TN)
            dv_ref[pl.ds(off, tk), :] += _dot(w, dob, TN)
            return dq + _dot(dzb, kblk, NN), grun + jnp.sum(gw, axis=1, keepdims=True)

        carry = lax.fori_loop(0, i * nkb, lambda n, c: block(n, False, c),
                              (jnp.zeros((tq, HEAD), F32), jnp.zeros((tq, 1), F32)))
        for kbl in range(nkb):
            carry = block(i * nkb + kbl, True, carry)
        dq_ref[...] = carry[0]

    blk = lambda h, i: (i, h)
    head = lambda h, i: (0, h)
    return pl.pallas_call(
        body, name="sb_bwd", grid=(NHEAD, nq),
        in_specs=[pl.BlockSpec((tq, HEAD), lambda h, i: (i, C_AQ * NHEAD + h)),
                  pl.BlockSpec((S, HEAD), lambda h, i: (0, C_AK * NHEAD + h)),
                  pl.BlockSpec((S, HEAD), lambda h, i: (0, C_AV * NHEAD + h)),
                  pl.BlockSpec((tq, HEAD), lambda h, i: (i, C_AG * NHEAD + h)),
                  pl.BlockSpec((tq, HEAD), blk), pl.BlockSpec((tq, HEAD), blk), pl.BlockSpec((tq, HEAD), blk)],
        out_specs=[pl.BlockSpec((tq, HEAD), blk), pl.BlockSpec((S, HEAD), head), pl.BlockSpec((S, HEAD), head),
                   pl.BlockSpec((tq, HEAD), blk)],
        out_shape=[jax.ShapeDtypeStruct((S, BW), F32)] * 4,
        scratch_shapes=[pltpu.VMEM((S, HEAD), BF16), pltpu.VMEM((S, HEAD), BF16)],
        compiler_params=_cparams(("arbitrary", "arbitrary")),
    )(proj, proj, proj, proj, att, runs, dy)


CONV_TR = 256


def _conv_taps(buf_ref, w_ref, base, tr):
    acc = w_ref[0:1, :] * buf_ref[pl.ds(base, tr), :]
    for k in range(1, CONV_K):
        acc = acc + w_ref[k:k + 1, :] * buf_ref[pl.ds(base + k, tr), :]
    return acc


def _conv_fwd(proj, conv_w, conv_b, ln_g, ln_b):
    S = proj.shape[0]
    tr = min(CONV_TR, S)
    ni = S // tr

    def body(a_ref, b_ref, ap_ref, bp_ref, g_ref, w_ref, cb_ref, lg_ref, lb_ref, c_ref, y_ref, buf_ref):
        i = pl.program_id(0)
        prev = ap_ref[tr - HALO:, :] * _sigmoid(bp_ref[tr - HALO:, :])
        buf_ref[0:HALO, :] = jnp.where(i > 0, prev, 0.0)
        buf_ref[HALO:, :] = a_ref[...] * _sigmoid(b_ref[...])
        c = _conv_taps(buf_ref, w_ref, HALO - (CONV_K - 1), tr) + cb_ref[...]
        c_ref[...] = c
        yn, _, _ = _ln_fwd(c, lg_ref[...], lb_ref[...])
        act, _ = _silu_and_grad(yn)
        gate, _ = _silu_and_grad(g_ref[...])
        y_ref[...] = (act * gate).astype(BF16)

    cur = lambda k: (lambda i: (i, k))
    prv = lambda k: (lambda i: (jnp.maximum(i - 1, 0), k))
    full = lambda a: pl.BlockSpec(a.shape, lambda i: (0, 0))
    wpad = jnp.pad(conv_w, ((0, HALO - CONV_K), (0, 0)))
    small = [wpad, conv_b.reshape(1, BW), ln_g.reshape(1, BW), ln_b.reshape(1, BW)]
    return pl.pallas_call(
        body, name="conv_fwd", grid=(ni,),
        in_specs=[pl.BlockSpec((tr, BW), cur(C_BA)), pl.BlockSpec((tr, BW), cur(C_BB)),
                  pl.BlockSpec((tr, BW), prv(C_BA)), pl.BlockSpec((tr, BW), prv(C_BB)),
                  pl.BlockSpec((tr, BW), cur(C_BG))] + [full(a) for a in small],
        out_specs=[pl.BlockSpec((tr, BW), lambda i: (i, 0)), pl.BlockSpec((tr, BW), lambda i: (i, 0))],
        out_shape=[jax.ShapeDtypeStruct((S, BW), F32), jax.ShapeDtypeStruct((S, BW), BF16)],
        scratch_shapes=[pltpu.VMEM((tr + HALO, BW), F32)],
        compiler_params=_cparams(("arbitrary",)),
    )(proj, proj, proj, proj, proj, *small)


def _conv_bwd_norm(proj, cpre, dy, ln_g, ln_b):
    S = proj.shape[0]

    def fn(i, j, rv, bv):
        c, bg, dyv = rv
        lg, lb = bv
        yn, xh, rstd = _ln_fwd(c, lg, lb)
        act, dact = _silu_and_grad(yn)
        gate, dgate = _silu_and_grad(bg)
        dyn = dyv * gate * dact
        dc, dlg, dlb = _ln_bwd(dyn, xh, rstd, lg)
        return [dc, dyv * act * dgate], [dlg, dlb, jnp.sum(dc, axis=0, keepdims=True)]

    rows = [(cpre, BW, _c(0), 0), (proj, BW, _c(C_BG), 0), (dy, BW, _c(1), 0)]
    return _rowwise("conv_bwd_norm", fn, rows, [ln_g.reshape(1, BW), ln_b.reshape(1, BW)],
                    [(BW, BW, _c(0), F32), (BW, BW, _c(0), BF16)],
                    [(1, BW, BW, _c(0))] * 3, nrows=S, tr=256)


def _conv_bwd_taps(proj, dc, conv_w):
    S = proj.shape[0]
    tr = min(CONV_TR, S)
    ni = S // tr

    def body(a_ref, b_ref, ap_ref, bp_ref, dc_ref, dcn_ref, w_ref, da_ref, db_ref, dw_ref, gbuf, dbuf, wrev):
        i = pl.program_id(0)

        @pl.when(i == 0)
        def _():
            dw_ref[...] = jnp.zeros(dw_ref.shape, F32)
            for k in range(CONV_K):
                wrev[k:k + 1, :] = w_ref[CONV_K - 1 - k:CONV_K - k, :]

        sb = _sigmoid(b_ref[...])
        av = a_ref[...]
        prev = ap_ref[tr - HALO:, :] * _sigmoid(bp_ref[tr - HALO:, :])
        gbuf[0:HALO, :] = jnp.where(i > 0, prev, 0.0)
        gbuf[HALO:, :] = av * sb
        dcv = dc_ref[...]
        dbuf[0:tr, :] = dcv
        dbuf[tr:, :] = jnp.where(i < ni - 1, dcn_ref[0:HALO, :], 0.0)
        dg = _conv_taps(dbuf, wrev, 0, tr)
        da_ref[...] = (dg * sb).astype(BF16)
        db_ref[...] = (dg * av * sb * (1.0 - sb)).astype(BF16)
        base = HALO - (CONV_K - 1)
        for k in range(CONV_K):
            dw_ref[k:k + 1, :] += jnp.sum(dcv * gbuf[pl.ds(base + k, tr), :], axis=0, keepdims=True)

    cur = lambda k: (lambda i: (i, k))
    prv = lambda k: (lambda i: (jnp.maximum(i - 1, 0), k))
    wpad = jnp.pad(conv_w, ((0, HALO - CONV_K), (0, 0)))
    return pl.pallas_call(
        body, name="conv_bwd_taps", grid=(ni,),
        in_specs=[pl.BlockSpec((tr, BW), cur(C_BA)), pl.BlockSpec((tr, BW), cur(C_BB)),
                  pl.BlockSpec((tr, BW), prv(C_BA)), pl.BlockSpec((tr, BW), prv(C_BB)),
                  pl.BlockSpec((tr, BW), lambda i: (i, 0)),
                  pl.BlockSpec((tr, BW), lambda i: (jnp.minimum(i + 1, ni - 1), 0)),
                  pl.BlockSpec((HALO, BW), lambda i: (0, 0))],
        out_specs=[pl.BlockSpec((tr, BW), lambda i: (i, 0)), pl.BlockSpec((tr, BW), lambda i: (i, 0)),
                   pl.BlockSpec((HALO, BW), lambda i: (0, 0))],
        out_shape=[jax.ShapeDtypeStruct((S, BW), BF16), jax.ShapeDtypeStruct((S, BW), BF16),
                   jax.ShapeDtypeStruct((HALO, BW), F32)],
        scratch_shapes=[pltpu.VMEM((tr + HALO, BW), F32), pltpu.VMEM((tr + HALO, BW), F32),
                        pltpu.VMEM((HALO, BW), F32)],
        compiler_params=_cparams(("arbitrary",)),
    )(proj, proj, proj, proj, dc, dc, wpad)


SGU_TR = 256


def _sgu_mix(wm, vn, bias):
    tr = vn.shape[0]
    rows = []
    for n in range(tr // HEAD):
        cols = []
        for g in range(NHEAD):
            blk = vn[n * HEAD:(n + 1) * HEAD, g * HEAD:(g + 1) * HEAD]
            cols.append(_dot(wm[g], blk, NN) + bias[g * HEAD:(g + 1) * HEAD, :])
        rows.append(jnp.concatenate(cols, axis=1))
    return jnp.concatenate(rows, axis=0)


def _sgu_masked_w(w2d):
    rr = lax.broadcasted_iota(jnp.int32, (HEAD, HEAD), 0)
    cc = lax.broadcasted_iota(jnp.int32, (HEAD, HEAD), 1)
    tril = rr >= cc
    return [jnp.where(tril, w2d[g * HEAD:(g + 1) * HEAD, :], 0.0).astype(BF16) for g in range(NHEAD)], tril


def _sgu_inputs(sgu_w, sgu_b, ln_g, ln_b):
    w2d = sgu_w.reshape(NHEAD * HEAD, HEAD)
    bias = jnp.broadcast_to(sgu_b[:, :, None], (NHEAD, HEAD, HEAD)).reshape(NHEAD * HEAD, HEAD)
    return [w2d, bias, ln_g.reshape(1, BW), ln_b.reshape(1, BW)]


def _sgu_fwd(proj, sgu_w, sgu_b, ln_g, ln_b):
    S = proj.shape[0]

    def fn(i, j, rv, bv):
        cu, cv, cg = rv
        w2d, bias, lg, lb = bv
        wm, _ = _sgu_masked_w(w2d)
        u, _ = _gelu_and_grad(cu)
        v, _ = _gelu_and_grad(cv)
        vn, _, _ = _ln_fwd(v, lg, lb)
        z = _sgu_mix(wm, vn, bias)
        gate, _ = _silu_and_grad(cg)
        return [u * z * gate], []

    rows = [(proj, BW, _c(C_CU), 0), (proj, BW, _c(C_CV), 0), (proj, BW, _c(C_CG), 0)]
    return _rowwise("sgu_fwd", fn, rows, _sgu_inputs(sgu_w, sgu_b, ln_g, ln_b), [(BW, BW, _c(0), BF16)], [],
                    nrows=S, tr=min(SGU_TR, S))[0]


def _sgu_bwd(proj, dy, sgu_w, sgu_b, ln_g, ln_b):
    S = proj.shape[0]

    def fn(i, j, rv, bv):
        cu, cv, cg, dyv = rv
        w2d, bias, lg, lb = bv
        wm, tril = _sgu_masked_w(w2d)
        u, du_dcu = _gelu_and_grad(cu)
        v, dv_dcv = _gelu_and_grad(cv)
        vn, xh, rstd = _ln_fwd(v, lg, lb)
        z = _sgu_mix(wm, vn, bias)
        gate, dgate = _silu_and_grad(cg)
        dz = dyv * u * gate
        dcu = dyv * z * gate * du_dcu
        dcg = dyv * u * z * dgate
        tr = dz.shape[0]
        dvn_rows = []
        dw = [jnp.zeros((HEAD, HEAD), F32) for _ in range(NHEAD)]
        dbias = [jnp.zeros((HEAD, HEAD), F32) for _ in range(NHEAD)]
        for n in range(tr // HEAD):
            cols = []
            for g in range(NHEAD):
                dzb = dz[n * HEAD:(n + 1) * HEAD, g * HEAD:(g + 1) * HEAD]
                vnb = vn[n * HEAD:(n + 1) * HEAD, g * HEAD:(g + 1) * HEAD]
                cols.append(_dot(wm[g], dzb, TN))
                dw[g] = dw[g] + _dot(dzb, vnb, NT)
                dbias[g] = dbias[g] + dzb
            dvn_rows.append(jnp.concatenate(cols, axis=1))
        dvn = jnp.concatenate(dvn_rows, axis=0)
        dv, dlg, dlb = _ln_bwd(dvn, xh, rstd, lg)
        dw2d = jnp.concatenate([jnp.where(tril, d, 0.0) for d in dw], axis=0)
        return [dcu, dv * dv_dcv, dcg], [dw2d, jnp.concatenate(dbias, axis=0), dlg, dlb]

    rows = [(proj, BW, _c(C_CU), 0), (proj, BW, _c(C_CV), 0), (proj, BW, _c(C_CG), 0), (dy, BW, _c(2), 0)]
    return _rowwise("sgu_bwd", fn, rows, _sgu_inputs(sgu_w, sgu_b, ln_g, ln_b),
                    [(BW, BW, _c(0), BF16)] * 3,
                    [(NHEAD * HEAD, HEAD, HEAD, _c(0)), (NHEAD * HEAD, HEAD, HEAD, _c(0)), (1, BW, BW, _c(0)),
                     (1, BW, BW, _c(0))], nrows=S, tr=min(SGU_TR, S))


NEG = -1e30
NCOL = IN_WIDTH // BW


def _band_masks():
    a = lax.broadcasted_iota(jnp.int32, (HEAD, HEAD), 0)
    c = lax.broadcasted_iota(jnp.int32, (HEAD, HEAD), 1)
    return a >= c, a <= c


def _dil_view(proj, grp):
    S = proj.shape[0]
    dil = DIL_PATTERNS[grp][1]
    if dil == 1:
        return proj, NCOL, C_DQ + grp, C_DK + grp, C_DV
    cols = [proj[:, c * BW:(c + 1) * BW] for c in (C_DQ + grp, C_DK + grp, C_DV)]
    return jnp.concatenate(cols, axis=1).reshape(S // dil, dil * 3 * BW), 3, 0, 1, 2


def _dil_fwd(proj, grp):
    S = proj.shape[0]
    dil = DIL_PATTERNS[grp][1]
    L = S // dil
    nb = L // HEAD
    proj2, ncol, cq, ck, cv = _dil_view(proj, grp)
    scale = HEAD ** -0.5

    def body(q_ref, kc_ref, kp_ref, vc_ref, vp_ref, o_ref, l_ref):
        b = pl.program_id(1)
        m_cur, m_prev = _band_masks()
        m_prev = m_prev & (b > 0)
        for h in range(NHEAD):
            sl = slice(h * HEAD, (h + 1) * HEAD)
            q = q_ref[:, sl].astype(BF16)
            s_c = jnp.where(m_cur, _dot(q, kc_ref[:, sl], NT) * scale, NEG)
            s_p = jnp.where(m_prev, _dot(q, kp_ref[:, sl], NT) * scale, NEG)
            m = jnp.maximum(jnp.max(s_c, axis=1, keepdims=True), jnp.max(s_p, axis=1, keepdims=True))
            p_c = jnp.exp(s_c - m)
            p_p = jnp.exp(s_p - m)
            den = jnp.sum(p_c, axis=1, keepdims=True) + jnp.sum(p_p, axis=1, keepdims=True)
            o = (_dot(p_c, vc_ref[:, sl], NN) + _dot(p_p, vp_ref[:, sl], NN)) / den
            o_ref[:, sl] = o
            l_ref[:, sl] = jnp.broadcast_to(m + jnp.log(den), (HEAD, HEAD))

    cur = lambda col: (lambda r, b: (b, r * ncol + col))
    prv = lambda col: (lambda r, b: (jnp.maximum(b - 1, 0), r * ncol + col))
    out_map = lambda r, b: (b, r)
    o, lse = pl.pallas_call(
        body, name=f"dil_fwd{grp}", grid=(dil, nb),
        in_specs=[pl.BlockSpec((HEAD, BW), cur(cq)),
                  pl.BlockSpec((HEAD, BW), cur(ck)), pl.BlockSpec((HEAD, BW), prv(ck)),
                  pl.BlockSpec((HEAD, BW), cur(cv)), pl.BlockSpec((HEAD, BW), prv(cv))],
        out_specs=[pl.BlockSpec((HEAD, BW), out_map), pl.BlockSpec((HEAD, BW), out_map)],
        out_shape=[jax.ShapeDtypeStruct((L, dil * BW), F32), jax.ShapeDtypeStruct((L, dil * BW), F32)],
        compiler_params=_cparams(("arbitrary", "arbitrary")),
    )(proj2, proj2, proj2, proj2, proj2)
    return o.reshape(S, BW), lse.reshape(S, BW)


def _dil_combine(proj, os_, lses):
    S = proj.shape[0]

    def fn(i, j, rv, bv):
        o1, o2, o3, l1, l2, l3, dg = rv
        m = jnp.maximum(jnp.maximum(l1, l2), l3)
        e1, e2, e3 = jnp.exp(l1 - m), jnp.exp(l2 - m), jnp.exp(l3 - m)
        den = e1 + e2 + e3
        out = (e1 * o1 + e2 * o2 + e3 * o3) / den
        gate, _ = _silu_and_grad(dg)
        return [out, m + jnp.log(den), out * gate], []

    rows = [(a, BW, _c(0), 0) for a in (*os_, *lses)] + [(proj, BW, _c(C_DG), 0)]
    return _rowwise("dil_combine", fn, rows, [], [(BW, BW, _c(0), F32), (BW, BW, _c(0), F32), (BW, BW, _c(0), BF16)],
                    [], nrows=S, tr=256)


def _dil_gate_bwd(proj, att, dy):
    S = proj.shape[0]

    def fn(i, j, rv, bv):
        dg, av, dyv = rv
        gate, dgate = _silu_and_grad(dg)
        dout = dyv * gate
        prod = dout * av
        tr = prod.shape[0]
        delta = jnp.concatenate(
            [jnp.broadcast_to(jnp.sum(prod[:, h * HEAD:(h + 1) * HEAD], axis=1, keepdims=True), (tr, HEAD))
             for h in range(NHEAD)], axis=1)
        return [dout, dyv * av * dgate, delta], []

    rows = [(proj, BW, _c(C_DG), 0), (att, BW, _c(0), 0), (dy, BW, _c(3), 0)]
    return _rowwise("dil_gate_bwd", fn, rows, [], [(BW, BW, _c(0), F32), (BW, BW, _c(0), BF16), (BW, BW, _c(0), F32)],
                    [], nrows=S, tr=256)


def _dil_bwd(proj, dout, lse, delta, grp):
    S = proj.shape[0]
    dil = DIL_PATTERNS[grp][1]
    L = S // dil
    nb = L // HEAD
    proj2, ncol, cq, ck, cv = _dil_view(proj, grp)
    view = lambda a: a.reshape(L, dil * BW)
    scale = HEAD ** -0.5

    def body(q_ref, qn_ref, kc_ref, kp_ref, vc_ref, vp_ref, do_ref, don_ref, l_ref, ln_ref, d_ref, dn_ref,
             dq_ref, dk_ref, dv_ref):
        b = pl.program_id(1)
        m_cur, m_band = _band_masks()
        m_prev = m_band & (b > 0)
        m_next = m_band & (b < nb - 1)
        for h in range(NHEAD):
            sl = slice(h * HEAD, (h + 1) * HEAD)
            q, qn = q_ref[:, sl].astype(BF16), qn_ref[:, sl].astype(BF16)
            kc, kp = kc_ref[:, sl].astype(BF16), kp_ref[:, sl].astype(BF16)
            vc, vp = vc_ref[:, sl].astype(BF16), vp_ref[:, sl].astype(BF16)
            do, don = do_ref[:, sl].astype(BF16), don_ref[:, sl].astype(BF16)
            lse_c, lse_n = l_ref[:, sl], ln_ref[:, sl]
            dl_c, dl_n = d_ref[:, sl], dn_ref[:, sl]
            p_cc = jnp.exp(jnp.where(m_cur, _dot(q, kc, NT) * scale - lse_c, NEG))
            p_cp = jnp.exp(jnp.where(m_prev, _dot(q, kp, NT) * scale - lse_c, NEG))
            p_nc = jnp.exp(jnp.where(m_next, _dot(qn, kc, NT) * scale - lse_n, NEG))
            ds_cc = p_cc * (_dot(do, vc, NT) - dl_c)
            ds_cp = p_cp * (_dot(do, vp, NT) - dl_c)
            ds_nc = p_nc * (_dot(don, vc, NT) - dl_n)
            dq_ref[:, sl] = ((_dot(ds_cc, kc, NN) + _dot(ds_cp, kp, NN)) * scale).astype(BF16)
            dk_ref[:, sl] = ((_dot(ds_cc, q, TN) + _dot(ds_nc, qn, TN)) * scale).astype(BF16)
            dv_ref[:, sl] = _dot(p_cc, do, TN) + _dot(p_nc, don, TN)

    cur = lambda col: (lambda r, b: (b, r * ncol + col))
    prv = lambda col: (lambda r, b: (jnp.maximum(b - 1, 0), r * ncol + col))
    nxt = lambda col: (lambda r, b: (jnp.minimum(b + 1, nb - 1), r * ncol + col))
    o_cur = lambda r, b: (b, r)
    o_nxt = lambda r, b: (jnp.minimum(b + 1, nb - 1), r)
    blk = (HEAD, BW)
    dq, dk, dv = pl.pallas_call(
        body, name=f"dil_bwd{grp}", grid=(dil, nb),
        in_specs=[pl.BlockSpec(blk, cur(cq)), pl.BlockSpec(blk, nxt(cq)),
                  pl.BlockSpec(blk, cur(ck)), pl.BlockSpec(blk, prv(ck)),
                  pl.BlockSpec(blk, cur(cv)), pl.BlockSpec(blk, prv(cv)),
                  pl.BlockSpec(blk, o_cur), pl.BlockSpec(blk, o_nxt),
                  pl.BlockSpec(blk, o_cur), pl.BlockSpec(blk, o_nxt),
                  pl.BlockSpec(blk, o_cur), pl.BlockSpec(blk, o_nxt)],
        out_specs=[pl.BlockSpec(blk, o_cur)] * 3,
        out_shape=[jax.ShapeDtypeStruct((L, dil * BW), BF16), jax.ShapeDtypeStruct((L, dil * BW), BF16),
                   jax.ShapeDtypeStruct((L, dil * BW), F32)],
        compiler_params=_cparams(("arbitrary", "arbitrary")),
    )(proj2, proj2, proj2, proj2, proj2, proj2, view(dout), view(dout), view(lse), view(lse), view(delta), view(delta))
    return dq.reshape(S, BW), dk.reshape(S, BW), dv.reshape(S, BW)


def _add3(a, b, c):
    S, W = a.shape

    def fn(i, j, rv, bv):
        return [rv[0] + rv[1] + rv[2]], []

    return _rowwise("add3", fn, [(a, W, _c(0), 0), (b, W, _c(0), 0), (c, W, _c(0), 0)], [], [(W, W, _c(0), BF16)], [],
                    nrows=S, tr=512)[0]


def _layer_fwd(x, p):
    h = _rms_fwd(x, p["norm_g"])
    proj = _mm_nn("in_proj", h, p["w_in"], F32)
    gpre = _mm_nn("gate_proj", h, p["w_gate"], F32, bias=p["b_gate"].reshape(1, GATE_W))
    att_a, ya, runs_a = _sb_fwd(proj)
    cpre, yb = _conv_fwd(proj, p["conv_w"], p["conv_b"], p["conv_ln_g"], p["conv_ln_b"])
    yc = _sgu_fwd(proj, p["sgu_w"], p["sgu_b"], p["sgu_ln_g"], p["sgu_ln_b"])
    dil = [_dil_fwd(proj, g) for g in range(3)]
    att_d, lse_d, yd = _dil_combine(proj, [d[0] for d in dil], [d[1] for d in dil])
    y = jnp.concatenate([ya, yb, yc, yd], axis=1)
    yproj = _mm_nn("branch_proj", y, p["w_branch"], F32, groups=4)
    merged = _merge_fwd(gpre, yproj)
    x_next = _mm_nn("out_proj", merged, p["w_out"], F32, res=x)
    saved = dict(x=x, h=h, proj=proj, gpre=gpre, att_a=att_a, runs_a=runs_a, cpre=cpre, att_d=att_d, lse_d=lse_d, y=y,
                 yproj=yproj, merged=merged)
    return x_next, saved


def _layer_bwd(dout, s, p):
    proj = s["proj"]
    dmerged = _mm_nt("out_proj_dx", dout, p["w_out"], F32)
    g_w_out = _mm_tn("out_proj_dw", s["merged"], dout)
    dyproj, dgpre, g_b_gate = _merge_bwd(dmerged, s["gpre"], s["yproj"])
    dy = _mm_nt("branch_proj_dx", dyproj, p["w_branch"], F32, groups=4)
    g_w_branch = _mm_tn("branch_proj_dw", s["y"], dyproj, groups=4)
    d_aq, d_ak, d_av, d_ag = _sb_bwd(proj, s["att_a"], s["runs_a"], dy)
    dc, d_bg, g_cln_g, g_cln_b, g_conv_b = _conv_bwd_norm(proj, s["cpre"], dy, p["conv_ln_g"], p["conv_ln_b"])
    d_ba, d_bb, g_conv_w = _conv_bwd_taps(proj, dc, p["conv_w"])
    d_cu, d_cv, d_cg, g_sgu_w, g_sgu_bias, g_sln_g, g_sln_b = _sgu_bwd(
        proj, dy, p["sgu_w"], p["sgu_b"], p["sgu_ln_g"], p["sgu_ln_b"])
    dout_d, d_dg, delta = _dil_gate_bwd(proj, s["att_d"], dy)
    dil = [_dil_bwd(proj, dout_d, s["lse_d"], delta, g) for g in range(3)]
    d_dv = _add3(dil[0][2], dil[1][2], dil[2][2])
    pieces = [d_aq, d_ak, d_av, d_ag, d_ba, d_bb, d_bg, d_cu, d_cv, d_cg,
              dil[0][0], dil[1][0], dil[2][0], dil[0][1], dil[1][1], dil[2][1], d_dv, d_dg]
    dproj = jnp.concatenate([t.astype(BF16) for t in pieces], axis=1)
    dh_gate = _mm_nt("gate_proj_dx", dgpre, p["w_gate"], F32)
    dh = _mm_nt("in_proj_dx", dproj, p["w_in"], F32, res=dh_gate)
    g_w_in = _mm_tn("in_proj_dw", s["h"], dproj)
    g_w_gate = _mm_tn("gate_proj_dw", s["h"], dgpre)
    dx, g_norm_g = _rms_bwd(s["x"], dh, dout, p["norm_g"])
    grads = dict(
        norm_g=g_norm_g.reshape(D_MODEL), w_in=g_w_in, conv_w=g_conv_w[:CONV_K], conv_b=g_conv_b.reshape(BW),
        conv_ln_g=g_cln_g.reshape(BW), conv_ln_b=g_cln_b.reshape(BW), sgu_ln_g=g_sln_g.reshape(BW),
        sgu_ln_b=g_sln_b.reshape(BW), sgu_w=g_sgu_w.reshape(NHEAD, HEAD, HEAD),
        sgu_b=jnp.sum(g_sgu_bias.reshape(NHEAD, HEAD, HEAD), axis=-1), w_branch=g_w_branch.reshape(4, BW, D_MODEL),
        w_gate=g_w_gate, b_gate=g_b_gate.reshape(GATE_W), w_out=g_w_out)
    return dx, grads


ANY = pl.BlockSpec(memory_space=pl.ANY)
CHIP_FLIPS = ((1, 0), (0, 1), (1, 1))


def _at(ref, axis, start, size):
    idx = [slice(None)] * len(ref.shape)
    idx[axis] = pl.ds(start, size)
    return ref.at[tuple(idx)]


def _position():
    return lax.axis_index("x"), lax.axis_index("y"), lax.axis_index("c")


def _gather(name, w, axis):
    n = w.shape[axis]
    full = list(w.shape)
    full[axis] = 4 * n

    def body(w_ref, out_ref, send_sems, recv_sems, local_sem):
        x, y, c = _position()
        mine = pltpu.make_async_copy(w_ref, _at(out_ref, axis, (2 * x + y) * n, n), local_sem)
        mine.start()
        sends = []
        for k, (fx, fy) in enumerate(CHIP_FLIPS):
            cp = pltpu.make_async_remote_copy(
                src_ref=w_ref, dst_ref=_at(out_ref, axis, (2 * x + y) * n, n), send_sem=send_sems.at[k],
                recv_sem=recv_sems.at[k], device_id=(x ^ fx, y ^ fy, c), device_id_type=MESH)
            cp.start()
            sends.append(cp)
        for k, (fx, fy) in enumerate(CHIP_FLIPS):
            px, py = x ^ fx, y ^ fy
            pltpu.make_async_remote_copy(
                src_ref=w_ref, dst_ref=_at(out_ref, axis, (2 * px + py) * n, n), send_sem=send_sems.at[k],
                recv_sem=recv_sems.at[k], device_id=(px, py, c), device_id_type=MESH).wait_recv()
        for cp in sends:
            cp.wait_send()
        mine.wait()

    return pl.pallas_call(
        body, name=name, in_specs=[ANY], out_specs=ANY, out_shape=jax.ShapeDtypeStruct(tuple(full), w.dtype),
        scratch_shapes=[pltpu.SemaphoreType.DMA((3,)), pltpu.SemaphoreType.DMA((3,)), pltpu.SemaphoreType.DMA],
    )(w)


def _exchange_halves(name, g, hx):
    L = g.shape[0]
    nh = g.shape[hx] // 2
    out = list(g.shape)
    out[hx] = nh

    def body(g_ref, out_ref, send_sems, recv_sems):
        x, y, c = _position()
        copies = []
        for l in range(L):
            cp = pltpu.make_async_remote_copy(
                src_ref=_at(g_ref.at[l], hx - 1, (1 - c) * nh, nh), dst_ref=out_ref.at[l], send_sem=send_sems.at[l],
                recv_sem=recv_sems.at[l], device_id=(x, y, 1 - c), device_id_type=MESH)
            cp.start()
            copies.append(cp)
        for cp in copies:
            cp.wait()

    return pl.pallas_call(
        body, name=name, in_specs=[ANY], out_specs=ANY, out_shape=jax.ShapeDtypeStruct(tuple(out), g.dtype),
        scratch_shapes=[pltpu.SemaphoreType.DMA((L,)), pltpu.SemaphoreType.DMA((L,))],
    )(g)


def _scatter_shards(name, s1, ax):
    n = s1.shape[ax] // 4
    piece = list(s1.shape)
    piece[ax] = n

    def body(s_ref, out_ref, send_sems, recv_sems):
        x, y, c = _position()
        copies = []
        for k, (fx, fy) in enumerate(CHIP_FLIPS):
            px, py = x ^ fx, y ^ fy
            cp = pltpu.make_async_remote_copy(
                src_ref=_at(s_ref, ax, (2 * px + py) * n, n), dst_ref=out_ref.at[k], send_sem=send_sems.at[k],
                recv_sem=recv_sems.at[k], device_id=(px, py, c), device_id_type=MESH)
            cp.start()
            copies.append(cp)
        for cp in copies:
            cp.wait()

    return pl.pallas_call(
        body, name=name, in_specs=[ANY], out_specs=ANY, out_shape=jax.ShapeDtypeStruct((3, *piece), s1.dtype),
        scratch_shapes=[pltpu.SemaphoreType.DMA((3,)), pltpu.SemaphoreType.DMA((3,))],
    )(s1)


def _join_halves(name, s2, hx):
    nh = s2.shape[hx]
    out = list(s2.shape)
    out[hx] = 2 * nh

    def body(s_ref, out_ref, send_sem, recv_sem, local_sem):
        x, y, c = _position()
        mine = pltpu.make_async_copy(s_ref, _at(out_ref, hx, c * nh, nh), local_sem)
        mine.start()
        cp = pltpu.make_async_remote_copy(
            src_ref=s_ref, dst_ref=_at(out_ref, hx, c * nh, nh), send_sem=send_sem, recv_sem=recv_sem,
            device_id=(x, y, 1 - c), device_id_type=MESH)
        cp.start()
        pltpu.make_async_remote_copy(
            src_ref=s_ref, dst_ref=_at(out_ref, hx, (1 - c) * nh, nh), send_sem=send_sem, recv_sem=recv_sem,
            device_id=(x, y, 1 - c), device_id_type=MESH).wait_recv()
        cp.wait_send()
        mine.wait()

    return pl.pallas_call(
        body, name=name, in_specs=[ANY], out_specs=ANY, out_shape=jax.ShapeDtypeStruct(tuple(out), s2.dtype),
        scratch_shapes=[pltpu.SemaphoreType.DMA, pltpu.SemaphoreType.DMA, pltpu.SemaphoreType.DMA],
    )(s2)


def _sum_terms(name, terms, out_rows, out_cols, blk, grid, scalars):
    tr, tc = blk

    def body(s_ref, *refs):
        acc = refs[0][...]
        for r in refs[1:-1]:
            acc = acc + r[...]
        refs[-1][...] = acc

    in_specs = [pl.BlockSpec((None, tr, tc) if a.ndim == 3 else (tr, tc), m) for a, m in terms]
    nrb = grid[1]
    return pl.pallas_call(
        body, name=name,
        grid_spec=pltpu.PrefetchScalarGridSpec(
            num_scalar_prefetch=1, grid=grid, in_specs=in_specs,
            out_specs=pl.BlockSpec((tr, tc), lambda l, i, s: (l * nrb + i, 0))),
        out_shape=jax.ShapeDtypeStruct((out_rows, out_cols), F32),
        compiler_params=_cparams(("arbitrary", "arbitrary")),
    )(scalars, *[a for a, _ in terms])


def _rows_for(cols, rows, budget=1 << 19):
    tr = 8
    while tr * 2 * cols <= budget and tr * 2 <= min(rows, 256) and rows % (tr * 2) == 0:
        tr *= 2
    assert rows % tr == 0
    return tr


def _reduce_scatter(tag, g, ax):
    hx = 3 - ax
    L, R, C = g.shape
    x, y, c = _position()
    scalars = jnp.stack([c, 2 * x + y]).astype(jnp.int32)
    got = _exchange_halves(f"rs_halves_{tag}", g, hx)
    Rh, Ch = got.shape[1], got.shape[2]
    tr1 = _rows_for(Ch, Rh)
    nb1 = Rh // tr1
    if hx == 1:
        g_map = lambda l, i, s: ((2 * l + s[0]) * nb1 + i, 0)
    else:
        g_map = lambda l, i, s: (l * nb1 + i, s[0])
    s1 = _sum_terms(f"rs_sum2_{tag}",
                    [(g.reshape(L * R, C), g_map), (got.reshape(L * Rh, Ch), lambda l, i, s: (l * nb1 + i, 0))],
                    L * Rh, Ch, (tr1, Ch), (L, nb1), scalars).reshape(L, Rh, Ch)
    parts = _scatter_shards(f"rs_shards_{tag}", s1, ax)
    Rs, Cs = parts.shape[2], parts.shape[3]
    tr2 = _rows_for(Cs, Rs)
    nb2 = Rs // tr2
    if ax == 2:
        s_map = lambda l, i, s: (l * nb2 + i, s[1])
    else:
        s_map = lambda l, i, s: ((4 * l + s[1]) * nb2 + i, 0)
    terms = [(s1.reshape(L * Rh, Ch), s_map)]
    terms += [(parts.reshape(3, L * Rs, Cs), (lambda l, i, s, k=k: (k, l * nb2 + i, 0))) for k in range(3)]
    s2 = _sum_terms(f"rs_sum4_{tag}", terms, L * Rs, Cs, (tr2, Cs), (L, nb2), scalars).reshape(L, Rs, Cs)
    return _join_halves(f"rs_join_{tag}", s2, hx)


def _all_to_all_small(buf):
    nr = buf.shape[0]

    def body(b_ref, out_ref, send_sems, recv_sems, local_sem):
        x, y, c = _position()
        me = 4 * x + 2 * y + c
        mine = pltpu.make_async_copy(b_ref, out_ref.at[me], local_sem)
        mine.start()
        sends = []
        for r in range(1, 8):
            peer = (x ^ (r >> 2), y ^ ((r >> 1) & 1), c ^ (r & 1))
            cp = pltpu.make_async_remote_copy(
                src_ref=b_ref, dst_ref=out_ref.at[me], send_sem=send_sems.at[r - 1], recv_sem=recv_sems.at[r - 1],
                device_id=peer, device_id_type=MESH)
            cp.start()
            sends.append(cp)
        for r in range(1, 8):
            px, py, pc = x ^ (r >> 2), y ^ ((r >> 1) & 1), c ^ (r & 1)
            pltpu.make_async_remote_copy(
                src_ref=b_ref, dst_ref=out_ref.at[4 * px + 2 * py + pc], send_sem=send_sems.at[r - 1],
                recv_sem=recv_sems.at[r - 1], device_id=(px, py, pc), device_id_type=MESH).wait_recv()
        for cp in sends:
            cp.wait_send()
        mine.wait()

    return pl.pallas_call(
        body, name="small_exchange", in_specs=[ANY], out_specs=ANY,
        out_shape=jax.ShapeDtypeStruct((8, nr, 128), buf.dtype),
        scratch_shapes=[pltpu.SemaphoreType.DMA((7,)), pltpu.SemaphoreType.DMA((7,)), pltpu.SemaphoreType.DMA],
    )(buf)


SMALL_TR = 256


def _all_reduce_small(buf):
    nr = buf.shape[0]
    slots = _all_to_all_small(buf).reshape(8 * nr, 128)
    nblk = nr // SMALL_TR

    def fn(i, j, rv, bv):
        acc = rv[0]
        for v in rv[1:]:
            acc = acc + v
        return [acc], []

    rows = [(slots, 128, _c(0), (lambda i, d=d: d * nblk + i)) for d in range(8)]
    return _rowwise("small_sum", fn, rows, [], [(128, 128, _c(0), F32)], [], nrows=nr, tr=SMALL_TR)[0]


def _adamw(name, w, g, m, v):
    rows, cols = w.shape
    tr = _rows_for(cols, rows, budget=1 << 18)

    def fn(i, j, rv, bv):
        wv, gv, mv, vv = rv
        m2 = ADAM_B1 * mv + (1.0 - ADAM_B1) * gv
        v2 = ADAM_B2 * vv + (1.0 - ADAM_B2) * (gv * gv)
        m_hat = m2 / (1.0 - ADAM_B1 ** ADAM_STEP)
        v_hat = v2 / (1.0 - ADAM_B2 ** ADAM_STEP)
        delta = -ADAM_LR * (m_hat / (jnp.sqrt(v_hat) + ADAM_EPS) + ADAM_WD * wv)
        return [delta, m2, v2], []

    return _rowwise(name, fn, [(a, cols, _c(0), 0) for a in (w, g, m, v)], [], [(cols, cols, _c(0), F32)] * 3, [],
                    nrows=rows, tr=tr)


WEIGHTS = ("norm_g", "w_in", "conv_w", "conv_b", "conv_ln_g", "conv_ln_b", "sgu_ln_g", "sgu_ln_b", "sgu_w", "sgu_b",
           "w_branch", "w_gate", "b_gate", "w_out", "final_g")
BIG = ("w_in", "w_branch", "w_gate", "w_out")
SMALL = tuple(n for n in WEIGHTS if n not in BIG)


def _pack(arrays, pad_rows):
    flat = jnp.concatenate([a.reshape(-1).astype(F32) for a in arrays])
    unit = 128 * pad_rows
    total = -(-flat.shape[0] // unit) * unit
    return jnp.pad(flat, (0, total - flat.shape[0])).reshape(total // 128, 128)


def _unpack(buf, shapes):
    flat = buf.reshape(-1)
    out, off = [], 0
    for shp in shapes:
        size = math.prod(shp)
        out.append(flat[off:off + size].reshape(shp))
        off += size
    return out


def kernel(x, norm_g, w_in, conv_w, conv_b, conv_ln_g, conv_ln_b, sgu_ln_g, sgu_ln_b, sgu_w, sgu_b, w_branch, w_gate, b_gate, w_out, final_g, loss_target, m_norm_g, m_w_in, m_conv_w, m_conv_b, m_conv_ln_g, m_conv_ln_b, m_sgu_ln_g, m_sgu_ln_b, m_sgu_w, m_sgu_b, m_w_branch, m_w_gate, m_b_gate, m_w_out, m_final_g, v_norm_g, v_w_in, v_conv_w, v_conv_b, v_conv_ln_g, v_conv_ln_b, v_sgu_ln_g, v_sgu_ln_b, v_sgu_w, v_sgu_b, v_w_branch, v_w_gate, v_b_gate, v_w_out, v_final_g):
    w = dict(norm_g=norm_g, w_in=w_in, conv_w=conv_w, conv_b=conv_b, conv_ln_g=conv_ln_g, conv_ln_b=conv_ln_b,
             sgu_ln_g=sgu_ln_g, sgu_ln_b=sgu_ln_b, sgu_w=sgu_w, sgu_b=sgu_b, w_branch=w_branch, w_gate=w_gate,
             b_gate=b_gate, w_out=w_out, final_g=final_g)
    m = dict(norm_g=m_norm_g, w_in=m_w_in, conv_w=m_conv_w, conv_b=m_conv_b, conv_ln_g=m_conv_ln_g,
             conv_ln_b=m_conv_ln_b, sgu_ln_g=m_sgu_ln_g, sgu_ln_b=m_sgu_ln_b, sgu_w=m_sgu_w, sgu_b=m_sgu_b,
             w_branch=m_w_branch, w_gate=m_w_gate, b_gate=m_b_gate, w_out=m_w_out, final_g=m_final_g)
    v = dict(norm_g=v_norm_g, w_in=v_w_in, conv_w=v_conv_w, conv_b=v_conv_b, conv_ln_g=v_conv_ln_g,
             conv_ln_b=v_conv_ln_b, sgu_ln_g=v_sgu_ln_g, sgu_ln_b=v_sgu_ln_b, sgu_w=v_sgu_w, sgu_b=v_sgu_b,
             w_branch=v_w_branch, w_gate=v_w_gate, b_gate=v_b_gate, w_out=v_w_out, final_g=v_final_g)
    depth = w_in.shape[0]
    chip = 2 * lax.axis_index("x") + lax.axis_index("y")

    full = dict(
        w_in=_gather("gather_w_in", w_in.astype(BF16), 2),
        w_gate=_gather("gather_w_gate", w_gate.astype(BF16), 2),
        w_branch=_gather("gather_w_branch", w_branch.astype(BF16).reshape(depth * 4, BW, -1), 2).reshape(
            depth, 4 * BW, D_MODEL),
        w_out=_gather("gather_w_out", w_out.astype(BF16), 1),
        conv_w=_gather("gather_conv_w", conv_w, 2))
    layer_params = []
    for l in range(depth):
        p = {n: w[n][l] for n in SMALL if n not in ("final_g", "conv_w")}
        p.update({n: full[n][l] for n in full})
        layer_params.append(p)

    h = x[0]
    saved = []
    for l in range(depth):
        h, s = _layer_fwd(h, layer_params[l])
        saved.append(s)
    dh, g_final, loss = _final_loss(h, loss_target[0], final_g)
    layer_grads = [None] * depth
    for l in reversed(range(depth)):
        dh, layer_grads[l] = _layer_bwd(dh, saved[l], layer_params[l])
    grad_x = dh[None]
    local = {n: jnp.stack([layer_grads[l][n] for l in range(depth)]) for n in WEIGHTS if n != "final_g"}
    local["final_g"] = g_final.reshape(-1)

    grads = dict(
        w_in=_reduce_scatter("w_in", local["w_in"], 2),
        w_gate=_reduce_scatter("w_gate", local["w_gate"], 2),
        w_branch=_reduce_scatter("w_branch", local["w_branch"].reshape(depth * 4, BW, D_MODEL), 2).reshape(
            w_branch.shape),
        w_out=_reduce_scatter("w_out", local["w_out"], 1))
    small_shapes = [local[n].shape for n in SMALL] + [(128,)]
    reduced = _unpack(_all_reduce_small(_pack([local[n] for n in SMALL] + [loss.reshape(128)], SMALL_TR)), small_shapes)
    for n, r in zip(SMALL, reduced[:-1]):
        grads[n] = r
    loss_out = reduced[-1][0]
    ncw = conv_w.shape[2]
    grads["conv_w"] = lax.dynamic_slice_in_dim(grads["conv_w"], chip * ncw, ncw, axis=2)

    delta, new_m, new_v = {}, {}, {}
    for n in BIG:
        cols = w[n].shape[-1]
        d2, m2, v2 = _adamw(f"adamw_{n}", *[a.reshape(-1, cols) for a in (w[n], grads[n], m[n], v[n])])
        delta[n], new_m[n], new_v[n] = d2.reshape(w[n].shape), m2.reshape(w[n].shape), v2.reshape(w[n].shape)
    shapes = [w[n].shape for n in SMALL]
    packed = [_pack([t[n] for n in SMALL], SMALL_TR) for t in (w, grads, m, v)]
    outs = _adamw("adamw_small", *packed)
    for res, o in zip((delta, new_m, new_v), outs):
        for n, a in zip(SMALL, _unpack(o, shapes)):
            res[n] = a
    return (loss_out, grad_x, *[grads[n] for n in WEIGHTS], *[delta[n] for n in WEIGHTS],
            *[new_m[n] for n in WEIGHTS], *[new_v[n] for n in WEIGHTS])
```

```python
import functools
import math

import jax
import jax.numpy as jnp
from jax import lax
from jax.experimental import pallas as pl
from jax.experimental.pallas import tpu as pltpu

F32 = jnp.float32
BF16 = jnp.bfloat16
MESH = pl.DeviceIdType.MESH

DEPTH = 4
D_MODEL = 2048
HEAD = 128
NHEAD = 8
BW = 1024
IN_WIDTH = 18432
GATE_W = 4 * D_MODEL
NORM_EPS = 1e-6
CONV_K = 31
HALO = 32
DIL_PATTERNS = ((128, 1), (512, 4), (2048, 16))
C_AQ, C_AK, C_AV, C_AG, C_BA, C_BB, C_BG, C_CU, C_CV, C_CG, C_DQ, C_DK, C_DV, C_DG = (
    0, 1, 2, 3, 4, 5, 6, 7, 8, 9, 10, 13, 16, 17)
ADAM_LR, ADAM_B1, ADAM_B2, ADAM_EPS, ADAM_WD, ADAM_STEP = 0.001, 0.9, 0.999, 1e-08, 0.01, 10
VMEM_LIMIT = 56 * 1024 * 1024

NN = (((1,), (0,)), ((), ()))
NT = (((1,), (1,)), ((), ()))
TN = (((0,), (0,)), ((), ()))


def _cparams(sem):
    return pltpu.CompilerParams(dimension_semantics=sem, vmem_limit_bytes=VMEM_LIMIT)


def _dot(a, b, dims):
    return lax.dot_general(a.astype(BF16), b.astype(BF16), dims, preferred_element_type=F32)


def _sigmoid(x):
    return 1.0 / (1.0 + jnp.exp(-x))


def _silu_and_grad(x):
    s = _sigmoid(x)
    return x * s, s * (1.0 + x * (1.0 - s))


_GELU_C = math.sqrt(2.0 / math.pi)


def _gelu_and_grad(x):
    t = jnp.tanh(_GELU_C * (x + 0.044715 * x * x * x))
    y = 0.5 * x * (1.0 + t)
    dy = 0.5 * (1.0 + t) + 0.5 * x * (1.0 - t * t) * _GELU_C * (1.0 + 3.0 * 0.044715 * x * x)
    return y, dy


def _ln_fwd(x, g, b):
    mu = jnp.mean(x, axis=-1, keepdims=True)
    xc = x - mu
    var = jnp.mean(xc * xc, axis=-1, keepdims=True)
    rstd = lax.rsqrt(var + NORM_EPS)
    xh = xc * rstd
    return xh * g + b, xh, rstd


def _ln_bwd(dy, xh, rstd, g):
    dxh = dy * g
    dx = rstd * (dxh - jnp.mean(dxh, axis=-1, keepdims=True) - xh * jnp.mean(dxh * xh, axis=-1, keepdims=True))
    return dx, jnp.sum(dy * xh, axis=0, keepdims=True), jnp.sum(dy, axis=0, keepdims=True)


def _rowwise(name, fn, rows, bcast, outs, accs, *, nrows, tr, nj=1):
    ni = nrows // tr
    nr, nb, no = len(rows), len(bcast), len(outs)

    def rmap(colfn, shift):
        if callable(shift):
            return lambda j, i: (shift(i), colfn(j))
        if shift == 0:
            return lambda j, i: (i, colfn(j))
        return lambda j, i: (jnp.clip(i + shift, 0, ni - 1), colfn(j))

    in_specs = [pl.BlockSpec((tr, w), rmap(cf, sh)) for (_, w, cf, sh) in rows]
    in_specs += [pl.BlockSpec(b.shape, lambda j, i, nd=b.ndim: (0,) * nd) for b in bcast]
    out_specs = [pl.BlockSpec((tr, w), rmap(cf, 0)) for (_, w, cf, _) in outs]
    out_specs += [pl.BlockSpec((r, w), lambda j, i, cf=cf: (0, cf(j))) for (r, _, w, cf) in accs]
    out_shape = [jax.ShapeDtypeStruct((nrows, nc), dt) for (nc, _, _, dt) in outs]
    out_shape += [jax.ShapeDtypeStruct((r, nc), F32) for (r, nc, _, _) in accs]

    def body(*refs):
        j = pl.program_id(0)
        i = pl.program_id(1)
        rv = [r[...] for r in refs[:nr]]
        bv = [r[...] for r in refs[nr:nr + nb]]
        o_refs = refs[nr + nb:nr + nb + no]
        a_refs = refs[nr + nb + no:]
        o_vals, a_vals = fn(i, j, rv, bv)
        for ref, val in zip(o_refs, o_vals):
            ref[...] = val.astype(ref.dtype)
        if a_refs:
            @pl.when(i == 0)
            def _():
                for ref in a_refs:
                    ref[...] = jnp.zeros(ref.shape, F32)
            for ref, val in zip(a_refs, a_vals):
                ref[...] += val

    res = pl.pallas_call(
        body, name=name, grid=(nj, ni), in_specs=in_specs, out_specs=out_specs, out_shape=out_shape,
        compiler_params=_cparams(("arbitrary", "arbitrary")),
    )(*[r[0] for r in rows], *bcast)
    return res


def _c(k):
    return lambda j: k


def _mm(name, a, b, dims, *, a_blk, a_map, b_blk, b_map, o_shape, o_blk, o_map, grid, o_dtype,
        bias=None, bias_blk=None, bias_map=None, res=None):
    nk = grid[3]
    extra, extra_specs = [], []
    if bias is not None:
        extra.append(bias)
        extra_specs.append(pl.BlockSpec(bias_blk, bias_map))
    if res is not None:
        extra.append(res)
        extra_specs.append(pl.BlockSpec(o_blk, o_map))
    nbias = bias is not None
    nres = res is not None

    def body(*refs):
        a_ref, b_ref = refs[0], refs[1]
        idx = 2
        bias_ref = refs[idx] if nbias else None
        idx += nbias
        res_ref = refs[idx] if nres else None
        idx += nres
        o_ref = refs[idx]
        acc_ref = refs[idx + 1]
        k = pl.program_id(3)
        part = _dot(a_ref[...], b_ref[...], dims)

        def finish(r):
            if nbias:
                r = r + bias_ref[...]
            if nres:
                r = r + res_ref[...]
            o_ref[...] = r.astype(o_ref.dtype)

        if nk == 1:
            finish(part)
        else:
            @pl.when(k == 0)
            def _():
                acc_ref[...] = part

            @pl.when(k > 0)
            def _():
                acc_ref[...] += part

            @pl.when(k == nk - 1)
            def _():
                finish(acc_ref[...])

    acc_shape = o_blk if nk > 1 else (8, 128)
    return pl.pallas_call(
        body, name=name, grid=grid,
        in_specs=[pl.BlockSpec(a_blk, a_map), pl.BlockSpec(b_blk, b_map)] + extra_specs,
        out_specs=pl.BlockSpec(o_blk, o_map),
        out_shape=jax.ShapeDtypeStruct(o_shape, o_dtype),
        scratch_shapes=[pltpu.VMEM(acc_shape, F32)],
        compiler_params=_cparams(("arbitrary", "arbitrary", "arbitrary", "arbitrary")),
    )(a, b, *extra)


def _mm_nn(name, a, b, o_dtype, *, bias=None, res=None, groups=1, tm=1024, tn=1024):
    M = a.shape[0]
    K = a.shape[1] // groups
    N = b.shape[1]
    tm, tn = min(tm, M), min(tn, N)
    njn = N // tn
    return _mm(name, a, b, NN,
               a_blk=(tm, K), a_map=lambda g, j, i, k: (i, g),
               b_blk=(K, tn), b_map=lambda g, j, i, k: (g, j),
               o_shape=(M, groups * N), o_blk=(tm, tn), o_map=lambda g, j, i, k: (i, g * njn + j),
               grid=(groups, njn, M // tm, 1), o_dtype=o_dtype,
               bias=bias, bias_blk=(1, tn), bias_map=lambda g, j, i, k: (0, g * njn + j), res=res)


def _mm_nt(name, a, b, o_dtype, *, res=None, groups=1, tm=512, tk=1024):
    M = a.shape[0]
    K = a.shape[1] // groups
    N = b.shape[0] // groups
    tm, tk = min(tm, M), min(tk, K)
    nk = K // tk
    return _mm(name, a, b, NT,
               a_blk=(tm, tk), a_map=lambda g, j, i, k: (i, g * nk + k),
               b_blk=(N, tk), b_map=lambda g, j, i, k: (g, k),
               o_shape=(M, groups * N), o_blk=(tm, N), o_map=lambda g, j, i, k: (i, g),
               grid=(groups, 1, M // tm, nk), o_dtype=o_dtype, res=res)


def _mm_tn(name, a, b, *, groups=1, tn=1024, tk=1024):
    K = a.shape[0]
    M = a.shape[1] // groups
    N = b.shape[1] // groups
    tn, tk = min(tn, N), min(tk, K)
    njn = N // tn
    return _mm(name, a, b, TN,
               a_blk=(tk, M), a_map=lambda g, j, i, k: (k, g),
               b_blk=(tk, tn), b_map=lambda g, j, i, k: (k, g * njn + j),
               o_shape=(groups * M, N), o_blk=(M, tn), o_map=lambda g, j, i, k: (g, j),
               grid=(groups, njn, 1, K // tk), o_dtype=F32)


def _rms_fwd(x, g):
    S, D = x.shape

    def fn(i, j, rv, bv):
        xv, gv = rv[0], bv[0]
        r = lax.rsqrt(jnp.mean(xv * xv, axis=-1, keepdims=True) + NORM_EPS)
        return [xv * r * gv], []

    return _rowwise("rms_fwd", fn, [(x, D, _c(0), 0)], [g.reshape(1, D)], [(D, D, _c(0), BF16)], [],
                    nrows=S, tr=512)[0]


def _rms_bwd_math(xv, dyv, gv):
    r = lax.rsqrt(jnp.mean(xv * xv, axis=-1, keepdims=True) + NORM_EPS)
    xh = xv * r
    dg = jnp.sum(dyv * xh, axis=0, keepdims=True)
    dxh = dyv * gv
    dx = r * (dxh - xh * jnp.mean(dxh * xh, axis=-1, keepdims=True))
    return dx, dg


def _rms_bwd(x, dh, dout, g):
    S, D = x.shape

    def fn(i, j, rv, bv):
        dx, dg = _rms_bwd_math(rv[0], rv[1], bv[0])
        return [rv[2] + dx], [dg]

    return _rowwise("rms_bwd", fn, [(x, D, _c(0), 0), (dh, D, _c(0), 0), (dout, D, _c(0), 0)], [g.reshape(1, D)],
                    [(D, D, _c(0), F32)], [(1, D, D, _c(0))], nrows=S, tr=256)


def _final_loss(x, target, g):
    S, D = x.shape

    def fn(i, j, rv, bv):
        xv, tv, gv = rv[0], rv[1], bv[0]
        r = lax.rsqrt(jnp.mean(xv * xv, axis=-1, keepdims=True) + NORM_EPS)
        err = xv * r * gv - tv
        loss = 0.5 * jnp.sum(jnp.mean(err * err, axis=-1, keepdims=True), axis=0, keepdims=True)
        dx, dg = _rms_bwd_math(xv, err * (1.0 / D), gv)
        return [dx], [dg, jnp.broadcast_to(loss, (1, 128))]

    return _rowwise("final_loss", fn, [(x, D, _c(0), 0), (target, D, _c(0), 0)], [g.reshape(1, D)],
                    [(D, D, _c(0), F32)], [(1, D, D, _c(0)), (1, 128, 128, _c(0))], nrows=S, tr=256)


MERGE_W = 512


def _merge_fwd(gpre, yproj):
    S = gpre.shape[0]
    nj = D_MODEL // MERGE_W

    def fn(i, j, rv, bv):
        acc = _sigmoid(rv[0]) * rv[4]
        for n in range(1, 4):
            acc = acc + _sigmoid(rv[n]) * rv[4 + n]
        return [acc], []

    rows = [(gpre, MERGE_W, (lambda j, n=n: n * nj + j), 0) for n in range(4)]
    rows += [(yproj, MERGE_W, (lambda j, n=n: n * nj + j), 0) for n in range(4)]
    return _rowwise("merge_fwd", fn, rows, [], [(D_MODEL, MERGE_W, lambda j: j, BF16)], [],
                    nrows=S, tr=512, nj=nj)[0]


def _merge_bwd(dmerged, gpre, yproj):
    S = gpre.shape[0]
    nj4 = D_MODEL // MERGE_W

    def fn(i, j, rv, bv):
        dm, gp, yp = rv
        sg = _sigmoid(gp)
        dgp = dm * yp * sg * (1.0 - sg)
        return [dm * sg, dgp], [jnp.sum(dgp, axis=0, keepdims=True)]

    rows = [(dmerged, MERGE_W, lambda j: j % nj4, 0), (gpre, MERGE_W, lambda j: j, 0), (yproj, MERGE_W, lambda j: j, 0)]
    return _rowwise("merge_bwd", fn, rows, [],
                    [(GATE_W, MERGE_W, lambda j: j, BF16), (GATE_W, MERGE_W, lambda j: j, BF16)],
                    [(1, GATE_W, MERGE_W, lambda j: j)], nrows=S, tr=512, nj=4 * nj4)


SB_TQ = 1024
SB_TK = 256
LOG2E = math.log2(math.e)


def _split2(v):
    hi = v.astype(BF16)
    lo = (v - hi.astype(F32)).astype(BF16)
    return hi, lo


def _stacked_triangle(n, pred):
    r = lax.broadcasted_iota(jnp.int32, (n, n), 0)
    c = lax.broadcasted_iota(jnp.int32, (n, n), 1)
    t = pred(r, c).astype(BF16)
    return jnp.concatenate([t, t], axis=0)


ANY = pl.BlockSpec(memory_space=pl.ANY)


class _Cargo:
    def __init__(self, arrays, out_shapes, aliases, nsem, start, finish):
        self.arrays, self.out_shapes, self.aliases, self.nsem = list(arrays), list(out_shapes), dict(aliases), nsem
        self.start, self.finish = start, finish


def _call_with_cargo(body, cargo, *, name, grid, in_specs, out_specs, out_shape, scratch_shapes, operands):
    sem = ("arbitrary",) * len(grid)
    if cargo is None:
        res = pl.pallas_call(body, name=name, grid=grid, in_specs=in_specs, out_specs=out_specs, out_shape=out_shape,
                             scratch_shapes=scratch_shapes, compiler_params=_cparams(sem))(*operands)
        return res, []
    n_in, n_out, n_scr = len(in_specs), len(out_specs), len(scratch_shapes)
    nci, nco = len(cargo.arrays), len(cargo.out_shapes)

    def wrapped(*refs):
        ins, refs = refs[:n_in], refs[n_in:]
        cin, refs = refs[:nci], refs[nci:]
        outs, refs = refs[:n_out], refs[n_out:]
        cout, refs = refs[:nco], refs[nco:]
        scr, (send, recv) = refs[:n_scr], refs[n_scr:]
        g0, g1 = pl.program_id(0), pl.program_id(1)

        @pl.when((g0 == 0) & (g1 == 0))
        def _():
            cargo.start(cin, cout, send, recv)

        body(*ins, *outs, *scr)

        @pl.when((g0 == grid[0] - 1) & (g1 == grid[1] - 1))
        def _():
            cargo.finish(cin, cout, send, recv)

    res = pl.pallas_call(
        wrapped, name=name, grid=grid, in_specs=list(in_specs) + [ANY] * nci, out_specs=list(out_specs) + [ANY] * nco,
        out_shape=list(out_shape) + cargo.out_shapes,
        scratch_shapes=list(scratch_shapes) + [pltpu.SemaphoreType.DMA((cargo.nsem,)), pltpu.SemaphoreType.DMA((cargo.nsem,))],
        input_output_aliases={n_in + k: n_out + v for k, v in cargo.aliases.items()},
        compiler_params=_cparams(sem))(*operands, *cargo.arrays)
    return res[:n_out], res[n_out:]


def _run_cargo(name, cargo):
    nci = len(cargo.arrays)

    def body(*refs):
        cin, cout, (send, recv) = refs[:nci], refs[nci:-2], refs[-2:]
        cargo.start(cin, cout, send, recv)
        cargo.finish(cin, cout, send, recv)

    return pl.pallas_call(
        body, name=name, in_specs=[ANY] * nci, out_specs=[ANY] * len(cargo.out_shapes), out_shape=cargo.out_shapes,
        scratch_shapes=[pltpu.SemaphoreType.DMA((cargo.nsem,)), pltpu.SemaphoreType.DMA((cargo.nsem,))],
        input_output_aliases=cargo.aliases)(*cargo.arrays)


def _sb_fwd(proj, cargo=None):
    S = proj.shape[0]
    tq, tk = min(SB_TQ, S), SB_TK
    assert S // tk <= HEAD
    nq = S // tq
    nkb = tq // tk
    scale = HEAD ** -0.5

    def body(q_ref, k_ref, v_ref, g_ref, o_ref, y_ref, r_ref, kb_ref, vb_ref):
        i = pl.program_id(1)

        @pl.when(i == 0)
        def _():
            kb_ref[...] = k_ref[...].astype(BF16)
            vb_ref[...] = v_ref[...].astype(BF16)

        qb = q_ref[...].astype(BF16)
        qpos = i * tq + lax.broadcasted_iota(jnp.int32, (tq, tk), 0)
        kidx = lax.broadcasted_iota(jnp.int32, (tq, tk), 1)
        lane = lax.broadcasted_iota(jnp.int32, (tq, HEAD), 1)
        upper2 = _stacked_triangle(tk, lambda r, c: r > c)

        def block(kblock, masked, carry):
            acc, run, runs = carry
            off = pl.multiple_of(kblock * tk, tk)
            zz = _dot(qb, kb_ref[pl.ds(off, tk), :], NT) * (scale * LOG2E)
            sp = jnp.maximum(zz, 0.0) + jnp.log2(1.0 + jnp.exp2(-jnp.abs(zz)))
            if masked:
                keep = (off + kidx) < qpos
                sp_sum = jnp.where(keep, sp, 0.0)
            else:
                sp_sum = sp
            hi, lo = _split2(sp_sum)
            w = jnp.exp2((zz - sp) - (_dot(jnp.concatenate([hi, lo], axis=1), upper2, NN) + run))
            if masked:
                w = jnp.where(keep, w, 0.0)
            acc = acc + _dot(w, vb_ref[pl.ds(off, tk), :], NN)
            runs = jnp.where(lane == kblock, run, runs)
            return acc, run + jnp.sum(sp_sum, axis=1, keepdims=True), runs

        runs0 = jnp.where(qpos[:, :HEAD] + lane < 0, 1.0, 0.0)
        carry = (jnp.zeros((tq, HEAD), F32), jnp.zeros((tq, 1), F32), runs0)
        for kbl in reversed(range(nkb)):
            carry = block(i * nkb + kbl, True, carry)
        acc, _, runs = lax.fori_loop(0, i * nkb, lambda n, c: block(i * nkb - 1 - n, False, c), carry)
        o_ref[...] = acc
        r_ref[...] = runs
        gate, _ = _silu_and_grad(g_ref[...])
        y_ref[...] = (acc * gate).astype(BF16)

    return _call_with_cargo(
        body, cargo, name="sb_fwd", grid=(NHEAD, nq),
        in_specs=[pl.BlockSpec((tq, HEAD), lambda h, i: (i, C_AQ * NHEAD + h)),
                  pl.BlockSpec((S, HEAD), lambda h, i: (0, C_AK * NHEAD + h)),
                  pl.BlockSpec((S, HEAD), lambda h, i: (0, C_AV * NHEAD + h)),
                  pl.BlockSpec((tq, HEAD), lambda h, i: (i, C_AG * NHEAD + h))],
        out_specs=[pl.BlockSpec((tq, HEAD), lambda h, i: (i, h))] * 3,
        out_shape=[jax.ShapeDtypeStruct((S, BW), F32), jax.ShapeDtypeStruct((S, BW), BF16),
                   jax.ShapeDtypeStruct((S, BW), F32)],
        scratch_shapes=[pltpu.VMEM((S, HEAD), BF16), pltpu.VMEM((S, HEAD), BF16)],
        operands=(proj, proj, proj, proj))


def _sb_bwd(proj, att, runs, dy, cargo=None):
    S = proj.shape[0]
    tq, tk = min(SB_TQ, S), SB_TK
    nq = S // tq
    nkb = tq // tk
    scale = HEAD ** -0.5

    def body(q_ref, k_ref, v_ref, g_ref, o_ref, r_ref, dy_ref, dq_ref, dk_ref, dv_ref, dg_ref, kb_ref, vb_ref):
        i = pl.program_id(1)

        @pl.when(i == 0)
        def _():
            kb_ref[...] = k_ref[...].astype(BF16)
            vb_ref[...] = v_ref[...].astype(BF16)
            dk_ref[...] = jnp.zeros(dk_ref.shape, F32)
            dv_ref[...] = jnp.zeros(dv_ref.shape, F32)

        gate, dgate = _silu_and_grad(g_ref[...])
        dyv = dy_ref[...]
        dg_ref[...] = dyv * o_ref[...] * dgate
        dob = (dyv * gate).astype(BF16)
        qb = q_ref[...].astype(BF16)
        runs = r_ref[...]
        qpos = i * tq + lax.broadcasted_iota(jnp.int32, (tq, tk), 0)
        kidx = lax.broadcasted_iota(jnp.int32, (tq, tk), 1)
        lane = lax.broadcasted_iota(jnp.int32, (tq, HEAD), 1)
        upper2 = _stacked_triangle(tk, lambda r, c: r > c)
        lower2 = _stacked_triangle(tk, lambda r, c: r < c)

        def block(kblock, masked, carry):
            dq, grun = carry
            off = pl.multiple_of(kblock * tk, tk)
            kblk = kb_ref[pl.ds(off, tk), :]
            zz = _dot(qb, kblk, NT) * (scale * LOG2E)
            sp = jnp.maximum(zz, 0.0) + jnp.log2(1.0 + jnp.exp2(-jnp.abs(zz)))
            lb = zz - sp
            if masked:
                keep = (off + kidx) < qpos
                sp_sum = jnp.where(keep, sp, 0.0)
            else:
                sp_sum = sp
            hi, lo = _split2(sp_sum)
            run = jnp.sum(jnp.where(lane == kblock, runs, 0.0), axis=1, keepdims=True)
            w = jnp.exp2(lb - (_dot(jnp.concatenate([hi, lo], axis=1), upper2, NN) + run))
            if masked:
                w = jnp.where(keep, w, 0.0)
            gw = _dot(dob, vb_ref[pl.ds(off, tk), :], NT) * w
            ghi, glo = _split2(gw)
            gsum = grun + _dot(jnp.concatenate([ghi, glo], axis=1), lower2, NN)
            dz = (gw - (gw + gsum) * jnp.exp2(lb)) * scale
            if masked:
                dz = jnp.where(keep, dz, 0.0)
            dzb = dz.astype(BF16)
            dk_ref[pl.ds(off, tk), :] += _dot(dzb, qb, TN)
            dv_ref[pl.ds(off, tk), :] += _dot(w, dob, TN)
            return dq + _dot(dzb, kblk, NN), grun + jnp.sum(gw, axis=1, keepdims=True)

        carry = lax.fori_loop(0, i * nkb, lambda n, c: block(n, False, c),
                              (jnp.zeros((tq, HEAD), F32), jnp.zeros((tq, 1), F32)))
        for kbl in range(nkb):
            carry = block(i * nkb + kbl, True, carry)
        dq_ref[...] = carry[0]

    blk = lambda h, i: (i, h)
    head = lambda h, i: (0, h)
    return _call_with_cargo(
        body, cargo, name="sb_bwd", grid=(NHEAD, nq),
        in_specs=[pl.BlockSpec((tq, HEAD), lambda h, i: (i, C_AQ * NHEAD + h)),
                  pl.BlockSpec((S, HEAD), lambda h, i: (0, C_AK * NHEAD + h)),
                  pl.BlockSpec((S, HEAD), lambda h, i: (0, C_AV * NHEAD + h)),
                  pl.BlockSpec((tq, HEAD), lambda h, i: (i, C_AG * NHEAD + h)),
                  pl.BlockSpec((tq, HEAD), blk), pl.BlockSpec((tq, HEAD), blk), pl.BlockSpec((tq, HEAD), blk)],
        out_specs=[pl.BlockSpec((tq, HEAD), blk), pl.BlockSpec((S, HEAD), head), pl.BlockSpec((S, HEAD), head),
                   pl.BlockSpec((tq, HEAD), blk)],
        out_shape=[jax.ShapeDtypeStruct((S, BW), F32)] * 4,
        scratch_shapes=[pltpu.VMEM((S, HEAD), BF16), pltpu.VMEM((S, HEAD), BF16)],
        operands=(proj, proj, proj, proj, att, runs, dy))


CONV_TR = 256


def _conv_taps(buf_ref, w_ref, base, tr):
    acc = w_ref[0:1, :] * buf_ref[pl.ds(base, tr), :]
    for k in range(1, CONV_K):
        acc = acc + w_ref[k:k + 1, :] * buf_ref[pl.ds(base + k, tr), :]
    return acc


def _conv_fwd(proj, conv_w, conv_b, ln_g, ln_b):
    S = proj.shape[0]
    tr = min(CONV_TR, S)
    ni = S // tr

    def body(a_ref, b_ref, ap_ref, bp_ref, g_ref, w_ref, cb_ref, lg_ref, lb_ref, c_ref, y_ref, buf_ref):
        i = pl.program_id(0)
        prev = ap_ref[tr - HALO:, :] * _sigmoid(bp_ref[tr - HALO:, :])
        buf_ref[0:HALO, :] = jnp.where(i > 0, prev, 0.0)
        buf_ref[HALO:, :] = a_ref[...] * _sigmoid(b_ref[...])
        c = _conv_taps(buf_ref, w_ref, HALO - (CONV_K - 1), tr) + cb_ref[...]
        c_ref[...] = c
        yn, _, _ = _ln_fwd(c, lg_ref[...], lb_ref[...])
        act, _ = _silu_and_grad(yn)
        gate, _ = _silu_and_grad(g_ref[...])
        y_ref[...] = (act * gate).astype(BF16)

    cur = lambda k: (lambda i: (i, k))
    prv = lambda k: (lambda i: (jnp.maximum(i - 1, 0), k))
    full = lambda a: pl.BlockSpec(a.shape, lambda i: (0, 0))
    wpad = jnp.pad(conv_w, ((0, HALO - CONV_K), (0, 0)))
    small = [wpad, conv_b.reshape(1, BW), ln_g.reshape(1, BW), ln_b.reshape(1, BW)]
    return pl.pallas_call(
        body, name="conv_fwd", grid=(ni,),
        in_specs=[pl.BlockSpec((tr, BW), cur(C_BA)), pl.BlockSpec((tr, BW), cur(C_BB)),
                  pl.BlockSpec((tr, BW), prv(C_BA)), pl.BlockSpec((tr, BW), prv(C_BB)),
                  pl.BlockSpec((tr, BW), cur(C_BG))] + [full(a) for a in small],
        out_specs=[pl.BlockSpec((tr, BW), lambda i: (i, 0)), pl.BlockSpec((tr, BW), lambda i: (i, 0))],
        out_shape=[jax.ShapeDtypeStruct((S, BW), F32), jax.ShapeDtypeStruct((S, BW), BF16)],
        scratch_shapes=[pltpu.VMEM((tr + HALO, BW), F32)],
        compiler_params=_cparams(("arbitrary",)),
    )(proj, proj, proj, proj, proj, *small)


def _conv_bwd_norm(proj, cpre, dy, ln_g, ln_b):
    S = proj.shape[0]

    def fn(i, j, rv, bv):
        c, bg, dyv = rv
        lg, lb = bv
        yn, xh, rstd = _ln_fwd(c, lg, lb)
        act, dact = _silu_and_grad(yn)
        gate, dgate = _silu_and_grad(bg)
        dyn = dyv * gate * dact
        dc, dlg, dlb = _ln_bwd(dyn, xh, rstd, lg)
        return [dc, dyv * act * dgate], [dlg, dlb, jnp.sum(dc, axis=0, keepdims=True)]

    rows = [(cpre, BW, _c(0), 0), (proj, BW, _c(C_BG), 0), (dy, BW, _c(1), 0)]
    return _rowwise("conv_bwd_norm", fn, rows, [ln_g.reshape(1, BW), ln_b.reshape(1, BW)],
                    [(BW, BW, _c(0), F32), (BW, BW, _c(0), BF16)],
                    [(1, BW, BW, _c(0))] * 3, nrows=S, tr=256)


def _conv_bwd_taps(proj, dc, conv_w):
    S = proj.shape[0]
    tr = min(CONV_TR, S)
    ni = S // tr

    def body(a_ref, b_ref, ap_ref, bp_ref, dc_ref, dcn_ref, w_ref, da_ref, db_ref, dw_ref, gbuf, dbuf, wrev):
        i = pl.program_id(0)

        @pl.when(i == 0)
        def _():
            dw_ref[...] = jnp.zeros(dw_ref.shape, F32)
            for k in range(CONV_K):
                wrev[k:k + 1, :] = w_ref[CONV_K - 1 - k:CONV_K - k, :]

        sb = _sigmoid(b_ref[...])
        av = a_ref[...]
        prev = ap_ref[tr - HALO:, :] * _sigmoid(bp_ref[tr - HALO:, :])
        gbuf[0:HALO, :] = jnp.where(i > 0, prev, 0.0)
        gbuf[HALO:, :] = av * sb
        dcv = dc_ref[...]
        dbuf[0:tr, :] = dcv
        dbuf[tr:, :] = jnp.where(i < ni - 1, dcn_ref[0:HALO, :], 0.0)
        dg = _conv_taps(dbuf, wrev, 0, tr)
        da_ref[...] = (dg * sb).astype(BF16)
        db_ref[...] = (dg * av * sb * (1.0 - sb)).astype(BF16)
        base = HALO - (CONV_K - 1)
        for k in range(CONV_K):
            dw_ref[k:k + 1, :] += jnp.sum(dcv * gbuf[pl.ds(base + k, tr), :], axis=0, keepdims=True)

    cur = lambda k: (lambda i: (i, k))
    prv = lambda k: (lambda i: (jnp.maximum(i - 1, 0), k))
    wpad = jnp.pad(conv_w, ((0, HALO - CONV_K), (0, 0)))
    return pl.pallas_call(
        body, name="conv_bwd_taps", grid=(ni,),
        in_specs=[pl.BlockSpec((tr, BW), cur(C_BA)), pl.BlockSpec((tr, BW), cur(C_BB)),
                  pl.BlockSpec((tr, BW), prv(C_BA)), pl.BlockSpec((tr, BW), prv(C_BB)),
                  pl.BlockSpec((tr, BW), lambda i: (i, 0)),
                  pl.BlockSpec((tr, BW), lambda i: (jnp.minimum(i + 1, ni - 1), 0)),
                  pl.BlockSpec((HALO, BW), lambda i: (0, 0))],
        out_specs=[pl.BlockSpec((tr, BW), lambda i: (i, 0)), pl.BlockSpec((tr, BW), lambda i: (i, 0)),
                   pl.BlockSpec((HALO, BW), lambda i: (0, 0))],
        out_shape=[jax.ShapeDtypeStruct((S, BW), BF16), jax.ShapeDtypeStruct((S, BW), BF16),
                   jax.ShapeDtypeStruct((HALO, BW), F32)],
        scratch_shapes=[pltpu.VMEM((tr + HALO, BW), F32), pltpu.VMEM((tr + HALO, BW), F32),
                        pltpu.VMEM((HALO, BW), F32)],
        compiler_params=_cparams(("arbitrary",)),
    )(proj, proj, proj, proj, dc, dc, wpad)


SGU_TR = 256


def _sgu_mix(wm, vn, bias):
    tr = vn.shape[0]
    rows = []
    for n in range(tr // HEAD):
        cols = []
        for g in range(NHEAD):
            blk = vn[n * HEAD:(n + 1) * HEAD, g * HEAD:(g + 1) * HEAD]
            cols.append(_dot(wm[g], blk, NN) + bias[g * HEAD:(g + 1) * HEAD, :])
        rows.append(jnp.concatenate(cols, axis=1))
    return jnp.concatenate(rows, axis=0)


def _sgu_masked_w(w2d):
    rr = lax.broadcasted_iota(jnp.int32, (HEAD, HEAD), 0)
    cc = lax.broadcasted_iota(jnp.int32, (HEAD, HEAD), 1)
    tril = rr >= cc
    return [jnp.where(tril, w2d[g * HEAD:(g + 1) * HEAD, :], 0.0).astype(BF16) for g in range(NHEAD)], tril


def _sgu_inputs(sgu_w, sgu_b, ln_g, ln_b):
    w2d = sgu_w.reshape(NHEAD * HEAD, HEAD)
    bias = jnp.broadcast_to(sgu_b[:, :, None], (NHEAD, HEAD, HEAD)).reshape(NHEAD * HEAD, HEAD)
    return [w2d, bias, ln_g.reshape(1, BW), ln_b.reshape(1, BW)]


def _sgu_fwd(proj, sgu_w, sgu_b, ln_g, ln_b):
    S = proj.shape[0]

    def fn(i, j, rv, bv):
        cu, cv, cg = rv
        w2d, bias, lg, lb = bv
        wm, _ = _sgu_masked_w(w2d)
        u, _ = _gelu_and_grad(cu)
        v, _ = _gelu_and_grad(cv)
        vn, _, _ = _ln_fwd(v, lg, lb)
        z = _sgu_mix(wm, vn, bias)
        gate, _ = _silu_and_grad(cg)
        return [u * z * gate], []

    rows = [(proj, BW, _c(C_CU), 0), (proj, BW, _c(C_CV), 0), (proj, BW, _c(C_CG), 0)]
    return _rowwise("sgu_fwd", fn, rows, _sgu_inputs(sgu_w, sgu_b, ln_g, ln_b), [(BW, BW, _c(0), BF16)], [],
                    nrows=S, tr=min(SGU_TR, S))[0]


def _sgu_bwd(proj, dy, sgu_w, sgu_b, ln_g, ln_b):
    S = proj.shape[0]

    def fn(i, j, rv, bv):
        cu, cv, cg, dyv = rv
        w2d, bias, lg, lb = bv
        wm, tril = _sgu_masked_w(w2d)
        u, du_dcu = _gelu_and_grad(cu)
        v, dv_dcv = _gelu_and_grad(cv)
        vn, xh, rstd = _ln_fwd(v, lg, lb)
        z = _sgu_mix(wm, vn, bias)
        gate, dgate = _silu_and_grad(cg)
        dz = dyv * u * gate
        dcu = dyv * z * gate * du_dcu
        dcg = dyv * u * z * dgate
        tr = dz.shape[0]
        dvn_rows = []
        dw = [jnp.zeros((HEAD, HEAD), F32) for _ in range(NHEAD)]
        dbias = [jnp.zeros((HEAD, HEAD), F32) for _ in range(NHEAD)]
        for n in range(tr // HEAD):
            cols = []
            for g in range(NHEAD):
                dzb = dz[n * HEAD:(n + 1) * HEAD, g * HEAD:(g + 1) * HEAD]
                vnb = vn[n * HEAD:(n + 1) * HEAD, g * HEAD:(g + 1) * HEAD]
                cols.append(_dot(wm[g], dzb, TN))
                dw[g] = dw[g] + _dot(dzb, vnb, NT)
                dbias[g] = dbias[g] + dzb
            dvn_rows.append(jnp.concatenate(cols, axis=1))
        dvn = jnp.concatenate(dvn_rows, axis=0)
        dv, dlg, dlb = _ln_bwd(dvn, xh, rstd, lg)
        dw2d = jnp.concatenate([jnp.where(tril, d, 0.0) for d in dw], axis=0)
        return [dcu, dv * dv_dcv, dcg], [dw2d, jnp.concatenate(dbias, axis=0), dlg, dlb]

    rows = [(proj, BW, _c(C_CU), 0), (proj, BW, _c(C_CV), 0), (proj, BW, _c(C_CG), 0), (dy, BW, _c(2), 0)]
    return _rowwise("sgu_bwd", fn, rows, _sgu_inputs(sgu_w, sgu_b, ln_g, ln_b),
                    [(BW, BW, _c(0), BF16)] * 3,
                    [(NHEAD * HEAD, HEAD, HEAD, _c(0)), (NHEAD * HEAD, HEAD, HEAD, _c(0)), (1, BW, BW, _c(0)),
                     (1, BW, BW, _c(0))], nrows=S, tr=min(SGU_TR, S))


NEG = -1e30
NCOL = IN_WIDTH // BW


def _band_masks():
    a = lax.broadcasted_iota(jnp.int32, (HEAD, HEAD), 0)
    c = lax.broadcasted_iota(jnp.int32, (HEAD, HEAD), 1)
    return a >= c, a <= c


def _dil_view(proj, grp):
    S = proj.shape[0]
    dil = DIL_PATTERNS[grp][1]
    if dil == 1:
        return proj, NCOL, C_DQ + grp, C_DK + grp, C_DV
    cols = [proj[:, c * BW:(c + 1) * BW] for c in (C_DQ + grp, C_DK + grp, C_DV)]
    return jnp.concatenate(cols, axis=1).reshape(S // dil, dil * 3 * BW), 3, 0, 1, 2


def _dil_fwd(proj, grp):
    S = proj.shape[0]
    dil = DIL_PATTERNS[grp][1]
    L = S // dil
    nb = L // HEAD
    proj2, ncol, cq, ck, cv = _dil_view(proj, grp)
    scale = HEAD ** -0.5

    def body(q_ref, kc_ref, kp_ref, vc_ref, vp_ref, o_ref, l_ref):
        b = pl.program_id(1)
        m_cur, m_prev = _band_masks()
        m_prev = m_prev & (b > 0)
        for h in range(NHEAD):
            sl = slice(h * HEAD, (h + 1) * HEAD)
            q = q_ref[:, sl].astype(BF16)
            s_c = jnp.where(m_cur, _dot(q, kc_ref[:, sl], NT) * scale, NEG)
            s_p = jnp.where(m_prev, _dot(q, kp_ref[:, sl], NT) * scale, NEG)
            m = jnp.maximum(jnp.max(s_c, axis=1, keepdims=True), jnp.max(s_p, axis=1, keepdims=True))
            p_c = jnp.exp(s_c - m)
            p_p = jnp.exp(s_p - m)
            den = jnp.sum(p_c, axis=1, keepdims=True) + jnp.sum(p_p, axis=1, keepdims=True)
            o = (_dot(p_c, vc_ref[:, sl], NN) + _dot(p_p, vp_ref[:, sl], NN)) / den
            o_ref[:, sl] = o
            l_ref[:, sl] = jnp.broadcast_to(m + jnp.log(den), (HEAD, HEAD))

    cur = lambda col: (lambda r, b: (b, r * ncol + col))
    prv = lambda col: (lambda r, b: (jnp.maximum(b - 1, 0), r * ncol + col))
    out_map = lambda r, b: (b, r)
    o, lse = pl.pallas_call(
        body, name=f"dil_fwd{grp}", grid=(dil, nb),
        in_specs=[pl.BlockSpec((HEAD, BW), cur(cq)),
                  pl.BlockSpec((HEAD, BW), cur(ck)), pl.BlockSpec((HEAD, BW), prv(ck)),
                  pl.BlockSpec((HEAD, BW), cur(cv)), pl.BlockSpec((HEAD, BW), prv(cv))],
        out_specs=[pl.BlockSpec((HEAD, BW), out_map), pl.BlockSpec((HEAD, BW), out_map)],
        out_shape=[jax.ShapeDtypeStruct((L, dil * BW), F32), jax.ShapeDtypeStruct((L, dil * BW), F32)],
        compiler_params=_cparams(("arbitrary", "arbitrary")),
    )(proj2, proj2, proj2, proj2, proj2)
    return o.reshape(S, BW), lse.reshape(S, BW)


def _dil_combine(proj, os_, lses):
    S = proj.shape[0]

    def fn(i, j, rv, bv):
        o1, o2, o3, l1, l2, l3, dg = rv
        m = jnp.maximum(jnp.maximum(l1, l2), l3)
        e1, e2, e3 = jnp.exp(l1 - m), jnp.exp(l2 - m), jnp.exp(l3 - m)
        den = e1 + e2 + e3
        out = (e1 * o1 + e2 * o2 + e3 * o3) / den
        gate, _ = _silu_and_grad(dg)
        return [out, m + jnp.log(den), out * gate], []

    rows = [(a, BW, _c(0), 0) for a in (*os_, *lses)] + [(proj, BW, _c(C_DG), 0)]
    return _rowwise("dil_combine", fn, rows, [], [(BW, BW, _c(0), F32), (BW, BW, _c(0), F32), (BW, BW, _c(0), BF16)],
                    [], nrows=S, tr=256)


def _dil_gate_bwd(proj, att, dy):
    S = proj.shape[0]

    def fn(i, j, rv, bv):
        dg, av, dyv = rv
        gate, dgate = _silu_and_grad(dg)
        dout = dyv * gate
        prod = dout * av
        tr = prod.shape[0]
        delta = jnp.concatenate(
            [jnp.broadcast_to(jnp.sum(prod[:, h * HEAD:(h + 1) * HEAD], axis=1, keepdims=True), (tr, HEAD))
             for h in range(NHEAD)], axis=1)
        return [dout, dyv * av * dgate, delta], []

    rows = [(proj, BW, _c(C_DG), 0), (att, BW, _c(0), 0), (dy, BW, _c(3), 0)]
    return _rowwise("dil_gate_bwd", fn, rows, [], [(BW, BW, _c(0), F32), (BW, BW, _c(0), BF16), (BW, BW, _c(0), F32)],
                    [], nrows=S, tr=256)


def _dil_bwd(proj, dout, lse, delta, grp):
    S = proj.shape[0]
    dil = DIL_PATTERNS[grp][1]
    L = S // dil
    nb = L // HEAD
    proj2, ncol, cq, ck, cv = _dil_view(proj, grp)
    view = lambda a: a.reshape(L, dil * BW)
    scale = HEAD ** -0.5

    def body(q_ref, qn_ref, kc_ref, kp_ref, vc_ref, vp_ref, do_ref, don_ref, l_ref, ln_ref, d_ref, dn_ref,
             dq_ref, dk_ref, dv_ref):
        b = pl.program_id(1)
        m_cur, m_band = _band_masks()
        m_prev = m_band & (b > 0)
        m_next = m_band & (b < nb - 1)
        for h in range(NHEAD):
            sl = slice(h * HEAD, (h + 1) * HEAD)
            q, qn = q_ref[:, sl].astype(BF16), qn_ref[:, sl].astype(BF16)
            kc, kp = kc_ref[:, sl].astype(BF16), kp_ref[:, sl].astype(BF16)
            vc, vp = vc_ref[:, sl].astype(BF16), vp_ref[:, sl].astype(BF16)
            do, don = do_ref[:, sl].astype(BF16), don_ref[:, sl].astype(BF16)
            lse_c, lse_n = l_ref[:, sl], ln_ref[:, sl]
            dl_c, dl_n = d_ref[:, sl], dn_ref[:, sl]
            p_cc = jnp.exp(jnp.where(m_cur, _dot(q, kc, NT) * scale - lse_c, NEG))
            p_cp = jnp.exp(jnp.where(m_prev, _dot(q, kp, NT) * scale - lse_c, NEG))
            p_nc = jnp.exp(jnp.where(m_next, _dot(qn, kc, NT) * scale - lse_n, NEG))
            ds_cc = p_cc * (_dot(do, vc, NT) - dl_c)
            ds_cp = p_cp * (_dot(do, vp, NT) - dl_c)
            ds_nc = p_nc * (_dot(don, vc, NT) - dl_n)
            dq_ref[:, sl] = ((_dot(ds_cc, kc, NN) + _dot(ds_cp, kp, NN)) * scale).astype(BF16)
            dk_ref[:, sl] = ((_dot(ds_cc, q, TN) + _dot(ds_nc, qn, TN)) * scale).astype(BF16)
            dv_ref[:, sl] = _dot(p_cc, do, TN) + _dot(p_nc, don, TN)

    cur = lambda col: (lambda r, b: (b, r * ncol + col))
    prv = lambda col: (lambda r, b: (jnp.maximum(b - 1, 0), r * ncol + col))
    nxt = lambda col: (lambda r, b: (jnp.minimum(b + 1, nb - 1), r * ncol + col))
    o_cur = lambda r, b: (b, r)
    o_nxt = lambda r, b: (jnp.minimum(b + 1, nb - 1), r)
    blk = (HEAD, BW)
    dq, dk, dv = pl.pallas_call(
        body, name=f"dil_bwd{grp}", grid=(dil, nb),
        in_specs=[pl.BlockSpec(blk, cur(cq)), pl.BlockSpec(blk, nxt(cq)),
                  pl.BlockSpec(blk, cur(ck)), pl.BlockSpec(blk, prv(ck)),
                  pl.BlockSpec(blk, cur(cv)), pl.BlockSpec(blk, prv(cv)),
                  pl.BlockSpec(blk, o_cur), pl.BlockSpec(blk, o_nxt),
                  pl.BlockSpec(blk, o_cur), pl.BlockSpec(blk, o_nxt),
                  pl.BlockSpec(blk, o_cur), pl.BlockSpec(blk, o_nxt)],
        out_specs=[pl.BlockSpec(blk, o_cur)] * 3,
        out_shape=[jax.ShapeDtypeStruct((L, dil * BW), BF16), jax.ShapeDtypeStruct((L, dil * BW), BF16),
                   jax.ShapeDtypeStruct((L, dil * BW), F32)],
        compiler_params=_cparams(("arbitrary", "arbitrary")),
    )(proj2, proj2, proj2, proj2, proj2, proj2, view(dout), view(dout), view(lse), view(lse), view(delta), view(delta))
    return dq.reshape(S, BW), dk.reshape(S, BW), dv.reshape(S, BW)


def _add3(a, b, c):
    S, W = a.shape

    def fn(i, j, rv, bv):
        return [rv[0] + rv[1] + rv[2]], []

    return _rowwise("add3", fn, [(a, W, _c(0), 0), (b, W, _c(0), 0), (c, W, _c(0), 0)], [], [(W, W, _c(0), BF16)], [],
                    nrows=S, tr=512)[0]


def _layer_fwd(x, p, cargo=None):
    h = _rms_fwd(x, p["norm_g"])
    proj = _mm_nn("in_proj", h, p["w_in"], F32)
    gpre = _mm_nn("gate_proj", h, p["w_gate"], F32, bias=p["b_gate"].reshape(1, GATE_W))
    (att_a, ya, runs_a), carried = _sb_fwd(proj, cargo)
    cpre, yb = _conv_fwd(proj, p["conv_w"], p["conv_b"], p["conv_ln_g"], p["conv_ln_b"])
    yc = _sgu_fwd(proj, p["sgu_w"], p["sgu_b"], p["sgu_ln_g"], p["sgu_ln_b"])
    dil = [_dil_fwd(proj, g) for g in range(3)]
    att_d, lse_d, yd = _dil_combine(proj, [d[0] for d in dil], [d[1] for d in dil])
    y = jnp.concatenate([ya, yb, yc, yd], axis=1)
    yproj = _mm_nn("branch_proj", y, p["w_branch"], F32, groups=4)
    merged = _merge_fwd(gpre, yproj)
    x_next = _mm_nn("out_proj", merged, p["w_out"], F32, res=x)
    saved = dict(x=x, h=h, proj=proj, gpre=gpre, att_a=att_a, runs_a=runs_a, cpre=cpre, att_d=att_d, lse_d=lse_d, y=y,
                 yproj=yproj, merged=merged)
    return x_next, saved, carried


def _layer_bwd(dout, s, p, cargo=None):
    proj = s["proj"]
    dmerged = _mm_nt("out_proj_dx", dout, p["w_out"], F32)
    g_w_out = _mm_tn("out_proj_dw", s["merged"], dout)
    dyproj, dgpre, g_b_gate = _merge_bwd(dmerged, s["gpre"], s["yproj"])
    dy = _mm_nt("branch_proj_dx", dyproj, p["w_branch"], F32, groups=4)
    g_w_branch = _mm_tn("branch_proj_dw", s["y"], dyproj, groups=4)
    (d_aq, d_ak, d_av, d_ag), carried = _sb_bwd(proj, s["att_a"], s["runs_a"], dy, cargo)
    dc, d_bg, g_cln_g, g_cln_b, g_conv_b = _conv_bwd_norm(proj, s["cpre"], dy, p["conv_ln_g"], p["conv_ln_b"])
    d_ba, d_bb, g_conv_w = _conv_bwd_taps(proj, dc, p["conv_w"])
    d_cu, d_cv, d_cg, g_sgu_w, g_sgu_bias, g_sln_g, g_sln_b = _sgu_bwd(
        proj, dy, p["sgu_w"], p["sgu_b"], p["sgu_ln_g"], p["sgu_ln_b"])
    dout_d, d_dg, delta = _dil_gate_bwd(proj, s["att_d"], dy)
    dil = [_dil_bwd(proj, dout_d, s["lse_d"], delta, g) for g in range(3)]
    d_dv = _add3(dil[0][2], dil[1][2], dil[2][2])
    pieces = [d_aq, d_ak, d_av, d_ag, d_ba, d_bb, d_bg, d_cu, d_cv, d_cg,
              dil[0][0], dil[1][0], dil[2][0], dil[0][1], dil[1][1], dil[2][1], d_dv, d_dg]
    dproj = jnp.concatenate([t.astype(BF16) for t in pieces], axis=1)
    dh_gate = _mm_nt("gate_proj_dx", dgpre, p["w_gate"], F32)
    dh = _mm_nt("in_proj_dx", dproj, p["w_in"], F32, res=dh_gate)
    g_w_in = _mm_tn("in_proj_dw", s["h"], dproj)
    g_w_gate = _mm_tn("gate_proj_dw", s["h"], dgpre)
    dx, g_norm_g = _rms_bwd(s["x"], dh, dout, p["norm_g"])
    grads = dict(
        norm_g=g_norm_g.reshape(D_MODEL), w_in=g_w_in, conv_w=g_conv_w[:CONV_K], conv_b=g_conv_b.reshape(BW),
        conv_ln_g=g_cln_g.reshape(BW), conv_ln_b=g_cln_b.reshape(BW), sgu_ln_g=g_sln_g.reshape(BW),
        sgu_ln_b=g_sln_b.reshape(BW), sgu_w=g_sgu_w.reshape(NHEAD, HEAD, HEAD),
        sgu_b=jnp.sum(g_sgu_bias.reshape(NHEAD, HEAD, HEAD), axis=-1), w_branch=g_w_branch,
        w_gate=g_w_gate, b_gate=g_b_gate.reshape(GATE_W), w_out=g_w_out)
    return dx, grads, carried


CHIP_FLIPS =((1, 0), (0, 1), (1, 1))


def _at(ref, axis, start, size):
    idx = [slice(None)] * len(ref.shape)
    idx[axis] = pl.ds(start, size)
    return ref.at[tuple(idx)]


def _position():
    return lax.axis_index("x"), lax.axis_index("y"), lax.axis_index("c")


def _gather(name, w, axis):
    n = w.shape[axis]
    full = list(w.shape)
    full[axis] = 4 * n

    def body(w_ref, out_ref, send_sems, recv_sems, local_sem):
        x, y, c = _position()
        mine = pltpu.make_async_copy(w_ref, _at(out_ref, axis, (2 * x + y) * n, n), local_sem)
        mine.start()
        sends = []
        for k, (fx, fy) in enumerate(CHIP_FLIPS):
            cp = pltpu.make_async_remote_copy(
                src_ref=w_ref, dst_ref=_at(out_ref, axis, (2 * x + y) * n, n), send_sem=send_sems.at[k],
                recv_sem=recv_sems.at[k], device_id=(x ^ fx, y ^ fy, c), device_id_type=MESH)
            cp.start()
            sends.append(cp)
        for k, (fx, fy) in enumerate(CHIP_FLIPS):
            px, py = x ^ fx, y ^ fy
            pltpu.make_async_remote_copy(
                src_ref=w_ref, dst_ref=_at(out_ref, axis, (2 * px + py) * n, n), send_sem=send_sems.at[k],
                recv_sem=recv_sems.at[k], device_id=(px, py, c), device_id_type=MESH).wait_recv()
        for cp in sends:
            cp.wait_send()
        mine.wait()

    return pl.pallas_call(
        body, name=name, in_specs=[ANY], out_specs=ANY, out_shape=jax.ShapeDtypeStruct(tuple(full), w.dtype),
        scratch_shapes=[pltpu.SemaphoreType.DMA((3,)), pltpu.SemaphoreType.DMA((3,)), pltpu.SemaphoreType.DMA],
    )(w)


def _block(ref, ax, shard, half):
    if shard is not None:
        n = ref.shape[ax] // 4
        ref = _at(ref, ax, shard * n, n)
    if half is not None:
        nh = ref.shape[1 - ax] // 2
        ref = _at(ref, 1 - ax, half * nh, nh)
    return ref


def _remote(src, dst, send_sems, recv_sems, k, device):
    return pltpu.make_async_remote_copy(src_ref=src, dst_ref=dst, send_sem=send_sems.at[k], recv_sem=recv_sems.at[k],
                                        device_id=device, device_id_type=MESH)


def _gather_cargo(fulls, axes):
    na = len(fulls)

    def start(ins, outs, send, recv):
        x, y, c = _position()
        for a in range(na):
            mine = _block(outs[a], axes[a], 2 * x + y, c)
            for k, (fx, fy) in enumerate(CHIP_FLIPS):
                _remote(mine, mine, send, recv, 3 * a + k, (x ^ fx, y ^ fy, c)).start()

    def finish(ins, outs, send, recv):
        x, y, c = _position()
        sib = (x, y, 1 - c)
        for a in range(na):
            for k, (fx, fy) in enumerate(CHIP_FLIPS):
                got = _block(outs[a], axes[a], 2 * (x ^ fx) + (y ^ fy), c)
                _remote(got, got, send, recv, 3 * a + k, sib).wait_recv()
                _remote(got, got, send, recv, 3 * (na + a) + k, sib).start()
        for a in range(na):
            mine = _block(outs[a], axes[a], 2 * x + y, c)
            for k, (fx, fy) in enumerate(CHIP_FLIPS):
                passed = _block(outs[a], axes[a], 2 * (x ^ fx) + (y ^ fy), c)
                theirs = _block(outs[a], axes[a], 2 * (x ^ fx) + (y ^ fy), 1 - c)
                _remote(theirs, theirs, send, recv, 3 * (na + a) + k, sib).wait_recv()
                _remote(passed, passed, send, recv, 3 * (na + a) + k, sib).wait_send()
                _remote(mine, mine, send, recv, 3 * a + k, sib).wait_send()

    shapes = [jax.ShapeDtypeStruct(f.shape, f.dtype) for f in fulls]
    return _Cargo(fulls, shapes, {a: a for a in range(na)}, 6 * na, start, finish)


def _scatter_cargo(sums, axes):
    na = len(sums)
    shapes = []
    for s, ax in zip(sums, axes):
        piece = list(s.shape)
        piece[ax] //= 4
        shapes.append(jax.ShapeDtypeStruct((3, *piece), s.dtype))

    def copies(ins, outs, send, recv):
        x, y, c = _position()
        return [_remote(_block(ins[a], axes[a], 2 * (x ^ fx) + (y ^ fy), None), outs[a].at[k], send, recv, 3 * a + k,
                        (x ^ fx, y ^ fy, c))
                for a in range(na) for k, (fx, fy) in enumerate(CHIP_FLIPS)]

    def start(ins, outs, send, recv):
        for cp in copies(ins, outs, send, recv):
            cp.start()

    def finish(ins, outs, send, recv):
        for cp in copies(ins, outs, send, recv):
            cp.wait()

    return _Cargo(sums, shapes, {}, 3 * na, start, finish)


def _swap_halves(name, grads, axes):
    na = len(grads)
    shapes = []
    for g, ax in zip(grads, axes):
        half = list(g.shape)
        half[1 - ax] //= 2
        shapes.append(jax.ShapeDtypeStruct(tuple(half), g.dtype))

    def body(*refs):
        ins, outs, (send, recv) = refs[:na], refs[na:2 * na], refs[2 * na:]
        x, y, c = _position()
        copies = [_remote(_block(ins[a], axes[a], None, 1 - c), outs[a], send, recv, a, (x, y, 1 - c)) for a in range(na)]
        for cp in copies:
            cp.start()
        for cp in copies:
            cp.wait()

    return pl.pallas_call(
        body, name=name, in_specs=[ANY] * na, out_specs=[ANY] * na, out_shape=shapes,
        scratch_shapes=[pltpu.SemaphoreType.DMA((na,)), pltpu.SemaphoreType.DMA((na,))],
    )(*grads)


def _join_halves(name, shards, axes):
    na = len(shards)

    def body(*refs):
        outs, (send, recv) = refs[na:2 * na], refs[2 * na:]
        x, y, c = _position()
        sib = (x, y, 1 - c)
        for a in range(na):
            mine = _block(outs[a], axes[a], None, c)
            _remote(mine, mine, send, recv, a, sib).start()
        for a in range(na):
            mine = _block(outs[a], axes[a], None, c)
            theirs = _block(outs[a], axes[a], None, 1 - c)
            _remote(theirs, theirs, send, recv, a, sib).wait_recv()
            _remote(mine, mine, send, recv, a, sib).wait_send()

    return pl.pallas_call(
        body, name=name, in_specs=[ANY] * na, out_specs=[ANY] * na,
        out_shape=[jax.ShapeDtypeStruct(s.shape, s.dtype) for s in shards],
        scratch_shapes=[pltpu.SemaphoreType.DMA((na,)), pltpu.SemaphoreType.DMA((na,))],
        input_output_aliases={a: a for a in range(na)},
    )(*shards)


def _sum_terms(name, terms, out_shape, out_dtype, blk, nblocks, out_map, scalars):
    tr, tc = blk

    def body(s_ref, *refs):
        acc = refs[0][...].astype(F32)
        for r in refs[1:-1]:
            acc = acc + r[...].astype(F32)
        refs[-1][...] = acc.astype(refs[-1].dtype)

    in_specs = [pl.BlockSpec((None, tr, tc) if a.ndim == 3 else (tr, tc), m) for a, m in terms]
    return pl.pallas_call(
        body, name=name,
        grid_spec=pltpu.PrefetchScalarGridSpec(
            num_scalar_prefetch=1, grid=(nblocks,), in_specs=in_specs, out_specs=pl.BlockSpec((tr, tc), out_map)),
        out_shape=jax.ShapeDtypeStruct(out_shape, out_dtype),
        compiler_params=_cparams(("arbitrary",)),
    )(scalars, *[a for a, _ in terms])


def _rows_for(cols, rows, budget=1 << 19):
    tr = 8
    while tr * 2 * cols <= budget and tr * 2 <= min(rows, 256) and rows % (tr * 2) == 0:
        tr *= 2
    assert rows % tr == 0
    return tr


def _place_shard(name, w, layer, ax, scalars):
    _, R, C = w.shape
    tr = _rows_for(C, R)
    nb = R // tr
    if ax == 0:
        shape, out_map = (4 * R, C), (lambda i, s: (s[1] * nb + i, 0))
    else:
        shape, out_map = (R, 4 * C), (lambda i, s: (i, s[1]))
    return _sum_terms(name, [(w, lambda i, s: (layer, i, 0))], shape, BF16, (tr, C), nb, out_map, scalars)


def _reduce_begin(grads, axes, scalars):
    got = _swap_halves("rs_halves", grads, axes)
    sums = []
    for a, (g, h, ax) in enumerate(zip(grads, got, axes)):
        Rh, Ch = h.shape
        tr = _rows_for(Ch, Rh)
        nb = Rh // tr
        g_map = (lambda i, s, nb=nb: (s[0] * nb + i, 0)) if ax == 1 else (lambda i, s: (i, s[0]))
        sums.append(_sum_terms(f"rs_sum2_{a}", [(g, g_map), (h, lambda i, s: (i, 0))], (Rh, Ch), F32, (tr, Ch), nb,
                               lambda i, s: (i, 0), scalars))
    return sums


def _reduce_end(sums, parts, axes, scalars):
    shards = []
    for a, (s1, pt, ax) in enumerate(zip(sums, parts, axes)):
        _, Rs, Cs = pt.shape
        tr = _rows_for(Cs, Rs)
        nb = Rs // tr
        if ax == 1:
            shape, s_map, o_map = (2 * Rs, Cs), (lambda i, s: (i, s[1])), (lambda i, s, nb=nb: (s[0] * nb + i, 0))
        else:
            shape, s_map, o_map = (Rs, 2 * Cs), (lambda i, s, nb=nb: (s[1] * nb + i, 0)), (lambda i, s: (i, s[0]))
        terms = [(s1, s_map)] + [(pt, (lambda i, s, k=k: (k, i, 0))) for k in range(3)]
        shards.append(_sum_terms(f"rs_sum4_{a}", terms, shape, F32, (tr, Cs), nb, o_map, scalars))
    return _join_halves("rs_join", shards, axes)


def _all_to_all_small(buf):
    nr = buf.shape[0]

    def body(b_ref, out_ref, send_sems, recv_sems, local_sem):
        x, y, c = _position()
        me = 4 * x + 2 * y + c
        mine = pltpu.make_async_copy(b_ref, out_ref.at[me], local_sem)
        mine.start()
        sends = []
        for r in range(1, 8):
            peer = (x ^ (r >> 2), y ^ ((r >> 1) & 1), c ^ (r & 1))
            cp = pltpu.make_async_remote_copy(
                src_ref=b_ref, dst_ref=out_ref.at[me], send_sem=send_sems.at[r - 1], recv_sem=recv_sems.at[r - 1],
                device_id=peer, device_id_type=MESH)
            cp.start()
            sends.append(cp)
        for r in range(1, 8):
            px, py, pc = x ^ (r >> 2), y ^ ((r >> 1) & 1), c ^ (r & 1)
            pltpu.make_async_remote_copy(
                src_ref=b_ref, dst_ref=out_ref.at[4 * px + 2 * py + pc], send_sem=send_sems.at[r - 1],
                recv_sem=recv_sems.at[r - 1], device_id=(px, py, pc), device_id_type=MESH).wait_recv()
        for cp in sends:
            cp.wait_send()
        mine.wait()

    return pl.pallas_call(
        body, name="small_exchange", in_specs=[ANY], out_specs=ANY,
        out_shape=jax.ShapeDtypeStruct((8, nr, 128), buf.dtype),
        scratch_shapes=[pltpu.SemaphoreType.DMA((7,)), pltpu.SemaphoreType.DMA((7,)), pltpu.SemaphoreType.DMA],
    )(buf)


SMALL_TR = 256


def _all_reduce_small(buf):
    nr = buf.shape[0]
    slots = _all_to_all_small(buf).reshape(8 * nr, 128)
    nblk = nr // SMALL_TR

    def fn(i, j, rv, bv):
        acc = rv[0]
        for v in rv[1:]:
            acc = acc + v
        return [acc], []

    rows = [(slots, 128, _c(0), (lambda i, d=d: d * nblk + i)) for d in range(8)]
    return _rowwise("small_sum", fn, rows, [], [(128, 128, _c(0), F32)], [], nrows=nr, tr=SMALL_TR)[0]


def _adamw(name, w, g, m, v):
    rows, cols = w.shape
    tr = _rows_for(cols, rows, budget=1 << 18)

    def fn(i, j, rv, bv):
        wv, gv, mv, vv = rv
        m2 = ADAM_B1 * mv + (1.0 - ADAM_B1) * gv
        v2 = ADAM_B2 * vv + (1.0 - ADAM_B2) * (gv * gv)
        m_hat = m2 / (1.0 - ADAM_B1 ** ADAM_STEP)
        v_hat = v2 / (1.0 - ADAM_B2 ** ADAM_STEP)
        delta = -ADAM_LR * (m_hat / (jnp.sqrt(v_hat) + ADAM_EPS) + ADAM_WD * wv)
        return [delta, m2, v2], []

    return _rowwise(name, fn, [(a, cols, _c(0), 0) for a in (w, g, m, v)], [], [(cols, cols, _c(0), F32)] * 3, [],
                    nrows=rows, tr=tr)


WEIGHTS = ("norm_g", "w_in", "conv_w", "conv_b", "conv_ln_g", "conv_ln_b", "sgu_ln_g", "sgu_ln_b", "sgu_w", "sgu_b",
           "w_branch", "w_gate", "b_gate", "w_out", "final_g")
BIG = ("w_in", "w_branch", "w_gate", "w_out")
BIG_AXIS = dict(w_in=1, w_branch=1, w_gate=1, w_out=0)
SMALL = tuple(n for n in WEIGHTS if n not in BIG)


def _pack(arrays, pad_rows):
    flat = jnp.concatenate([a.reshape(-1).astype(F32) for a in arrays])
    unit = 128 * pad_rows
    total = -(-flat.shape[0] // unit) * unit
    return jnp.pad(flat, (0, total - flat.shape[0])).reshape(total // 128, 128)


def _unpack(buf, shapes):
    flat = buf.reshape(-1)
    out, off = [], 0
    for shp in shapes:
        size = math.prod(shp)
        out.append(flat[off:off + size].reshape(shp))
        off += size
    return out


def kernel(x, norm_g, w_in, conv_w, conv_b, conv_ln_g, conv_ln_b, sgu_ln_g, sgu_ln_b, sgu_w, sgu_b, w_branch, w_gate, b_gate, w_out, final_g, loss_target, m_norm_g, m_w_in, m_conv_w, m_conv_b, m_conv_ln_g, m_conv_ln_b, m_sgu_ln_g, m_sgu_ln_b, m_sgu_w, m_sgu_b, m_w_branch, m_w_gate, m_b_gate, m_w_out, m_final_g, v_norm_g, v_w_in, v_conv_w, v_conv_b, v_conv_ln_g, v_conv_ln_b, v_sgu_ln_g, v_sgu_ln_b, v_sgu_w, v_sgu_b, v_w_branch, v_w_gate, v_b_gate, v_w_out, v_final_g):
    w = dict(norm_g=norm_g, w_in=w_in, conv_w=conv_w, conv_b=conv_b, conv_ln_g=conv_ln_g, conv_ln_b=conv_ln_b,
             sgu_ln_g=sgu_ln_g, sgu_ln_b=sgu_ln_b, sgu_w=sgu_w, sgu_b=sgu_b, w_branch=w_branch, w_gate=w_gate,
             b_gate=b_gate, w_out=w_out, final_g=final_g)
    m = dict(norm_g=m_norm_g, w_in=m_w_in, conv_w=m_conv_w, conv_b=m_conv_b, conv_ln_g=m_conv_ln_g,
             conv_ln_b=m_conv_ln_b, sgu_ln_g=m_sgu_ln_g, sgu_ln_b=m_sgu_ln_b, sgu_w=m_sgu_w, sgu_b=m_sgu_b,
             w_branch=m_w_branch, w_gate=m_w_gate, b_gate=m_b_gate, w_out=m_w_out, final_g=m_final_g)
    v = dict(norm_g=v_norm_g, w_in=v_w_in, conv_w=v_conv_w, conv_b=v_conv_b, conv_ln_g=v_conv_ln_g,
             conv_ln_b=v_conv_ln_b, sgu_ln_g=v_sgu_ln_g, sgu_ln_b=v_sgu_ln_b, sgu_w=v_sgu_w, sgu_b=v_sgu_b,
             w_branch=v_w_branch, w_gate=v_w_gate, b_gate=v_b_gate, w_out=v_w_out, final_g=v_final_g)
    depth = w_in.shape[0]
    chip = 2 * lax.axis_index("x") + lax.axis_index("y")
    scalars = jnp.stack([lax.axis_index("c"), chip]).astype(jnp.int32)
    axes = [BIG_AXIS[n] for n in BIG]

    stacked = dict(w_in=w_in, w_branch=w_branch.reshape(depth, 4 * BW, -1), w_gate=w_gate, w_out=w_out)
    buffers = [[_place_shard(f"place_{n}", stacked[n], l, BIG_AXIS[n], scalars) for n in BIG] for l in range(depth)]
    conv_w_full = _gather("gather_conv_w", conv_w, 2)
    gathered = _run_cargo("gather_first", _gather_cargo(buffers[0], axes))
    h = x[0]
    saved, layer_params = [], []
    for l in range(depth):
        p = {n: w[n][l] for n in SMALL if n not in ("final_g", "conv_w")}
        p["conv_w"] = conv_w_full[l]
        p.update(dict(zip(BIG, gathered)))
        cargo = _gather_cargo(buffers[l + 1], axes) if l + 1 < depth else None
        h, s, gathered = _layer_fwd(h, p, cargo)
        saved.append(s)
        layer_params.append(p)
    dh, g_final, loss = _final_loss(h, loss_target[0], final_g)

    layer_grads, reduced_big = [None] * depth, [None] * depth
    sums = None
    for l in reversed(range(depth)):
        cargo = _scatter_cargo(sums, axes) if sums is not None else None
        dh, layer_grads[l], parts = _layer_bwd(dh, saved[l], layer_params[l], cargo)
        if sums is not None:
            reduced_big[l + 1] = _reduce_end(sums, parts, axes, scalars)
        sums = _reduce_begin([layer_grads[l][n] for n in BIG], axes, scalars)
    reduced_big[0] = _reduce_end(sums, _run_cargo("rs_shards_last", _scatter_cargo(sums, axes)), axes, scalars)
    grad_x = dh[None]
    local = {n: jnp.stack([layer_grads[l][n] for l in range(depth)]) for n in SMALL if n != "final_g"}
    local["final_g"] = g_final.reshape(-1)
    grads = {n: jnp.stack([reduced_big[l][a] for l in range(depth)]).reshape(w[n].shape) for a, n in enumerate(BIG)}
    small_shapes = [local[n].shape for n in SMALL] + [(128,)]
    reduced = _unpack(_all_reduce_small(_pack([local[n] for n in SMALL] + [loss.reshape(128)], SMALL_TR)), small_shapes)
    for n, r in zip(SMALL, reduced[:-1]):
        grads[n] = r
    loss_out = reduced[-1][0]
    ncw = conv_w.shape[2]
    grads["conv_w"] = lax.dynamic_slice_in_dim(grads["conv_w"], chip * ncw, ncw, axis=2)

    delta, new_m, new_v = {}, {}, {}
    for n in BIG:
        cols = w[n].shape[-1]
        d2, m2, v2 = _adamw(f"adamw_{n}", *[a.reshape(-1, cols) for a in (w[n], grads[n], m[n], v[n])])
        delta[n], new_m[n], new_v[n] = d2.reshape(w[n].shape), m2.reshape(w[n].shape), v2.reshape(w[n].shape)
    shapes = [w[n].shape for n in SMALL]
    packed = [_pack([t[n] for n in SMALL], SMALL_TR) for t in (w, grads, m, v)]
    outs = _adamw("adamw_small", *packed)
    for res, o in zip((delta, new_m, new_v), outs):
        for n, a in zip(SMALL, _unpack(o, shapes)):
            res[n] = a
    return (loss_out, grad_x, *[grads[n] for n in WEIGHTS], *[delta[n] for n in WEIGHTS],
            *[new_m[n] for n in WEIGHTS], *[new_v[n] for n in WEIGHTS])
```

```python
import functools
import math

import jax
import jax.numpy as jnp
from jax import lax
from jax.experimental import pallas as pl
from jax.experimental.pallas import tpu as pltpu

F32 = jnp.float32
BF16 = jnp.bfloat16
MESH = pl.DeviceIdType.MESH

DEPTH = 4
D_MODEL = 2048
HEAD = 128
NHEAD = 8
BW = 1024
IN_WIDTH = 18432
GATE_W = 4 * D_MODEL
NORM_EPS = 1e-6
CONV_K = 31
HALO = 32
DIL_PATTERNS = ((128, 1), (512, 4), (2048, 16))
C_AQ, C_AK, C_AV, C_AG, C_BA, C_BB, C_BG, C_CU, C_CV, C_CG, C_DQ, C_DK, C_DV, C_DG = (
    0, 1, 2, 3, 4, 5, 6, 7, 8, 9, 10, 13, 16, 17)
ADAM_LR, ADAM_B1, ADAM_B2, ADAM_EPS, ADAM_WD, ADAM_STEP = 0.001, 0.9, 0.999, 1e-08, 0.01, 10
VMEM_LIMIT = 56 * 1024 * 1024

NN = (((1,), (0,)), ((), ()))
NT = (((1,), (1,)), ((), ()))
TN = (((0,), (0,)), ((), ()))


def _cparams(sem):
    return pltpu.CompilerParams(dimension_semantics=sem, vmem_limit_bytes=VMEM_LIMIT)


def _dot(a, b, dims):
    return lax.dot_general(a.astype(BF16), b.astype(BF16), dims, preferred_element_type=F32)


def _sigmoid(x):
    return 1.0 / (1.0 + jnp.exp(-x))


def _silu_and_grad(x):
    s = _sigmoid(x)
    return x * s, s * (1.0 + x * (1.0 - s))


_GELU_C = math.sqrt(2.0 / math.pi)


def _gelu_and_grad(x):
    t = jnp.tanh(_GELU_C * (x + 0.044715 * x * x * x))
    y = 0.5 * x * (1.0 + t)
    dy = 0.5 * (1.0 + t) + 0.5 * x * (1.0 - t * t) * _GELU_C * (1.0 + 3.0 * 0.044715 * x * x)
    return y, dy


def _ln_fwd(x, g, b):
    mu = jnp.mean(x, axis=-1, keepdims=True)
    xc = x - mu
    var = jnp.mean(xc * xc, axis=-1, keepdims=True)
    rstd = lax.rsqrt(var + NORM_EPS)
    xh = xc * rstd
    return xh * g + b, xh, rstd


def _ln_bwd(dy, xh, rstd, g):
    dxh = dy * g
    dx = rstd * (dxh - jnp.mean(dxh, axis=-1, keepdims=True) - xh * jnp.mean(dxh * xh, axis=-1, keepdims=True))
    return dx, jnp.sum(dy * xh, axis=0, keepdims=True), jnp.sum(dy, axis=0, keepdims=True)


def _rowwise(name, fn, rows, bcast, outs, accs, *, nrows, tr, nj=1, cargo=None):
    ni = nrows // tr
    nr, nb, no = len(rows), len(bcast), len(outs)

    def rmap(colfn, shift):
        if callable(shift):
            return lambda j, i: (shift(i), colfn(j))
        if shift == 0:
            return lambda j, i: (i, colfn(j))
        return lambda j, i: (jnp.clip(i + shift, 0, ni - 1), colfn(j))

    in_specs = [pl.BlockSpec((tr, w), rmap(cf, sh)) for (_, w, cf, sh) in rows]
    in_specs += [pl.BlockSpec(b.shape, lambda j, i, nd=b.ndim: (0,) * nd) for b in bcast]
    out_specs = [pl.BlockSpec((tr, w), rmap(cf, 0)) for (_, w, cf, _) in outs]
    out_specs += [pl.BlockSpec((r, w), lambda j, i, cf=cf: (0, cf(j))) for (r, _, w, cf) in accs]
    out_shape = [jax.ShapeDtypeStruct((nrows, nc), dt) for (nc, _, _, dt) in outs]
    out_shape += [jax.ShapeDtypeStruct((r, nc), F32) for (r, nc, _, _) in accs]

    def body(*refs):
        j = pl.program_id(0)
        i = pl.program_id(1)
        rv = [r[...] for r in refs[:nr]]
        bv = [r[...] for r in refs[nr:nr + nb]]
        o_refs = refs[nr + nb:nr + nb + no]
        a_refs = refs[nr + nb + no:]
        o_vals, a_vals = fn(i, j, rv, bv)
        for ref, val in zip(o_refs, o_vals):
            ref[...] = val.astype(ref.dtype)
        if a_refs:
            @pl.when(i == 0)
            def _():
                for ref in a_refs:
                    ref[...] = jnp.zeros(ref.shape, F32)
            for ref, val in zip(a_refs, a_vals):
                ref[...] += val

    res, carried = _call_with_cargo(body, cargo, name=name, grid=(nj, ni), in_specs=in_specs, out_specs=out_specs,
                                    out_shape=out_shape, scratch_shapes=[], operands=(*[r[0] for r in rows], *bcast))
    return res if cargo is None else (res, carried)


def _c(k):
    return lambda j: k


def _mm(name, a, b, dims, *, a_blk, a_map, b_blk, b_map, o_shape, o_blk, o_map, grid, o_dtype,
        bias=None, bias_blk=None, bias_map=None, res=None):
    nk = grid[3]
    extra, extra_specs = [], []
    if bias is not None:
        extra.append(bias)
        extra_specs.append(pl.BlockSpec(bias_blk, bias_map))
    if res is not None:
        extra.append(res)
        extra_specs.append(pl.BlockSpec(o_blk, o_map))
    nbias = bias is not None
    nres = res is not None

    def body(*refs):
        a_ref, b_ref = refs[0], refs[1]
        idx = 2
        bias_ref = refs[idx] if nbias else None
        idx += nbias
        res_ref = refs[idx] if nres else None
        idx += nres
        o_ref = refs[idx]
        acc_ref = refs[idx + 1]
        k = pl.program_id(3)
        part = _dot(a_ref[...], b_ref[...], dims)

        def finish(r):
            if nbias:
                r = r + bias_ref[...]
            if nres:
                r = r + res_ref[...]
            o_ref[...] = r.astype(o_ref.dtype)

        if nk == 1:
            finish(part)
        else:
            @pl.when(k == 0)
            def _():
                acc_ref[...] = part

            @pl.when(k > 0)
            def _():
                acc_ref[...] += part

            @pl.when(k == nk - 1)
            def _():
                finish(acc_ref[...])

    acc_shape = o_blk if nk > 1 else (8, 128)
    return pl.pallas_call(
        body, name=name, grid=grid,
        in_specs=[pl.BlockSpec(a_blk, a_map), pl.BlockSpec(b_blk, b_map)] + extra_specs,
        out_specs=pl.BlockSpec(o_blk, o_map),
        out_shape=jax.ShapeDtypeStruct(o_shape, o_dtype),
        scratch_shapes=[pltpu.VMEM(acc_shape, F32)],
        compiler_params=_cparams(("arbitrary", "arbitrary", "arbitrary", "arbitrary")),
    )(a, b, *extra)


def _mm_nn(name, a, b, o_dtype, *, bias=None, res=None, groups=1, tm=1024, tn=1024):
    M = a.shape[0]
    K = a.shape[1] // groups
    N = b.shape[1]
    tm, tn = min(tm, M), min(tn, N)
    njn = N // tn
    return _mm(name, a, b, NN,
               a_blk=(tm, K), a_map=lambda g, j, i, k: (i, g),
               b_blk=(K, tn), b_map=lambda g, j, i, k: (g, j),
               o_shape=(M, groups * N), o_blk=(tm, tn), o_map=lambda g, j, i, k: (i, g * njn + j),
               grid=(groups, njn, M // tm, 1), o_dtype=o_dtype,
               bias=bias, bias_blk=(1, tn), bias_map=lambda g, j, i, k: (0, g * njn + j), res=res)


def _mm_nt(name, a, b, o_dtype, *, res=None, groups=1, tm=512, tk=1024):
    M = a.shape[0]
    K = a.shape[1] // groups
    N = b.shape[0] // groups
    tm, tk = min(tm, M), min(tk, K)
    nk = K // tk
    return _mm(name, a, b, NT,
               a_blk=(tm, tk), a_map=lambda g, j, i, k: (i, g * nk + k),
               b_blk=(N, tk), b_map=lambda g, j, i, k: (g, k),
               o_shape=(M, groups * N), o_blk=(tm, N), o_map=lambda g, j, i, k: (i, g),
               grid=(groups, 1, M // tm, nk), o_dtype=o_dtype, res=res)


def _mm_tn(name, a, b, *, groups=1, tn=1024, tk=1024):
    K = a.shape[0]
    M = a.shape[1] // groups
    N = b.shape[1] // groups
    tn, tk = min(tn, N), min(tk, K)
    njn = N // tn
    return _mm(name, a, b, TN,
               a_blk=(tk, M), a_map=lambda g, j, i, k: (k, g),
               b_blk=(tk, tn), b_map=lambda g, j, i, k: (k, g * njn + j),
               o_shape=(groups * M, N), o_blk=(M, tn), o_map=lambda g, j, i, k: (g, j),
               grid=(groups, njn, 1, K // tk), o_dtype=F32)


def _rms_fwd(x, g):
    S, D = x.shape

    def fn(i, j, rv, bv):
        xv, gv = rv[0], bv[0]
        r = lax.rsqrt(jnp.mean(xv * xv, axis=-1, keepdims=True) + NORM_EPS)
        return [xv * r * gv], []

    return _rowwise("rms_fwd", fn, [(x, D, _c(0), 0)], [g.reshape(1, D)], [(D, D, _c(0), BF16)], [],
                    nrows=S, tr=512)[0]


def _rms_bwd_math(xv, dyv, gv):
    r = lax.rsqrt(jnp.mean(xv * xv, axis=-1, keepdims=True) + NORM_EPS)
    xh = xv * r
    dg = jnp.sum(dyv * xh, axis=0, keepdims=True)
    dxh = dyv * gv
    dx = r * (dxh - xh * jnp.mean(dxh * xh, axis=-1, keepdims=True))
    return dx, dg


def _rms_bwd(x, dh, dout, g):
    S, D = x.shape

    def fn(i, j, rv, bv):
        dx, dg = _rms_bwd_math(rv[0], rv[1], bv[0])
        return [rv[2] + dx], [dg]

    return _rowwise("rms_bwd", fn, [(x, D, _c(0), 0), (dh, D, _c(0), 0), (dout, D, _c(0), 0)], [g.reshape(1, D)],
                    [(D, D, _c(0), F32)], [(1, D, D, _c(0))], nrows=S, tr=256)


def _final_loss(x, target, g):
    S, D = x.shape

    def fn(i, j, rv, bv):
        xv, tv, gv = rv[0], rv[1], bv[0]
        r = lax.rsqrt(jnp.mean(xv * xv, axis=-1, keepdims=True) + NORM_EPS)
        err = xv * r * gv - tv
        loss = 0.5 * jnp.sum(jnp.mean(err * err, axis=-1, keepdims=True), axis=0, keepdims=True)
        dx, dg = _rms_bwd_math(xv, err * (1.0 / D), gv)
        return [dx], [dg, jnp.broadcast_to(loss, (1, 128))]

    return _rowwise("final_loss", fn, [(x, D, _c(0), 0), (target, D, _c(0), 0)], [g.reshape(1, D)],
                    [(D, D, _c(0), F32)], [(1, D, D, _c(0)), (1, 128, 128, _c(0))], nrows=S, tr=256)


MERGE_W = 512


def _merge_fwd(gpre, yproj):
    S = gpre.shape[0]
    nj = D_MODEL // MERGE_W

    def fn(i, j, rv, bv):
        acc = _sigmoid(rv[0]) * rv[4]
        for n in range(1, 4):
            acc = acc + _sigmoid(rv[n]) * rv[4 + n]
        return [acc], []

    rows = [(gpre, MERGE_W, (lambda j, n=n: n * nj + j), 0) for n in range(4)]
    rows += [(yproj, MERGE_W, (lambda j, n=n: n * nj + j), 0) for n in range(4)]
    return _rowwise("merge_fwd", fn, rows, [], [(D_MODEL, MERGE_W, lambda j: j, BF16)], [],
                    nrows=S, tr=512, nj=nj)[0]


def _merge_bwd(dmerged, gpre, yproj, cargo=None):
    S = gpre.shape[0]
    nj4 = D_MODEL // MERGE_W

    def fn(i, j, rv, bv):
        dm, gp, yp = rv
        sg = _sigmoid(gp)
        dgp = dm * yp * sg * (1.0 - sg)
        return [dm * sg, dgp], [jnp.sum(dgp, axis=0, keepdims=True)]

    rows = [(dmerged, MERGE_W, lambda j: j % nj4, 0), (gpre, MERGE_W, lambda j: j, 0), (yproj, MERGE_W, lambda j: j, 0)]
    return _rowwise("merge_bwd", fn, rows, [],
                    [(GATE_W, MERGE_W, lambda j: j, BF16), (GATE_W, MERGE_W, lambda j: j, BF16)],
                    [(1, GATE_W, MERGE_W, lambda j: j)], nrows=S, tr=512, nj=4 * nj4, cargo=cargo)


SB_TQ = 1024
SB_TQ_BWD = 512
SB_TK = 512
SB_TRI = 256


def _softplus(z):
    neg_abs = lax.bitcast_convert_type(lax.bitcast_convert_type(z, jnp.uint32) | jnp.uint32(0x80000000), F32)
    return jnp.maximum(z, 0.0) + jnp.log(1.0 + jnp.exp(neg_abs))


def _split2(v):
    hi = v.astype(BF16)
    lo = (v - hi.astype(F32)).astype(BF16)
    return hi, lo


def _stacked_triangle(n, pred):
    r = lax.broadcasted_iota(jnp.int32, (n, n), 0)
    c = lax.broadcasted_iota(jnp.int32, (n, n), 1)
    t = pred(r, c).astype(BF16)
    return jnp.concatenate([t, t], axis=0)


ANY = pl.BlockSpec(memory_space=pl.ANY)


class _Cargo:
    def __init__(self, arrays, out_shapes, aliases, nsem, start, finish):
        self.arrays, self.out_shapes, self.aliases, self.nsem = list(arrays), list(out_shapes), dict(aliases), nsem
        self.start, self.finish = start, finish


def _call_with_cargo(body, cargo, *, name, grid, in_specs, out_specs, out_shape, scratch_shapes, operands):
    sem = ("arbitrary",) * len(grid)
    if cargo is None:
        res = pl.pallas_call(body, name=name, grid=grid, in_specs=in_specs, out_specs=out_specs, out_shape=out_shape,
                             scratch_shapes=scratch_shapes, compiler_params=_cparams(sem))(*operands)
        return res, []
    n_in, n_out, n_scr = len(in_specs), len(out_specs), len(scratch_shapes)
    nci, nco = len(cargo.arrays), len(cargo.out_shapes)

    def wrapped(*refs):
        ins, refs = refs[:n_in], refs[n_in:]
        cin, refs = refs[:nci], refs[nci:]
        outs, refs = refs[:n_out], refs[n_out:]
        cout, refs = refs[:nco], refs[nco:]
        scr, (send, recv) = refs[:n_scr], refs[n_scr:]
        ids = [pl.program_id(a) for a in range(len(grid))]
        first = functools.reduce(lambda p, q: p & q, [g == 0 for g in ids])
        last = functools.reduce(lambda p, q: p & q, [g == n - 1 for g, n in zip(ids, grid)])

        @pl.when(first)
        def _():
            cargo.start(cin, cout, send, recv)

        body(*ins, *outs, *scr)

        @pl.when(last)
        def _():
            cargo.finish(cin, cout, send, recv)

    res = pl.pallas_call(
        wrapped, name=name, grid=grid, in_specs=list(in_specs) + [ANY] * nci, out_specs=list(out_specs) + [ANY] * nco,
        out_shape=list(out_shape) + cargo.out_shapes,
        scratch_shapes=list(scratch_shapes) + [pltpu.SemaphoreType.DMA((cargo.nsem,)), pltpu.SemaphoreType.DMA((cargo.nsem,))],
        input_output_aliases={n_in + k: n_out + v for k, v in cargo.aliases.items()},
        compiler_params=_cparams(sem))(*operands, *cargo.arrays)
    return res[:n_out], res[n_out:]


def _run_cargo(name, cargo):
    nci = len(cargo.arrays)

    def body(*refs):
        cin, cout, (send, recv) = refs[:nci], refs[nci:-2], refs[-2:]
        cargo.start(cin, cout, send, recv)
        cargo.finish(cin, cout, send, recv)

    return pl.pallas_call(
        body, name=name, in_specs=[ANY] * nci, out_specs=[ANY] * len(cargo.out_shapes), out_shape=cargo.out_shapes,
        scratch_shapes=[pltpu.SemaphoreType.DMA((cargo.nsem,)), pltpu.SemaphoreType.DMA((cargo.nsem,))],
        input_output_aliases=cargo.aliases)(*cargo.arrays)


def _sb_fwd(proj, cargo=None):
    S = proj.shape[0]
    tq, tk = min(SB_TQ, S), SB_TK
    assert S // SB_TRI <= HEAD
    nq = S // tq
    nkb = tq // tk
    nsub = tk // SB_TRI
    scale = HEAD ** -0.5

    def body(q_ref, k_ref, v_ref, g_ref, o_ref, y_ref, r_ref, kb_ref, vb_ref):
        i = pl.program_id(1)

        @pl.when(i == 0)
        def _():
            kb_ref[...] = k_ref[...].astype(BF16)
            vb_ref[...] = v_ref[...].astype(BF16)

        qb = q_ref[...].astype(BF16)
        qpos = i * tq + lax.broadcasted_iota(jnp.int32, (tq, tk), 0)
        kidx = lax.broadcasted_iota(jnp.int32, (tq, tk), 1)
        lane = lax.broadcasted_iota(jnp.int32, (tq, HEAD), 1)
        upper2 = _stacked_triangle(SB_TRI, lambda r, c: r > c)

        def block(kblock, masked, carry):
            acc, run, runs = carry
            off = pl.multiple_of(kblock * tk, tk)
            zz = _dot(qb, kb_ref[pl.ds(off, tk), :], NT) * scale
            sp = _softplus(zz)
            if masked:
                keep = (off + kidx) < qpos
                sp_sum = jnp.where(keep, sp, 0.0)
            else:
                sp_sum = sp
            hi, lo = _split2(sp_sum)
            right, after = [None] * nsub, [None] * nsub
            for j in reversed(range(nsub)):
                cols = slice(j * SB_TRI, (j + 1) * SB_TRI)
                right[j] = run
                after[j] = _dot(jnp.concatenate([hi[:, cols], lo[:, cols]], axis=1), upper2, NN) + run
                run = run + jnp.sum(sp_sum[:, cols], axis=1, keepdims=True)
                runs = jnp.where(lane == kblock * nsub + j, right[j], runs)
            w = jnp.exp((zz - sp) - jnp.concatenate(after, axis=1))
            if masked:
                w = jnp.where(keep, w, 0.0)
            return acc + _dot(w, vb_ref[pl.ds(off, tk), :], NN), run, runs

        runs0 = jnp.where(qpos[:, :HEAD] + lane < 0, 1.0, 0.0)
        carry = (jnp.zeros((tq, HEAD), F32), jnp.zeros((tq, 1), F32), runs0)
        for kbl in reversed(range(nkb)):
            carry = block(i * nkb + kbl, True, carry)
        acc, _, runs = lax.fori_loop(0, i * nkb, lambda n, c: block(i * nkb - 1 - n, False, c), carry)
        o_ref[...] = acc
        r_ref[...] = runs
        gate, _ = _silu_and_grad(g_ref[...])
        y_ref[...] = (acc * gate).astype(BF16)

    return _call_with_cargo(
        body, cargo, name="sb_fwd", grid=(NHEAD, nq),
        in_specs=[pl.BlockSpec((tq, HEAD), lambda h, i: (i, C_AQ * NHEAD + h)),
                  pl.BlockSpec((S, HEAD), lambda h, i: (0, C_AK * NHEAD + h)),
                  pl.BlockSpec((S, HEAD), lambda h, i: (0, C_AV * NHEAD + h)),
                  pl.BlockSpec((tq, HEAD), lambda h, i: (i, C_AG * NHEAD + h))],
        out_specs=[pl.BlockSpec((tq, HEAD), lambda h, i: (i, h))] * 3,
        out_shape=[jax.ShapeDtypeStruct((S, BW), F32), jax.ShapeDtypeStruct((S, BW), BF16),
                   jax.ShapeDtypeStruct((S, BW), F32)],
        scratch_shapes=[pltpu.VMEM((S, HEAD), BF16), pltpu.VMEM((S, HEAD), BF16)],
        operands=(proj, proj, proj, proj))


def _sb_bwd(proj, att, runs, dy, cargo=None):
    S = proj.shape[0]
    tq, tk = min(SB_TQ_BWD, S), SB_TK
    nq = S // tq
    nkb = tq // tk
    nsub = tk // SB_TRI
    scale = HEAD ** -0.5

    def body(q_ref, k_ref, v_ref, g_ref, o_ref, r_ref, dy_ref, dq_ref, dk_ref, dv_ref, dg_ref, kb_ref, vb_ref):
        i = pl.program_id(1)

        @pl.when(i == 0)
        def _():
            kb_ref[...] = k_ref[...].astype(BF16)
            vb_ref[...] = v_ref[...].astype(BF16)
            dk_ref[...] = jnp.zeros(dk_ref.shape, F32)
            dv_ref[...] = jnp.zeros(dv_ref.shape, F32)

        gate, dgate = _silu_and_grad(g_ref[...])
        dyv = dy_ref[...]
        dg_ref[...] = dyv * o_ref[...] * dgate
        dob = (dyv * gate).astype(BF16)
        qb = q_ref[...].astype(BF16)
        runs = r_ref[...]
        qpos = i * tq + lax.broadcasted_iota(jnp.int32, (tq, tk), 0)
        kidx = lax.broadcasted_iota(jnp.int32, (tq, tk), 1)
        lane = lax.broadcasted_iota(jnp.int32, (tq, HEAD), 1)
        upper2 = _stacked_triangle(SB_TRI, lambda r, c: r > c)
        lower2 = _stacked_triangle(SB_TRI, lambda r, c: r < c)

        def block(kblock, masked, carry):
            dq, grun = carry
            off = pl.multiple_of(kblock * tk, tk)
            kblk = kb_ref[pl.ds(off, tk), :]
            zz = _dot(qb, kblk, NT) * scale
            sp = _softplus(zz)
            lb = zz - sp
            if masked:
                keep = (off + kidx) < qpos
                sp_sum = jnp.where(keep, sp, 0.0)
            else:
                sp_sum = sp
            hi, lo = _split2(sp_sum)
            after = []
            for j in range(nsub):
                cols = slice(j * SB_TRI, (j + 1) * SB_TRI)
                run = jnp.sum(jnp.where(lane == kblock * nsub + j, runs, 0.0), axis=1, keepdims=True)
                after.append(_dot(jnp.concatenate([hi[:, cols], lo[:, cols]], axis=1), upper2, NN) + run)
            w = jnp.exp(lb - jnp.concatenate(after, axis=1))
            if masked:
                w = jnp.where(keep, w, 0.0)
            gw = _dot(dob, vb_ref[pl.ds(off, tk), :], NT) * w
            ghi, glo = _split2(gw)
            gsum = []
            for j in range(nsub):
                cols = slice(j * SB_TRI, (j + 1) * SB_TRI)
                gsum.append(_dot(jnp.concatenate([ghi[:, cols], glo[:, cols]], axis=1), lower2, NN) + grun)
                grun = grun + jnp.sum(gw[:, cols], axis=1, keepdims=True)
            dz = (gw - (gw + jnp.concatenate(gsum, axis=1)) * jnp.exp(lb)) * scale
            if masked:
                dz = jnp.where(keep, dz, 0.0)
            dzb = dz.astype(BF16)
            dk_ref[pl.ds(off, tk), :] += _dot(dzb, qb, TN)
            dv_ref[pl.ds(off, tk), :] += _dot(w, dob, TN)
            return dq + _dot(dzb, kblk, NN), grun

        carry = lax.fori_loop(0, i * nkb, lambda n, c: block(n, False, c),
                              (jnp.zeros((tq, HEAD), F32), jnp.zeros((tq, 1), F32)))
        for kbl in range(nkb):
            carry = block(i * nkb + kbl, True, carry)
        dq_ref[...] = carry[0]

    blk = lambda h, i: (i, h)
    head = lambda h, i: (0, h)
    return _call_with_cargo(
        body, cargo, name="sb_bwd", grid=(NHEAD, nq),
        in_specs=[pl.BlockSpec((tq, HEAD), lambda h, i: (i, C_AQ * NHEAD + h)),
                  pl.BlockSpec((S, HEAD), lambda h, i: (0, C_AK * NHEAD + h)),
                  pl.BlockSpec((S, HEAD), lambda h, i: (0, C_AV * NHEAD + h)),
                  pl.BlockSpec((tq, HEAD), lambda h, i: (i, C_AG * NHEAD + h)),
                  pl.BlockSpec((tq, HEAD), blk), pl.BlockSpec((tq, HEAD), blk), pl.BlockSpec((tq, HEAD), blk)],
        out_specs=[pl.BlockSpec((tq, HEAD), blk), pl.BlockSpec((S, HEAD), head), pl.BlockSpec((S, HEAD), head),
                   pl.BlockSpec((tq, HEAD), blk)],
        out_shape=[jax.ShapeDtypeStruct((S, BW), F32)] * 4,
        scratch_shapes=[pltpu.VMEM((S, HEAD), BF16), pltpu.VMEM((S, HEAD), BF16)],
        operands=(proj, proj, proj, proj, att, runs, dy))


CONV_TR = 256


def _conv_taps(buf_ref, w_ref, base, tr):
    acc = w_ref[0:1, :] * buf_ref[pl.ds(base, tr), :]
    for k in range(1, CONV_K):
        acc = acc + w_ref[k:k + 1, :] * buf_ref[pl.ds(base + k, tr), :]
    return acc


def _conv_fwd(proj, conv_w, conv_b, ln_g, ln_b):
    S = proj.shape[0]
    tr = min(CONV_TR, S)
    ni = S // tr

    def body(a_ref, b_ref, ap_ref, bp_ref, g_ref, w_ref, cb_ref, lg_ref, lb_ref, c_ref, y_ref, buf_ref):
        i = pl.program_id(0)
        prev = ap_ref[tr - HALO:, :] * _sigmoid(bp_ref[tr - HALO:, :])
        buf_ref[0:HALO, :] = jnp.where(i > 0, prev, 0.0)
        buf_ref[HALO:, :] = a_ref[...] * _sigmoid(b_ref[...])
        c = _conv_taps(buf_ref, w_ref, HALO - (CONV_K - 1), tr) + cb_ref[...]
        c_ref[...] = c
        yn, _, _ = _ln_fwd(c, lg_ref[...], lb_ref[...])
        act, _ = _silu_and_grad(yn)
        gate, _ = _silu_and_grad(g_ref[...])
        y_ref[...] = (act * gate).astype(BF16)

    cur = lambda k: (lambda i: (i, k))
    prv = lambda k: (lambda i: (jnp.maximum(i - 1, 0), k))
    full = lambda a: pl.BlockSpec(a.shape, lambda i: (0, 0))
    wpad = jnp.pad(conv_w, ((0, HALO - CONV_K), (0, 0)))
    small = [wpad, conv_b.reshape(1, BW), ln_g.reshape(1, BW), ln_b.reshape(1, BW)]
    return pl.pallas_call(
        body, name="conv_fwd", grid=(ni,),
        in_specs=[pl.BlockSpec((tr, BW), cur(C_BA)), pl.BlockSpec((tr, BW), cur(C_BB)),
                  pl.BlockSpec((tr, BW), prv(C_BA)), pl.BlockSpec((tr, BW), prv(C_BB)),
                  pl.BlockSpec((tr, BW), cur(C_BG))] + [full(a) for a in small],
        out_specs=[pl.BlockSpec((tr, BW), lambda i: (i, 0)), pl.BlockSpec((tr, BW), lambda i: (i, 0))],
        out_shape=[jax.ShapeDtypeStruct((S, BW), F32), jax.ShapeDtypeStruct((S, BW), BF16)],
        scratch_shapes=[pltpu.VMEM((tr + HALO, BW), F32)],
        compiler_params=_cparams(("arbitrary",)),
    )(proj, proj, proj, proj, proj, *small)


def _conv_bwd_norm(proj, cpre, dy, ln_g, ln_b):
    S = proj.shape[0]

    def fn(i, j, rv, bv):
        c, bg, dyv = rv
        lg, lb = bv
        yn, xh, rstd = _ln_fwd(c, lg, lb)
        act, dact = _silu_and_grad(yn)
        gate, dgate = _silu_and_grad(bg)
        dyn = dyv * gate * dact
        dc, dlg, dlb = _ln_bwd(dyn, xh, rstd, lg)
        return [dc, dyv * act * dgate], [dlg, dlb, jnp.sum(dc, axis=0, keepdims=True)]

    rows = [(cpre, BW, _c(0), 0), (proj, BW, _c(C_BG), 0), (dy, BW, _c(1), 0)]
    return _rowwise("conv_bwd_norm", fn, rows, [ln_g.reshape(1, BW), ln_b.reshape(1, BW)],
                    [(BW, BW, _c(0), F32), (BW, BW, _c(0), BF16)],
                    [(1, BW, BW, _c(0))] * 3, nrows=S, tr=256)


def _conv_bwd_taps(proj, dc, conv_w):
    S = proj.shape[0]
    tr = min(CONV_TR, S)
    ni = S // tr

    def body(a_ref, b_ref, ap_ref, bp_ref, dc_ref, dcn_ref, w_ref, da_ref, db_ref, dw_ref, gbuf, dbuf, wrev):
        i = pl.program_id(0)

        @pl.when(i == 0)
        def _():
            dw_ref[...] = jnp.zeros(dw_ref.shape, F32)
            for k in range(CONV_K):
                wrev[k:k + 1, :] = w_ref[CONV_K - 1 - k:CONV_K - k, :]

        sb = _sigmoid(b_ref[...])
        av = a_ref[...]
        prev = ap_ref[tr - HALO:, :] * _sigmoid(bp_ref[tr - HALO:, :])
        gbuf[0:HALO, :] = jnp.where(i > 0, prev, 0.0)
        gbuf[HALO:, :] = av * sb
        dcv = dc_ref[...]
        dbuf[0:tr, :] = dcv
        dbuf[tr:, :] = jnp.where(i < ni - 1, dcn_ref[0:HALO, :], 0.0)
        dg = _conv_taps(dbuf, wrev, 0, tr)
        da_ref[...] = (dg * sb).astype(BF16)
        db_ref[...] = (dg * av * sb * (1.0 - sb)).astype(BF16)
        base = HALO - (CONV_K - 1)
        for k in range(CONV_K):
            dw_ref[k:k + 1, :] += jnp.sum(dcv * gbuf[pl.ds(base + k, tr), :], axis=0, keepdims=True)

    cur = lambda k: (lambda i: (i, k))
    prv = lambda k: (lambda i: (jnp.maximum(i - 1, 0), k))
    wpad = jnp.pad(conv_w, ((0, HALO - CONV_K), (0, 0)))
    return pl.pallas_call(
        body, name="conv_bwd_taps", grid=(ni,),
        in_specs=[pl.BlockSpec((tr, BW), cur(C_BA)), pl.BlockSpec((tr, BW), cur(C_BB)),
                  pl.BlockSpec((tr, BW), prv(C_BA)), pl.BlockSpec((tr, BW), prv(C_BB)),
                  pl.BlockSpec((tr, BW), lambda i: (i, 0)),
                  pl.BlockSpec((tr, BW), lambda i: (jnp.minimum(i + 1, ni - 1), 0)),
                  pl.BlockSpec((HALO, BW), lambda i: (0, 0))],
        out_specs=[pl.BlockSpec((tr, BW), lambda i: (i, 0)), pl.BlockSpec((tr, BW), lambda i: (i, 0)),
                   pl.BlockSpec((HALO, BW), lambda i: (0, 0))],
        out_shape=[jax.ShapeDtypeStruct((S, BW), BF16), jax.ShapeDtypeStruct((S, BW), BF16),
                   jax.ShapeDtypeStruct((HALO, BW), F32)],
        scratch_shapes=[pltpu.VMEM((tr + HALO, BW), F32), pltpu.VMEM((tr + HALO, BW), F32),
                        pltpu.VMEM((HALO, BW), F32)],
        compiler_params=_cparams(("arbitrary",)),
    )(proj, proj, proj, proj, dc, dc, wpad)


SGU_TR = 256


def _sgu_mix(wm, vn, bias):
    tr = vn.shape[0]
    rows = []
    for n in range(tr // HEAD):
        cols = []
        for g in range(NHEAD):
            blk = vn[n * HEAD:(n + 1) * HEAD, g * HEAD:(g + 1) * HEAD]
            cols.append(_dot(wm[g], blk, NN) + bias[g * HEAD:(g + 1) * HEAD, :])
        rows.append(jnp.concatenate(cols, axis=1))
    return jnp.concatenate(rows, axis=0)


def _sgu_masked_w(w2d):
    rr = lax.broadcasted_iota(jnp.int32, (HEAD, HEAD), 0)
    cc = lax.broadcasted_iota(jnp.int32, (HEAD, HEAD), 1)
    tril = rr >= cc
    return [jnp.where(tril, w2d[g * HEAD:(g + 1) * HEAD, :], 0.0).astype(BF16) for g in range(NHEAD)], tril


def _sgu_inputs(sgu_w, sgu_b, ln_g, ln_b):
    w2d = sgu_w.reshape(NHEAD * HEAD, HEAD)
    bias = jnp.broadcast_to(sgu_b[:, :, None], (NHEAD, HEAD, HEAD)).reshape(NHEAD * HEAD, HEAD)
    return [w2d, bias, ln_g.reshape(1, BW), ln_b.reshape(1, BW)]


def _sgu_fwd(proj, sgu_w, sgu_b, ln_g, ln_b):
    S = proj.shape[0]

    def fn(i, j, rv, bv):
        cu, cv, cg = rv
        w2d, bias, lg, lb = bv
        wm, _ = _sgu_masked_w(w2d)
        u, _ = _gelu_and_grad(cu)
        v, _ = _gelu_and_grad(cv)
        vn, _, _ = _ln_fwd(v, lg, lb)
        z = _sgu_mix(wm, vn, bias)
        gate, _ = _silu_and_grad(cg)
        return [u * z * gate], []

    rows = [(proj, BW, _c(C_CU), 0), (proj, BW, _c(C_CV), 0), (proj, BW, _c(C_CG), 0)]
    return _rowwise("sgu_fwd", fn, rows, _sgu_inputs(sgu_w, sgu_b, ln_g, ln_b), [(BW, BW, _c(0), BF16)], [],
                    nrows=S, tr=min(SGU_TR, S))[0]


def _sgu_bwd(proj, dy, sgu_w, sgu_b, ln_g, ln_b):
    S = proj.shape[0]

    def fn(i, j, rv, bv):
        cu, cv, cg, dyv = rv
        w2d, bias, lg, lb = bv
        wm, tril = _sgu_masked_w(w2d)
        u, du_dcu = _gelu_and_grad(cu)
        v, dv_dcv = _gelu_and_grad(cv)
        vn, xh, rstd = _ln_fwd(v, lg, lb)
        z = _sgu_mix(wm, vn, bias)
        gate, dgate = _silu_and_grad(cg)
        dz = dyv * u * gate
        dcu = dyv * z * gate * du_dcu
        dcg = dyv * u * z * dgate
        tr = dz.shape[0]
        dvn_rows = []
        dw = [jnp.zeros((HEAD, HEAD), F32) for _ in range(NHEAD)]
        dbias = [jnp.zeros((HEAD, HEAD), F32) for _ in range(NHEAD)]
        for n in range(tr // HEAD):
            cols = []
            for g in range(NHEAD):
                dzb = dz[n * HEAD:(n + 1) * HEAD, g * HEAD:(g + 1) * HEAD]
                vnb = vn[n * HEAD:(n + 1) * HEAD, g * HEAD:(g + 1) * HEAD]
                cols.append(_dot(wm[g], dzb, TN))
                dw[g] = dw[g] + _dot(dzb, vnb, NT)
                dbias[g] = dbias[g] + dzb
            dvn_rows.append(jnp.concatenate(cols, axis=1))
        dvn = jnp.concatenate(dvn_rows, axis=0)
        dv, dlg, dlb = _ln_bwd(dvn, xh, rstd, lg)
        dw2d = jnp.concatenate([jnp.where(tril, d, 0.0) for d in dw], axis=0)
        return [dcu, dv * dv_dcv, dcg], [dw2d, jnp.concatenate(dbias, axis=0), dlg, dlb]

    rows = [(proj, BW, _c(C_CU), 0), (proj, BW, _c(C_CV), 0), (proj, BW, _c(C_CG), 0), (dy, BW, _c(2), 0)]
    return _rowwise("sgu_bwd", fn, rows, _sgu_inputs(sgu_w, sgu_b, ln_g, ln_b),
                    [(BW, BW, _c(0), BF16)] * 3,
                    [(NHEAD * HEAD, HEAD, HEAD, _c(0)), (NHEAD * HEAD, HEAD, HEAD, _c(0)), (1, BW, BW, _c(0)),
                     (1, BW, BW, _c(0))], nrows=S, tr=min(SGU_TR, S))


NEG = -1e30
NCOL = IN_WIDTH // BW


def _band_masks():
    a = lax.broadcasted_iota(jnp.int32, (HEAD, HEAD), 0)
    c = lax.broadcasted_iota(jnp.int32, (HEAD, HEAD), 1)
    return a >= c, a <= c


def _dil_view(proj, grp):
    S = proj.shape[0]
    dil = DIL_PATTERNS[grp][1]
    if dil == 1:
        return proj, NCOL, C_DQ + grp, C_DK + grp, C_DV
    cols = [proj[:, c * BW:(c + 1) * BW] for c in (C_DQ + grp, C_DK + grp, C_DV)]
    return jnp.concatenate(cols, axis=1).reshape(S // dil, dil * 3 * BW), 3, 0, 1, 2


def _dil_fwd(proj, grp):
    S = proj.shape[0]
    dil = DIL_PATTERNS[grp][1]
    L = S // dil
    nb = L // HEAD
    proj2, ncol, cq, ck, cv = _dil_view(proj, grp)
    scale = HEAD ** -0.5

    def body(q_ref, kc_ref, kp_ref, vc_ref, vp_ref, o_ref, l_ref):
        b = pl.program_id(1)
        m_cur, m_prev = _band_masks()
        m_prev = m_prev & (b > 0)
        for h in range(NHEAD):
            sl = slice(h * HEAD, (h + 1) * HEAD)
            q = q_ref[:, sl].astype(BF16)
            s_c = jnp.where(m_cur, _dot(q, kc_ref[:, sl], NT) * scale, NEG)
            s_p = jnp.where(m_prev, _dot(q, kp_ref[:, sl], NT) * scale, NEG)
            m = jnp.maximum(jnp.max(s_c, axis=1, keepdims=True), jnp.max(s_p, axis=1, keepdims=True))
            p_c = jnp.exp(s_c - m)
            p_p = jnp.exp(s_p - m)
            den = jnp.sum(p_c, axis=1, keepdims=True) + jnp.sum(p_p, axis=1, keepdims=True)
            o = (_dot(p_c, vc_ref[:, sl], NN) + _dot(p_p, vp_ref[:, sl], NN)) / den
            o_ref[:, sl] = o
            l_ref[:, sl] = jnp.broadcast_to(m + jnp.log(den), (HEAD, HEAD))

    cur = lambda col: (lambda r, b: (b, r * ncol + col))
    prv = lambda col: (lambda r, b: (jnp.maximum(b - 1, 0), r * ncol + col))
    out_map = lambda r, b: (b, r)
    o, lse = pl.pallas_call(
        body, name=f"dil_fwd{grp}", grid=(dil, nb),
        in_specs=[pl.BlockSpec((HEAD, BW), cur(cq)),
                  pl.BlockSpec((HEAD, BW), cur(ck)), pl.BlockSpec((HEAD, BW), prv(ck)),
                  pl.BlockSpec((HEAD, BW), cur(cv)), pl.BlockSpec((HEAD, BW), prv(cv))],
        out_specs=[pl.BlockSpec((HEAD, BW), out_map), pl.BlockSpec((HEAD, BW), out_map)],
        out_shape=[jax.ShapeDtypeStruct((L, dil * BW), F32), jax.ShapeDtypeStruct((L, dil * BW), F32)],
        compiler_params=_cparams(("arbitrary", "arbitrary")),
    )(proj2, proj2, proj2, proj2, proj2)
    return o.reshape(S, BW), lse.reshape(S, BW)


def _dil_combine(proj, os_, lses):
    S = proj.shape[0]

    def fn(i, j, rv, bv):
        o1, o2, o3, l1, l2, l3, dg = rv
        m = jnp.maximum(jnp.maximum(l1, l2), l3)
        e1, e2, e3 = jnp.exp(l1 - m), jnp.exp(l2 - m), jnp.exp(l3 - m)
        den = e1 + e2 + e3
        out = (e1 * o1 + e2 * o2 + e3 * o3) / den
        gate, _ = _silu_and_grad(dg)
        return [out, m + jnp.log(den), out * gate], []

    rows = [(a, BW, _c(0), 0) for a in (*os_, *lses)] + [(proj, BW, _c(C_DG), 0)]
    return _rowwise("dil_combine", fn, rows, [], [(BW, BW, _c(0), F32), (BW, BW, _c(0), F32), (BW, BW, _c(0), BF16)],
                    [], nrows=S, tr=256)


def _dil_gate_bwd(proj, att, dy):
    S = proj.shape[0]

    def fn(i, j, rv, bv):
        dg, av, dyv = rv
        gate, dgate = _silu_and_grad(dg)
        dout = dyv * gate
        prod = dout * av
        tr = prod.shape[0]
        delta = jnp.concatenate(
            [jnp.broadcast_to(jnp.sum(prod[:, h * HEAD:(h + 1) * HEAD], axis=1, keepdims=True), (tr, HEAD))
             for h in range(NHEAD)], axis=1)
        return [dout, dyv * av * dgate, delta], []

    rows = [(proj, BW, _c(C_DG), 0), (att, BW, _c(0), 0), (dy, BW, _c(3), 0)]
    return _rowwise("dil_gate_bwd", fn, rows, [], [(BW, BW, _c(0), F32), (BW, BW, _c(0), BF16), (BW, BW, _c(0), F32)],
                    [], nrows=S, tr=256)


def _dil_bwd(proj, dout, lse, delta, grp):
    S = proj.shape[0]
    dil = DIL_PATTERNS[grp][1]
    L = S // dil
    nb = L // HEAD
    proj2, ncol, cq, ck, cv = _dil_view(proj, grp)
    view = lambda a: a.reshape(L, dil * BW)
    scale = HEAD ** -0.5

    def body(q_ref, qn_ref, kc_ref, kp_ref, vc_ref, vp_ref, do_ref, don_ref, l_ref, ln_ref, d_ref, dn_ref,
             dq_ref, dk_ref, dv_ref):
        b = pl.program_id(1)
        m_cur, m_band = _band_masks()
        m_prev = m_band & (b > 0)
        m_next = m_band & (b < nb - 1)
        for h in range(NHEAD):
            sl = slice(h * HEAD, (h + 1) * HEAD)
            q, qn = q_ref[:, sl].astype(BF16), qn_ref[:, sl].astype(BF16)
            kc, kp = kc_ref[:, sl].astype(BF16), kp_ref[:, sl].astype(BF16)
            vc, vp = vc_ref[:, sl].astype(BF16), vp_ref[:, sl].astype(BF16)
            do, don = do_ref[:, sl].astype(BF16), don_ref[:, sl].astype(BF16)
            lse_c, lse_n = l_ref[:, sl], ln_ref[:, sl]
            dl_c, dl_n = d_ref[:, sl], dn_ref[:, sl]
            p_cc = jnp.exp(jnp.where(m_cur, _dot(q, kc, NT) * scale - lse_c, NEG))
            p_cp = jnp.exp(jnp.where(m_prev, _dot(q, kp, NT) * scale - lse_c, NEG))
            p_nc = jnp.exp(jnp.where(m_next, _dot(qn, kc, NT) * scale - lse_n, NEG))
            ds_cc = p_cc * (_dot(do, vc, NT) - dl_c)
            ds_cp = p_cp * (_dot(do, vp, NT) - dl_c)
            ds_nc = p_nc * (_dot(don, vc, NT) - dl_n)
            dq_ref[:, sl] = ((_dot(ds_cc, kc, NN) + _dot(ds_cp, kp, NN)) * scale).astype(BF16)
            dk_ref[:, sl] = ((_dot(ds_cc, q, TN) + _dot(ds_nc, qn, TN)) * scale).astype(BF16)
            dv_ref[:, sl] = _dot(p_cc, do, TN) + _dot(p_nc, don, TN)

    cur = lambda col: (lambda r, b: (b, r * ncol + col))
    prv = lambda col: (lambda r, b: (jnp.maximum(b - 1, 0), r * ncol + col))
    nxt = lambda col: (lambda r, b: (jnp.minimum(b + 1, nb - 1), r * ncol + col))
    o_cur = lambda r, b: (b, r)
    o_nxt = lambda r, b: (jnp.minimum(b + 1, nb - 1), r)
    blk = (HEAD, BW)
    dq, dk, dv = pl.pallas_call(
        body, name=f"dil_bwd{grp}", grid=(dil, nb),
        in_specs=[pl.BlockSpec(blk, cur(cq)), pl.BlockSpec(blk, nxt(cq)),
                  pl.BlockSpec(blk, cur(ck)), pl.BlockSpec(blk, prv(ck)),
                  pl.BlockSpec(blk, cur(cv)), pl.BlockSpec(blk, prv(cv)),
                  pl.BlockSpec(blk, o_cur), pl.BlockSpec(blk, o_nxt),
                  pl.BlockSpec(blk, o_cur), pl.BlockSpec(blk, o_nxt),
                  pl.BlockSpec(blk, o_cur), pl.BlockSpec(blk, o_nxt)],
        out_specs=[pl.BlockSpec(blk, o_cur)] * 3,
        out_shape=[jax.ShapeDtypeStruct((L, dil * BW), BF16), jax.ShapeDtypeStruct((L, dil * BW), BF16),
                   jax.ShapeDtypeStruct((L, dil * BW), F32)],
        compiler_params=_cparams(("arbitrary", "arbitrary")),
    )(proj2, proj2, proj2, proj2, proj2, proj2, view(dout), view(dout), view(lse), view(lse), view(delta), view(delta))
    return dq.reshape(S, BW), dk.reshape(S, BW), dv.reshape(S, BW)


def _add3(a, b, c):
    S, W = a.shape

    def fn(i, j, rv, bv):
        return [rv[0] + rv[1] + rv[2]], []

    return _rowwise("add3", fn, [(a, W, _c(0), 0), (b, W, _c(0), 0), (c, W, _c(0), 0)], [], [(W, W, _c(0), BF16)], [],
                    nrows=S, tr=512)[0]


def _layer_fwd(x, p, cargo=None):
    h = _rms_fwd(x, p["norm_g"])
    proj = _mm_nn("in_proj", h, p["w_in"], F32)
    gpre = _mm_nn("gate_proj", h, p["w_gate"], F32, bias=p["b_gate"].reshape(1, GATE_W))
    (att_a, ya, runs_a), carried = _sb_fwd(proj, cargo)
    cpre, yb = _conv_fwd(proj, p["conv_w"], p["conv_b"], p["conv_ln_g"], p["conv_ln_b"])
    yc = _sgu_fwd(proj, p["sgu_w"], p["sgu_b"], p["sgu_ln_g"], p["sgu_ln_b"])
    dil = [_dil_fwd(proj, g) for g in range(3)]
    att_d, lse_d, yd = _dil_combine(proj, [d[0] for d in dil], [d[1] for d in dil])
    y = jnp.concatenate([ya, yb, yc, yd], axis=1)
    yproj = _mm_nn("branch_proj", y, p["w_branch"], F32, groups=4)
    merged = _merge_fwd(gpre, yproj)
    x_next = _mm_nn("out_proj", merged, p["w_out"], F32, res=x)
    saved = dict(x=x, h=h, proj=proj, gpre=gpre, att_a=att_a, runs_a=runs_a, cpre=cpre, att_d=att_d, lse_d=lse_d, y=y,
                 yproj=yproj, merged=merged)
    return x_next, saved, carried


def _layer_bwd(dout, s, p, pending=None):
    proj = s["proj"]
    dmerged = _mm_nt("out_proj_dx", dout, p["w_out"], F32)
    g_w_out = _mm_tn("out_proj_dw", s["merged"], dout)
    if pending is None:
        dyproj, dgpre, g_b_gate = _merge_bwd(dmerged, s["gpre"], s["yproj"])
        cargo = None
    else:
        grads_above, axes, scalars = pending
        (dyproj, dgpre, g_b_gate), got = _merge_bwd(dmerged, s["gpre"], s["yproj"], _swap_cargo(grads_above, axes))
        sums = _reduce_sums(grads_above, got, axes, scalars)
        cargo = _scatter_cargo(sums, axes)
    dy = _mm_nt("branch_proj_dx", dyproj, p["w_branch"], F32, groups=4)
    g_w_branch = _mm_tn("branch_proj_dw", s["y"], dyproj, groups=4)
    (d_aq, d_ak, d_av, d_ag), parts = _sb_bwd(proj, s["att_a"], s["runs_a"], dy, cargo)
    dc, d_bg, g_cln_g, g_cln_b, g_conv_b = _conv_bwd_norm(proj, s["cpre"], dy, p["conv_ln_g"], p["conv_ln_b"])
    d_ba, d_bb, g_conv_w = _conv_bwd_taps(proj, dc, p["conv_w"])
    d_cu, d_cv, d_cg, g_sgu_w, g_sgu_bias, g_sln_g, g_sln_b = _sgu_bwd(
        proj, dy, p["sgu_w"], p["sgu_b"], p["sgu_ln_g"], p["sgu_ln_b"])
    dout_d, d_dg, delta = _dil_gate_bwd(proj, s["att_d"], dy)
    dil = [_dil_bwd(proj, dout_d, s["lse_d"], delta, g) for g in range(3)]
    d_dv = _add3(dil[0][2], dil[1][2], dil[2][2])
    pieces = [d_aq, d_ak, d_av, d_ag, d_ba, d_bb, d_bg, d_cu, d_cv, d_cg,
              dil[0][0], dil[1][0], dil[2][0], dil[0][1], dil[1][1], dil[2][1], d_dv, d_dg]
    dproj = jnp.concatenate([t.astype(BF16) for t in pieces], axis=1)
    dh_gate = _mm_nt("gate_proj_dx", dgpre, p["w_gate"], F32)
    dh = _mm_nt("in_proj_dx", dproj, p["w_in"], F32, res=dh_gate)
    g_w_in = _mm_tn("in_proj_dw", s["h"], dproj)
    g_w_gate = _mm_tn("gate_proj_dw", s["h"], dgpre)
    dx, g_norm_g = _rms_bwd(s["x"], dh, dout, p["norm_g"])
    grads = dict(
        norm_g=g_norm_g.reshape(D_MODEL), w_in=g_w_in, conv_w=g_conv_w[:CONV_K], conv_b=g_conv_b.reshape(BW),
        conv_ln_g=g_cln_g.reshape(BW), conv_ln_b=g_cln_b.reshape(BW), sgu_ln_g=g_sln_g.reshape(BW),
        sgu_ln_b=g_sln_b.reshape(BW), sgu_w=g_sgu_w.reshape(NHEAD, HEAD, HEAD),
        sgu_b=jnp.sum(g_sgu_bias.reshape(NHEAD, HEAD, HEAD), axis=-1), w_branch=g_w_branch,
        w_gate=g_w_gate, b_gate=g_b_gate.reshape(GATE_W), w_out=g_w_out)
    return dx, grads, (None if pending is None else (sums, parts))


CHIP_FLIPS =((1, 0), (0, 1), (1, 1))


def _at(ref, axis, start, size):
    idx = [slice(None)] * len(ref.shape)
    idx[axis] = pl.ds(start, size)
    return ref.at[tuple(idx)]


def _position():
    return lax.axis_index("x"), lax.axis_index("y"), lax.axis_index("c")


def _gather(name, w, axis):
    n = w.shape[axis]
    full = list(w.shape)
    full[axis] = 4 * n

    def body(w_ref, out_ref, send_sems, recv_sems, local_sem):
        x, y, c = _position()
        mine = pltpu.make_async_copy(w_ref, _at(out_ref, axis, (2 * x + y) * n, n), local_sem)
        mine.start()
        sends = []
        for k, (fx, fy) in enumerate(CHIP_FLIPS):
            cp = pltpu.make_async_remote_copy(
                src_ref=w_ref, dst_ref=_at(out_ref, axis, (2 * x + y) * n, n), send_sem=send_sems.at[k],
                recv_sem=recv_sems.at[k], device_id=(x ^ fx, y ^ fy, c), device_id_type=MESH)
            cp.start()
            sends.append(cp)
        for k, (fx, fy) in enumerate(CHIP_FLIPS):
            px, py = x ^ fx, y ^ fy
            pltpu.make_async_remote_copy(
                src_ref=w_ref, dst_ref=_at(out_ref, axis, (2 * px + py) * n, n), send_sem=send_sems.at[k],
                recv_sem=recv_sems.at[k], device_id=(px, py, c), device_id_type=MESH).wait_recv()
        for cp in sends:
            cp.wait_send()
        mine.wait()

    return pl.pallas_call(
        body, name=name, in_specs=[ANY], out_specs=ANY, out_shape=jax.ShapeDtypeStruct(tuple(full), w.dtype),
        scratch_shapes=[pltpu.SemaphoreType.DMA((3,)), pltpu.SemaphoreType.DMA((3,)), pltpu.SemaphoreType.DMA],
    )(w)


def _block(ref, ax, shard, half):
    if shard is not None:
        n = ref.shape[ax] // 4
        ref = _at(ref, ax, shard * n, n)
    if half is not None:
        nh = ref.shape[1 - ax] // 2
        ref = _at(ref, 1 - ax, half * nh, nh)
    return ref


def _remote(src, dst, send_sems, recv_sems, k, device):
    return pltpu.make_async_remote_copy(src_ref=src, dst_ref=dst, send_sem=send_sems.at[k], recv_sem=recv_sems.at[k],
                                        device_id=device, device_id_type=MESH)


def _gather_cargo(fulls, axes):
    na = len(fulls)

    def start(ins, outs, send, recv):
        x, y, c = _position()
        for a in range(na):
            mine = _block(outs[a], axes[a], 2 * x + y, c)
            for k, (fx, fy) in enumerate(CHIP_FLIPS):
                _remote(mine, mine, send, recv, 3 * a + k, (x ^ fx, y ^ fy, c)).start()

    def finish(ins, outs, send, recv):
        x, y, c = _position()
        sib = (x, y, 1 - c)
        for a in range(na):
            for k, (fx, fy) in enumerate(CHIP_FLIPS):
                got = _block(outs[a], axes[a], 2 * (x ^ fx) + (y ^ fy), c)
                _remote(got, got, send, recv, 3 * a + k, sib).wait_recv()
                _remote(got, got, send, recv, 3 * (na + a) + k, sib).start()
        for a in range(na):
            mine = _block(outs[a], axes[a], 2 * x + y, c)
            for k, (fx, fy) in enumerate(CHIP_FLIPS):
                passed = _block(outs[a], axes[a], 2 * (x ^ fx) + (y ^ fy), c)
                theirs = _block(outs[a], axes[a], 2 * (x ^ fx) + (y ^ fy), 1 - c)
                _remote(theirs, theirs, send, recv, 3 * (na + a) + k, sib).wait_recv()
                _remote(passed, passed, send, recv, 3 * (na + a) + k, sib).wait_send()
                _remote(mine, mine, send, recv, 3 * a + k, sib).wait_send()

    shapes = [jax.ShapeDtypeStruct(f.shape, f.dtype) for f in fulls]
    return _Cargo(fulls, shapes, {a: a for a in range(na)}, 6 * na, start, finish)


def _scatter_cargo(sums, axes):
    na = len(sums)
    shapes = []
    for s, ax in zip(sums, axes):
        piece = list(s.shape)
        piece[ax] //= 4
        shapes.append(jax.ShapeDtypeStruct((3, *piece), s.dtype))

    def copies(ins, outs, send, recv):
        x, y, c = _position()
        return [_remote(_block(ins[a], axes[a], 2 * (x ^ fx) + (y ^ fy), None), outs[a].at[k], send, recv, 3 * a + k,
                        (x ^ fx, y ^ fy, c))
                for a in range(na) for k, (fx, fy) in enumerate(CHIP_FLIPS)]

    def start(ins, outs, send, recv):
        for cp in copies(ins, outs, send, recv):
            cp.start()

    def finish(ins, outs, send, recv):
        for cp in copies(ins, outs, send, recv):
            cp.wait()

    return _Cargo(sums, shapes, {}, 3 * na, start, finish)


def _swap_cargo(grads, axes):
    na = len(grads)
    shapes = []
    for g, ax in zip(grads, axes):
        half = list(g.shape)
        half[1 - ax] //= 2
        shapes.append(jax.ShapeDtypeStruct(tuple(half), g.dtype))

    def copies(ins, outs, send, recv):
        x, y, c = _position()
        return [_remote(_block(ins[a], axes[a], None, 1 - c), outs[a], send, recv, a, (x, y, 1 - c)) for a in range(na)]

    def start(ins, outs, send, recv):
        for cp in copies(ins, outs, send, recv):
            cp.start()

    def finish(ins, outs, send, recv):
        for cp in copies(ins, outs, send, recv):
            cp.wait()

    return _Cargo(grads, shapes, {}, na, start, finish)


def _join_halves(name, shards, axes):
    na = len(shards)

    def body(*refs):
        outs, (send, recv) = refs[na:2 * na], refs[2 * na:]
        x, y, c = _position()
        sib = (x, y, 1 - c)
        for a in range(na):
            mine = _block(outs[a], axes[a], None, c)
            _remote(mine, mine, send, recv, a, sib).start()
        for a in range(na):
            mine = _block(outs[a], axes[a], None, c)
            theirs = _block(outs[a], axes[a], None, 1 - c)
            _remote(theirs, theirs, send, recv, a, sib).wait_recv()
            _remote(mine, mine, send, recv, a, sib).wait_send()

    return pl.pallas_call(
        body, name=name, in_specs=[ANY] * na, out_specs=[ANY] * na,
        out_shape=[jax.ShapeDtypeStruct(s.shape, s.dtype) for s in shards],
        scratch_shapes=[pltpu.SemaphoreType.DMA((na,)), pltpu.SemaphoreType.DMA((na,))],
        input_output_aliases={a: a for a in range(na)},
    )(*shards)


def _sum_terms(name, terms, out_shape, out_dtype, blk, nblocks, out_map, scalars):
    tr, tc = blk

    def body(s_ref, *refs):
        acc = refs[0][...].astype(F32)
        for r in refs[1:-1]:
            acc = acc + r[...].astype(F32)
        refs[-1][...] = acc.astype(refs[-1].dtype)

    in_specs = [pl.BlockSpec((None, tr, tc) if a.ndim == 3 else (tr, tc), m) for a, m in terms]
    return pl.pallas_call(
        body, name=name,
        grid_spec=pltpu.PrefetchScalarGridSpec(
            num_scalar_prefetch=1, grid=(nblocks,), in_specs=in_specs, out_specs=pl.BlockSpec((tr, tc), out_map)),
        out_shape=jax.ShapeDtypeStruct(out_shape, out_dtype),
        compiler_params=_cparams(("arbitrary",)),
    )(scalars, *[a for a, _ in terms])


def _rows_for(cols, rows, budget=1 << 19):
    tr = 8
    while tr * 2 * cols <= budget and tr * 2 <= min(rows, 256) and rows % (tr * 2) == 0:
        tr *= 2
    assert rows % tr == 0
    return tr


def _place_shard(name, w, layer, ax, scalars):
    _, R, C = w.shape
    tr = _rows_for(C, R)
    nb = R // tr
    if ax == 0:
        shape, out_map = (4 * R, C), (lambda i, s: (s[1] * nb + i, 0))
    else:
        shape, out_map = (R, 4 * C), (lambda i, s: (i, s[1]))
    return _sum_terms(name, [(w, lambda i, s: (layer, i, 0))], shape, BF16, (tr, C), nb, out_map, scalars)


def _reduce_sums(grads, got, axes, scalars):
    sums = []
    for a, (g, h, ax) in enumerate(zip(grads, got, axes)):
        Rh, Ch = h.shape
        tr = _rows_for(Ch, Rh)
        nb = Rh // tr
        g_map = (lambda i, s, nb=nb: (s[0] * nb + i, 0)) if ax == 1 else (lambda i, s: (i, s[0]))
        sums.append(_sum_terms(f"rs_sum2_{a}", [(g, g_map), (h, lambda i, s: (i, 0))], (Rh, Ch), BF16, (tr, Ch), nb,
                               lambda i, s: (i, 0), scalars))
    return sums


def _reduce_end(sums, parts, axes, scalars):
    shards = []
    for a, (s1, pt, ax) in enumerate(zip(sums, parts, axes)):
        _, Rs, Cs = pt.shape
        tr = _rows_for(Cs, Rs)
        nb = Rs // tr
        if ax == 1:
            shape, s_map, o_map = (2 * Rs, Cs), (lambda i, s: (i, s[1])), (lambda i, s, nb=nb: (s[0] * nb + i, 0))
        else:
            shape, s_map, o_map = (Rs, 2 * Cs), (lambda i, s, nb=nb: (s[1] * nb + i, 0)), (lambda i, s: (i, s[0]))
        terms = [(s1, s_map)] + [(pt, (lambda i, s, k=k: (k, i, 0))) for k in range(3)]
        shards.append(_sum_terms(f"rs_sum4_{a}", terms, shape, F32, (tr, Cs), nb, o_map, scalars))
    return _join_halves("rs_join", shards, axes)


def _all_to_all_small(buf):
    nr = buf.shape[0]

    def body(b_ref, out_ref, send_sems, recv_sems, local_sem):
        x, y, c = _position()
        me = 4 * x + 2 * y + c
        mine = pltpu.make_async_copy(b_ref, out_ref.at[me], local_sem)
        mine.start()
        sends = []
        for r in range(1, 8):
            peer = (x ^ (r >> 2), y ^ ((r >> 1) & 1), c ^ (r & 1))
            cp = pltpu.make_async_remote_copy(
                src_ref=b_ref, dst_ref=out_ref.at[me], send_sem=send_sems.at[r - 1], recv_sem=recv_sems.at[r - 1],
                device_id=peer, device_id_type=MESH)
            cp.start()
            sends.append(cp)
        for r in range(1, 8):
            px, py, pc = x ^ (r >> 2), y ^ ((r >> 1) & 1), c ^ (r & 1)
            pltpu.make_async_remote_copy(
                src_ref=b_ref, dst_ref=out_ref.at[4 * px + 2 * py + pc], send_sem=send_sems.at[r - 1],
                recv_sem=recv_sems.at[r - 1], device_id=(px, py, pc), device_id_type=MESH).wait_recv()
        for cp in sends:
            cp.wait_send()
        mine.wait()

    return pl.pallas_call(
        body, name="small_exchange", in_specs=[ANY], out_specs=ANY,
        out_shape=jax.ShapeDtypeStruct((8, nr, 128), buf.dtype),
        scratch_shapes=[pltpu.SemaphoreType.DMA((7,)), pltpu.SemaphoreType.DMA((7,)), pltpu.SemaphoreType.DMA],
    )(buf)


SMALL_TR = 256


def _all_reduce_small(buf):
    nr = buf.shape[0]
    slots = _all_to_all_small(buf).reshape(8 * nr, 128)
    nblk = nr // SMALL_TR

    def fn(i, j, rv, bv):
        acc = rv[0]
        for v in rv[1:]:
            acc = acc + v
        return [acc], []

    rows = [(slots, 128, _c(0), (lambda i, d=d: d * nblk + i)) for d in range(8)]
    return _rowwise("small_sum", fn, rows, [], [(128, 128, _c(0), F32)], [], nrows=nr, tr=SMALL_TR)[0]


def _adamw(name, w, g, m, v):
    rows, cols = w.shape
    tr = _rows_for(cols, rows, budget=1 << 18)

    def fn(i, j, rv, bv):
        wv, gv, mv, vv = rv
        m2 = ADAM_B1 * mv + (1.0 - ADAM_B1) * gv
        v2 = ADAM_B2 * vv + (1.0 - ADAM_B2) * (gv * gv)
        m_hat = m2 / (1.0 - ADAM_B1 ** ADAM_STEP)
        v_hat = v2 / (1.0 - ADAM_B2 ** ADAM_STEP)
        delta = -ADAM_LR * (m_hat / (jnp.sqrt(v_hat) + ADAM_EPS) + ADAM_WD * wv)
        return [delta, m2, v2], []

    return _rowwise(name, fn, [(a, cols, _c(0), 0) for a in (w, g, m, v)], [], [(cols, cols, _c(0), F32)] * 3, [],
                    nrows=rows, tr=tr)


WEIGHTS = ("norm_g", "w_in", "conv_w", "conv_b", "conv_ln_g", "conv_ln_b", "sgu_ln_g", "sgu_ln_b", "sgu_w", "sgu_b",
           "w_branch", "w_gate", "b_gate", "w_out", "final_g")
BIG = ("w_in", "w_branch", "w_gate", "w_out")
BIG_AXIS = dict(w_in=1, w_branch=1, w_gate=1, w_out=0)
SMALL = tuple(n for n in WEIGHTS if n not in BIG)


def _pack(arrays, pad_rows):
    flat = jnp.concatenate([a.reshape(-1).astype(F32) for a in arrays])
    unit = 128 * pad_rows
    total = -(-flat.shape[0] // unit) * unit
    return jnp.pad(flat, (0, total - flat.shape[0])).reshape(total // 128, 128)


def _unpack(buf, shapes):
    flat = buf.reshape(-1)
    out, off = [], 0
    for shp in shapes:
        size = math.prod(shp)
        out.append(flat[off:off + size].reshape(shp))
        off += size
    return out


def kernel(x, norm_g, w_in, conv_w, conv_b, conv_ln_g, conv_ln_b, sgu_ln_g, sgu_ln_b, sgu_w, sgu_b, w_branch, w_gate, b_gate, w_out, final_g, loss_target, m_norm_g, m_w_in, m_conv_w, m_conv_b, m_conv_ln_g, m_conv_ln_b, m_sgu_ln_g, m_sgu_ln_b, m_sgu_w, m_sgu_b, m_w_branch, m_w_gate, m_b_gate, m_w_out, m_final_g, v_norm_g, v_w_in, v_conv_w, v_conv_b, v_conv_ln_g, v_conv_ln_b, v_sgu_ln_g, v_sgu_ln_b, v_sgu_w, v_sgu_b, v_w_branch, v_w_gate, v_b_gate, v_w_out, v_final_g):
    w = dict(norm_g=norm_g, w_in=w_in, conv_w=conv_w, conv_b=conv_b, conv_ln_g=conv_ln_g, conv_ln_b=conv_ln_b,
             sgu_ln_g=sgu_ln_g, sgu_ln_b=sgu_ln_b, sgu_w=sgu_w, sgu_b=sgu_b, w_branch=w_branch, w_gate=w_gate,
             b_gate=b_gate, w_out=w_out, final_g=final_g)
    m = dict(norm_g=m_norm_g, w_in=m_w_in, conv_w=m_conv_w, conv_b=m_conv_b, conv_ln_g=m_conv_ln_g,
             conv_ln_b=m_conv_ln_b, sgu_ln_g=m_sgu_ln_g, sgu_ln_b=m_sgu_ln_b, sgu_w=m_sgu_w, sgu_b=m_sgu_b,
             w_branch=m_w_branch, w_gate=m_w_gate, b_gate=m_b_gate, w_out=m_w_out, final_g=m_final_g)
    v = dict(norm_g=v_norm_g, w_in=v_w_in, conv_w=v_conv_w, conv_b=v_conv_b, conv_ln_g=v_conv_ln_g,
             conv_ln_b=v_conv_ln_b, sgu_ln_g=v_sgu_ln_g, sgu_ln_b=v_sgu_ln_b, sgu_w=v_sgu_w, sgu_b=v_sgu_b,
             w_branch=v_w_branch, w_gate=v_w_gate, b_gate=v_b_gate, w_out=v_w_out, final_g=v_final_g)
    depth = w_in.shape[0]
    chip = 2 * lax.axis_index("x") + lax.axis_index("y")
    scalars = jnp.stack([lax.axis_index("c"), chip]).astype(jnp.int32)
    axes = [BIG_AXIS[n] for n in BIG]

    stacked = dict(w_in=w_in, w_branch=w_branch.reshape(depth, 4 * BW, -1), w_gate=w_gate, w_out=w_out)
    buffers = [[_place_shard(f"place_{n}", stacked[n], l, BIG_AXIS[n], scalars) for n in BIG] for l in range(depth)]
    conv_w_full = _gather("gather_conv_w", conv_w, 2)
    gathered = _run_cargo("gather_first", _gather_cargo(buffers[0], axes))
    h = x[0]
    saved, layer_params = [], []
    for l in range(depth):
        p = {n: w[n][l] for n in SMALL if n not in ("final_g", "conv_w")}
        p["conv_w"] = conv_w_full[l]
        p.update(dict(zip(BIG, gathered)))
        cargo = _gather_cargo(buffers[l + 1], axes) if l + 1 < depth else None
        h, s, gathered = _layer_fwd(h, p, cargo)
        saved.append(s)
        layer_params.append(p)
    dh, g_final, loss = _final_loss(h, loss_target[0], final_g)

    layer_grads, reduced_big = [None] * depth, [None] * depth
    pending = None
    for l in reversed(range(depth)):
        dh, layer_grads[l], exchanged = _layer_bwd(dh, saved[l], layer_params[l], pending)
        if exchanged is not None:
            reduced_big[l + 1] = _reduce_end(*exchanged, axes, scalars)
        pending = ([layer_grads[l][n] for n in BIG], axes, scalars)
    sums = _reduce_sums(pending[0], _run_cargo("rs_halves_last", _swap_cargo(pending[0], axes)), axes, scalars)
    reduced_big[0] = _reduce_end(sums, _run_cargo("rs_shards_last", _scatter_cargo(sums, axes)), axes, scalars)
    grad_x = dh[None]
    local = {n: jnp.stack([layer_grads[l][n] for l in range(depth)]) for n in SMALL if n != "final_g"}
    local["final_g"] = g_final.reshape(-1)
    grads = {n: jnp.stack([reduced_big[l][a] for l in range(depth)]).reshape(w[n].shape) for a, n in enumerate(BIG)}
    small_shapes = [local[n].shape for n in SMALL] + [(128,)]
    reduced = _unpack(_all_reduce_small(_pack([local[n] for n in SMALL] + [loss.reshape(128)], SMALL_TR)), small_shapes)
    for n, r in zip(SMALL, reduced[:-1]):
        grads[n] = r
    loss_out = reduced[-1][0]
    ncw = conv_w.shape[2]
    grads["conv_w"] = lax.dynamic_slice_in_dim(grads["conv_w"], chip * ncw, ncw, axis=2)

    delta, new_m, new_v = {}, {}, {}
    for n in BIG:
        cols = w[n].shape[-1]
        d2, m2, v2 = _adamw(f"adamw_{n}", *[a.reshape(-1, cols) for a in (w[n], grads[n], m[n], v[n])])
        delta[n], new_m[n], new_v[n] = d2.reshape(w[n].shape), m2.reshape(w[n].shape), v2.reshape(w[n].shape)
    shapes = [w[n].shape for n in SMALL]
    packed = [_pack([t[n] for n in SMALL], SMALL_TR) for t in (w, grads, m, v)]
    outs = _adamw("adamw_small", *packed)
    for res, o in zip((delta, new_m, new_v), outs):
        for n, a in zip(SMALL, _unpack(o, shapes)):
            res[n] = a
    return (loss_out, grad_x, *[grads[n] for n in WEIGHTS], *[delta[n] for n in WEIGHTS],
            *[new_m[n] for n in WEIGHTS], *[new_v[n] for n in WEIGHTS])
```

```python
import functools
import math

import jax
import jax.numpy as jnp
from jax import lax
from jax.experimental import pallas as pl
from jax.experimental.pallas import tpu as pltpu

F32 = jnp.float32
BF16 = jnp.bfloat16
MESH = pl.DeviceIdType.MESH

DEPTH = 4
D_MODEL = 2048
HEAD = 128
NHEAD = 8
BW = 1024
IN_WIDTH = 18432
GATE_W = 4 * D_MODEL
NORM_EPS = 1e-6
CONV_K = 31
HALO = 32
DIL_PATTERNS = ((128, 1), (512, 4), (2048, 16))
C_AQ, C_AK, C_AV, C_AG, C_BA, C_BB, C_BG, C_CU, C_CV, C_CG, C_DQ, C_DK, C_DV, C_DG = (
    0, 1, 2, 3, 4, 5, 6, 7, 8, 9, 10, 13, 16, 17)
ADAM_LR, ADAM_B1, ADAM_B2, ADAM_EPS, ADAM_WD, ADAM_STEP = 0.001, 0.9, 0.999, 1e-08, 0.01, 10
VMEM_LIMIT = 56 * 1024 * 1024

NN = (((1,), (0,)), ((), ()))
NT = (((1,), (1,)), ((), ()))
TN = (((0,), (0,)), ((), ()))


def _cparams(sem):
    return pltpu.CompilerParams(dimension_semantics=sem, vmem_limit_bytes=VMEM_LIMIT)


def _dot(a, b, dims):
    return lax.dot_general(a.astype(BF16), b.astype(BF16), dims, preferred_element_type=F32)


def _sigmoid(x):
    return 1.0 / (1.0 + jnp.exp(-x))


def _silu_and_grad(x):
    s = _sigmoid(x)
    return x * s, s * (1.0 + x * (1.0 - s))


_GELU_C = math.sqrt(2.0 / math.pi)


def _gelu_and_grad(x):
    t = jnp.tanh(_GELU_C * (x + 0.044715 * x * x * x))
    y = 0.5 * x * (1.0 + t)
    dy = 0.5 * (1.0 + t) + 0.5 * x * (1.0 - t * t) * _GELU_C * (1.0 + 3.0 * 0.044715 * x * x)
    return y, dy


def _ln_fwd(x, g, b):
    mu = jnp.mean(x, axis=-1, keepdims=True)
    xc = x - mu
    var = jnp.mean(xc * xc, axis=-1, keepdims=True)
    rstd = lax.rsqrt(var + NORM_EPS)
    xh = xc * rstd
    return xh * g + b, xh, rstd


def _ln_bwd(dy, xh, rstd, g):
    dxh = dy * g
    dx = rstd * (dxh - jnp.mean(dxh, axis=-1, keepdims=True) - xh * jnp.mean(dxh * xh, axis=-1, keepdims=True))
    return dx, jnp.sum(dy * xh, axis=0, keepdims=True), jnp.sum(dy, axis=0, keepdims=True)


def _rowwise(name, fn, rows, bcast, outs, accs, *, nrows, tr, nj=1, cargo=None):
    ni = nrows // tr
    nr, nb, no = len(rows), len(bcast), len(outs)

    def rmap(colfn, shift):
        if callable(shift):
            return lambda j, i: (shift(i), colfn(j))
        if shift == 0:
            return lambda j, i: (i, colfn(j))
        return lambda j, i: (jnp.clip(i + shift, 0, ni - 1), colfn(j))

    in_specs = [pl.BlockSpec((tr, w), rmap(cf, sh)) for (_, w, cf, sh) in rows]
    in_specs += [pl.BlockSpec(b.shape, lambda j, i, nd=b.ndim: (0,) * nd) for b in bcast]
    out_specs = [pl.BlockSpec((tr, w), rmap(cf, 0)) for (_, w, cf, _) in outs]
    out_specs += [pl.BlockSpec((r, w), lambda j, i, cf=cf: (0, cf(j))) for (r, _, w, cf) in accs]
    out_shape = [jax.ShapeDtypeStruct((nrows, nc), dt) for (nc, _, _, dt) in outs]
    out_shape += [jax.ShapeDtypeStruct((r, nc), F32) for (r, nc, _, _) in accs]

    def body(*refs):
        j = pl.program_id(0)
        i = pl.program_id(1)
        rv = [r[...] for r in refs[:nr]]
        bv = [r[...] for r in refs[nr:nr + nb]]
        o_refs = refs[nr + nb:nr + nb + no]
        a_refs = refs[nr + nb + no:]
        o_vals, a_vals = fn(i, j, rv, bv)
        for ref, val in zip(o_refs, o_vals):
            ref[...] = val.astype(ref.dtype)
        if a_refs:
            @pl.when(i == 0)
            def _():
                for ref in a_refs:
                    ref[...] = jnp.zeros(ref.shape, F32)
            for ref, val in zip(a_refs, a_vals):
                ref[...] += val

    res, carried = _call_with_cargo(body, cargo, name=name, grid=(nj, ni), in_specs=in_specs, out_specs=out_specs,
                                    out_shape=out_shape, scratch_shapes=[], operands=(*[r[0] for r in rows], *bcast))
    return res if cargo is None else (res, carried)


def _c(k):
    return lambda j: k


def _mm(name, a, b, dims, *, a_blk, a_map, b_blk, b_map, o_shape, o_blk, o_map, grid, o_dtype,
        bias=None, bias_blk=None, bias_map=None, res=None):
    nk = grid[3]
    extra, extra_specs = [], []
    if bias is not None:
        extra.append(bias)
        extra_specs.append(pl.BlockSpec(bias_blk, bias_map))
    if res is not None:
        extra.append(res)
        extra_specs.append(pl.BlockSpec(o_blk, o_map))
    nbias = bias is not None
    nres = res is not None

    def body(*refs):
        a_ref, b_ref = refs[0], refs[1]
        idx = 2
        bias_ref = refs[idx] if nbias else None
        idx += nbias
        res_ref = refs[idx] if nres else None
        idx += nres
        o_ref = refs[idx]
        acc_ref = refs[idx + 1]
        k = pl.program_id(3)
        part = _dot(a_ref[...], b_ref[...], dims)

        def finish(r):
            if nbias:
                r = r + bias_ref[...]
            if nres:
                r = r + res_ref[...]
            o_ref[...] = r.astype(o_ref.dtype)

        if nk == 1:
            finish(part)
        else:
            @pl.when(k == 0)
            def _():
                acc_ref[...] = part

            @pl.when(k > 0)
            def _():
                acc_ref[...] += part

            @pl.when(k == nk - 1)
            def _():
                finish(acc_ref[...])

    acc_shape = o_blk if nk > 1 else (8, 128)
    return pl.pallas_call(
        body, name=name, grid=grid,
        in_specs=[pl.BlockSpec(a_blk, a_map), pl.BlockSpec(b_blk, b_map)] + extra_specs,
        out_specs=pl.BlockSpec(o_blk, o_map),
        out_shape=jax.ShapeDtypeStruct(o_shape, o_dtype),
        scratch_shapes=[pltpu.VMEM(acc_shape, F32)],
        compiler_params=_cparams(("arbitrary", "arbitrary", "arbitrary", "arbitrary")),
    )(a, b, *extra)


def _mm_nn(name, a, b, o_dtype, *, bias=None, res=None, groups=1, tm=1024, tn=1024):
    M = a.shape[0]
    K = a.shape[1] // groups
    N = b.shape[1]
    tm, tn = min(tm, M), min(tn, N)
    njn = N // tn
    return _mm(name, a, b, NN,
               a_blk=(tm, K), a_map=lambda g, j, i, k: (i, g),
               b_blk=(K, tn), b_map=lambda g, j, i, k: (g, j),
               o_shape=(M, groups * N), o_blk=(tm, tn), o_map=lambda g, j, i, k: (i, g * njn + j),
               grid=(groups, njn, M // tm, 1), o_dtype=o_dtype,
               bias=bias, bias_blk=(1, tn), bias_map=lambda g, j, i, k: (0, g * njn + j), res=res)


def _mm_nt(name, a, b, o_dtype, *, res=None, groups=1, tm=512, tk=2048):
    M = a.shape[0]
    K = a.shape[1] // groups
    N = b.shape[0] // groups
    tm, tk = min(tm, M), min(tk, K)
    nk = K // tk
    return _mm(name, a, b, NT,
               a_blk=(tm, tk), a_map=lambda g, j, i, k: (i, g * nk + k),
               b_blk=(N, tk), b_map=lambda g, j, i, k: (g, k),
               o_shape=(M, groups * N), o_blk=(tm, N), o_map=lambda g, j, i, k: (i, g),
               grid=(groups, 1, M // tm, nk), o_dtype=o_dtype, res=res)


def _mm_tn(name, a, b, *, groups=1, tn=1024, tk=1024):
    K = a.shape[0]
    M = a.shape[1] // groups
    N = b.shape[1] // groups
    tn, tk = min(tn, N), min(tk, K)
    njn = N // tn
    return _mm(name, a, b, TN,
               a_blk=(tk, M), a_map=lambda g, j, i, k: (k, g),
               b_blk=(tk, tn), b_map=lambda g, j, i, k: (k, g * njn + j),
               o_shape=(groups * M, N), o_blk=(M, tn), o_map=lambda g, j, i, k: (g, j),
               grid=(groups, njn, 1, K // tk), o_dtype=F32)


def _rms_fwd(x, g):
    S, D = x.shape

    def fn(i, j, rv, bv):
        xv, gv = rv[0], bv[0]
        r = lax.rsqrt(jnp.mean(xv * xv, axis=-1, keepdims=True) + NORM_EPS)
        return [xv * r * gv], []

    return _rowwise("rms_fwd", fn, [(x, D, _c(0), 0)], [g.reshape(1, D)], [(D, D, _c(0), BF16)], [],
                    nrows=S, tr=512)[0]


def _rms_bwd_math(xv, dyv, gv):
    r = lax.rsqrt(jnp.mean(xv * xv, axis=-1, keepdims=True) + NORM_EPS)
    xh = xv * r
    dg = jnp.sum(dyv * xh, axis=0, keepdims=True)
    dxh = dyv * gv
    dx = r * (dxh - xh * jnp.mean(dxh * xh, axis=-1, keepdims=True))
    return dx, dg


def _rms_bwd(x, dh, dout, g):
    S, D = x.shape

    def fn(i, j, rv, bv):
        dx, dg = _rms_bwd_math(rv[0], rv[1], bv[0])
        return [rv[2] + dx], [dg]

    return _rowwise("rms_bwd", fn, [(x, D, _c(0), 0), (dh, D, _c(0), 0), (dout, D, _c(0), 0)], [g.reshape(1, D)],
                    [(D, D, _c(0), F32)], [(1, D, D, _c(0))], nrows=S, tr=256)


def _final_loss(x, target, g):
    S, D = x.shape

    def fn(i, j, rv, bv):
        xv, tv, gv = rv[0], rv[1], bv[0]
        r = lax.rsqrt(jnp.mean(xv * xv, axis=-1, keepdims=True) + NORM_EPS)
        err = xv * r * gv - tv
        loss = 0.5 * jnp.sum(jnp.mean(err * err, axis=-1, keepdims=True), axis=0, keepdims=True)
        dx, dg = _rms_bwd_math(xv, err * (1.0 / D), gv)
        return [dx], [dg, jnp.broadcast_to(loss, (1, 128))]

    return _rowwise("final_loss", fn, [(x, D, _c(0), 0), (target, D, _c(0), 0)], [g.reshape(1, D)],
                    [(D, D, _c(0), F32)], [(1, D, D, _c(0)), (1, 128, 128, _c(0))], nrows=S, tr=256)


MERGE_W = 512


def _merge_fwd(gpre, yproj):
    S = gpre.shape[0]
    nj = D_MODEL // MERGE_W

    def fn(i, j, rv, bv):
        acc = _sigmoid(rv[0]) * rv[4]
        for n in range(1, 4):
            acc = acc + _sigmoid(rv[n]) * rv[4 + n]
        return [acc], []

    rows = [(gpre, MERGE_W, (lambda j, n=n: n * nj + j), 0) for n in range(4)]
    rows += [(yproj, MERGE_W, (lambda j, n=n: n * nj + j), 0) for n in range(4)]
    return _rowwise("merge_fwd", fn, rows, [], [(D_MODEL, MERGE_W, lambda j: j, BF16)], [],
                    nrows=S, tr=512, nj=nj)[0]


def _merge_bwd(dmerged, gpre, yproj, cargo=None):
    S = gpre.shape[0]
    nj4 = D_MODEL // MERGE_W

    def fn(i, j, rv, bv):
        dm, gp, yp = rv
        sg = _sigmoid(gp)
        dgp = dm * yp * sg * (1.0 - sg)
        return [dm * sg, dgp], [jnp.sum(dgp, axis=0, keepdims=True)]

    rows = [(dmerged, MERGE_W, lambda j: j % nj4, 0), (gpre, MERGE_W, lambda j: j, 0), (yproj, MERGE_W, lambda j: j, 0)]
    return _rowwise("merge_bwd", fn, rows, [],
                    [(GATE_W, MERGE_W, lambda j: j, BF16), (GATE_W, MERGE_W, lambda j: j, BF16)],
                    [(1, GATE_W, MERGE_W, lambda j: j)], nrows=S, tr=512, nj=4 * nj4, cargo=cargo)


SB_TQ = 1024
SB_TQ_BWD = 512
SB_TK = 512
SB_TRI = 256


def _softplus(z):
    bits, sign = (jnp.uint32, 0x80000000) if z.dtype == F32 else (jnp.uint16, 0x8000)
    neg_abs = lax.bitcast_convert_type(lax.bitcast_convert_type(z, bits) | bits(sign), z.dtype)
    return jnp.maximum(z, 0.0) + jnp.log(1.0 + jnp.exp(neg_abs))


def _sb_sum_matrices():
    r = lax.broadcasted_iota(jnp.int32, (SB_TRI, SB_TRI), 0)
    c = lax.broadcasted_iota(jnp.int32, (SB_TRI, SB_TRI), 1)
    return (r > c).astype(BF16), (r < c).astype(BF16)


ANY =pl.BlockSpec(memory_space=pl.ANY)


class _Cargo:
    def __init__(self, arrays, out_shapes, aliases, nsem, start, finish):
        self.arrays, self.out_shapes, self.aliases, self.nsem = list(arrays), list(out_shapes), dict(aliases), nsem
        self.start, self.finish = start, finish


def _call_with_cargo(body, cargo, *, name, grid, in_specs, out_specs, out_shape, scratch_shapes, operands):
    sem = ("arbitrary",) * len(grid)
    if cargo is None:
        res = pl.pallas_call(body, name=name, grid=grid, in_specs=in_specs, out_specs=out_specs, out_shape=out_shape,
                             scratch_shapes=scratch_shapes, compiler_params=_cparams(sem))(*operands)
        return res, []
    n_in, n_out, n_scr = len(in_specs), len(out_specs), len(scratch_shapes)
    nci, nco = len(cargo.arrays), len(cargo.out_shapes)

    def wrapped(*refs):
        ins, refs = refs[:n_in], refs[n_in:]
        cin, refs = refs[:nci], refs[nci:]
        outs, refs = refs[:n_out], refs[n_out:]
        cout, refs = refs[:nco], refs[nco:]
        scr, (send, recv) = refs[:n_scr], refs[n_scr:]
        ids = [pl.program_id(a) for a in range(len(grid))]
        first = functools.reduce(lambda p, q: p & q, [g == 0 for g in ids])
        last = functools.reduce(lambda p, q: p & q, [g == n - 1 for g, n in zip(ids, grid)])

        @pl.when(first)
        def _():
            cargo.start(cin, cout, send, recv)

        body(*ins, *outs, *scr)

        @pl.when(last)
        def _():
            cargo.finish(cin, cout, send, recv)

    res = pl.pallas_call(
        wrapped, name=name, grid=grid, in_specs=list(in_specs) + [ANY] * nci, out_specs=list(out_specs) + [ANY] * nco,
        out_shape=list(out_shape) + cargo.out_shapes,
        scratch_shapes=list(scratch_shapes) + [pltpu.SemaphoreType.DMA((cargo.nsem,)), pltpu.SemaphoreType.DMA((cargo.nsem,))],
        input_output_aliases={n_in + k: n_out + v for k, v in cargo.aliases.items()},
        compiler_params=_cparams(sem))(*operands, *cargo.arrays)
    return res[:n_out], res[n_out:]


def _run_cargo(name, cargo):
    nci = len(cargo.arrays)

    def body(*refs):
        cin, cout, (send, recv) = refs[:nci], refs[nci:-2], refs[-2:]
        cargo.start(cin, cout, send, recv)
        cargo.finish(cin, cout, send, recv)

    return pl.pallas_call(
        body, name=name, in_specs=[ANY] * nci, out_specs=[ANY] * len(cargo.out_shapes), out_shape=cargo.out_shapes,
        scratch_shapes=[pltpu.SemaphoreType.DMA((cargo.nsem,)), pltpu.SemaphoreType.DMA((cargo.nsem,))],
        input_output_aliases=cargo.aliases)(*cargo.arrays)


def _sb_fwd(proj, cargo=None):
    S = proj.shape[0]
    tq, tk = min(SB_TQ, S), SB_TK
    assert S // SB_TRI <= HEAD
    nq = S // tq
    nkb = tq // tk
    nsub = tk // SB_TRI
    scale = HEAD ** -0.5

    def body(q_ref, k_ref, v_ref, g_ref, o_ref, y_ref, r_ref, kb_ref, vb_ref):
        i = pl.program_id(1)

        @pl.when(i == 0)
        def _():
            kb_ref[...] = k_ref[...].astype(BF16)
            vb_ref[...] = v_ref[...].astype(BF16)

        qb = (q_ref[...] * scale).astype(BF16)
        qpos = i * tq + lax.broadcasted_iota(jnp.int32, (tq, tk), 0)
        kidx = lax.broadcasted_iota(jnp.int32, (tq, tk), 1)
        lane = lax.broadcasted_iota(jnp.int32, (tq, HEAD), 1)
        upper, _ = _sb_sum_matrices()

        def block(kblock, masked, carry):
            acc, run, runs = carry
            off = pl.multiple_of(kblock * tk, tk)
            zz = _dot(qb, kb_ref[pl.ds(off, tk), :], NT).astype(BF16)
            sp = _softplus(zz)
            if masked:
                keep = (off + kidx) < qpos
                sp_sum = jnp.where(keep, sp, 0.0)
            else:
                sp_sum = sp
            right, after = [None] * nsub, [None] * nsub
            for j in reversed(range(nsub)):
                cols = slice(j * SB_TRI, (j + 1) * SB_TRI)
                right[j] = run
                suffix = _dot(sp_sum[:, cols], upper, NN)
                after[j] = (suffix + run).astype(BF16)
                run = run + suffix[:, :1] + sp_sum[:, j * SB_TRI:j * SB_TRI + 1].astype(F32)
                runs = jnp.where(lane == kblock * nsub + j, right[j], runs)
            w = jnp.exp((zz - sp) - jnp.concatenate(after, axis=1))
            if masked:
                w = jnp.where(keep, w, 0.0)
            return acc + _dot(w, vb_ref[pl.ds(off, tk), :], NN), run, runs

        runs0 = jnp.where(qpos[:, :HEAD] + lane < 0, 1.0, 0.0)
        carry = (jnp.zeros((tq, HEAD), F32), jnp.zeros((tq, 1), F32), runs0)
        for kbl in reversed(range(nkb)):
            carry = block(i * nkb + kbl, True, carry)
        acc, _, runs = lax.fori_loop(0, i * nkb, lambda n, c: block(i * nkb - 1 - n, False, c), carry)
        o_ref[...] = acc
        r_ref[...] = runs
        gate, _ = _silu_and_grad(g_ref[...])
        y_ref[...] = (acc * gate).astype(BF16)

    return _call_with_cargo(
        body, cargo, name="sb_fwd", grid=(NHEAD, nq),
        in_specs=[pl.BlockSpec((tq, HEAD), lambda h, i: (i, C_AQ * NHEAD + h)),
                  pl.BlockSpec((S, HEAD), lambda h, i: (0, C_AK * NHEAD + h)),
                  pl.BlockSpec((S, HEAD), lambda h, i: (0, C_AV * NHEAD + h)),
                  pl.BlockSpec((tq, HEAD), lambda h, i: (i, C_AG * NHEAD + h))],
        out_specs=[pl.BlockSpec((tq, HEAD), lambda h, i: (i, h))] * 3,
        out_shape=[jax.ShapeDtypeStruct((S, BW), F32), jax.ShapeDtypeStruct((S, BW), BF16),
                   jax.ShapeDtypeStruct((S, BW), F32)],
        scratch_shapes=[pltpu.VMEM((S, HEAD), BF16), pltpu.VMEM((S, HEAD), BF16)],
        operands=(proj, proj, proj, proj))


def _sb_bwd(proj, att, runs, dy, cargo=None):
    S = proj.shape[0]
    tq, tk = min(SB_TQ_BWD, S), SB_TK
    nq = S // tq
    nkb = tq // tk
    nsub = tk // SB_TRI
    scale = HEAD ** -0.5

    def body(q_ref, k_ref, v_ref, g_ref, o_ref, r_ref, dy_ref, dq_ref, dk_ref, dv_ref, dg_ref, kb_ref, vb_ref):
        i = pl.program_id(1)

        @pl.when(i == 0)
        def _():
            kb_ref[...] = k_ref[...].astype(BF16)
            vb_ref[...] = v_ref[...].astype(BF16)
            dk_ref[...] = jnp.zeros(dk_ref.shape, F32)
            dv_ref[...] = jnp.zeros(dv_ref.shape, F32)

        gate, dgate = _silu_and_grad(g_ref[...])
        dyv = dy_ref[...]
        dg_ref[...] = dyv * o_ref[...] * dgate
        dob = (dyv * gate).astype(BF16)
        qb = (q_ref[...] * scale).astype(BF16)
        runs = r_ref[...]
        qpos = i * tq + lax.broadcasted_iota(jnp.int32, (tq, tk), 0)
        kidx = lax.broadcasted_iota(jnp.int32, (tq, tk), 1)
        lane = lax.broadcasted_iota(jnp.int32, (tq, HEAD), 1)
        upper, lower = _sb_sum_matrices()

        def block(kblock, masked, carry):
            dq, grun = carry
            off = pl.multiple_of(kblock * tk, tk)
            kblk = kb_ref[pl.ds(off, tk), :]
            zz = _dot(qb, kblk, NT).astype(BF16)
            sp = _softplus(zz)
            lb = zz - sp
            if masked:
                keep = (off + kidx) < qpos
                sp_sum = jnp.where(keep, sp, 0.0)
            else:
                sp_sum = sp
            after = []
            for j in range(nsub):
                cols = slice(j * SB_TRI, (j + 1) * SB_TRI)
                run = jnp.sum(jnp.where(lane == kblock * nsub + j, runs, 0.0), axis=1, keepdims=True)
                after.append((_dot(sp_sum[:, cols], upper, NN) + run).astype(BF16))
            w = jnp.exp(lb - jnp.concatenate(after, axis=1))
            if masked:
                w = jnp.where(keep, w, 0.0)
            gw = _dot(dob, vb_ref[pl.ds(off, tk), :], NT).astype(BF16) * w
            gsum = []
            for j in range(nsub):
                cols = slice(j * SB_TRI, (j + 1) * SB_TRI)
                prefix = _dot(gw[:, cols], lower, NN)
                gsum.append((prefix + grun).astype(BF16))
                last = (j + 1) * SB_TRI - 1
                grun = grun + prefix[:, SB_TRI - 1:] + gw[:, last:last + 1].astype(F32)
            dz = gw - (gw + jnp.concatenate(gsum, axis=1)) * jnp.exp(lb)
            if masked:
                dz = jnp.where(keep, dz, 0.0)
            dk_ref[pl.ds(off, tk), :] += _dot(dz, qb, TN)
            dv_ref[pl.ds(off, tk), :] += _dot(w, dob, TN)
            return dq + _dot(dz, kblk, NN), grun

        carry = lax.fori_loop(0, i * nkb, lambda n, c: block(n, False, c),
                              (jnp.zeros((tq, HEAD), F32), jnp.zeros((tq, 1), F32)))
        for kbl in range(nkb):
            carry = block(i * nkb + kbl, True, carry)
        dq_ref[...] = carry[0] * scale

    blk = lambda h, i: (i, h)
    head = lambda h, i: (0, h)
    return _call_with_cargo(
        body, cargo, name="sb_bwd", grid=(NHEAD, nq),
        in_specs=[pl.BlockSpec((tq, HEAD), lambda h, i: (i, C_AQ * NHEAD + h)),
                  pl.BlockSpec((S, HEAD), lambda h, i: (0, C_AK * NHEAD + h)),
                  pl.BlockSpec((S, HEAD), lambda h, i: (0, C_AV * NHEAD + h)),
                  pl.BlockSpec((tq, HEAD), lambda h, i: (i, C_AG * NHEAD + h)),
                  pl.BlockSpec((tq, HEAD), blk), pl.BlockSpec((tq, HEAD), blk), pl.BlockSpec((tq, HEAD), blk)],
        out_specs=[pl.BlockSpec((tq, HEAD), blk), pl.BlockSpec((S, HEAD), head), pl.BlockSpec((S, HEAD), head),
                   pl.BlockSpec((tq, HEAD), blk)],
        out_shape=[jax.ShapeDtypeStruct((S, BW), F32)] * 4,
        scratch_shapes=[pltpu.VMEM((S, HEAD), BF16), pltpu.VMEM((S, HEAD), BF16)],
        operands=(proj, proj, proj, proj, att, runs, dy))


CONV_TR = 256


def _conv_taps(buf_ref, w_ref, base, tr):
    acc = w_ref[0:1, :] * buf_ref[pl.ds(base, tr), :]
    for k in range(1, CONV_K):
        acc = acc + w_ref[k:k + 1, :] * buf_ref[pl.ds(base + k, tr), :]
    return acc


def _conv_fwd(proj, conv_w, conv_b, ln_g, ln_b):
    S = proj.shape[0]
    tr = min(CONV_TR, S)
    ni = S // tr

    def body(a_ref, b_ref, ap_ref, bp_ref, g_ref, w_ref, cb_ref, lg_ref, lb_ref, c_ref, y_ref, buf_ref):
        i = pl.program_id(0)
        prev = ap_ref[tr - HALO:, :] * _sigmoid(bp_ref[tr - HALO:, :])
        buf_ref[0:HALO, :] = jnp.where(i > 0, prev, 0.0)
        buf_ref[HALO:, :] = a_ref[...] * _sigmoid(b_ref[...])
        c = _conv_taps(buf_ref, w_ref, HALO - (CONV_K - 1), tr) + cb_ref[...]
        c_ref[...] = c
        yn, _, _ = _ln_fwd(c, lg_ref[...], lb_ref[...])
        act, _ = _silu_and_grad(yn)
        gate, _ = _silu_and_grad(g_ref[...])
        y_ref[...] = (act * gate).astype(BF16)

    cur = lambda k: (lambda i: (i, k))
    prv = lambda k: (lambda i: (jnp.maximum(i - 1, 0), k))
    full = lambda a: pl.BlockSpec(a.shape, lambda i: (0, 0))
    wpad = jnp.pad(conv_w, ((0, HALO - CONV_K), (0, 0)))
    small = [wpad, conv_b.reshape(1, BW), ln_g.reshape(1, BW), ln_b.reshape(1, BW)]
    return pl.pallas_call(
        body, name="conv_fwd", grid=(ni,),
        in_specs=[pl.BlockSpec((tr, BW), cur(C_BA)), pl.BlockSpec((tr, BW), cur(C_BB)),
                  pl.BlockSpec((tr, BW), prv(C_BA)), pl.BlockSpec((tr, BW), prv(C_BB)),
                  pl.BlockSpec((tr, BW), cur(C_BG))] + [full(a) for a in small],
        out_specs=[pl.BlockSpec((tr, BW), lambda i: (i, 0)), pl.BlockSpec((tr, BW), lambda i: (i, 0))],
        out_shape=[jax.ShapeDtypeStruct((S, BW), F32), jax.ShapeDtypeStruct((S, BW), BF16)],
        scratch_shapes=[pltpu.VMEM((tr + HALO, BW), F32)],
        compiler_params=_cparams(("arbitrary",)),
    )(proj, proj, proj, proj, proj, *small)


def _conv_bwd_norm(proj, cpre, dy, ln_g, ln_b):
    S = proj.shape[0]

    def fn(i, j, rv, bv):
        c, bg, dyv = rv
        lg, lb = bv
        yn, xh, rstd = _ln_fwd(c, lg, lb)
        act, dact = _silu_and_grad(yn)
        gate, dgate = _silu_and_grad(bg)
        dyn = dyv * gate * dact
        dc, dlg, dlb = _ln_bwd(dyn, xh, rstd, lg)
        return [dc, dyv * act * dgate], [dlg, dlb, jnp.sum(dc, axis=0, keepdims=True)]

    rows = [(cpre, BW, _c(0), 0), (proj, BW, _c(C_BG), 0), (dy, BW, _c(1), 0)]
    return _rowwise("conv_bwd_norm", fn, rows, [ln_g.reshape(1, BW), ln_b.reshape(1, BW)],
                    [(BW, BW, _c(0), F32), (BW, BW, _c(0), BF16)],
                    [(1, BW, BW, _c(0))] * 3, nrows=S, tr=256)


def _conv_bwd_taps(proj, dc, conv_w):
    S = proj.shape[0]
    tr = min(CONV_TR, S)
    ni = S // tr

    def body(a_ref, b_ref, ap_ref, bp_ref, dc_ref, dcn_ref, w_ref, da_ref, db_ref, dw_ref, gbuf, dbuf, wrev):
        i = pl.program_id(0)

        @pl.when(i == 0)
        def _():
            dw_ref[...] = jnp.zeros(dw_ref.shape, F32)
            for k in range(CONV_K):
                wrev[k:k + 1, :] = w_ref[CONV_K - 1 - k:CONV_K - k, :]

        sb = _sigmoid(b_ref[...])
        av = a_ref[...]
        prev = ap_ref[tr - HALO:, :] * _sigmoid(bp_ref[tr - HALO:, :])
        gbuf[0:HALO, :] = jnp.where(i > 0, prev, 0.0)
        gbuf[HALO:, :] = av * sb
        dcv = dc_ref[...]
        dbuf[0:tr, :] = dcv
        dbuf[tr:, :] = jnp.where(i < ni - 1, dcn_ref[0:HALO, :], 0.0)
        dg = _conv_taps(dbuf, wrev, 0, tr)
        da_ref[...] = (dg * sb).astype(BF16)
        db_ref[...] = (dg * av * sb * (1.0 - sb)).astype(BF16)
        base = HALO - (CONV_K - 1)
        for k in range(CONV_K):
            dw_ref[k:k + 1, :] += jnp.sum(dcv * gbuf[pl.ds(base + k, tr), :], axis=0, keepdims=True)

    cur = lambda k: (lambda i: (i, k))
    prv = lambda k: (lambda i: (jnp.maximum(i - 1, 0), k))
    wpad = jnp.pad(conv_w, ((0, HALO - CONV_K), (0, 0)))
    return pl.pallas_call(
        body, name="conv_bwd_taps", grid=(ni,),
        in_specs=[pl.BlockSpec((tr, BW), cur(C_BA)), pl.BlockSpec((tr, BW), cur(C_BB)),
                  pl.BlockSpec((tr, BW), prv(C_BA)), pl.BlockSpec((tr, BW), prv(C_BB)),
                  pl.BlockSpec((tr, BW), lambda i: (i, 0)),
                  pl.BlockSpec((tr, BW), lambda i: (jnp.minimum(i + 1, ni - 1), 0)),
                  pl.BlockSpec((HALO, BW), lambda i: (0, 0))],
        out_specs=[pl.BlockSpec((tr, BW), lambda i: (i, 0)), pl.BlockSpec((tr, BW), lambda i: (i, 0)),
                   pl.BlockSpec((HALO, BW), lambda i: (0, 0))],
        out_shape=[jax.ShapeDtypeStruct((S, BW), BF16), jax.ShapeDtypeStruct((S, BW), BF16),
                   jax.ShapeDtypeStruct((HALO, BW), F32)],
        scratch_shapes=[pltpu.VMEM((tr + HALO, BW), F32), pltpu.VMEM((tr + HALO, BW), F32),
                        pltpu.VMEM((HALO, BW), F32)],
        compiler_params=_cparams(("arbitrary",)),
    )(proj, proj, proj, proj, dc, dc, wpad)


SGU_TR = 256


def _sgu_mix(wm, vn, bias):
    tr = vn.shape[0]
    rows = []
    for n in range(tr // HEAD):
        cols = []
        for g in range(NHEAD):
            blk = vn[n * HEAD:(n + 1) * HEAD, g * HEAD:(g + 1) * HEAD]
            cols.append(_dot(wm[g], blk, NN) + bias[g * HEAD:(g + 1) * HEAD, :])
        rows.append(jnp.concatenate(cols, axis=1))
    return jnp.concatenate(rows, axis=0)


def _sgu_masked_w(w2d):
    rr = lax.broadcasted_iota(jnp.int32, (HEAD, HEAD), 0)
    cc = lax.broadcasted_iota(jnp.int32, (HEAD, HEAD), 1)
    tril = rr >= cc
    return [jnp.where(tril, w2d[g * HEAD:(g + 1) * HEAD, :], 0.0).astype(BF16) for g in range(NHEAD)], tril


def _sgu_inputs(sgu_w, sgu_b, ln_g, ln_b):
    w2d = sgu_w.reshape(NHEAD * HEAD, HEAD)
    bias = jnp.broadcast_to(sgu_b[:, :, None], (NHEAD, HEAD, HEAD)).reshape(NHEAD * HEAD, HEAD)
    return [w2d, bias, ln_g.reshape(1, BW), ln_b.reshape(1, BW)]


def _sgu_fwd(proj, sgu_w, sgu_b, ln_g, ln_b):
    S = proj.shape[0]

    def fn(i, j, rv, bv):
        cu, cv, cg = rv
        w2d, bias, lg, lb = bv
        wm, _ = _sgu_masked_w(w2d)
        u, _ = _gelu_and_grad(cu)
        v, _ = _gelu_and_grad(cv)
        vn, _, _ = _ln_fwd(v, lg, lb)
        z = _sgu_mix(wm, vn, bias)
        gate, _ = _silu_and_grad(cg)
        return [u * z * gate], []

    rows = [(proj, BW, _c(C_CU), 0), (proj, BW, _c(C_CV), 0), (proj, BW, _c(C_CG), 0)]
    return _rowwise("sgu_fwd", fn, rows, _sgu_inputs(sgu_w, sgu_b, ln_g, ln_b), [(BW, BW, _c(0), BF16)], [],
                    nrows=S, tr=min(SGU_TR, S))[0]


def _sgu_bwd(proj, dy, sgu_w, sgu_b, ln_g, ln_b):
    S = proj.shape[0]

    def fn(i, j, rv, bv):
        cu, cv, cg, dyv = rv
        w2d, bias, lg, lb = bv
        wm, tril = _sgu_masked_w(w2d)
        u, du_dcu = _gelu_and_grad(cu)
        v, dv_dcv = _gelu_and_grad(cv)
        vn, xh, rstd = _ln_fwd(v, lg, lb)
        z = _sgu_mix(wm, vn, bias)
        gate, dgate = _silu_and_grad(cg)
        dz = dyv * u * gate
        dcu = dyv * z * gate * du_dcu
        dcg = dyv * u * z * dgate
        tr = dz.shape[0]
        dvn_rows = []
        dw = [jnp.zeros((HEAD, HEAD), F32) for _ in range(NHEAD)]
        dbias = [jnp.zeros((HEAD, HEAD), F32) for _ in range(NHEAD)]
        for n in range(tr // HEAD):
            cols = []
            for g in range(NHEAD):
                dzb = dz[n * HEAD:(n + 1) * HEAD, g * HEAD:(g + 1) * HEAD]
                vnb = vn[n * HEAD:(n + 1) * HEAD, g * HEAD:(g + 1) * HEAD]
                cols.append(_dot(wm[g], dzb, TN))
                dw[g] = dw[g] + _dot(dzb, vnb, NT)
                dbias[g] = dbias[g] + dzb
            dvn_rows.append(jnp.concatenate(cols, axis=1))
        dvn = jnp.concatenate(dvn_rows, axis=0)
        dv, dlg, dlb = _ln_bwd(dvn, xh, rstd, lg)
        dw2d = jnp.concatenate([jnp.where(tril, d, 0.0) for d in dw], axis=0)
        return [dcu, dv * dv_dcv, dcg], [dw2d, jnp.concatenate(dbias, axis=0), dlg, dlb]

    rows = [(proj, BW, _c(C_CU), 0), (proj, BW, _c(C_CV), 0), (proj, BW, _c(C_CG), 0), (dy, BW, _c(2), 0)]
    return _rowwise("sgu_bwd", fn, rows, _sgu_inputs(sgu_w, sgu_b, ln_g, ln_b),
                    [(BW, BW, _c(0), BF16)] * 3,
                    [(NHEAD * HEAD, HEAD, HEAD, _c(0)), (NHEAD * HEAD, HEAD, HEAD, _c(0)), (1, BW, BW, _c(0)),
                     (1, BW, BW, _c(0))], nrows=S, tr=min(SGU_TR, S))


NEG = -1e30
NCOL = IN_WIDTH // BW


def _band_masks():
    a = lax.broadcasted_iota(jnp.int32, (HEAD, HEAD), 0)
    c = lax.broadcasted_iota(jnp.int32, (HEAD, HEAD), 1)
    return a >= c, a <= c


def _dil_view(proj, grp):
    S = proj.shape[0]
    dil = DIL_PATTERNS[grp][1]
    if dil == 1:
        return proj, NCOL, C_DQ + grp, C_DK + grp, C_DV
    cols = [proj[:, c * BW:(c + 1) * BW] for c in (C_DQ + grp, C_DK + grp, C_DV)]
    return jnp.concatenate(cols, axis=1).reshape(S // dil, dil * 3 * BW), 3, 0, 1, 2


def _dil_fwd(proj, grp):
    S = proj.shape[0]
    dil = DIL_PATTERNS[grp][1]
    L = S // dil
    nb = L // HEAD
    proj2, ncol, cq, ck, cv = _dil_view(proj, grp)
    scale = HEAD ** -0.5

    def body(q_ref, kc_ref, kp_ref, vc_ref, vp_ref, o_ref, l_ref):
        b = pl.program_id(1)
        m_cur, m_prev = _band_masks()
        m_prev = m_prev & (b > 0)
        for h in range(NHEAD):
            sl = slice(h * HEAD, (h + 1) * HEAD)
            q = q_ref[:, sl].astype(BF16)
            s_c = jnp.where(m_cur, _dot(q, kc_ref[:, sl], NT) * scale, NEG)
            s_p = jnp.where(m_prev, _dot(q, kp_ref[:, sl], NT) * scale, NEG)
            m = jnp.maximum(jnp.max(s_c, axis=1, keepdims=True), jnp.max(s_p, axis=1, keepdims=True))
            p_c = jnp.exp(s_c - m)
            p_p = jnp.exp(s_p - m)
            den = jnp.sum(p_c, axis=1, keepdims=True) + jnp.sum(p_p, axis=1, keepdims=True)
            o = (_dot(p_c, vc_ref[:, sl], NN) + _dot(p_p, vp_ref[:, sl], NN)) / den
            o_ref[:, sl] = o
            l_ref[:, sl] = jnp.broadcast_to(m + jnp.log(den), (HEAD, HEAD))

    cur = lambda col: (lambda r, b: (b, r * ncol + col))
    prv = lambda col: (lambda r, b: (jnp.maximum(b - 1, 0), r * ncol + col))
    out_map = lambda r, b: (b, r)
    o, lse = pl.pallas_call(
        body, name=f"dil_fwd{grp}", grid=(dil, nb),
        in_specs=[pl.BlockSpec((HEAD, BW), cur(cq)),
                  pl.BlockSpec((HEAD, BW), cur(ck)), pl.BlockSpec((HEAD, BW), prv(ck)),
                  pl.BlockSpec((HEAD, BW), cur(cv)), pl.BlockSpec((HEAD, BW), prv(cv))],
        out_specs=[pl.BlockSpec((HEAD, BW), out_map), pl.BlockSpec((HEAD, BW), out_map)],
        out_shape=[jax.ShapeDtypeStruct((L, dil * BW), F32), jax.ShapeDtypeStruct((L, dil * BW), F32)],
        compiler_params=_cparams(("arbitrary", "arbitrary")),
    )(proj2, proj2, proj2, proj2, proj2)
    return o.reshape(S, BW), lse.reshape(S, BW)


def _dil_combine(proj, os_, lses):
    S = proj.shape[0]

    def fn(i, j, rv, bv):
        o1, o2, o3, l1, l2, l3, dg = rv
        m = jnp.maximum(jnp.maximum(l1, l2), l3)
        e1, e2, e3 = jnp.exp(l1 - m), jnp.exp(l2 - m), jnp.exp(l3 - m)
        den = e1 + e2 + e3
        out = (e1 * o1 + e2 * o2 + e3 * o3) / den
        gate, _ = _silu_and_grad(dg)
        return [out, m + jnp.log(den), out * gate], []

    rows = [(a, BW, _c(0), 0) for a in (*os_, *lses)] + [(proj, BW, _c(C_DG), 0)]
    return _rowwise("dil_combine", fn, rows, [], [(BW, BW, _c(0), F32), (BW, BW, _c(0), F32), (BW, BW, _c(0), BF16)],
                    [], nrows=S, tr=256)


def _dil_gate_bwd(proj, att, dy):
    S = proj.shape[0]

    def fn(i, j, rv, bv):
        dg, av, dyv = rv
        gate, dgate = _silu_and_grad(dg)
        dout = dyv * gate
        prod = dout * av
        tr = prod.shape[0]
        delta = jnp.concatenate(
            [jnp.broadcast_to(jnp.sum(prod[:, h * HEAD:(h + 1) * HEAD], axis=1, keepdims=True), (tr, HEAD))
             for h in range(NHEAD)], axis=1)
        return [dout, dyv * av * dgate, delta], []

    rows = [(proj, BW, _c(C_DG), 0), (att, BW, _c(0), 0), (dy, BW, _c(3), 0)]
    return _rowwise("dil_gate_bwd", fn, rows, [], [(BW, BW, _c(0), F32), (BW, BW, _c(0), BF16), (BW, BW, _c(0), F32)],
                    [], nrows=S, tr=256)


def _dil_bwd(proj, dout, lse, delta, grp):
    S = proj.shape[0]
    dil = DIL_PATTERNS[grp][1]
    L = S // dil
    nb = L // HEAD
    proj2, ncol, cq, ck, cv = _dil_view(proj, grp)
    view = lambda a: a.reshape(L, dil * BW)
    scale = HEAD ** -0.5

    def body(q_ref, qn_ref, kc_ref, kp_ref, vc_ref, vp_ref, do_ref, don_ref, l_ref, ln_ref, d_ref, dn_ref,
             dq_ref, dk_ref, dv_ref):
        b = pl.program_id(1)
        m_cur, m_band = _band_masks()
        m_prev = m_band & (b > 0)
        m_next = m_band & (b < nb - 1)
        for h in range(NHEAD):
            sl = slice(h * HEAD, (h + 1) * HEAD)
            q, qn = q_ref[:, sl].astype(BF16), qn_ref[:, sl].astype(BF16)
            kc, kp = kc_ref[:, sl].astype(BF16), kp_ref[:, sl].astype(BF16)
            vc, vp = vc_ref[:, sl].astype(BF16), vp_ref[:, sl].astype(BF16)
            do, don = do_ref[:, sl].astype(BF16), don_ref[:, sl].astype(BF16)
            lse_c, lse_n = l_ref[:, sl], ln_ref[:, sl]
            dl_c, dl_n = d_ref[:, sl], dn_ref[:, sl]
            p_cc = jnp.exp(jnp.where(m_cur, _dot(q, kc, NT) * scale - lse_c, NEG))
            p_cp = jnp.exp(jnp.where(m_prev, _dot(q, kp, NT) * scale - lse_c, NEG))
            p_nc = jnp.exp(jnp.where(m_next, _dot(qn, kc, NT) * scale - lse_n, NEG))
            ds_cc = p_cc * (_dot(do, vc, NT) - dl_c)
            ds_cp = p_cp * (_dot(do, vp, NT) - dl_c)
            ds_nc = p_nc * (_dot(don, vc, NT) - dl_n)
            dq_ref[:, sl] = ((_dot(ds_cc, kc, NN) + _dot(ds_cp, kp, NN)) * scale).astype(BF16)
            dk_ref[:, sl] = ((_dot(ds_cc, q, TN) + _dot(ds_nc, qn, TN)) * scale).astype(BF16)
            dv_ref[:, sl] = _dot(p_cc, do, TN) + _dot(p_nc, don, TN)

    cur = lambda col: (lambda r, b: (b, r * ncol + col))
    prv = lambda col: (lambda r, b: (jnp.maximum(b - 1, 0), r * ncol + col))
    nxt = lambda col: (lambda r, b: (jnp.minimum(b + 1, nb - 1), r * ncol + col))
    o_cur = lambda r, b: (b, r)
    o_nxt = lambda r, b: (jnp.minimum(b + 1, nb - 1), r)
    blk = (HEAD, BW)
    dq, dk, dv = pl.pallas_call(
        body, name=f"dil_bwd{grp}", grid=(dil, nb),
        in_specs=[pl.BlockSpec(blk, cur(cq)), pl.BlockSpec(blk, nxt(cq)),
                  pl.BlockSpec(blk, cur(ck)), pl.BlockSpec(blk, prv(ck)),
                  pl.BlockSpec(blk, cur(cv)), pl.BlockSpec(blk, prv(cv)),
                  pl.BlockSpec(blk, o_cur), pl.BlockSpec(blk, o_nxt),
                  pl.BlockSpec(blk, o_cur), pl.BlockSpec(blk, o_nxt),
                  pl.BlockSpec(blk, o_cur), pl.BlockSpec(blk, o_nxt)],
        out_specs=[pl.BlockSpec(blk, o_cur)] * 3,
        out_shape=[jax.ShapeDtypeStruct((L, dil * BW), BF16), jax.ShapeDtypeStruct((L, dil * BW), BF16),
                   jax.ShapeDtypeStruct((L, dil * BW), F32)],
        compiler_params=_cparams(("arbitrary", "arbitrary")),
    )(proj2, proj2, proj2, proj2, proj2, proj2, view(dout), view(dout), view(lse), view(lse), view(delta), view(delta))
    return dq.reshape(S, BW), dk.reshape(S, BW), dv.reshape(S, BW)


def _add3(a, b, c):
    S, W = a.shape

    def fn(i, j, rv, bv):
        return [rv[0] + rv[1] + rv[2]], []

    return _rowwise("add3", fn, [(a, W, _c(0), 0), (b, W, _c(0), 0), (c, W, _c(0), 0)], [], [(W, W, _c(0), BF16)], [],
                    nrows=S, tr=512)[0]


def _layer_fwd(x, p, cargo=None):
    h = _rms_fwd(x, p["norm_g"])
    proj = _mm_nn("in_proj", h, p["w_in"], F32)
    gpre = _mm_nn("gate_proj", h, p["w_gate"], F32, bias=p["b_gate"].reshape(1, GATE_W))
    (att_a, ya, runs_a), carried = _sb_fwd(proj, cargo)
    cpre, yb = _conv_fwd(proj, p["conv_w"], p["conv_b"], p["conv_ln_g"], p["conv_ln_b"])
    yc = _sgu_fwd(proj, p["sgu_w"], p["sgu_b"], p["sgu_ln_g"], p["sgu_ln_b"])
    dil = [_dil_fwd(proj, g) for g in range(3)]
    att_d, lse_d, yd = _dil_combine(proj, [d[0] for d in dil], [d[1] for d in dil])
    y = jnp.concatenate([ya, yb, yc, yd], axis=1)
    yproj = _mm_nn("branch_proj", y, p["w_branch"], F32, groups=4)
    merged = _merge_fwd(gpre, yproj)
    x_next = _mm_nn("out_proj", merged, p["w_out"], F32, res=x)
    saved = dict(x=x, h=h, proj=proj, gpre=gpre, att_a=att_a, runs_a=runs_a, cpre=cpre, att_d=att_d, lse_d=lse_d, y=y,
                 yproj=yproj, merged=merged)
    return x_next, saved, carried


def _layer_bwd(dout, s, p, pending=None):
    proj = s["proj"]
    dmerged = _mm_nt("out_proj_dx", dout, p["w_out"], F32)
    g_w_out = _mm_tn("out_proj_dw", s["merged"], dout)
    if pending is None:
        dyproj, dgpre, g_b_gate = _merge_bwd(dmerged, s["gpre"], s["yproj"])
        cargo = None
    else:
        grads_above, axes, scalars = pending
        (dyproj, dgpre, g_b_gate), got = _merge_bwd(dmerged, s["gpre"], s["yproj"], _swap_cargo(grads_above, axes))
        sums = _reduce_sums(grads_above, got, axes, scalars)
        cargo = _scatter_cargo(sums, axes)
    dy = _mm_nt("branch_proj_dx", dyproj, p["w_branch"], F32, groups=4)
    g_w_branch = _mm_tn("branch_proj_dw", s["y"], dyproj, groups=4)
    (d_aq, d_ak, d_av, d_ag), parts = _sb_bwd(proj, s["att_a"], s["runs_a"], dy, cargo)
    dc, d_bg, g_cln_g, g_cln_b, g_conv_b = _conv_bwd_norm(proj, s["cpre"], dy, p["conv_ln_g"], p["conv_ln_b"])
    d_ba, d_bb, g_conv_w = _conv_bwd_taps(proj, dc, p["conv_w"])
    d_cu, d_cv, d_cg, g_sgu_w, g_sgu_bias, g_sln_g, g_sln_b = _sgu_bwd(
        proj, dy, p["sgu_w"], p["sgu_b"], p["sgu_ln_g"], p["sgu_ln_b"])
    dout_d, d_dg, delta = _dil_gate_bwd(proj, s["att_d"], dy)
    dil = [_dil_bwd(proj, dout_d, s["lse_d"], delta, g) for g in range(3)]
    d_dv = _add3(dil[0][2], dil[1][2], dil[2][2])
    pieces = [d_aq, d_ak, d_av, d_ag, d_ba, d_bb, d_bg, d_cu, d_cv, d_cg,
              dil[0][0], dil[1][0], dil[2][0], dil[0][1], dil[1][1], dil[2][1], d_dv, d_dg]
    dproj = jnp.concatenate([t.astype(BF16) for t in pieces], axis=1)
    dh_gate = _mm_nt("gate_proj_dx", dgpre, p["w_gate"], F32)
    dh = _mm_nt("in_proj_dx", dproj, p["w_in"], F32, res=dh_gate)
    g_w_in = _mm_tn("in_proj_dw", s["h"], dproj)
    g_w_gate = _mm_tn("gate_proj_dw", s["h"], dgpre)
    dx, g_norm_g = _rms_bwd(s["x"], dh, dout, p["norm_g"])
    grads = dict(
        norm_g=g_norm_g.reshape(D_MODEL), w_in=g_w_in, conv_w=g_conv_w[:CONV_K], conv_b=g_conv_b.reshape(BW),
        conv_ln_g=g_cln_g.reshape(BW), conv_ln_b=g_cln_b.reshape(BW), sgu_ln_g=g_sln_g.reshape(BW),
        sgu_ln_b=g_sln_b.reshape(BW), sgu_w=g_sgu_w.reshape(NHEAD, HEAD, HEAD),
        sgu_b=jnp.sum(g_sgu_bias.reshape(NHEAD, HEAD, HEAD), axis=-1), w_branch=g_w_branch,
        w_gate=g_w_gate, b_gate=g_b_gate.reshape(GATE_W), w_out=g_w_out)
    return dx, grads, (None if pending is None else (sums, parts))


CHIP_FLIPS =((1, 0), (0, 1), (1, 1))


def _at(ref, axis, start, size):
    idx = [slice(None)] * len(ref.shape)
    idx[axis] = pl.ds(start, size)
    return ref.at[tuple(idx)]


def _position():
    return lax.axis_index("x"), lax.axis_index("y"), lax.axis_index("c")


def _gather(name, w, axis):
    n = w.shape[axis]
    full = list(w.shape)
    full[axis] = 4 * n

    def body(w_ref, out_ref, send_sems, recv_sems, local_sem):
        x, y, c = _position()
        mine = pltpu.make_async_copy(w_ref, _at(out_ref, axis, (2 * x + y) * n, n), local_sem)
        mine.start()
        sends = []
        for k, (fx, fy) in enumerate(CHIP_FLIPS):
            cp = pltpu.make_async_remote_copy(
                src_ref=w_ref, dst_ref=_at(out_ref, axis, (2 * x + y) * n, n), send_sem=send_sems.at[k],
                recv_sem=recv_sems.at[k], device_id=(x ^ fx, y ^ fy, c), device_id_type=MESH)
            cp.start()
            sends.append(cp)
        for k, (fx, fy) in enumerate(CHIP_FLIPS):
            px, py = x ^ fx, y ^ fy
            pltpu.make_async_remote_copy(
                src_ref=w_ref, dst_ref=_at(out_ref, axis, (2 * px + py) * n, n), send_sem=send_sems.at[k],
                recv_sem=recv_sems.at[k], device_id=(px, py, c), device_id_type=MESH).wait_recv()
        for cp in sends:
            cp.wait_send()
        mine.wait()

    return pl.pallas_call(
        body, name=name, in_specs=[ANY], out_specs=ANY, out_shape=jax.ShapeDtypeStruct(tuple(full), w.dtype),
        scratch_shapes=[pltpu.SemaphoreType.DMA((3,)), pltpu.SemaphoreType.DMA((3,)), pltpu.SemaphoreType.DMA],
    )(w)


def _block(ref, ax, shard, half):
    if shard is not None:
        n = ref.shape[ax] // 4
        ref = _at(ref, ax, shard * n, n)
    if half is not None:
        nh = ref.shape[1 - ax] // 2
        ref = _at(ref, 1 - ax, half * nh, nh)
    return ref


def _remote(src, dst, send_sems, recv_sems, k, device):
    return pltpu.make_async_remote_copy(src_ref=src, dst_ref=dst, send_sem=send_sems.at[k], recv_sem=recv_sems.at[k],
                                        device_id=device, device_id_type=MESH)


def _gather_cargo(fulls, axes):
    na = len(fulls)

    def start(ins, outs, send, recv):
        x, y, c = _position()
        for a in range(na):
            mine = _block(outs[a], axes[a], 2 * x + y, c)
            for k, (fx, fy) in enumerate(CHIP_FLIPS):
                _remote(mine, mine, send, recv, 3 * a + k, (x ^ fx, y ^ fy, c)).start()

    def finish(ins, outs, send, recv):
        x, y, c = _position()
        sib = (x, y, 1 - c)
        for a in range(na):
            for k, (fx, fy) in enumerate(CHIP_FLIPS):
                got = _block(outs[a], axes[a], 2 * (x ^ fx) + (y ^ fy), c)
                _remote(got, got, send, recv, 3 * a + k, sib).wait_recv()
                _remote(got, got, send, recv, 3 * (na + a) + k, sib).start()
        for a in range(na):
            mine = _block(outs[a], axes[a], 2 * x + y, c)
            for k, (fx, fy) in enumerate(CHIP_FLIPS):
                passed = _block(outs[a], axes[a], 2 * (x ^ fx) + (y ^ fy), c)
                theirs = _block(outs[a], axes[a], 2 * (x ^ fx) + (y ^ fy), 1 - c)
                _remote(theirs, theirs, send, recv, 3 * (na + a) + k, sib).wait_recv()
                _remote(passed, passed, send, recv, 3 * (na + a) + k, sib).wait_send()
                _remote(mine, mine, send, recv, 3 * a + k, sib).wait_send()

    shapes = [jax.ShapeDtypeStruct(f.shape, f.dtype) for f in fulls]
    return _Cargo(fulls, shapes, {a: a for a in range(na)}, 6 * na, start, finish)


def _scatter_cargo(sums, axes):
    na = len(sums)
    shapes = []
    for s, ax in zip(sums, axes):
        piece = list(s.shape)
        piece[ax] //= 4
        shapes.append(jax.ShapeDtypeStruct((3, *piece), s.dtype))

    def copies(ins, outs, send, recv):
        x, y, c = _position()
        return [_remote(_block(ins[a], axes[a], 2 * (x ^ fx) + (y ^ fy), None), outs[a].at[k], send, recv, 3 * a + k,
                        (x ^ fx, y ^ fy, c))
                for a in range(na) for k, (fx, fy) in enumerate(CHIP_FLIPS)]

    def start(ins, outs, send, recv):
        for cp in copies(ins, outs, send, recv):
            cp.start()

    def finish(ins, outs, send, recv):
        for cp in copies(ins, outs, send, recv):
            cp.wait()

    return _Cargo(sums, shapes, {}, 3 * na, start, finish)


def _swap_cargo(grads, axes):
    na = len(grads)
    shapes = []
    for g, ax in zip(grads, axes):
        half = list(g.shape)
        half[1 - ax] //= 2
        shapes.append(jax.ShapeDtypeStruct(tuple(half), g.dtype))

    def copies(ins, outs, send, recv):
        x, y, c = _position()
        return [_remote(_block(ins[a], axes[a], None, 1 - c), outs[a], send, recv, a, (x, y, 1 - c)) for a in range(na)]

    def start(ins, outs, send, recv):
        for cp in copies(ins, outs, send, recv):
            cp.start()

    def finish(ins, outs, send, recv):
        for cp in copies(ins, outs, send, recv):
            cp.wait()

    return _Cargo(grads, shapes, {}, na, start, finish)


def _join_halves(name, shards, axes):
    na = len(shards)

    def body(*refs):
        outs, (send, recv) = refs[na:2 * na], refs[2 * na:]
        x, y, c = _position()
        sib = (x, y, 1 - c)
        for a in range(na):
            mine = _block(outs[a], axes[a], None, c)
            _remote(mine, mine, send, recv, a, sib).start()
        for a in range(na):
            mine = _block(outs[a], axes[a], None, c)
            theirs = _block(outs[a], axes[a], None, 1 - c)
            _remote(theirs, theirs, send, recv, a, sib).wait_recv()
            _remote(mine, mine, send, recv, a, sib).wait_send()

    return pl.pallas_call(
        body, name=name, in_specs=[ANY] * na, out_specs=[ANY] * na,
        out_shape=[jax.ShapeDtypeStruct(s.shape, s.dtype) for s in shards],
        scratch_shapes=[pltpu.SemaphoreType.DMA((na,)), pltpu.SemaphoreType.DMA((na,))],
        input_output_aliases={a: a for a in range(na)},
    )(*shards)


def _sum_terms(name, terms, out_shape, out_dtype, blk, nblocks, out_map, scalars):
    tr, tc = blk

    def body(s_ref, *refs):
        acc = refs[0][...].astype(F32)
        for r in refs[1:-1]:
            acc = acc + r[...].astype(F32)
        refs[-1][...] = acc.astype(refs[-1].dtype)

    in_specs = [pl.BlockSpec((None, tr, tc) if a.ndim == 3 else (tr, tc), m) for a, m in terms]
    return pl.pallas_call(
        body, name=name,
        grid_spec=pltpu.PrefetchScalarGridSpec(
            num_scalar_prefetch=1, grid=(nblocks,), in_specs=in_specs, out_specs=pl.BlockSpec((tr, tc), out_map)),
        out_shape=jax.ShapeDtypeStruct(out_shape, out_dtype),
        compiler_params=_cparams(("arbitrary",)),
    )(scalars, *[a for a, _ in terms])


def _rows_for(cols, rows, budget=1 << 19):
    tr = 8
    while tr * 2 * cols <= budget and tr * 2 <= min(rows, 256) and rows % (tr * 2) == 0:
        tr *= 2
    assert rows % tr == 0
    return tr


def _place_shard(name, w, layer, ax, scalars):
    _, R, C = w.shape
    tr = _rows_for(C, R)
    nb = R // tr
    if ax == 0:
        shape, out_map = (4 * R, C), (lambda i, s: (s[1] * nb + i, 0))
    else:
        shape, out_map = (R, 4 * C), (lambda i, s: (i, s[1]))
    return _sum_terms(name, [(w, lambda i, s: (layer, i, 0))], shape, BF16, (tr, C), nb, out_map, scalars)


def _reduce_sums(grads, got, axes, scalars):
    sums = []
    for a, (g, h, ax) in enumerate(zip(grads, got, axes)):
        Rh, Ch = h.shape
        tr = _rows_for(Ch, Rh)
        nb = Rh // tr
        g_map = (lambda i, s, nb=nb: (s[0] * nb + i, 0)) if ax == 1 else (lambda i, s: (i, s[0]))
        sums.append(_sum_terms(f"rs_sum2_{a}", [(g, g_map), (h, lambda i, s: (i, 0))], (Rh, Ch), BF16, (tr, Ch), nb,
                               lambda i, s: (i, 0), scalars))
    return sums


def _reduce_end(sums, parts, axes, scalars):
    shards = []
    for a, (s1, pt, ax) in enumerate(zip(sums, parts, axes)):
        _, Rs, Cs = pt.shape
        tr = _rows_for(Cs, Rs)
        nb = Rs // tr
        if ax == 1:
            shape, s_map, o_map = (2 * Rs, Cs), (lambda i, s: (i, s[1])), (lambda i, s, nb=nb: (s[0] * nb + i, 0))
        else:
            shape, s_map, o_map = (Rs, 2 * Cs), (lambda i, s, nb=nb: (s[1] * nb + i, 0)), (lambda i, s: (i, s[0]))
        terms = [(s1, s_map)] + [(pt, (lambda i, s, k=k: (k, i, 0))) for k in range(3)]
        shards.append(_sum_terms(f"rs_sum4_{a}", terms, shape, F32, (tr, Cs), nb, o_map, scalars))
    return _join_halves("rs_join", shards, axes)


def _all_to_all_small(buf):
    nr = buf.shape[0]

    def body(b_ref, out_ref, send_sems, recv_sems, local_sem):
        x, y, c = _position()
        me = 4 * x + 2 * y + c
        mine = pltpu.make_async_copy(b_ref, out_ref.at[me], local_sem)
        mine.start()
        sends = []
        for r in range(1, 8):
            peer = (x ^ (r >> 2), y ^ ((r >> 1) & 1), c ^ (r & 1))
            cp = pltpu.make_async_remote_copy(
                src_ref=b_ref, dst_ref=out_ref.at[me], send_sem=send_sems.at[r - 1], recv_sem=recv_sems.at[r - 1],
                device_id=peer, device_id_type=MESH)
            cp.start()
            sends.append(cp)
        for r in range(1, 8):
            px, py, pc = x ^ (r >> 2), y ^ ((r >> 1) & 1), c ^ (r & 1)
            pltpu.make_async_remote_copy(
                src_ref=b_ref, dst_ref=out_ref.at[4 * px + 2 * py + pc], send_sem=send_sems.at[r - 1],
                recv_sem=recv_sems.at[r - 1], device_id=(px, py, pc), device_id_type=MESH).wait_recv()
        for cp in sends:
            cp.wait_send()
        mine.wait()

    return pl.pallas_call(
        body, name="small_exchange", in_specs=[ANY], out_specs=ANY,
        out_shape=jax.ShapeDtypeStruct((8, nr, 128), buf.dtype),
        scratch_shapes=[pltpu.SemaphoreType.DMA((7,)), pltpu.SemaphoreType.DMA((7,)), pltpu.SemaphoreType.DMA],
    )(buf)


SMALL_TR = 256


def _all_reduce_small(buf):
    nr = buf.shape[0]
    slots = _all_to_all_small(buf).reshape(8 * nr, 128)
    nblk = nr // SMALL_TR

    def fn(i, j, rv, bv):
        acc = rv[0]
        for v in rv[1:]:
            acc = acc + v
        return [acc], []

    rows = [(slots, 128, _c(0), (lambda i, d=d: d * nblk + i)) for d in range(8)]
    return _rowwise("small_sum", fn, rows, [], [(128, 128, _c(0), F32)], [], nrows=nr, tr=SMALL_TR)[0]


def _adamw(name, w, g, m, v):
    rows, cols = w.shape
    tr = _rows_for(cols, rows, budget=1 << 18)

    def fn(i, j, rv, bv):
        wv, gv, mv, vv = rv
        m2 = ADAM_B1 * mv + (1.0 - ADAM_B1) * gv
        v2 = ADAM_B2 * vv + (1.0 - ADAM_B2) * (gv * gv)
        m_hat = m2 / (1.0 - ADAM_B1 ** ADAM_STEP)
        v_hat = v2 / (1.0 - ADAM_B2 ** ADAM_STEP)
        delta = -ADAM_LR * (m_hat / (jnp.sqrt(v_hat) + ADAM_EPS) + ADAM_WD * wv)
        return [delta, m2, v2], []

    return _rowwise(name, fn, [(a, cols, _c(0), 0) for a in (w, g, m, v)], [], [(cols, cols, _c(0), F32)] * 3, [],
                    nrows=rows, tr=tr)


WEIGHTS = ("norm_g", "w_in", "conv_w", "conv_b", "conv_ln_g", "conv_ln_b", "sgu_ln_g", "sgu_ln_b", "sgu_w", "sgu_b",
           "w_branch", "w_gate", "b_gate", "w_out", "final_g")
BIG = ("w_in", "w_branch", "w_gate", "w_out")
BIG_AXIS = dict(w_in=1, w_branch=1, w_gate=1, w_out=0)
SMALL = tuple(n for n in WEIGHTS if n not in BIG)


def _pack(arrays, pad_rows):
    flat = jnp.concatenate([a.reshape(-1).astype(F32) for a in arrays])
    unit = 128 * pad_rows
    total = -(-flat.shape[0] // unit) * unit
    return jnp.pad(flat, (0, total - flat.shape[0])).reshape(total // 128, 128)


def _unpack(buf, shapes):
    flat = buf.reshape(-1)
    out, off = [], 0
    for shp in shapes:
        size = math.prod(shp)
        out.append(flat[off:off + size].reshape(shp))
        off += size
    return out


def kernel(x, norm_g, w_in, conv_w, conv_b, conv_ln_g, conv_ln_b, sgu_ln_g, sgu_ln_b, sgu_w, sgu_b, w_branch, w_gate, b_gate, w_out, final_g, loss_target, m_norm_g, m_w_in, m_conv_w, m_conv_b, m_conv_ln_g, m_conv_ln_b, m_sgu_ln_g, m_sgu_ln_b, m_sgu_w, m_sgu_b, m_w_branch, m_w_gate, m_b_gate, m_w_out, m_final_g, v_norm_g, v_w_in, v_conv_w, v_conv_b, v_conv_ln_g, v_conv_ln_b, v_sgu_ln_g, v_sgu_ln_b, v_sgu_w, v_sgu_b, v_w_branch, v_w_gate, v_b_gate, v_w_out, v_final_g):
    w = dict(norm_g=norm_g, w_in=w_in, conv_w=conv_w, conv_b=conv_b, conv_ln_g=conv_ln_g, conv_ln_b=conv_ln_b,
             sgu_ln_g=sgu_ln_g, sgu_ln_b=sgu_ln_b, sgu_w=sgu_w, sgu_b=sgu_b, w_branch=w_branch, w_gate=w_gate,
             b_gate=b_gate, w_out=w_out, final_g=final_g)
    m = dict(norm_g=m_norm_g, w_in=m_w_in, conv_w=m_conv_w, conv_b=m_conv_b, conv_ln_g=m_conv_ln_g,
             conv_ln_b=m_conv_ln_b, sgu_ln_g=m_sgu_ln_g, sgu_ln_b=m_sgu_ln_b, sgu_w=m_sgu_w, sgu_b=m_sgu_b,
             w_branch=m_w_branch, w_gate=m_w_gate, b_gate=m_b_gate, w_out=m_w_out, final_g=m_final_g)
    v = dict(norm_g=v_norm_g, w_in=v_w_in, conv_w=v_conv_w, conv_b=v_conv_b, conv_ln_g=v_conv_ln_g,
             conv_ln_b=v_conv_ln_b, sgu_ln_g=v_sgu_ln_g, sgu_ln_b=v_sgu_ln_b, sgu_w=v_sgu_w, sgu_b=v_sgu_b,
             w_branch=v_w_branch, w_gate=v_w_gate, b_gate=v_b_gate, w_out=v_w_out, final_g=v_final_g)
    depth = w_in.shape[0]
    chip = 2 * lax.axis_index("x") + lax.axis_index("y")
    scalars = jnp.stack([lax.axis_index("c"), chip]).astype(jnp.int32)
    axes = [BIG_AXIS[n] for n in BIG]

    stacked = dict(w_in=w_in, w_branch=w_branch.reshape(depth, 4 * BW, -1), w_gate=w_gate, w_out=w_out)
    buffers = [[_place_shard(f"place_{n}", stacked[n], l, BIG_AXIS[n], scalars) for n in BIG] for l in range(depth)]
    conv_w_full = _gather("gather_conv_w", conv_w, 2)
    gathered = _run_cargo("gather_first", _gather_cargo(buffers[0], axes))
    h = x[0]
    saved, layer_params = [], []
    for l in range(depth):
        p = {n: w[n][l] for n in SMALL if n not in ("final_g", "conv_w")}
        p["conv_w"] = conv_w_full[l]
        p.update(dict(zip(BIG, gathered)))
        cargo = _gather_cargo(buffers[l + 1], axes) if l + 1 < depth else None
        h, s, gathered = _layer_fwd(h, p, cargo)
        saved.append(s)
        layer_params.append(p)
    dh, g_final, loss = _final_loss(h, loss_target[0], final_g)

    layer_grads, reduced_big = [None] * depth, [None] * depth
    pending = None
    for l in reversed(range(depth)):
        dh, layer_grads[l], exchanged = _layer_bwd(dh, saved[l], layer_params[l], pending)
        if exchanged is not None:
            reduced_big[l + 1] = _reduce_end(*exchanged, axes, scalars)
        pending = ([layer_grads[l][n] for n in BIG], axes, scalars)
    sums = _reduce_sums(pending[0], _run_cargo("rs_halves_last", _swap_cargo(pending[0], axes)), axes, scalars)
    reduced_big[0] = _reduce_end(sums, _run_cargo("rs_shards_last", _scatter_cargo(sums, axes)), axes, scalars)
    grad_x = dh[None]
    local = {n: jnp.stack([layer_grads[l][n] for l in range(depth)]) for n in SMALL if n != "final_g"}
    local["final_g"] = g_final.reshape(-1)
    grads = {n: jnp.stack([reduced_big[l][a] for l in range(depth)]).reshape(w[n].shape) for a, n in enumerate(BIG)}
    small_shapes = [local[n].shape for n in SMALL] + [(128,)]
    reduced = _unpack(_all_reduce_small(_pack([local[n] for n in SMALL] + [loss.reshape(128)], SMALL_TR)), small_shapes)
    for n, r in zip(SMALL, reduced[:-1]):
        grads[n] = r
    loss_out = reduced[-1][0]
    ncw = conv_w.shape[2]
    grads["conv_w"] = lax.dynamic_slice_in_dim(grads["conv_w"], chip * ncw, ncw, axis=2)

    delta, new_m, new_v = {}, {}, {}
    for n in BIG:
        cols = w[n].shape[-1]
        d2, m2, v2 = _adamw(f"adamw_{n}", *[a.reshape(-1, cols) for a in (w[n], grads[n], m[n], v[n])])
        delta[n], new_m[n], new_v[n] = d2.reshape(w[n].shape), m2.reshape(w[n].shape), v2.reshape(w[n].shape)
    shapes = [w[n].shape for n in SMALL]
    packed = [_pack([t[n] for n in SMALL], SMALL_TR) for t in (w, grads, m, v)]
    outs = _adamw("adamw_small", *packed)
    for res, o in zip((delta, new_m, new_v), outs):
        for n, a in zip(SMALL, _unpack(o, shapes)):
            res[n] = a
    return (loss_out, grad_x, *[grads[n] for n in WEIGHTS], *[delta[n] for n in WEIGHTS],
            *[new_m[n] for n in WEIGHTS], *[new_v[n] for n in WEIGHTS])
```

```python
import functools
import math

import jax
import jax.numpy as jnp
from jax import lax
from jax.experimental import pallas as pl
from jax.experimental.pallas import tpu as pltpu

F32 = jnp.float32
BF16 = jnp.bfloat16
MESH = pl.DeviceIdType.MESH

DEPTH = 4
D_MODEL = 2048
HEAD = 128
NHEAD = 8
BW = 1024
IN_WIDTH = 18432
GATE_W = 4 * D_MODEL
NORM_EPS = 1e-6
CONV_K = 31
HALO = 32
DIL_PATTERNS = ((128, 1), (512, 4), (2048, 16))
C_AQ, C_AK, C_AV, C_AG, C_BA, C_BB, C_BG, C_CU, C_CV, C_CG, C_DQ, C_DK, C_DV, C_DG = (
    0, 1, 2, 3, 4, 5, 6, 7, 8, 9, 10, 13, 16, 17)
ADAM_LR, ADAM_B1, ADAM_B2, ADAM_EPS, ADAM_WD, ADAM_STEP = 0.001, 0.9, 0.999, 1e-08, 0.01, 10
VMEM_LIMIT = 56 * 1024 * 1024

NN = (((1,), (0,)), ((), ()))
NT = (((1,), (1,)), ((), ()))
TN = (((0,), (0,)), ((), ()))


def _cparams(sem):
    return pltpu.CompilerParams(dimension_semantics=sem, vmem_limit_bytes=VMEM_LIMIT)


def _dot(a, b, dims):
    return lax.dot_general(a.astype(BF16), b.astype(BF16), dims, preferred_element_type=F32)


def _sigmoid(x):
    return 1.0 / (1.0 + jnp.exp(-x))


def _silu_and_grad(x):
    s = _sigmoid(x)
    return x * s, s * (1.0 + x * (1.0 - s))


_GELU_C = math.sqrt(2.0 / math.pi)


def _gelu_and_grad(x):
    t = jnp.tanh(_GELU_C * (x + 0.044715 * x * x * x))
    y = 0.5 * x * (1.0 + t)
    dy = 0.5 * (1.0 + t) + 0.5 * x * (1.0 - t * t) * _GELU_C * (1.0 + 3.0 * 0.044715 * x * x)
    return y, dy


def _ln_fwd(x, g, b):
    mu = jnp.mean(x, axis=-1, keepdims=True)
    xc = x - mu
    var = jnp.mean(xc * xc, axis=-1, keepdims=True)
    rstd = lax.rsqrt(var + NORM_EPS)
    xh = xc * rstd
    return xh * g + b, xh, rstd


def _ln_bwd(dy, xh, rstd, g):
    dxh = dy * g
    dx = rstd * (dxh - jnp.mean(dxh, axis=-1, keepdims=True) - xh * jnp.mean(dxh * xh, axis=-1, keepdims=True))
    return dx, jnp.sum(dy * xh, axis=0, keepdims=True), jnp.sum(dy, axis=0, keepdims=True)


def _rowwise(name, fn, rows, bcast, outs, accs, *, nrows, tr, nj=1, cargo=None):
    ni = nrows // tr
    nr, nb, no = len(rows), len(bcast), len(outs)

    def rmap(colfn, shift):
        if callable(shift):
            return lambda j, i: (shift(i), colfn(j))
        if shift == 0:
            return lambda j, i: (i, colfn(j))
        return lambda j, i: (jnp.clip(i + shift, 0, ni - 1), colfn(j))

    in_specs = [pl.BlockSpec((tr, w), rmap(cf, sh)) for (_, w, cf, sh) in rows]
    in_specs += [pl.BlockSpec(b.shape, lambda j, i, nd=b.ndim: (0,) * nd) for b in bcast]
    out_specs = [pl.BlockSpec((tr, w), rmap(cf, 0)) for (_, w, cf, _) in outs]
    out_specs += [pl.BlockSpec((r, w), lambda j, i, cf=cf: (0, cf(j))) for (r, _, w, cf) in accs]
    out_shape = [jax.ShapeDtypeStruct((nrows, nc), dt) for (nc, _, _, dt) in outs]
    out_shape += [jax.ShapeDtypeStruct((r, nc), F32) for (r, nc, _, _) in accs]

    def body(*refs):
        j = pl.program_id(0)
        i = pl.program_id(1)
        rv = [r[...] for r in refs[:nr]]
        bv = [r[...] for r in refs[nr:nr + nb]]
        o_refs = refs[nr + nb:nr + nb + no]
        a_refs = refs[nr + nb + no:]
        o_vals, a_vals = fn(i, j, rv, bv)
        for ref, val in zip(o_refs, o_vals):
            ref[...] = val.astype(ref.dtype)
        if a_refs:
            @pl.when(i == 0)
            def _():
                for ref in a_refs:
                    ref[...] = jnp.zeros(ref.shape, F32)
            for ref, val in zip(a_refs, a_vals):
                ref[...] += val

    res, carried = _call_with_cargo(body, cargo, name=name, grid=(nj, ni), in_specs=in_specs, out_specs=out_specs,
                                    out_shape=out_shape, scratch_shapes=[], operands=(*[r[0] for r in rows], *bcast))
    return res if cargo is None else (res, carried)


def _c(k):
    return lambda j: k


def _mm(name, a, b, dims, *, a_blk, a_map, b_blk, b_map, o_shape, o_blk, o_map, grid, o_dtype,
        bias=None, bias_blk=None, bias_map=None, res=None):
    nk = grid[3]
    extra, extra_specs = [], []
    if bias is not None:
        extra.append(bias)
        extra_specs.append(pl.BlockSpec(bias_blk, bias_map))
    if res is not None:
        extra.append(res)
        extra_specs.append(pl.BlockSpec(o_blk, o_map))
    nbias = bias is not None
    nres = res is not None

    def body(*refs):
        a_ref, b_ref = refs[0], refs[1]
        idx = 2
        bias_ref = refs[idx] if nbias else None
        idx += nbias
        res_ref = refs[idx] if nres else None
        idx += nres
        o_ref = refs[idx]
        acc_ref = refs[idx + 1]
        k = pl.program_id(3)
        part = _dot(a_ref[...], b_ref[...], dims)

        def finish(r):
            if nbias:
                r = r + bias_ref[...]
            if nres:
                r = r + res_ref[...]
            o_ref[...] = r.astype(o_ref.dtype)

        if nk == 1:
            finish(part)
        else:
            @pl.when(k == 0)
            def _():
                acc_ref[...] = part

            @pl.when(k > 0)
            def _():
                acc_ref[...] += part

            @pl.when(k == nk - 1)
            def _():
                finish(acc_ref[...])

    acc_shape = o_blk if nk > 1 else (8, 128)
    return pl.pallas_call(
        body, name=name, grid=grid,
        in_specs=[pl.BlockSpec(a_blk, a_map), pl.BlockSpec(b_blk, b_map)] + extra_specs,
        out_specs=pl.BlockSpec(o_blk, o_map),
        out_shape=jax.ShapeDtypeStruct(o_shape, o_dtype),
        scratch_shapes=[pltpu.VMEM(acc_shape, F32)],
        compiler_params=_cparams(("arbitrary", "arbitrary", "arbitrary", "arbitrary")),
    )(a, b, *extra)


def _mm_nn(name, a, b, o_dtype, *, bias=None, res=None, groups=1, tm=1024, tn=1024):
    M = a.shape[0]
    K = a.shape[1] // groups
    N = b.shape[1]
    tm, tn = min(tm, M), min(tn, N)
    njn = N // tn
    return _mm(name, a, b, NN,
               a_blk=(tm, K), a_map=lambda g, j, i, k: (i, g),
               b_blk=(K, tn), b_map=lambda g, j, i, k: (g, j),
               o_shape=(M, groups * N), o_blk=(tm, tn), o_map=lambda g, j, i, k: (i, g * njn + j),
               grid=(groups, njn, M // tm, 1), o_dtype=o_dtype,
               bias=bias, bias_blk=(1, tn), bias_map=lambda g, j, i, k: (0, g * njn + j), res=res)


def _mm_nt(name, a, b, o_dtype, *, res=None, groups=1, tm=512, tk=2048):
    M = a.shape[0]
    K = a.shape[1] // groups
    N = b.shape[0] // groups
    tm, tk = min(tm, M), min(tk, K)
    nk = K // tk
    return _mm(name, a, b, NT,
               a_blk=(tm, tk), a_map=lambda g, j, i, k: (i, g * nk + k),
               b_blk=(N, tk), b_map=lambda g, j, i, k: (g, k),
               o_shape=(M, groups * N), o_blk=(tm, N), o_map=lambda g, j, i, k: (i, g),
               grid=(groups, 1, M // tm, nk), o_dtype=o_dtype, res=res)


def _mm_tn(name, a, b, *, groups=1, tm=1024, tn=1024, tk=2048):
    K = a.shape[0]
    M = a.shape[1] // groups
    N = b.shape[1] // groups
    tm, tn, tk = min(tm, M), min(tn, N), min(tk, K)
    nim, njn = M // tm, N // tn
    return _mm(name, a, b, TN,
               a_blk=(tk, tm), a_map=lambda g, j, i, k: (k, g * nim + i),
               b_blk=(tk, tn), b_map=lambda g, j, i, k: (k, g * njn + j),
               o_shape=(groups * M, N), o_blk=(tm, tn), o_map=lambda g, j, i, k: (g * nim + i, j),
               grid=(groups, njn, nim, K // tk), o_dtype=F32)


def _rms_fwd(x, g):
    S, D = x.shape

    def fn(i, j, rv, bv):
        xv, gv = rv[0], bv[0]
        r = lax.rsqrt(jnp.mean(xv * xv, axis=-1, keepdims=True) + NORM_EPS)
        return [xv * r * gv], []

    return _rowwise("rms_fwd", fn, [(x, D, _c(0), 0)], [g.reshape(1, D)], [(D, D, _c(0), BF16)], [],
                    nrows=S, tr=512)[0]


def _rms_bwd_math(xv, dyv, gv):
    r = lax.rsqrt(jnp.mean(xv * xv, axis=-1, keepdims=True) + NORM_EPS)
    xh = xv * r
    dg = jnp.sum(dyv * xh, axis=0, keepdims=True)
    dxh = dyv * gv
    dx = r * (dxh - xh * jnp.mean(dxh * xh, axis=-1, keepdims=True))
    return dx, dg


def _rms_bwd(x, dh, dout, g):
    S, D = x.shape

    def fn(i, j, rv, bv):
        dx, dg = _rms_bwd_math(rv[0], rv[1], bv[0])
        return [rv[2] + dx], [dg]

    return _rowwise("rms_bwd", fn, [(x, D, _c(0), 0), (dh, D, _c(0), 0), (dout, D, _c(0), 0)], [g.reshape(1, D)],
                    [(D, D, _c(0), F32)], [(1, D, D, _c(0))], nrows=S, tr=256)


def _final_loss(x, target, g):
    S, D = x.shape

    def fn(i, j, rv, bv):
        xv, tv, gv = rv[0], rv[1], bv[0]
        r = lax.rsqrt(jnp.mean(xv * xv, axis=-1, keepdims=True) + NORM_EPS)
        err = xv * r * gv - tv
        loss = 0.5 * jnp.sum(jnp.mean(err * err, axis=-1, keepdims=True), axis=0, keepdims=True)
        dx, dg = _rms_bwd_math(xv, err * (1.0 / D), gv)
        return [dx], [dg, jnp.broadcast_to(loss, (1, 128))]

    return _rowwise("final_loss", fn, [(x, D, _c(0), 0), (target, D, _c(0), 0)], [g.reshape(1, D)],
                    [(D, D, _c(0), F32)], [(1, D, D, _c(0)), (1, 128, 128, _c(0))], nrows=S, tr=256)


MERGE_W = 512


def _merge_fwd(gpre, yproj):
    S = gpre.shape[0]
    nj = D_MODEL // MERGE_W

    def fn(i, j, rv, bv):
        acc = _sigmoid(rv[0]) * rv[4]
        for n in range(1, 4):
            acc = acc + _sigmoid(rv[n]) * rv[4 + n]
        return [acc], []

    rows = [(gpre, MERGE_W, (lambda j, n=n: n * nj + j), 0) for n in range(4)]
    rows += [(yproj, MERGE_W, (lambda j, n=n: n * nj + j), 0) for n in range(4)]
    return _rowwise("merge_fwd", fn, rows, [], [(D_MODEL, MERGE_W, lambda j: j, BF16)], [],
                    nrows=S, tr=512, nj=nj)[0]


def _merge_bwd(dmerged, gpre, yproj, cargo=None):
    S = gpre.shape[0]
    nj4 = D_MODEL // MERGE_W

    def fn(i, j, rv, bv):
        dm, gp, yp = rv
        sg = _sigmoid(gp)
        dgp = dm * yp * sg * (1.0 - sg)
        return [dm * sg, dgp], [jnp.sum(dgp, axis=0, keepdims=True)]

    rows = [(dmerged, MERGE_W, lambda j: j % nj4, 0), (gpre, MERGE_W, lambda j: j, 0), (yproj, MERGE_W, lambda j: j, 0)]
    return _rowwise("merge_bwd", fn, rows, [],
                    [(GATE_W, MERGE_W, lambda j: j, BF16), (GATE_W, MERGE_W, lambda j: j, BF16)],
                    [(1, GATE_W, MERGE_W, lambda j: j)], nrows=S, tr=512, nj=4 * nj4, cargo=cargo)


SB_TQ = 1024
SB_TQ_BWD = 512
SB_TK = 512
SB_TRI = 256


def _softplus(z):
    bits, sign = (jnp.uint32, 0x80000000) if z.dtype == F32 else (jnp.uint16, 0x8000)
    neg_abs = lax.bitcast_convert_type(lax.bitcast_convert_type(z, bits) | bits(sign), z.dtype)
    return jnp.maximum(z, 0.0) + jnp.log(1.0 + jnp.exp(neg_abs))


def _sb_sum_matrices():
    r = lax.broadcasted_iota(jnp.int32, (SB_TRI, SB_TRI), 0)
    c = lax.broadcasted_iota(jnp.int32, (SB_TRI, SB_TRI), 1)
    return (r > c).astype(BF16), (r < c).astype(BF16)


ANY =pl.BlockSpec(memory_space=pl.ANY)


class _Cargo:
    def __init__(self, arrays, out_shapes, aliases, nsem, start, finish):
        self.arrays, self.out_shapes, self.aliases, self.nsem = list(arrays), list(out_shapes), dict(aliases), nsem
        self.start, self.finish = start, finish


def _call_with_cargo(body, cargo, *, name, grid, in_specs, out_specs, out_shape, scratch_shapes, operands):
    sem = ("arbitrary",) * len(grid)
    if cargo is None:
        res = pl.pallas_call(body, name=name, grid=grid, in_specs=in_specs, out_specs=out_specs, out_shape=out_shape,
                             scratch_shapes=scratch_shapes, compiler_params=_cparams(sem))(*operands)
        return res, []
    n_in, n_out, n_scr = len(in_specs), len(out_specs), len(scratch_shapes)
    nci, nco = len(cargo.arrays), len(cargo.out_shapes)

    def wrapped(*refs):
        ins, refs = refs[:n_in], refs[n_in:]
        cin, refs = refs[:nci], refs[nci:]
        outs, refs = refs[:n_out], refs[n_out:]
        cout, refs = refs[:nco], refs[nco:]
        scr, (send, recv) = refs[:n_scr], refs[n_scr:]
        ids = [pl.program_id(a) for a in range(len(grid))]
        first = functools.reduce(lambda p, q: p & q, [g == 0 for g in ids])
        last = functools.reduce(lambda p, q: p & q, [g == n - 1 for g, n in zip(ids, grid)])

        @pl.when(first)
        def _():
            cargo.start(cin, cout, send, recv)

        body(*ins, *outs, *scr)

        @pl.when(last)
        def _():
            cargo.finish(cin, cout, send, recv)

    res = pl.pallas_call(
        wrapped, name=name, grid=grid, in_specs=list(in_specs) + [ANY] * nci, out_specs=list(out_specs) + [ANY] * nco,
        out_shape=list(out_shape) + cargo.out_shapes,
        scratch_shapes=list(scratch_shapes) + [pltpu.SemaphoreType.DMA((cargo.nsem,)), pltpu.SemaphoreType.DMA((cargo.nsem,))],
        input_output_aliases={n_in + k: n_out + v for k, v in cargo.aliases.items()},
        compiler_params=_cparams(sem))(*operands, *cargo.arrays)
    return res[:n_out], res[n_out:]


def _run_cargo(name, cargo):
    nci = len(cargo.arrays)

    def body(*refs):
        cin, cout, (send, recv) = refs[:nci], refs[nci:-2], refs[-2:]
        cargo.start(cin, cout, send, recv)
        cargo.finish(cin, cout, send, recv)

    return pl.pallas_call(
        body, name=name, in_specs=[ANY] * nci, out_specs=[ANY] * len(cargo.out_shapes), out_shape=cargo.out_shapes,
        scratch_shapes=[pltpu.SemaphoreType.DMA((cargo.nsem,)), pltpu.SemaphoreType.DMA((cargo.nsem,))],
        input_output_aliases=cargo.aliases)(*cargo.arrays)


def _sb_fwd(proj, cargo=None):
    S = proj.shape[0]
    tq, tk = min(SB_TQ, S), SB_TK
    assert S // SB_TRI <= HEAD
    nq = S // tq
    nkb = tq // tk
    nsub = tk // SB_TRI
    scale = HEAD ** -0.5

    def body(q_ref, k_ref, v_ref, g_ref, o_ref, y_ref, r_ref, kb_ref, vb_ref):
        i = pl.program_id(1)

        @pl.when(i == 0)
        def _():
            kb_ref[...] = k_ref[...].astype(BF16)
            vb_ref[...] = v_ref[...].astype(BF16)

        qb = (q_ref[...] * scale).astype(BF16)
        qpos = i * tq + lax.broadcasted_iota(jnp.int32, (tq, tk), 0)
        kidx = lax.broadcasted_iota(jnp.int32, (tq, tk), 1)
        lane = lax.broadcasted_iota(jnp.int32, (tq, HEAD), 1)
        upper, _ = _sb_sum_matrices()

        def block(kblock, masked, carry):
            acc, run, runs = carry
            off = pl.multiple_of(kblock * tk, tk)
            zz = _dot(qb, kb_ref[pl.ds(off, tk), :], NT).astype(BF16)
            sp = _softplus(zz)
            if masked:
                keep = (off + kidx) < qpos
                sp_sum = jnp.where(keep, sp, 0.0)
            else:
                sp_sum = sp
            right, after = [None] * nsub, [None] * nsub
            for j in reversed(range(nsub)):
                cols = slice(j * SB_TRI, (j + 1) * SB_TRI)
                right[j] = run
                suffix = _dot(sp_sum[:, cols], upper, NN)
                after[j] = (suffix + run).astype(BF16)
                run = run + suffix[:, :1] + sp_sum[:, j * SB_TRI:j * SB_TRI + 1].astype(F32)
                runs = jnp.where(lane == kblock * nsub + j, right[j], runs)
            w = jnp.exp((zz - sp) - jnp.concatenate(after, axis=1))
            if masked:
                w = jnp.where(keep, w, 0.0)
            return acc + _dot(w, vb_ref[pl.ds(off, tk), :], NN), run, runs

        runs0 = jnp.where(qpos[:, :HEAD] + lane < 0, 1.0, 0.0)
        carry = (jnp.zeros((tq, HEAD), F32), jnp.zeros((tq, 1), F32), runs0)
        for kbl in reversed(range(nkb)):
            carry = block(i * nkb + kbl, True, carry)
        acc, _, runs = lax.fori_loop(0, i * nkb, lambda n, c: block(i * nkb - 1 - n, False, c), carry)
        o_ref[...] = acc
        r_ref[...] = runs
        gate, _ = _silu_and_grad(g_ref[...])
        y_ref[...] = (acc * gate).astype(BF16)

    return _call_with_cargo(
        body, cargo, name="sb_fwd", grid=(NHEAD, nq),
        in_specs=[pl.BlockSpec((tq, HEAD), lambda h, i: (i, C_AQ * NHEAD + h)),
                  pl.BlockSpec((S, HEAD), lambda h, i: (0, C_AK * NHEAD + h)),
                  pl.BlockSpec((S, HEAD), lambda h, i: (0, C_AV * NHEAD + h)),
                  pl.BlockSpec((tq, HEAD), lambda h, i: (i, C_AG * NHEAD + h))],
        out_specs=[pl.BlockSpec((tq, HEAD), lambda h, i: (i, h))] * 3,
        out_shape=[jax.ShapeDtypeStruct((S, BW), F32), jax.ShapeDtypeStruct((S, BW), BF16),
                   jax.ShapeDtypeStruct((S, BW), F32)],
        scratch_shapes=[pltpu.VMEM((S, HEAD), BF16), pltpu.VMEM((S, HEAD), BF16)],
        operands=(proj, proj, proj, proj))


def _sb_bwd(proj, att, runs, dy, cargo=None):
    S = proj.shape[0]
    tq, tk = min(SB_TQ_BWD, S), SB_TK
    nq = S // tq
    nkb = tq // tk
    nsub = tk // SB_TRI
    scale = HEAD ** -0.5

    def body(q_ref, k_ref, v_ref, g_ref, o_ref, r_ref, dy_ref, dq_ref, dk_ref, dv_ref, dg_ref, kb_ref, vb_ref):
        i = pl.program_id(1)

        @pl.when(i == 0)
        def _():
            kb_ref[...] = k_ref[...].astype(BF16)
            vb_ref[...] = v_ref[...].astype(BF16)
            dk_ref[...] = jnp.zeros(dk_ref.shape, F32)
            dv_ref[...] = jnp.zeros(dv_ref.shape, F32)

        gate, dgate = _silu_and_grad(g_ref[...])
        dyv = dy_ref[...]
        dg_ref[...] = dyv * o_ref[...] * dgate
        dob = (dyv * gate).astype(BF16)
        qb = (q_ref[...] * scale).astype(BF16)
        runs = r_ref[...]
        qpos = i * tq + lax.broadcasted_iota(jnp.int32, (tq, tk), 0)
        kidx = lax.broadcasted_iota(jnp.int32, (tq, tk), 1)
        lane = lax.broadcasted_iota(jnp.int32, (tq, HEAD), 1)
        upper, lower = _sb_sum_matrices()

        def block(kblock, masked, carry):
            dq, grun = carry
            off = pl.multiple_of(kblock * tk, tk)
            kblk = kb_ref[pl.ds(off, tk), :]
            zz = _dot(qb, kblk, NT).astype(BF16)
            sp = _softplus(zz)
            lb = zz - sp
            if masked:
                keep = (off + kidx) < qpos
                sp_sum = jnp.where(keep, sp, 0.0)
            else:
                sp_sum = sp
            after = []
            for j in range(nsub):
                cols = slice(j * SB_TRI, (j + 1) * SB_TRI)
                run = jnp.sum(jnp.where(lane == kblock * nsub + j, runs, 0.0), axis=1, keepdims=True)
                after.append((_dot(sp_sum[:, cols], upper, NN) + run).astype(BF16))
            w = jnp.exp(lb - jnp.concatenate(after, axis=1))
            if masked:
                w = jnp.where(keep, w, 0.0)
            gw = _dot(dob, vb_ref[pl.ds(off, tk), :], NT).astype(BF16) * w
            gsum = []
            for j in range(nsub):
                cols = slice(j * SB_TRI, (j + 1) * SB_TRI)
                prefix = _dot(gw[:, cols], lower, NN)
                gsum.append((prefix + grun).astype(BF16))
                last = (j + 1) * SB_TRI - 1
                grun = grun + prefix[:, SB_TRI - 1:] + gw[:, last:last + 1].astype(F32)
            dz = gw - (gw + jnp.concatenate(gsum, axis=1)) * jnp.exp(lb)
            if masked:
                dz = jnp.where(keep, dz, 0.0)
            dk_ref[pl.ds(off, tk), :] += _dot(dz, qb, TN)
            dv_ref[pl.ds(off, tk), :] += _dot(w, dob, TN)
            return dq + _dot(dz, kblk, NN), grun

        carry = lax.fori_loop(0, i * nkb, lambda n, c: block(n, False, c),
                              (jnp.zeros((tq, HEAD), F32), jnp.zeros((tq, 1), F32)))
        for kbl in range(nkb):
            carry = block(i * nkb + kbl, True, carry)
        dq_ref[...] = carry[0] * scale

    blk = lambda h, i: (i, h)
    head = lambda h, i: (0, h)
    return _call_with_cargo(
        body, cargo, name="sb_bwd", grid=(NHEAD, nq),
        in_specs=[pl.BlockSpec((tq, HEAD), lambda h, i: (i, C_AQ * NHEAD + h)),
                  pl.BlockSpec((S, HEAD), lambda h, i: (0, C_AK * NHEAD + h)),
                  pl.BlockSpec((S, HEAD), lambda h, i: (0, C_AV * NHEAD + h)),
                  pl.BlockSpec((tq, HEAD), lambda h, i: (i, C_AG * NHEAD + h)),
                  pl.BlockSpec((tq, HEAD), blk), pl.BlockSpec((tq, HEAD), blk), pl.BlockSpec((tq, HEAD), blk)],
        out_specs=[pl.BlockSpec((tq, HEAD), blk), pl.BlockSpec((S, HEAD), head), pl.BlockSpec((S, HEAD), head),
                   pl.BlockSpec((tq, HEAD), blk)],
        out_shape=[jax.ShapeDtypeStruct((S, BW), F32)] * 4,
        scratch_shapes=[pltpu.VMEM((S, HEAD), BF16), pltpu.VMEM((S, HEAD), BF16)],
        operands=(proj, proj, proj, proj, att, runs, dy))


CONV_TR = 256


CONV_ROWS = 64


SUBLANES = 8


def _fill_shifted(shift_ref, buf_ref):
    n = buf_ref.shape[0] - SUBLANES
    buf_ref[n:, :] = jnp.zeros((SUBLANES, buf_ref.shape[1]), F32)
    for b in range(SUBLANES):
        shift_ref[b, :, :] = buf_ref[pl.ds(b, n), :]


def _shifted(shift_ref, row, nrows, cols):
    return shift_ref[row % SUBLANES, pl.ds(row - row % SUBLANES, nrows), cols]


def _conv_taps(shift_ref, w_ref, base, tr, emit):
    for c0 in range(0, BW, HEAD):
        cols = slice(c0, c0 + HEAD)
        taps = [w_ref[k:k + 1, cols] for k in range(CONV_K)]
        for r0 in range(0, tr, CONV_ROWS):
            acc = taps[0] * _shifted(shift_ref, base + r0, CONV_ROWS, cols)
            for k in range(1, CONV_K):
                acc = acc + taps[k] * _shifted(shift_ref, base + r0 + k, CONV_ROWS, cols)
            emit(slice(r0, r0 + CONV_ROWS), cols, acc)


def _conv_fwd(proj, conv_w, conv_b, ln_g, ln_b):
    S = proj.shape[0]
    tr = min(CONV_TR, S)
    ni = S // tr

    def body(a_ref, b_ref, ap_ref, bp_ref, g_ref, w_ref, cb_ref, lg_ref, lb_ref, c_ref, y_ref, buf_ref, shift_ref):
        i = pl.program_id(0)
        prev = ap_ref[tr - HALO:, :] * _sigmoid(bp_ref[tr - HALO:, :])
        buf_ref[0:HALO, :] = jnp.where(i > 0, prev, 0.0)
        buf_ref[HALO:HALO + tr, :] = a_ref[...] * _sigmoid(b_ref[...])
        _fill_shifted(shift_ref, buf_ref)

        def emit(rows, cols, acc):
            c_ref[rows, cols] = acc + cb_ref[:, cols]

        _conv_taps(shift_ref, w_ref, HALO - (CONV_K - 1), tr, emit)
        yn, _, _ = _ln_fwd(c_ref[...], lg_ref[...], lb_ref[...])
        act, _ = _silu_and_grad(yn)
        gate, _ = _silu_and_grad(g_ref[...])
        y_ref[...] = (act * gate).astype(BF16)

    cur = lambda k: (lambda i: (i, k))
    prv = lambda k: (lambda i: (jnp.maximum(i - 1, 0), k))
    full = lambda a: pl.BlockSpec(a.shape, lambda i: (0, 0))
    wpad = jnp.pad(conv_w, ((0, HALO - CONV_K), (0, 0)))
    small = [wpad, conv_b.reshape(1, BW), ln_g.reshape(1, BW), ln_b.reshape(1, BW)]
    return pl.pallas_call(
        body, name="conv_fwd", grid=(ni,),
        in_specs=[pl.BlockSpec((tr, BW), cur(C_BA)), pl.BlockSpec((tr, BW), cur(C_BB)),
                  pl.BlockSpec((tr, BW), prv(C_BA)), pl.BlockSpec((tr, BW), prv(C_BB)),
                  pl.BlockSpec((tr, BW), cur(C_BG))] + [full(a) for a in small],
        out_specs=[pl.BlockSpec((tr, BW), lambda i: (i, 0)), pl.BlockSpec((tr, BW), lambda i: (i, 0))],
        out_shape=[jax.ShapeDtypeStruct((S, BW), F32), jax.ShapeDtypeStruct((S, BW), BF16)],
        scratch_shapes=[pltpu.VMEM((tr + HALO + SUBLANES, BW), F32), pltpu.VMEM((SUBLANES, tr + HALO, BW), F32)],
        compiler_params=_cparams(("arbitrary",)),
    )(proj, proj, proj, proj, proj, *small)


def _conv_bwd_norm(proj, cpre, dy, ln_g, ln_b):
    S = proj.shape[0]

    def fn(i, j, rv, bv):
        c, bg, dyv = rv
        lg, lb = bv
        yn, xh, rstd = _ln_fwd(c, lg, lb)
        act, dact = _silu_and_grad(yn)
        gate, dgate = _silu_and_grad(bg)
        dyn = dyv * gate * dact
        dc, dlg, dlb = _ln_bwd(dyn, xh, rstd, lg)
        return [dc, dyv * act * dgate], [dlg, dlb, jnp.sum(dc, axis=0, keepdims=True)]

    rows = [(cpre, BW, _c(0), 0), (proj, BW, _c(C_BG), 0), (dy, BW, _c(1), 0)]
    return _rowwise("conv_bwd_norm", fn, rows, [ln_g.reshape(1, BW), ln_b.reshape(1, BW)],
                    [(BW, BW, _c(0), F32), (BW, BW, _c(0), BF16)],
                    [(1, BW, BW, _c(0))] * 3, nrows=S, tr=256)


def _conv_bwd_taps(proj, dc, conv_w):
    S = proj.shape[0]
    tr = min(CONV_TR, S)
    ni = S // tr

    def body(a_ref, b_ref, ap_ref, bp_ref, dc_ref, dcn_ref, w_ref, da_ref, db_ref, dw_ref, gbuf, dbuf, sbuf, wrev,
             shift_ref):
        i = pl.program_id(0)

        @pl.when(i == 0)
        def _():
            dw_ref[...] = jnp.zeros(dw_ref.shape, F32)
            for k in range(CONV_K):
                wrev[k:k + 1, :] = w_ref[CONV_K - 1 - k:CONV_K - k, :]

        sb = _sigmoid(b_ref[...])
        sbuf[...] = sb
        prev = ap_ref[tr - HALO:, :] * _sigmoid(bp_ref[tr - HALO:, :])
        gbuf[0:HALO, :] = jnp.where(i > 0, prev, 0.0)
        gbuf[HALO:HALO + tr, :] = a_ref[...] * sb
        dbuf[0:tr, :] = dc_ref[...]
        dbuf[tr:tr + HALO, :] = jnp.where(i < ni - 1, dcn_ref[0:HALO, :], 0.0)

        def emit(rows, cols, dg):
            s = sbuf[rows, cols]
            da_ref[rows, cols] = (dg * s).astype(BF16)
            db_ref[rows, cols] = (dg * a_ref[rows, cols] * s * (1.0 - s)).astype(BF16)

        _fill_shifted(shift_ref, dbuf)
        _conv_taps(shift_ref, wrev, 0, tr, emit)
        _fill_shifted(shift_ref, gbuf)
        base = HALO - (CONV_K - 1)
        for c0 in range(0, BW, HEAD):
            cols = slice(c0, c0 + HEAD)
            dcs = [dc_ref[r0:r0 + CONV_ROWS, cols] for r0 in range(0, tr, CONV_ROWS)]
            for k in range(CONV_K):
                prod = dcs[0] * _shifted(shift_ref, base + k, CONV_ROWS, cols)
                for n in range(1, len(dcs)):
                    prod = prod + dcs[n] * _shifted(shift_ref, base + n * CONV_ROWS + k, CONV_ROWS, cols)
                dw_ref[k:k + 1, cols] += jnp.sum(prod, axis=0, keepdims=True)

    cur = lambda k: (lambda i: (i, k))
    prv = lambda k: (lambda i: (jnp.maximum(i - 1, 0), k))
    wpad = jnp.pad(conv_w, ((0, HALO - CONV_K), (0, 0)))
    return pl.pallas_call(
        body, name="conv_bwd_taps", grid=(ni,),
        in_specs=[pl.BlockSpec((tr, BW), cur(C_BA)), pl.BlockSpec((tr, BW), cur(C_BB)),
                  pl.BlockSpec((tr, BW), prv(C_BA)), pl.BlockSpec((tr, BW), prv(C_BB)),
                  pl.BlockSpec((tr, BW), lambda i: (i, 0)),
                  pl.BlockSpec((tr, BW), lambda i: (jnp.minimum(i + 1, ni - 1), 0)),
                  pl.BlockSpec((HALO, BW), lambda i: (0, 0))],
        out_specs=[pl.BlockSpec((tr, BW), lambda i: (i, 0)), pl.BlockSpec((tr, BW), lambda i: (i, 0)),
                   pl.BlockSpec((HALO, BW), lambda i: (0, 0))],
        out_shape=[jax.ShapeDtypeStruct((S, BW), BF16), jax.ShapeDtypeStruct((S, BW), BF16),
                   jax.ShapeDtypeStruct((HALO, BW), F32)],
        scratch_shapes=[pltpu.VMEM((tr + HALO + SUBLANES, BW), F32), pltpu.VMEM((tr + HALO + SUBLANES, BW), F32),
                        pltpu.VMEM((tr, BW), F32), pltpu.VMEM((HALO, BW), F32),
                        pltpu.VMEM((SUBLANES, tr + HALO, BW), F32)],
        compiler_params=_cparams(("arbitrary",)),
    )(proj, proj, proj, proj, dc, dc, wpad)


SGU_TR = 256


def _sgu_mix(wm, vn, bias):
    tr = vn.shape[0]
    rows = []
    for n in range(tr // HEAD):
        cols = []
        for g in range(NHEAD):
            blk = vn[n * HEAD:(n + 1) * HEAD, g * HEAD:(g + 1) * HEAD]
            cols.append(_dot(wm[g], blk, NN) + bias[g * HEAD:(g + 1) * HEAD, :])
        rows.append(jnp.concatenate(cols, axis=1))
    return jnp.concatenate(rows, axis=0)


def _sgu_masked_w(w2d):
    rr = lax.broadcasted_iota(jnp.int32, (HEAD, HEAD), 0)
    cc = lax.broadcasted_iota(jnp.int32, (HEAD, HEAD), 1)
    tril = rr >= cc
    return [jnp.where(tril, w2d[g * HEAD:(g + 1) * HEAD, :], 0.0).astype(BF16) for g in range(NHEAD)], tril


def _sgu_inputs(sgu_w, sgu_b, ln_g, ln_b):
    w2d = sgu_w.reshape(NHEAD * HEAD, HEAD)
    bias = jnp.broadcast_to(sgu_b[:, :, None], (NHEAD, HEAD, HEAD)).reshape(NHEAD * HEAD, HEAD)
    return [w2d, bias, ln_g.reshape(1, BW), ln_b.reshape(1, BW)]


def _sgu_fwd(proj, sgu_w, sgu_b, ln_g, ln_b):
    S = proj.shape[0]

    def fn(i, j, rv, bv):
        cu, cv, cg = rv
        w2d, bias, lg, lb = bv
        wm, _ = _sgu_masked_w(w2d)
        u, _ = _gelu_and_grad(cu)
        v, _ = _gelu_and_grad(cv)
        vn, _, _ = _ln_fwd(v, lg, lb)
        z = _sgu_mix(wm, vn, bias)
        gate, _ = _silu_and_grad(cg)
        return [u * z * gate], []

    rows = [(proj, BW, _c(C_CU), 0), (proj, BW, _c(C_CV), 0), (proj, BW, _c(C_CG), 0)]
    return _rowwise("sgu_fwd", fn, rows, _sgu_inputs(sgu_w, sgu_b, ln_g, ln_b), [(BW, BW, _c(0), BF16)], [],
                    nrows=S, tr=min(SGU_TR, S))[0]


def _sgu_bwd(proj, dy, sgu_w, sgu_b, ln_g, ln_b):
    S = proj.shape[0]

    def fn(i, j, rv, bv):
        cu, cv, cg, dyv = rv
        w2d, bias, lg, lb = bv
        wm, tril = _sgu_masked_w(w2d)
        u, du_dcu = _gelu_and_grad(cu)
        v, dv_dcv = _gelu_and_grad(cv)
        vn, xh, rstd = _ln_fwd(v, lg, lb)
        z = _sgu_mix(wm, vn, bias)
        gate, dgate = _silu_and_grad(cg)
        dz = dyv * u * gate
        dcu = dyv * z * gate * du_dcu
        dcg = dyv * u * z * dgate
        tr = dz.shape[0]
        dvn_rows = []
        dw = [jnp.zeros((HEAD, HEAD), F32) for _ in range(NHEAD)]
        dbias = [jnp.zeros((HEAD, HEAD), F32) for _ in range(NHEAD)]
        for n in range(tr // HEAD):
            cols = []
            for g in range(NHEAD):
                dzb = dz[n * HEAD:(n + 1) * HEAD, g * HEAD:(g + 1) * HEAD]
                vnb = vn[n * HEAD:(n + 1) * HEAD, g * HEAD:(g + 1) * HEAD]
                cols.append(_dot(wm[g], dzb, TN))
                dw[g] = dw[g] + _dot(dzb, vnb, NT)
                dbias[g] = dbias[g] + dzb
            dvn_rows.append(jnp.concatenate(cols, axis=1))
        dvn = jnp.concatenate(dvn_rows, axis=0)
        dv, dlg, dlb = _ln_bwd(dvn, xh, rstd, lg)
        dw2d = jnp.concatenate([jnp.where(tril, d, 0.0) for d in dw], axis=0)
        return [dcu, dv * dv_dcv, dcg], [dw2d, jnp.concatenate(dbias, axis=0), dlg, dlb]

    rows = [(proj, BW, _c(C_CU), 0), (proj, BW, _c(C_CV), 0), (proj, BW, _c(C_CG), 0), (dy, BW, _c(2), 0)]
    return _rowwise("sgu_bwd", fn, rows, _sgu_inputs(sgu_w, sgu_b, ln_g, ln_b),
                    [(BW, BW, _c(0), BF16)] * 3,
                    [(NHEAD * HEAD, HEAD, HEAD, _c(0)), (NHEAD * HEAD, HEAD, HEAD, _c(0)), (1, BW, BW, _c(0)),
                     (1, BW, BW, _c(0))], nrows=S, tr=min(SGU_TR, S))


NEG = -1e30
NCOL = IN_WIDTH // BW


def _band_masks():
    a = lax.broadcasted_iota(jnp.int32, (HEAD, HEAD), 0)
    c = lax.broadcasted_iota(jnp.int32, (HEAD, HEAD), 1)
    return a >= c, a <= c


def _dil_view(proj, grp):
    S = proj.shape[0]
    dil = DIL_PATTERNS[grp][1]
    if dil == 1:
        return proj, NCOL, C_DQ + grp, C_DK + grp, C_DV
    cols = [proj[:, c * BW:(c + 1) * BW] for c in (C_DQ + grp, C_DK + grp, C_DV)]
    return jnp.concatenate(cols, axis=1).reshape(S // dil, dil * 3 * BW), 3, 0, 1, 2


def _dil_fwd(proj, grp):
    S = proj.shape[0]
    dil = DIL_PATTERNS[grp][1]
    L = S // dil
    nb = L // HEAD
    proj2, ncol, cq, ck, cv = _dil_view(proj, grp)
    scale = HEAD ** -0.5

    def body(q_ref, kc_ref, kp_ref, vc_ref, vp_ref, o_ref, l_ref):
        b = pl.program_id(1)
        m_cur, m_prev = _band_masks()
        m_prev = m_prev & (b > 0)
        for h in range(NHEAD):
            sl = slice(h * HEAD, (h + 1) * HEAD)
            q = q_ref[:, sl].astype(BF16)
            s_c = jnp.where(m_cur, _dot(q, kc_ref[:, sl], NT) * scale, NEG)
            s_p = jnp.where(m_prev, _dot(q, kp_ref[:, sl], NT) * scale, NEG)
            m = jnp.maximum(jnp.max(s_c, axis=1, keepdims=True), jnp.max(s_p, axis=1, keepdims=True))
            p_c = jnp.exp(s_c - m)
            p_p = jnp.exp(s_p - m)
            den = jnp.sum(p_c, axis=1, keepdims=True) + jnp.sum(p_p, axis=1, keepdims=True)
            o = (_dot(p_c, vc_ref[:, sl], NN) + _dot(p_p, vp_ref[:, sl], NN)) / den
            o_ref[:, sl] = o
            l_ref[:, sl] = jnp.broadcast_to(m + jnp.log(den), (HEAD, HEAD))

    cur = lambda col: (lambda r, b: (b, r * ncol + col))
    prv = lambda col: (lambda r, b: (jnp.maximum(b - 1, 0), r * ncol + col))
    out_map = lambda r, b: (b, r)
    o, lse = pl.pallas_call(
        body, name=f"dil_fwd{grp}", grid=(dil, nb),
        in_specs=[pl.BlockSpec((HEAD, BW), cur(cq)),
                  pl.BlockSpec((HEAD, BW), cur(ck)), pl.BlockSpec((HEAD, BW), prv(ck)),
                  pl.BlockSpec((HEAD, BW), cur(cv)), pl.BlockSpec((HEAD, BW), prv(cv))],
        out_specs=[pl.BlockSpec((HEAD, BW), out_map), pl.BlockSpec((HEAD, BW), out_map)],
        out_shape=[jax.ShapeDtypeStruct((L, dil * BW), F32), jax.ShapeDtypeStruct((L, dil * BW), F32)],
        compiler_params=_cparams(("arbitrary", "arbitrary")),
    )(proj2, proj2, proj2, proj2, proj2)
    return o.reshape(S, BW), lse.reshape(S, BW)


def _dil_combine(proj, os_, lses):
    S = proj.shape[0]

    def fn(i, j, rv, bv):
        o1, o2, o3, l1, l2, l3, dg = rv
        m = jnp.maximum(jnp.maximum(l1, l2), l3)
        e1, e2, e3 = jnp.exp(l1 - m), jnp.exp(l2 - m), jnp.exp(l3 - m)
        den = e1 + e2 + e3
        out = (e1 * o1 + e2 * o2 + e3 * o3) / den
        gate, _ = _silu_and_grad(dg)
        return [out, m + jnp.log(den), out * gate], []

    rows = [(a, BW, _c(0), 0) for a in (*os_, *lses)] + [(proj, BW, _c(C_DG), 0)]
    return _rowwise("dil_combine", fn, rows, [], [(BW, BW, _c(0), F32), (BW, BW, _c(0), F32), (BW, BW, _c(0), BF16)],
                    [], nrows=S, tr=256)


def _dil_gate_bwd(proj, att, dy):
    S = proj.shape[0]

    def fn(i, j, rv, bv):
        dg, av, dyv = rv
        gate, dgate = _silu_and_grad(dg)
        dout = dyv * gate
        prod = dout * av
        tr = prod.shape[0]
        delta = jnp.concatenate(
            [jnp.broadcast_to(jnp.sum(prod[:, h * HEAD:(h + 1) * HEAD], axis=1, keepdims=True), (tr, HEAD))
             for h in range(NHEAD)], axis=1)
        return [dout, dyv * av * dgate, delta], []

    rows = [(proj, BW, _c(C_DG), 0), (att, BW, _c(0), 0), (dy, BW, _c(3), 0)]
    return _rowwise("dil_gate_bwd", fn, rows, [], [(BW, BW, _c(0), F32), (BW, BW, _c(0), BF16), (BW, BW, _c(0), F32)],
                    [], nrows=S, tr=256)


def _dil_bwd(proj, dout, lse, delta, grp):
    S = proj.shape[0]
    dil = DIL_PATTERNS[grp][1]
    L = S // dil
    nb = L // HEAD
    proj2, ncol, cq, ck, cv = _dil_view(proj, grp)
    view = lambda a: a.reshape(L, dil * BW)
    scale = HEAD ** -0.5

    def body(q_ref, qn_ref, kc_ref, kp_ref, vc_ref, vp_ref, do_ref, don_ref, l_ref, ln_ref, d_ref, dn_ref,
             dq_ref, dk_ref, dv_ref):
        b = pl.program_id(1)
        m_cur, m_band = _band_masks()
        m_prev = m_band & (b > 0)
        m_next = m_band & (b < nb - 1)
        for h in range(NHEAD):
            sl = slice(h * HEAD, (h + 1) * HEAD)
            q, qn = q_ref[:, sl].astype(BF16), qn_ref[:, sl].astype(BF16)
            kc, kp = kc_ref[:, sl].astype(BF16), kp_ref[:, sl].astype(BF16)
            vc, vp = vc_ref[:, sl].astype(BF16), vp_ref[:, sl].astype(BF16)
            do, don = do_ref[:, sl].astype(BF16), don_ref[:, sl].astype(BF16)
            lse_c, lse_n = l_ref[:, sl], ln_ref[:, sl]
            dl_c, dl_n = d_ref[:, sl], dn_ref[:, sl]
            p_cc = jnp.exp(jnp.where(m_cur, _dot(q, kc, NT) * scale - lse_c, NEG))
            p_cp = jnp.exp(jnp.where(m_prev, _dot(q, kp, NT) * scale - lse_c, NEG))
            p_nc = jnp.exp(jnp.where(m_next, _dot(qn, kc, NT) * scale - lse_n, NEG))
            ds_cc = p_cc * (_dot(do, vc, NT) - dl_c)
            ds_cp = p_cp * (_dot(do, vp, NT) - dl_c)
            ds_nc = p_nc * (_dot(don, vc, NT) - dl_n)
            dq_ref[:, sl] = ((_dot(ds_cc, kc, NN) + _dot(ds_cp, kp, NN)) * scale).astype(BF16)
            dk_ref[:, sl] = ((_dot(ds_cc, q, TN) + _dot(ds_nc, qn, TN)) * scale).astype(BF16)
            dv_ref[:, sl] = _dot(p_cc, do, TN) + _dot(p_nc, don, TN)

    cur = lambda col: (lambda r, b: (b, r * ncol + col))
    prv = lambda col: (lambda r, b: (jnp.maximum(b - 1, 0), r * ncol + col))
    nxt = lambda col: (lambda r, b: (jnp.minimum(b + 1, nb - 1), r * ncol + col))
    o_cur = lambda r, b: (b, r)
    o_nxt = lambda r, b: (jnp.minimum(b + 1, nb - 1), r)
    blk = (HEAD, BW)
    dq, dk, dv = pl.pallas_call(
        body, name=f"dil_bwd{grp}", grid=(dil, nb),
        in_specs=[pl.BlockSpec(blk, cur(cq)), pl.BlockSpec(blk, nxt(cq)),
                  pl.BlockSpec(blk, cur(ck)), pl.BlockSpec(blk, prv(ck)),
                  pl.BlockSpec(blk, cur(cv)), pl.BlockSpec(blk, prv(cv)),
                  pl.BlockSpec(blk, o_cur), pl.BlockSpec(blk, o_nxt),
                  pl.BlockSpec(blk, o_cur), pl.BlockSpec(blk, o_nxt),
                  pl.BlockSpec(blk, o_cur), pl.BlockSpec(blk, o_nxt)],
        out_specs=[pl.BlockSpec(blk, o_cur)] * 3,
        out_shape=[jax.ShapeDtypeStruct((L, dil * BW), BF16), jax.ShapeDtypeStruct((L, dil * BW), BF16),
                   jax.ShapeDtypeStruct((L, dil * BW), F32)],
        compiler_params=_cparams(("arbitrary", "arbitrary")),
    )(proj2, proj2, proj2, proj2, proj2, proj2, view(dout), view(dout), view(lse), view(lse), view(delta), view(delta))
    return dq.reshape(S, BW), dk.reshape(S, BW), dv.reshape(S, BW)


def _add3(a, b, c):
    S, W = a.shape

    def fn(i, j, rv, bv):
        return [rv[0] + rv[1] + rv[2]], []

    return _rowwise("add3", fn, [(a, W, _c(0), 0), (b, W, _c(0), 0), (c, W, _c(0), 0)], [], [(W, W, _c(0), BF16)], [],
                    nrows=S, tr=512)[0]


def _layer_fwd(x, p, cargo=None):
    h = _rms_fwd(x, p["norm_g"])
    proj = _mm_nn("in_proj", h, p["w_in"], F32)
    gpre = _mm_nn("gate_proj", h, p["w_gate"], F32, bias=p["b_gate"].reshape(1, GATE_W))
    (att_a, ya, runs_a), carried = _sb_fwd(proj, cargo)
    cpre, yb = _conv_fwd(proj, p["conv_w"], p["conv_b"], p["conv_ln_g"], p["conv_ln_b"])
    yc = _sgu_fwd(proj, p["sgu_w"], p["sgu_b"], p["sgu_ln_g"], p["sgu_ln_b"])
    dil = [_dil_fwd(proj, g) for g in range(3)]
    att_d, lse_d, yd = _dil_combine(proj, [d[0] for d in dil], [d[1] for d in dil])
    y = jnp.concatenate([ya, yb, yc, yd], axis=1)
    yproj = _mm_nn("branch_proj", y, p["w_branch"], F32, groups=4)
    merged = _merge_fwd(gpre, yproj)
    x_next = _mm_nn("out_proj", merged, p["w_out"], F32, res=x)
    saved = dict(x=x, h=h, proj=proj, gpre=gpre, att_a=att_a, runs_a=runs_a, cpre=cpre, att_d=att_d, lse_d=lse_d, y=y,
                 yproj=yproj, merged=merged)
    return x_next, saved, carried


def _layer_bwd(dout, s, p, pending=None):
    proj = s["proj"]
    dmerged = _mm_nt("out_proj_dx", dout, p["w_out"], F32)
    g_w_out = _mm_tn("out_proj_dw", s["merged"], dout)
    if pending is None:
        dyproj, dgpre, g_b_gate = _merge_bwd(dmerged, s["gpre"], s["yproj"])
        cargo = None
    else:
        grads_above, axes, scalars = pending
        (dyproj, dgpre, g_b_gate), got = _merge_bwd(dmerged, s["gpre"], s["yproj"], _swap_cargo(grads_above, axes))
        sums = _reduce_sums(grads_above, got, axes, scalars)
        cargo = _scatter_cargo(sums, axes)
    dy = _mm_nt("branch_proj_dx", dyproj, p["w_branch"], F32, groups=4)
    g_w_branch = _mm_tn("branch_proj_dw", s["y"], dyproj, groups=4)
    (d_aq, d_ak, d_av, d_ag), parts = _sb_bwd(proj, s["att_a"], s["runs_a"], dy, cargo)
    dc, d_bg, g_cln_g, g_cln_b, g_conv_b = _conv_bwd_norm(proj, s["cpre"], dy, p["conv_ln_g"], p["conv_ln_b"])
    d_ba, d_bb, g_conv_w = _conv_bwd_taps(proj, dc, p["conv_w"])
    d_cu, d_cv, d_cg, g_sgu_w, g_sgu_bias, g_sln_g, g_sln_b = _sgu_bwd(
        proj, dy, p["sgu_w"], p["sgu_b"], p["sgu_ln_g"], p["sgu_ln_b"])
    dout_d, d_dg, delta = _dil_gate_bwd(proj, s["att_d"], dy)
    dil = [_dil_bwd(proj, dout_d, s["lse_d"], delta, g) for g in range(3)]
    d_dv = _add3(dil[0][2], dil[1][2], dil[2][2])
    pieces = [d_aq, d_ak, d_av, d_ag, d_ba, d_bb, d_bg, d_cu, d_cv, d_cg,
              dil[0][0], dil[1][0], dil[2][0], dil[0][1], dil[1][1], dil[2][1], d_dv, d_dg]
    dproj = jnp.concatenate([t.astype(BF16) for t in pieces], axis=1)
    dh_gate = _mm_nt("gate_proj_dx", dgpre, p["w_gate"], F32)
    dh = _mm_nt("in_proj_dx", dproj, p["w_in"], F32, res=dh_gate)
    g_w_in = _mm_tn("in_proj_dw", s["h"], dproj)
    g_w_gate = _mm_tn("gate_proj_dw", s["h"], dgpre)
    dx, g_norm_g = _rms_bwd(s["x"], dh, dout, p["norm_g"])
    grads = dict(
        norm_g=g_norm_g.reshape(D_MODEL), w_in=g_w_in, conv_w=g_conv_w[:CONV_K], conv_b=g_conv_b.reshape(BW),
        conv_ln_g=g_cln_g.reshape(BW), conv_ln_b=g_cln_b.reshape(BW), sgu_ln_g=g_sln_g.reshape(BW),
        sgu_ln_b=g_sln_b.reshape(BW), sgu_w=g_sgu_w.reshape(NHEAD, HEAD, HEAD),
        sgu_b=jnp.sum(g_sgu_bias.reshape(NHEAD, HEAD, HEAD), axis=-1), w_branch=g_w_branch,
        w_gate=g_w_gate, b_gate=g_b_gate.reshape(GATE_W), w_out=g_w_out)
    return dx, grads, (None if pending is None else (sums, parts))


CHIP_FLIPS =((1, 0), (0, 1), (1, 1))


def _at(ref, axis, start, size):
    idx = [slice(None)] * len(ref.shape)
    idx[axis] = pl.ds(start, size)
    return ref.at[tuple(idx)]


def _position():
    return lax.axis_index("x"), lax.axis_index("y"), lax.axis_index("c")


def _gather(name, w, axis):
    n = w.shape[axis]
    full = list(w.shape)
    full[axis] = 4 * n

    def body(w_ref, out_ref, send_sems, recv_sems, local_sem):
        x, y, c = _position()
        mine = pltpu.make_async_copy(w_ref, _at(out_ref, axis, (2 * x + y) * n, n), local_sem)
        mine.start()
        sends = []
        for k, (fx, fy) in enumerate(CHIP_FLIPS):
            cp = pltpu.make_async_remote_copy(
                src_ref=w_ref, dst_ref=_at(out_ref, axis, (2 * x + y) * n, n), send_sem=send_sems.at[k],
                recv_sem=recv_sems.at[k], device_id=(x ^ fx, y ^ fy, c), device_id_type=MESH)
            cp.start()
            sends.append(cp)
        for k, (fx, fy) in enumerate(CHIP_FLIPS):
            px, py = x ^ fx, y ^ fy
            pltpu.make_async_remote_copy(
                src_ref=w_ref, dst_ref=_at(out_ref, axis, (2 * px + py) * n, n), send_sem=send_sems.at[k],
                recv_sem=recv_sems.at[k], device_id=(px, py, c), device_id_type=MESH).wait_recv()
        for cp in sends:
            cp.wait_send()
        mine.wait()

    return pl.pallas_call(
        body, name=name, in_specs=[ANY], out_specs=ANY, out_shape=jax.ShapeDtypeStruct(tuple(full), w.dtype),
        scratch_shapes=[pltpu.SemaphoreType.DMA((3,)), pltpu.SemaphoreType.DMA((3,)), pltpu.SemaphoreType.DMA],
    )(w)


def _block(ref, ax, shard, half):
    if shard is not None:
        n = ref.shape[ax] // 4
        ref = _at(ref, ax, shard * n, n)
    if half is not None:
        nh = ref.shape[1 - ax] // 2
        ref = _at(ref, 1 - ax, half * nh, nh)
    return ref


def _remote(src, dst, send_sems, recv_sems, k, device):
    return pltpu.make_async_remote_copy(src_ref=src, dst_ref=dst, send_sem=send_sems.at[k], recv_sem=recv_sems.at[k],
                                        device_id=device, device_id_type=MESH)


def _gather_cargo(fulls, axes):
    na = len(fulls)

    def start(ins, outs, send, recv):
        x, y, c = _position()
        for a in range(na):
            mine = _block(outs[a], axes[a], 2 * x + y, c)
            for k, (fx, fy) in enumerate(CHIP_FLIPS):
                _remote(mine, mine, send, recv, 3 * a + k, (x ^ fx, y ^ fy, c)).start()

    def finish(ins, outs, send, recv):
        x, y, c = _position()
        sib = (x, y, 1 - c)
        for a in range(na):
            for k, (fx, fy) in enumerate(CHIP_FLIPS):
                got = _block(outs[a], axes[a], 2 * (x ^ fx) + (y ^ fy), c)
                _remote(got, got, send, recv, 3 * a + k, sib).wait_recv()
                _remote(got, got, send, recv, 3 * (na + a) + k, sib).start()
        for a in range(na):
            mine = _block(outs[a], axes[a], 2 * x + y, c)
            for k, (fx, fy) in enumerate(CHIP_FLIPS):
                passed = _block(outs[a], axes[a], 2 * (x ^ fx) + (y ^ fy), c)
                theirs = _block(outs[a], axes[a], 2 * (x ^ fx) + (y ^ fy), 1 - c)
                _remote(theirs, theirs, send, recv, 3 * (na + a) + k, sib).wait_recv()
                _remote(passed, passed, send, recv, 3 * (na + a) + k, sib).wait_send()
                _remote(mine, mine, send, recv, 3 * a + k, sib).wait_send()

    shapes = [jax.ShapeDtypeStruct(f.shape, f.dtype) for f in fulls]
    return _Cargo(fulls, shapes, {a: a for a in range(na)}, 6 * na, start, finish)


def _scatter_cargo(sums, axes):
    na = len(sums)
    shapes = []
    for s, ax in zip(sums, axes):
        piece = list(s.shape)
        piece[ax] //= 4
        shapes.append(jax.ShapeDtypeStruct((3, *piece), s.dtype))

    def copies(ins, outs, send, recv):
        x, y, c = _position()
        return [_remote(_block(ins[a], axes[a], 2 * (x ^ fx) + (y ^ fy), None), outs[a].at[k], send, recv, 3 * a + k,
                        (x ^ fx, y ^ fy, c))
                for a in range(na) for k, (fx, fy) in enumerate(CHIP_FLIPS)]

    def start(ins, outs, send, recv):
        for cp in copies(ins, outs, send, recv):
            cp.start()

    def finish(ins, outs, send, recv):
        for cp in copies(ins, outs, send, recv):
            cp.wait()

    return _Cargo(sums, shapes, {}, 3 * na, start, finish)


def _swap_cargo(grads, axes):
    na = len(grads)
    shapes = []
    for g, ax in zip(grads, axes):
        half = list(g.shape)
        half[1 - ax] //= 2
        shapes.append(jax.ShapeDtypeStruct(tuple(half), g.dtype))

    def copies(ins, outs, send, recv):
        x, y, c = _position()
        return [_remote(_block(ins[a], axes[a], None, 1 - c), outs[a], send, recv, a, (x, y, 1 - c)) for a in range(na)]

    def start(ins, outs, send, recv):
        for cp in copies(ins, outs, send, recv):
            cp.start()

    def finish(ins, outs, send, recv):
        for cp in copies(ins, outs, send, recv):
            cp.wait()

    return _Cargo(grads, shapes, {}, na, start, finish)


def _join_halves(name, shards, axes):
    na = len(shards)

    def body(*refs):
        outs, (send, recv) = refs[na:2 * na], refs[2 * na:]
        x, y, c = _position()
        sib = (x, y, 1 - c)
        for a in range(na):
            mine = _block(outs[a], axes[a], None, c)
            _remote(mine, mine, send, recv, a, sib).start()
        for a in range(na):
            mine = _block(outs[a], axes[a], None, c)
            theirs = _block(outs[a], axes[a], None, 1 - c)
            _remote(theirs, theirs, send, recv, a, sib).wait_recv()
            _remote(mine, mine, send, recv, a, sib).wait_send()

    return pl.pallas_call(
        body, name=name, in_specs=[ANY] * na, out_specs=[ANY] * na,
        out_shape=[jax.ShapeDtypeStruct(s.shape, s.dtype) for s in shards],
        scratch_shapes=[pltpu.SemaphoreType.DMA((na,)), pltpu.SemaphoreType.DMA((na,))],
        input_output_aliases={a: a for a in range(na)},
    )(*shards)


def _sum_terms(name, terms, out_shape, out_dtype, blk, nblocks, out_map, scalars):
    tr, tc = blk

    def body(s_ref, *refs):
        acc = refs[0][...].astype(F32)
        for r in refs[1:-1]:
            acc = acc + r[...].astype(F32)
        refs[-1][...] = acc.astype(refs[-1].dtype)

    in_specs = [pl.BlockSpec((None, tr, tc) if a.ndim == 3 else (tr, tc), m) for a, m in terms]
    return pl.pallas_call(
        body, name=name,
        grid_spec=pltpu.PrefetchScalarGridSpec(
            num_scalar_prefetch=1, grid=(nblocks,), in_specs=in_specs, out_specs=pl.BlockSpec((tr, tc), out_map)),
        out_shape=jax.ShapeDtypeStruct(out_shape, out_dtype),
        compiler_params=_cparams(("arbitrary",)),
    )(scalars, *[a for a, _ in terms])


def _rows_for(cols, rows, budget=1 << 19):
    tr = 8
    while tr * 2 * cols <= budget and tr * 2 <= min(rows, 256) and rows % (tr * 2) == 0:
        tr *= 2
    assert rows % tr == 0
    return tr


def _place_shard(name, w, layer, ax, scalars):
    _, R, C = w.shape
    tr = _rows_for(C, R)
    nb = R // tr
    if ax == 0:
        shape, out_map = (4 * R, C), (lambda i, s: (s[1] * nb + i, 0))
    else:
        shape, out_map = (R, 4 * C), (lambda i, s: (i, s[1]))
    return _sum_terms(name, [(w, lambda i, s: (layer, i, 0))], shape, BF16, (tr, C), nb, out_map, scalars)


def _reduce_sums(grads, got, axes, scalars):
    sums = []
    for a, (g, h, ax) in enumerate(zip(grads, got, axes)):
        Rh, Ch = h.shape
        tr = _rows_for(Ch, Rh)
        nb = Rh // tr
        g_map = (lambda i, s, nb=nb: (s[0] * nb + i, 0)) if ax == 1 else (lambda i, s: (i, s[0]))
        sums.append(_sum_terms(f"rs_sum2_{a}", [(g, g_map), (h, lambda i, s: (i, 0))], (Rh, Ch), BF16, (tr, Ch), nb,
                               lambda i, s: (i, 0), scalars))
    return sums


def _reduce_end(sums, parts, axes, scalars):
    shards = []
    for a, (s1, pt, ax) in enumerate(zip(sums, parts, axes)):
        _, Rs, Cs = pt.shape
        tr = _rows_for(Cs, Rs)
        nb = Rs // tr
        if ax == 1:
            shape, s_map, o_map = (2 * Rs, Cs), (lambda i, s: (i, s[1])), (lambda i, s, nb=nb: (s[0] * nb + i, 0))
        else:
            shape, s_map, o_map = (Rs, 2 * Cs), (lambda i, s, nb=nb: (s[1] * nb + i, 0)), (lambda i, s: (i, s[0]))
        terms = [(s1, s_map)] + [(pt, (lambda i, s, k=k: (k, i, 0))) for k in range(3)]
        shards.append(_sum_terms(f"rs_sum4_{a}", terms, shape, F32, (tr, Cs), nb, o_map, scalars))
    return _join_halves("rs_join", shards, axes)


def _all_to_all_small(buf):
    nr = buf.shape[0]

    def body(b_ref, out_ref, send_sems, recv_sems, local_sem):
        x, y, c = _position()
        me = 4 * x + 2 * y + c
        mine = pltpu.make_async_copy(b_ref, out_ref.at[me], local_sem)
        mine.start()
        sends = []
        for r in range(1, 8):
            peer = (x ^ (r >> 2), y ^ ((r >> 1) & 1), c ^ (r & 1))
            cp = pltpu.make_async_remote_copy(
                src_ref=b_ref, dst_ref=out_ref.at[me], send_sem=send_sems.at[r - 1], recv_sem=recv_sems.at[r - 1],
                device_id=peer, device_id_type=MESH)
            cp.start()
            sends.append(cp)
        for r in range(1, 8):
            px, py, pc = x ^ (r >> 2), y ^ ((r >> 1) & 1), c ^ (r & 1)
            pltpu.make_async_remote_copy(
                src_ref=b_ref, dst_ref=out_ref.at[4 * px + 2 * py + pc], send_sem=send_sems.at[r - 1],
                recv_sem=recv_sems.at[r - 1], device_id=(px, py, pc), device_id_type=MESH).wait_recv()
        for cp in sends:
            cp.wait_send()
        mine.wait()

    return pl.pallas_call(
        body, name="small_exchange", in_specs=[ANY], out_specs=ANY,
        out_shape=jax.ShapeDtypeStruct((8, nr, 128), buf.dtype),
        scratch_shapes=[pltpu.SemaphoreType.DMA((7,)), pltpu.SemaphoreType.DMA((7,)), pltpu.SemaphoreType.DMA],
    )(buf)


SMALL_TR = 256


def _all_reduce_small(buf):
    nr = buf.shape[0]
    slots = _all_to_all_small(buf).reshape(8 * nr, 128)
    nblk = nr // SMALL_TR

    def fn(i, j, rv, bv):
        acc = rv[0]
        for v in rv[1:]:
            acc = acc + v
        return [acc], []

    rows = [(slots, 128, _c(0), (lambda i, d=d: d * nblk + i)) for d in range(8)]
    return _rowwise("small_sum", fn, rows, [], [(128, 128, _c(0), F32)], [], nrows=nr, tr=SMALL_TR)[0]


def _adamw(name, w, g, m, v):
    rows, cols = w.shape
    tr = _rows_for(cols, rows, budget=1 << 18)

    def fn(i, j, rv, bv):
        wv, gv, mv, vv = rv
        m2 = ADAM_B1 * mv + (1.0 - ADAM_B1) * gv
        v2 = ADAM_B2 * vv + (1.0 - ADAM_B2) * (gv * gv)
        m_hat = m2 / (1.0 - ADAM_B1 ** ADAM_STEP)
        v_hat = v2 / (1.0 - ADAM_B2 ** ADAM_STEP)
        delta = -ADAM_LR * (m_hat / (jnp.sqrt(v_hat) + ADAM_EPS) + ADAM_WD * wv)
        return [delta, m2, v2], []

    return _rowwise(name, fn, [(a, cols, _c(0), 0) for a in (w, g, m, v)], [], [(cols, cols, _c(0), F32)] * 3, [],
                    nrows=rows, tr=tr)


WEIGHTS = ("norm_g", "w_in", "conv_w", "conv_b", "conv_ln_g", "conv_ln_b", "sgu_ln_g", "sgu_ln_b", "sgu_w", "sgu_b",
           "w_branch", "w_gate", "b_gate", "w_out", "final_g")
BIG = ("w_in", "w_branch", "w_gate", "w_out")
BIG_AXIS = dict(w_in=1, w_branch=1, w_gate=1, w_out=0)
SMALL = tuple(n for n in WEIGHTS if n not in BIG)


def _pack(arrays, pad_rows):
    flat = jnp.concatenate([a.reshape(-1).astype(F32) for a in arrays])
    unit = 128 * pad_rows
    total = -(-flat.shape[0] // unit) * unit
    return jnp.pad(flat, (0, total - flat.shape[0])).reshape(total // 128, 128)


def _unpack(buf, shapes):
    flat = buf.reshape(-1)
    out, off = [], 0
    for shp in shapes:
        size = math.prod(shp)
        out.append(flat[off:off + size].reshape(shp))
        off += size
    return out


def kernel(x, norm_g, w_in, conv_w, conv_b, conv_ln_g, conv_ln_b, sgu_ln_g, sgu_ln_b, sgu_w, sgu_b, w_branch, w_gate, b_gate, w_out, final_g, loss_target, m_norm_g, m_w_in, m_conv_w, m_conv_b, m_conv_ln_g, m_conv_ln_b, m_sgu_ln_g, m_sgu_ln_b, m_sgu_w, m_sgu_b, m_w_branch, m_w_gate, m_b_gate, m_w_out, m_final_g, v_norm_g, v_w_in, v_conv_w, v_conv_b, v_conv_ln_g, v_conv_ln_b, v_sgu_ln_g, v_sgu_ln_b, v_sgu_w, v_sgu_b, v_w_branch, v_w_gate, v_b_gate, v_w_out, v_final_g):
    w = dict(norm_g=norm_g, w_in=w_in, conv_w=conv_w, conv_b=conv_b, conv_ln_g=conv_ln_g, conv_ln_b=conv_ln_b,
             sgu_ln_g=sgu_ln_g, sgu_ln_b=sgu_ln_b, sgu_w=sgu_w, sgu_b=sgu_b, w_branch=w_branch, w_gate=w_gate,
             b_gate=b_gate, w_out=w_out, final_g=final_g)
    m = dict(norm_g=m_norm_g, w_in=m_w_in, conv_w=m_conv_w, conv_b=m_conv_b, conv_ln_g=m_conv_ln_g,
             conv_ln_b=m_conv_ln_b, sgu_ln_g=m_sgu_ln_g, sgu_ln_b=m_sgu_ln_b, sgu_w=m_sgu_w, sgu_b=m_sgu_b,
             w_branch=m_w_branch, w_gate=m_w_gate, b_gate=m_b_gate, w_out=m_w_out, final_g=m_final_g)
    v = dict(norm_g=v_norm_g, w_in=v_w_in, conv_w=v_conv_w, conv_b=v_conv_b, conv_ln_g=v_conv_ln_g,
             conv_ln_b=v_conv_ln_b, sgu_ln_g=v_sgu_ln_g, sgu_ln_b=v_sgu_ln_b, sgu_w=v_sgu_w, sgu_b=v_sgu_b,
             w_branch=v_w_branch, w_gate=v_w_gate, b_gate=v_b_gate, w_out=v_w_out, final_g=v_final_g)
    depth = w_in.shape[0]
    chip = 2 * lax.axis_index("x") + lax.axis_index("y")
    scalars = jnp.stack([lax.axis_index("c"), chip]).astype(jnp.int32)
    axes = [BIG_AXIS[n] for n in BIG]

    stacked = dict(w_in=w_in, w_branch=w_branch.reshape(depth, 4 * BW, -1), w_gate=w_gate, w_out=w_out)
    buffers = [[_place_shard(f"place_{n}", stacked[n], l, BIG_AXIS[n], scalars) for n in BIG] for l in range(depth)]
    conv_w_full = _gather("gather_conv_w", conv_w, 2)
    gathered = _run_cargo("gather_first", _gather_cargo(buffers[0], axes))
    h = x[0]
    saved, layer_params = [], []
    for l in range(depth):
        p = {n: w[n][l] for n in SMALL if n not in ("final_g", "conv_w")}
        p["conv_w"] = conv_w_full[l]
        p.update(dict(zip(BIG, gathered)))
        cargo = _gather_cargo(buffers[l + 1], axes) if l + 1 < depth else None
        h, s, gathered = _layer_fwd(h, p, cargo)
        saved.append(s)
        layer_params.append(p)
    dh, g_final, loss = _final_loss(h, loss_target[0], final_g)

    layer_grads, reduced_big = [None] * depth, [None] * depth
    pending = None
    for l in reversed(range(depth)):
        dh, layer_grads[l], exchanged = _layer_bwd(dh, saved[l], layer_params[l], pending)
        if exchanged is not None:
            reduced_big[l + 1] = _reduce_end(*exchanged, axes, scalars)
        pending = ([layer_grads[l][n] for n in BIG], axes, scalars)
    sums = _reduce_sums(pending[0], _run_cargo("rs_halves_last", _swap_cargo(pending[0], axes)), axes, scalars)
    reduced_big[0] = _reduce_end(sums, _run_cargo("rs_shards_last", _scatter_cargo(sums, axes)), axes, scalars)
    grad_x = dh[None]
    local = {n: jnp.stack([layer_grads[l][n] for l in range(depth)]) for n in SMALL if n != "final_g"}
    local["final_g"] = g_final.reshape(-1)
    grads = {n: jnp.stack([reduced_big[l][a] for l in range(depth)]).reshape(w[n].shape) for a, n in enumerate(BIG)}
    small_shapes = [local[n].shape for n in SMALL] + [(128,)]
    reduced = _unpack(_all_reduce_small(_pack([local[n] for n in SMALL] + [loss.reshape(128)], SMALL_TR)), small_shapes)
    for n, r in zip(SMALL, reduced[:-1]):
        grads[n] = r
    loss_out = reduced[-1][0]
    ncw = conv_w.shape[2]
    grads["conv_w"] = lax.dynamic_slice_in_dim(grads["conv_w"], chip * ncw, ncw, axis=2)

    delta, new_m, new_v = {}, {}, {}
    for n in BIG:
        cols = w[n].shape[-1]
        d2, m2, v2 = _adamw(f"adamw_{n}", *[a.reshape(-1, cols) for a in (w[n], grads[n], m[n], v[n])])
        delta[n], new_m[n], new_v[n] = d2.reshape(w[n].shape), m2.reshape(w[n].shape), v2.reshape(w[n].shape)
    shapes = [w[n].shape for n in SMALL]
    packed = [_pack([t[n] for n in SMALL], SMALL_TR) for t in (w, grads, m, v)]
    outs = _adamw("adamw_small", *packed)
    for res, o in zip((delta, new_m, new_v), outs):
        for n, a in zip(SMALL, _unpack(o, shapes)):
            res[n] = a
    return (loss_out, grad_x, *[grads[n] for n in WEIGHTS], *[delta[n] for n in WEIGHTS],
            *[new_m[n] for n in WEIGHTS], *[new_v[n] for n in WEIGHTS])
```

```python
import functools
import math

import jax
import jax.numpy as jnp
from jax import lax
from jax.experimental import pallas as pl
from jax.experimental.pallas import tpu as pltpu

F32 = jnp.float32
BF16 = jnp.bfloat16
MESH = pl.DeviceIdType.MESH

DEPTH = 4
D_MODEL = 2048
HEAD = 128
NHEAD = 8
BW = 1024
IN_WIDTH = 18432
GATE_W = 4 * D_MODEL
NORM_EPS = 1e-6
CONV_K = 31
HALO = 32
DIL_PATTERNS = ((128, 1), (512, 4), (2048, 16))
C_AQ, C_AK, C_AV, C_AG, C_BA, C_BB, C_BG, C_CU, C_CV, C_CG, C_DQ, C_DK, C_DV, C_DG = (
    0, 1, 2, 3, 4, 5, 6, 7, 8, 9, 10, 13, 16, 17)
ADAM_LR, ADAM_B1, ADAM_B2, ADAM_EPS, ADAM_WD, ADAM_STEP = 0.001, 0.9, 0.999, 1e-08, 0.01, 10
VMEM_LIMIT = 56 * 1024 * 1024

NN = (((1,), (0,)), ((), ()))
NT = (((1,), (1,)), ((), ()))
TN = (((0,), (0,)), ((), ()))


def _cparams(sem):
    return pltpu.CompilerParams(dimension_semantics=sem, vmem_limit_bytes=VMEM_LIMIT)


def _dot(a, b, dims):
    return lax.dot_general(a.astype(BF16), b.astype(BF16), dims, preferred_element_type=F32)


def _sigmoid(x):
    return 1.0 / (1.0 + jnp.exp(-x))


def _silu_and_grad(x):
    s = _sigmoid(x)
    return x * s, s * (1.0 + x * (1.0 - s))


_GELU_C = math.sqrt(2.0 / math.pi)


def _gelu_and_grad(x):
    t = jnp.tanh(_GELU_C * (x + 0.044715 * x * x * x))
    y = 0.5 * x * (1.0 + t)
    dy = 0.5 * (1.0 + t) + 0.5 * x * (1.0 - t * t) * _GELU_C * (1.0 + 3.0 * 0.044715 * x * x)
    return y, dy


def _ln_fwd(x, g, b):
    mu = jnp.mean(x, axis=-1, keepdims=True)
    xc = x - mu
    var = jnp.mean(xc * xc, axis=-1, keepdims=True)
    rstd = lax.rsqrt(var + NORM_EPS)
    xh = xc * rstd
    return xh * g + b, xh, rstd


def _ln_bwd(dy, xh, rstd, g):
    dxh = dy * g
    dx = rstd * (dxh - jnp.mean(dxh, axis=-1, keepdims=True) - xh * jnp.mean(dxh * xh, axis=-1, keepdims=True))
    return dx, jnp.sum(dy * xh, axis=0, keepdims=True), jnp.sum(dy, axis=0, keepdims=True)


def _rowwise(name, fn, rows, bcast, outs, accs, *, nrows, tr, nj=1, cargo=None):
    ni = nrows // tr
    nr, nb, no = len(rows), len(bcast), len(outs)

    def rmap(colfn, shift):
        if callable(shift):
            return lambda j, i: (shift(i), colfn(j))
        if shift == 0:
            return lambda j, i: (i, colfn(j))
        return lambda j, i: (jnp.clip(i + shift, 0, ni - 1), colfn(j))

    in_specs = [pl.BlockSpec((tr, w), rmap(cf, sh)) for (_, w, cf, sh) in rows]
    in_specs += [pl.BlockSpec(b.shape, lambda j, i, nd=b.ndim: (0,) * nd) for b in bcast]
    out_specs = [pl.BlockSpec((tr, w), rmap(cf, 0)) for (_, w, cf, _) in outs]
    out_specs += [pl.BlockSpec((r, w), lambda j, i, cf=cf: (0, cf(j))) for (r, _, w, cf) in accs]
    out_shape = [jax.ShapeDtypeStruct((nrows, nc), dt) for (nc, _, _, dt) in outs]
    out_shape += [jax.ShapeDtypeStruct((r, nc), F32) for (r, nc, _, _) in accs]

    def body(*refs):
        j = pl.program_id(0)
        i = pl.program_id(1)
        rv = [r[...] for r in refs[:nr]]
        bv = [r[...] for r in refs[nr:nr + nb]]
        o_refs = refs[nr + nb:nr + nb + no]
        a_refs = refs[nr + nb + no:]
        o_vals, a_vals = fn(i, j, rv, bv)
        for ref, val in zip(o_refs, o_vals):
            ref[...] = val.astype(ref.dtype)
        if a_refs:
            @pl.when(i == 0)
            def _():
                for ref in a_refs:
                    ref[...] = jnp.zeros(ref.shape, F32)
            for ref, val in zip(a_refs, a_vals):
                ref[...] += val

    res, carried = _call_with_cargo(body, cargo, name=name, grid=(nj, ni), in_specs=in_specs, out_specs=out_specs,
                                    out_shape=out_shape, scratch_shapes=[], operands=(*[r[0] for r in rows], *bcast))
    return res if cargo is None else (res, carried)


def _c(k):
    return lambda j: k


def _mm(name, a, b, dims, *, a_blk, a_map, b_blk, b_map, o_shape, o_blk, o_map, grid, o_dtype,
        bias=None, bias_blk=None, bias_map=None, res=None):
    nk = grid[3]
    extra, extra_specs = [], []
    if bias is not None:
        extra.append(bias)
        extra_specs.append(pl.BlockSpec(bias_blk, bias_map))
    if res is not None:
        extra.append(res)
        extra_specs.append(pl.BlockSpec(o_blk, o_map))
    nbias = bias is not None
    nres = res is not None

    def body(*refs):
        a_ref, b_ref = refs[0], refs[1]
        idx = 2
        bias_ref = refs[idx] if nbias else None
        idx += nbias
        res_ref = refs[idx] if nres else None
        idx += nres
        o_ref = refs[idx]
        acc_ref = refs[idx + 1]
        k = pl.program_id(3)
        part = _dot(a_ref[...], b_ref[...], dims)

        def finish(r):
            if nbias:
                r = r + bias_ref[...]
            if nres:
                r = r + res_ref[...]
            o_ref[...] = r.astype(o_ref.dtype)

        if nk == 1:
            finish(part)
        else:
            @pl.when(k == 0)
            def _():
                acc_ref[...] = part

            @pl.when(k > 0)
            def _():
                acc_ref[...] += part

            @pl.when(k == nk - 1)
            def _():
                finish(acc_ref[...])

    acc_shape = o_blk if nk > 1 else (8, 128)
    return pl.pallas_call(
        body, name=name, grid=grid,
        in_specs=[pl.BlockSpec(a_blk, a_map), pl.BlockSpec(b_blk, b_map)] + extra_specs,
        out_specs=pl.BlockSpec(o_blk, o_map),
        out_shape=jax.ShapeDtypeStruct(o_shape, o_dtype),
        scratch_shapes=[pltpu.VMEM(acc_shape, F32)],
        compiler_params=_cparams(("arbitrary", "arbitrary", "arbitrary", "arbitrary")),
    )(a, b, *extra)


def _mm_nn(name, a, b, o_dtype, *, bias=None, res=None, groups=1, tm=1024, tn=1024):
    M = a.shape[0]
    K = a.shape[1] // groups
    N = b.shape[1]
    tm, tn = min(tm, M), min(tn, N)
    njn = N // tn
    return _mm(name, a, b, NN,
               a_blk=(tm, K), a_map=lambda g, j, i, k: (i, g),
               b_blk=(K, tn), b_map=lambda g, j, i, k: (g, j),
               o_shape=(M, groups * N), o_blk=(tm, tn), o_map=lambda g, j, i, k: (i, g * njn + j),
               grid=(groups, njn, M // tm, 1), o_dtype=o_dtype,
               bias=bias, bias_blk=(1, tn), bias_map=lambda g, j, i, k: (0, g * njn + j), res=res)


def _mm_nt(name, a, b, o_dtype, *, res=None, groups=1, tm=512, tk=2048):
    M = a.shape[0]
    K = a.shape[1] // groups
    N = b.shape[0] // groups
    tm, tk = min(tm, M), min(tk, K)
    nk = K // tk
    return _mm(name, a, b, NT,
               a_blk=(tm, tk), a_map=lambda g, j, i, k: (i, g * nk + k),
               b_blk=(N, tk), b_map=lambda g, j, i, k: (g, k),
               o_shape=(M, groups * N), o_blk=(tm, N), o_map=lambda g, j, i, k: (i, g),
               grid=(groups, 1, M // tm, nk), o_dtype=o_dtype, res=res)


def _mm_tn(name, a, b, *, groups=1, tm=1024, tn=1024, tk=2048):
    K = a.shape[0]
    M = a.shape[1] // groups
    N = b.shape[1] // groups
    tm, tn, tk = min(tm, M), min(tn, N), min(tk, K)
    nim, njn = M // tm, N // tn
    return _mm(name, a, b, TN,
               a_blk=(tk, tm), a_map=lambda g, j, i, k: (k, g * nim + i),
               b_blk=(tk, tn), b_map=lambda g, j, i, k: (k, g * njn + j),
               o_shape=(groups * M, N), o_blk=(tm, tn), o_map=lambda g, j, i, k: (g * nim + i, j),
               grid=(groups, njn, nim, K // tk), o_dtype=F32)


def _rms_fwd(x, g):
    S, D = x.shape

    def fn(i, j, rv, bv):
        xv, gv = rv[0], bv[0]
        r = lax.rsqrt(jnp.mean(xv * xv, axis=-1, keepdims=True) + NORM_EPS)
        return [xv * r * gv], []

    return _rowwise("rms_fwd", fn, [(x, D, _c(0), 0)], [g.reshape(1, D)], [(D, D, _c(0), BF16)], [],
                    nrows=S, tr=512)[0]


def _rms_bwd_math(xv, dyv, gv):
    r = lax.rsqrt(jnp.mean(xv * xv, axis=-1, keepdims=True) + NORM_EPS)
    xh = xv * r
    dg = jnp.sum(dyv * xh, axis=0, keepdims=True)
    dxh = dyv * gv
    dx = r * (dxh - xh * jnp.mean(dxh * xh, axis=-1, keepdims=True))
    return dx, dg


def _rms_bwd(x, dh, dout, g):
    S, D = x.shape

    def fn(i, j, rv, bv):
        dx, dg = _rms_bwd_math(rv[0], rv[1], bv[0])
        return [rv[2] + dx], [dg]

    return _rowwise("rms_bwd", fn, [(x, D, _c(0), 0), (dh, D, _c(0), 0), (dout, D, _c(0), 0)], [g.reshape(1, D)],
                    [(D, D, _c(0), F32)], [(1, D, D, _c(0))], nrows=S, tr=256)


def _final_loss(x, target, g):
    S, D = x.shape

    def fn(i, j, rv, bv):
        xv, tv, gv = rv[0], rv[1], bv[0]
        r = lax.rsqrt(jnp.mean(xv * xv, axis=-1, keepdims=True) + NORM_EPS)
        err = xv * r * gv - tv
        loss = 0.5 * jnp.sum(jnp.mean(err * err, axis=-1, keepdims=True), axis=0, keepdims=True)
        dx, dg = _rms_bwd_math(xv, err * (1.0 / D), gv)
        return [dx], [dg, jnp.broadcast_to(loss, (1, 128))]

    return _rowwise("final_loss", fn, [(x, D, _c(0), 0), (target, D, _c(0), 0)], [g.reshape(1, D)],
                    [(D, D, _c(0), F32)], [(1, D, D, _c(0)), (1, 128, 128, _c(0))], nrows=S, tr=256)


MERGE_W = 512


def _merge_fwd(gpre, yproj):
    S = gpre.shape[0]
    nj = D_MODEL // MERGE_W

    def fn(i, j, rv, bv):
        acc = _sigmoid(rv[0]) * rv[4]
        for n in range(1, 4):
            acc = acc + _sigmoid(rv[n]) * rv[4 + n]
        return [acc], []

    rows = [(gpre, MERGE_W, (lambda j, n=n: n * nj + j), 0) for n in range(4)]
    rows += [(yproj, MERGE_W, (lambda j, n=n: n * nj + j), 0) for n in range(4)]
    return _rowwise("merge_fwd", fn, rows, [], [(D_MODEL, MERGE_W, lambda j: j, BF16)], [],
                    nrows=S, tr=512, nj=nj)[0]


def _merge_bwd(dmerged, gpre, yproj, cargo=None):
    S = gpre.shape[0]
    nj4 = D_MODEL // MERGE_W

    def fn(i, j, rv, bv):
        dm, gp, yp = rv
        sg = _sigmoid(gp)
        dgp = dm * yp * sg * (1.0 - sg)
        return [dm * sg, dgp], [jnp.sum(dgp, axis=0, keepdims=True)]

    rows = [(dmerged, MERGE_W, lambda j: j % nj4, 0), (gpre, MERGE_W, lambda j: j, 0), (yproj, MERGE_W, lambda j: j, 0)]
    return _rowwise("merge_bwd", fn, rows, [],
                    [(GATE_W, MERGE_W, lambda j: j, BF16), (GATE_W, MERGE_W, lambda j: j, BF16)],
                    [(1, GATE_W, MERGE_W, lambda j: j)], nrows=S, tr=512, nj=4 * nj4, cargo=cargo)


SB_TQ = 1024
SB_TQ_BWD = 512
SB_TK = 512
SB_TRI = 256


def _softplus(z):
    bits, sign = (jnp.uint32, 0x80000000) if z.dtype == F32 else (jnp.uint16, 0x8000)
    neg_abs = lax.bitcast_convert_type(lax.bitcast_convert_type(z, bits) | bits(sign), z.dtype)
    return jnp.maximum(z, 0.0) + jnp.log(1.0 + jnp.exp(neg_abs))


def _sb_sum_matrices():
    r = lax.broadcasted_iota(jnp.int32, (SB_TRI, SB_TRI), 0)
    c = lax.broadcasted_iota(jnp.int32, (SB_TRI, SB_TRI), 1)
    return (r > c).astype(BF16), (r < c).astype(BF16)


ANY =pl.BlockSpec(memory_space=pl.ANY)


class _Cargo:
    def __init__(self, arrays, out_shapes, aliases, nsem, start, finish):
        self.arrays, self.out_shapes, self.aliases, self.nsem = list(arrays), list(out_shapes), dict(aliases), nsem
        self.start, self.finish = start, finish


def _call_with_cargo(body, cargo, *, name, grid, in_specs, out_specs, out_shape, scratch_shapes, operands):
    sem = ("arbitrary",) * len(grid)
    if cargo is None:
        res = pl.pallas_call(body, name=name, grid=grid, in_specs=in_specs, out_specs=out_specs, out_shape=out_shape,
                             scratch_shapes=scratch_shapes, compiler_params=_cparams(sem))(*operands)
        return res, []
    n_in, n_out, n_scr = len(in_specs), len(out_specs), len(scratch_shapes)
    nci, nco = len(cargo.arrays), len(cargo.out_shapes)

    def wrapped(*refs):
        ins, refs = refs[:n_in], refs[n_in:]
        cin, refs = refs[:nci], refs[nci:]
        outs, refs = refs[:n_out], refs[n_out:]
        cout, refs = refs[:nco], refs[nco:]
        scr, (send, recv) = refs[:n_scr], refs[n_scr:]
        ids = [pl.program_id(a) for a in range(len(grid))]
        first = functools.reduce(lambda p, q: p & q, [g == 0 for g in ids])
        last = functools.reduce(lambda p, q: p & q, [g == n - 1 for g, n in zip(ids, grid)])

        @pl.when(first)
        def _():
            cargo.start(cin, cout, send, recv)

        body(*ins, *outs, *scr)

        @pl.when(last)
        def _():
            cargo.finish(cin, cout, send, recv)

    res = pl.pallas_call(
        wrapped, name=name, grid=grid, in_specs=list(in_specs) + [ANY] * nci, out_specs=list(out_specs) + [ANY] * nco,
        out_shape=list(out_shape) + cargo.out_shapes,
        scratch_shapes=list(scratch_shapes) + [pltpu.SemaphoreType.DMA((cargo.nsem,)), pltpu.SemaphoreType.DMA((cargo.nsem,))],
        input_output_aliases={n_in + k: n_out + v for k, v in cargo.aliases.items()},
        compiler_params=_cparams(sem))(*operands, *cargo.arrays)
    return res[:n_out], res[n_out:]


def _run_cargo(name, cargo):
    nci = len(cargo.arrays)

    def body(*refs):
        cin, cout, (send, recv) = refs[:nci], refs[nci:-2], refs[-2:]
        cargo.start(cin, cout, send, recv)
        cargo.finish(cin, cout, send, recv)

    return pl.pallas_call(
        body, name=name, in_specs=[ANY] * nci, out_specs=[ANY] * len(cargo.out_shapes), out_shape=cargo.out_shapes,
        scratch_shapes=[pltpu.SemaphoreType.DMA((cargo.nsem,)), pltpu.SemaphoreType.DMA((cargo.nsem,))],
        input_output_aliases=cargo.aliases)(*cargo.arrays)


def _sb_fwd(proj, cargo=None):
    S = proj.shape[0]
    tq, tk = min(SB_TQ, S), SB_TK
    assert S // SB_TRI <= HEAD
    nq = S // tq
    nkb = tq // tk
    nsub = tk // SB_TRI
    scale = HEAD ** -0.5

    def body(q_ref, k_ref, v_ref, g_ref, o_ref, y_ref, r_ref, kb_ref, vb_ref):
        i = pl.program_id(1)

        @pl.when(i == 0)
        def _():
            kb_ref[...] = k_ref[...].astype(BF16)
            vb_ref[...] = v_ref[...].astype(BF16)

        qb = (q_ref[...] * scale).astype(BF16)
        qpos = i * tq + lax.broadcasted_iota(jnp.int32, (tq, tk), 0)
        kidx = lax.broadcasted_iota(jnp.int32, (tq, tk), 1)
        lane = lax.broadcasted_iota(jnp.int32, (tq, HEAD), 1)
        upper, _ = _sb_sum_matrices()

        def block(kblock, masked, carry):
            acc, run, runs = carry
            off = pl.multiple_of(kblock * tk, tk)
            zz = _dot(qb, kb_ref[pl.ds(off, tk), :], NT).astype(BF16)
            sp = _softplus(zz)
            if masked:
                keep = (off + kidx) < qpos
                sp_sum = jnp.where(keep, sp, 0.0)
            else:
                sp_sum = sp
            right, after = [None] * nsub, [None] * nsub
            for j in reversed(range(nsub)):
                cols = slice(j * SB_TRI, (j + 1) * SB_TRI)
                right[j] = run
                suffix = _dot(sp_sum[:, cols], upper, NN)
                after[j] = (suffix + run).astype(BF16)
                run = run + suffix[:, :1] + sp_sum[:, j * SB_TRI:j * SB_TRI + 1].astype(F32)
                runs = jnp.where(lane == kblock * nsub + j, right[j], runs)
            w = jnp.exp((zz - sp) - jnp.concatenate(after, axis=1))
            if masked:
                w = jnp.where(keep, w, 0.0)
            return acc + _dot(w, vb_ref[pl.ds(off, tk), :], NN), run, runs

        runs0 = jnp.where(qpos[:, :HEAD] + lane < 0, 1.0, 0.0)
        carry = (jnp.zeros((tq, HEAD), F32), jnp.zeros((tq, 1), F32), runs0)
        for kbl in reversed(range(nkb)):
            carry = block(i * nkb + kbl, True, carry)
        acc, _, runs = lax.fori_loop(0, i * nkb, lambda n, c: block(i * nkb - 1 - n, False, c), carry)
        o_ref[...] = acc
        r_ref[...] = runs
        gate, _ = _silu_and_grad(g_ref[...])
        y_ref[...] = (acc * gate).astype(BF16)

    return _call_with_cargo(
        body, cargo, name="sb_fwd", grid=(NHEAD, nq),
        in_specs=[pl.BlockSpec((tq, HEAD), lambda h, i: (i, C_AQ * NHEAD + h)),
                  pl.BlockSpec((S, HEAD), lambda h, i: (0, C_AK * NHEAD + h)),
                  pl.BlockSpec((S, HEAD), lambda h, i: (0, C_AV * NHEAD + h)),
                  pl.BlockSpec((tq, HEAD), lambda h, i: (i, C_AG * NHEAD + h))],
        out_specs=[pl.BlockSpec((tq, HEAD), lambda h, i: (i, h))] * 3,
        out_shape=[jax.ShapeDtypeStruct((S, BW), F32), jax.ShapeDtypeStruct((S, BW), BF16),
                   jax.ShapeDtypeStruct((S, BW), F32)],
        scratch_shapes=[pltpu.VMEM((S, HEAD), BF16), pltpu.VMEM((S, HEAD), BF16)],
        operands=(proj, proj, proj, proj))


def _sb_bwd(proj, att, runs, dy, cargo=None):
    S = proj.shape[0]
    tq, tk = min(SB_TQ_BWD, S), SB_TK
    nq = S // tq
    nkb = tq // tk
    nsub = tk // SB_TRI
    scale = HEAD ** -0.5

    def body(q_ref, k_ref, v_ref, g_ref, o_ref, r_ref, dy_ref, dq_ref, dk_ref, dv_ref, dg_ref, kb_ref, vb_ref):
        i = pl.program_id(1)

        @pl.when(i == 0)
        def _():
            kb_ref[...] = k_ref[...].astype(BF16)
            vb_ref[...] = v_ref[...].astype(BF16)
            dk_ref[...] = jnp.zeros(dk_ref.shape, F32)
            dv_ref[...] = jnp.zeros(dv_ref.shape, F32)

        gate, dgate = _silu_and_grad(g_ref[...])
        dyv = dy_ref[...]
        dg_ref[...] = dyv * o_ref[...] * dgate
        dob = (dyv * gate).astype(BF16)
        qb = (q_ref[...] * scale).astype(BF16)
        runs = r_ref[...]
        qpos = i * tq + lax.broadcasted_iota(jnp.int32, (tq, tk), 0)
        kidx = lax.broadcasted_iota(jnp.int32, (tq, tk), 1)
        lane = lax.broadcasted_iota(jnp.int32, (tq, HEAD), 1)
        upper, lower = _sb_sum_matrices()

        def block(kblock, masked, carry):
            dq, grun = carry
            off = pl.multiple_of(kblock * tk, tk)
            kblk = kb_ref[pl.ds(off, tk), :]
            zz = _dot(qb, kblk, NT).astype(BF16)
            sp = _softplus(zz)
            lb = zz - sp
            if masked:
                keep = (off + kidx) < qpos
                sp_sum = jnp.where(keep, sp, 0.0)
            else:
                sp_sum = sp
            after = []
            for j in range(nsub):
                cols = slice(j * SB_TRI, (j + 1) * SB_TRI)
                run = jnp.sum(jnp.where(lane == kblock * nsub + j, runs, 0.0), axis=1, keepdims=True)
                after.append((_dot(sp_sum[:, cols], upper, NN) + run).astype(BF16))
            w = jnp.exp(lb - jnp.concatenate(after, axis=1))
            if masked:
                w = jnp.where(keep, w, 0.0)
            gw = _dot(dob, vb_ref[pl.ds(off, tk), :], NT).astype(BF16) * w
            gsum = []
            for j in range(nsub):
                cols = slice(j * SB_TRI, (j + 1) * SB_TRI)
                prefix = _dot(gw[:, cols], lower, NN)
                gsum.append((prefix + grun).astype(BF16))
                last = (j + 1) * SB_TRI - 1
                grun = grun + prefix[:, SB_TRI - 1:] + gw[:, last:last + 1].astype(F32)
            dz = gw - (gw + jnp.concatenate(gsum, axis=1)) * jnp.exp(lb)
            if masked:
                dz = jnp.where(keep, dz, 0.0)
            dk_ref[pl.ds(off, tk), :] += _dot(dz, qb, TN)
            dv_ref[pl.ds(off, tk), :] += _dot(w, dob, TN)
            return dq + _dot(dz, kblk, NN), grun

        carry = lax.fori_loop(0, i * nkb, lambda n, c: block(n, False, c),
                              (jnp.zeros((tq, HEAD), F32), jnp.zeros((tq, 1), F32)))
        for kbl in range(nkb):
            carry = block(i * nkb + kbl, True, carry)
        dq_ref[...] = carry[0] * scale

    blk = lambda h, i: (i, h)
    head = lambda h, i: (0, h)
    return _call_with_cargo(
        body, cargo, name="sb_bwd", grid=(NHEAD, nq),
        in_specs=[pl.BlockSpec((tq, HEAD), lambda h, i: (i, C_AQ * NHEAD + h)),
                  pl.BlockSpec((S, HEAD), lambda h, i: (0, C_AK * NHEAD + h)),
                  pl.BlockSpec((S, HEAD), lambda h, i: (0, C_AV * NHEAD + h)),
                  pl.BlockSpec((tq, HEAD), lambda h, i: (i, C_AG * NHEAD + h)),
                  pl.BlockSpec((tq, HEAD), blk), pl.BlockSpec((tq, HEAD), blk), pl.BlockSpec((tq, HEAD), blk)],
        out_specs=[pl.BlockSpec((tq, HEAD), blk), pl.BlockSpec((S, HEAD), head), pl.BlockSpec((S, HEAD), head),
                   pl.BlockSpec((tq, HEAD), blk)],
        out_shape=[jax.ShapeDtypeStruct((S, BW), F32)] * 4,
        scratch_shapes=[pltpu.VMEM((S, HEAD), BF16), pltpu.VMEM((S, HEAD), BF16)],
        operands=(proj, proj, proj, proj, att, runs, dy))


CONV_TR = 256


CONV_ROWS = 64


SUBLANES = 8


def _fill_shifted(shift_ref, buf_ref):
    n = buf_ref.shape[0] - SUBLANES
    buf_ref[n:, :] = jnp.zeros((SUBLANES, buf_ref.shape[1]), F32)
    for b in range(SUBLANES):
        shift_ref[b, :, :] = buf_ref[pl.ds(b, n), :]


def _shifted(shift_ref, row, nrows, cols):
    return shift_ref[row % SUBLANES, pl.ds(row - row % SUBLANES, nrows), cols]


def _conv_taps(shift_ref, w_ref, base, tr, emit):
    for c0 in range(0, BW, HEAD):
        cols = slice(c0, c0 + HEAD)
        taps = [w_ref[k:k + 1, cols] for k in range(CONV_K)]
        for r0 in range(0, tr, CONV_ROWS):
            acc = taps[0] * _shifted(shift_ref, base + r0, CONV_ROWS, cols)
            for k in range(1, CONV_K):
                acc = acc + taps[k] * _shifted(shift_ref, base + r0 + k, CONV_ROWS, cols)
            emit(slice(r0, r0 + CONV_ROWS), cols, acc)


def _conv_fwd(proj, conv_w, conv_b, ln_g, ln_b):
    S = proj.shape[0]
    tr = min(CONV_TR, S)
    ni = S // tr

    def body(a_ref, b_ref, ap_ref, bp_ref, g_ref, w_ref, cb_ref, lg_ref, lb_ref, c_ref, y_ref, buf_ref, shift_ref):
        i = pl.program_id(0)
        prev = ap_ref[tr - HALO:, :] * _sigmoid(bp_ref[tr - HALO:, :])
        buf_ref[0:HALO, :] = jnp.where(i > 0, prev, 0.0)
        buf_ref[HALO:HALO + tr, :] = a_ref[...] * _sigmoid(b_ref[...])
        _fill_shifted(shift_ref, buf_ref)

        def emit(rows, cols, acc):
            c_ref[rows, cols] = acc + cb_ref[:, cols]

        _conv_taps(shift_ref, w_ref, HALO - (CONV_K - 1), tr, emit)
        yn, _, _ = _ln_fwd(c_ref[...], lg_ref[...], lb_ref[...])
        act, _ = _silu_and_grad(yn)
        gate, _ = _silu_and_grad(g_ref[...])
        y_ref[...] = (act * gate).astype(BF16)

    cur = lambda k: (lambda i: (i, k))
    prv = lambda k: (lambda i: (jnp.maximum(i - 1, 0), k))
    full = lambda a: pl.BlockSpec(a.shape, lambda i: (0, 0))
    wpad = jnp.pad(conv_w, ((0, HALO - CONV_K), (0, 0)))
    small = [wpad, conv_b.reshape(1, BW), ln_g.reshape(1, BW), ln_b.reshape(1, BW)]
    return pl.pallas_call(
        body, name="conv_fwd", grid=(ni,),
        in_specs=[pl.BlockSpec((tr, BW), cur(C_BA)), pl.BlockSpec((tr, BW), cur(C_BB)),
                  pl.BlockSpec((tr, BW), prv(C_BA)), pl.BlockSpec((tr, BW), prv(C_BB)),
                  pl.BlockSpec((tr, BW), cur(C_BG))] + [full(a) for a in small],
        out_specs=[pl.BlockSpec((tr, BW), lambda i: (i, 0)), pl.BlockSpec((tr, BW), lambda i: (i, 0))],
        out_shape=[jax.ShapeDtypeStruct((S, BW), F32), jax.ShapeDtypeStruct((S, BW), BF16)],
        scratch_shapes=[pltpu.VMEM((tr + HALO + SUBLANES, BW), F32), pltpu.VMEM((SUBLANES, tr + HALO, BW), F32)],
        compiler_params=_cparams(("arbitrary",)),
    )(proj, proj, proj, proj, proj, *small)


def _conv_bwd_norm(proj, cpre, dy, ln_g, ln_b):
    S = proj.shape[0]

    def fn(i, j, rv, bv):
        c, bg, dyv = rv
        lg, lb = bv
        yn, xh, rstd = _ln_fwd(c, lg, lb)
        act, dact = _silu_and_grad(yn)
        gate, dgate = _silu_and_grad(bg)
        dyn = dyv * gate * dact
        dc, dlg, dlb = _ln_bwd(dyn, xh, rstd, lg)
        return [dc, dyv * act * dgate], [dlg, dlb, jnp.sum(dc, axis=0, keepdims=True)]

    rows = [(cpre, BW, _c(0), 0), (proj, BW, _c(C_BG), 0), (dy, BW, _c(1), 0)]
    return _rowwise("conv_bwd_norm", fn, rows, [ln_g.reshape(1, BW), ln_b.reshape(1, BW)],
                    [(BW, BW, _c(0), F32), (BW, BW, _c(0), BF16)],
                    [(1, BW, BW, _c(0))] * 3, nrows=S, tr=256)


def _conv_bwd_taps(proj, dc, conv_w):
    S = proj.shape[0]
    tr = min(CONV_TR, S)
    ni = S // tr

    def body(a_ref, b_ref, ap_ref, bp_ref, dc_ref, dcn_ref, w_ref, da_ref, db_ref, dw_ref, gbuf, dbuf, sbuf, wrev,
             shift_ref):
        i = pl.program_id(0)

        @pl.when(i == 0)
        def _():
            dw_ref[...] = jnp.zeros(dw_ref.shape, F32)
            for k in range(CONV_K):
                wrev[k:k + 1, :] = w_ref[CONV_K - 1 - k:CONV_K - k, :]

        sb = _sigmoid(b_ref[...])
        sbuf[...] = sb
        prev = ap_ref[tr - HALO:, :] * _sigmoid(bp_ref[tr - HALO:, :])
        gbuf[0:HALO, :] = jnp.where(i > 0, prev, 0.0)
        gbuf[HALO:HALO + tr, :] = a_ref[...] * sb
        dbuf[0:tr, :] = dc_ref[...]
        dbuf[tr:tr + HALO, :] = jnp.where(i < ni - 1, dcn_ref[0:HALO, :], 0.0)

        def emit(rows, cols, dg):
            s = sbuf[rows, cols]
            da_ref[rows, cols] = (dg * s).astype(BF16)
            db_ref[rows, cols] = (dg * a_ref[rows, cols] * s * (1.0 - s)).astype(BF16)

        _fill_shifted(shift_ref, dbuf)
        _conv_taps(shift_ref, wrev, 0, tr, emit)
        _fill_shifted(shift_ref, gbuf)
        base = HALO - (CONV_K - 1)
        for c0 in range(0, BW, HEAD):
            cols = slice(c0, c0 + HEAD)
            dcs = [dc_ref[r0:r0 + CONV_ROWS, cols] for r0 in range(0, tr, CONV_ROWS)]
            for k in range(CONV_K):
                prod = dcs[0] * _shifted(shift_ref, base + k, CONV_ROWS, cols)
                for n in range(1, len(dcs)):
                    prod = prod + dcs[n] * _shifted(shift_ref, base + n * CONV_ROWS + k, CONV_ROWS, cols)
                dw_ref[k:k + 1, cols] += jnp.sum(prod, axis=0, keepdims=True)

    cur = lambda k: (lambda i: (i, k))
    prv = lambda k: (lambda i: (jnp.maximum(i - 1, 0), k))
    wpad = jnp.pad(conv_w, ((0, HALO - CONV_K), (0, 0)))
    return pl.pallas_call(
        body, name="conv_bwd_taps", grid=(ni,),
        in_specs=[pl.BlockSpec((tr, BW), cur(C_BA)), pl.BlockSpec((tr, BW), cur(C_BB)),
                  pl.BlockSpec((tr, BW), prv(C_BA)), pl.BlockSpec((tr, BW), prv(C_BB)),
                  pl.BlockSpec((tr, BW), lambda i: (i, 0)),
                  pl.BlockSpec((tr, BW), lambda i: (jnp.minimum(i + 1, ni - 1), 0)),
                  pl.BlockSpec((HALO, BW), lambda i: (0, 0))],
        out_specs=[pl.BlockSpec((tr, BW), lambda i: (i, 0)), pl.BlockSpec((tr, BW), lambda i: (i, 0)),
                   pl.BlockSpec((HALO, BW), lambda i: (0, 0))],
        out_shape=[jax.ShapeDtypeStruct((S, BW), BF16), jax.ShapeDtypeStruct((S, BW), BF16),
                   jax.ShapeDtypeStruct((HALO, BW), F32)],
        scratch_shapes=[pltpu.VMEM((tr + HALO + SUBLANES, BW), F32), pltpu.VMEM((tr + HALO + SUBLANES, BW), F32),
                        pltpu.VMEM((tr, BW), F32), pltpu.VMEM((HALO, BW), F32),
                        pltpu.VMEM((SUBLANES, tr + HALO, BW), F32)],
        compiler_params=_cparams(("arbitrary",)),
    )(proj, proj, proj, proj, dc, dc, wpad)


SGU_TR = 256


def _sgu_mix(wm, vn, bias):
    tr = vn.shape[0]
    rows = []
    for n in range(tr // HEAD):
        cols = []
        for g in range(NHEAD):
            blk = vn[n * HEAD:(n + 1) * HEAD, g * HEAD:(g + 1) * HEAD]
            cols.append(_dot(wm[g], blk, NN) + bias[g * HEAD:(g + 1) * HEAD, :])
        rows.append(jnp.concatenate(cols, axis=1))
    return jnp.concatenate(rows, axis=0)


def _sgu_masked_w(w2d):
    rr = lax.broadcasted_iota(jnp.int32, (HEAD, HEAD), 0)
    cc = lax.broadcasted_iota(jnp.int32, (HEAD, HEAD), 1)
    tril = rr >= cc
    return [jnp.where(tril, w2d[g * HEAD:(g + 1) * HEAD, :], 0.0).astype(BF16) for g in range(NHEAD)], tril


def _sgu_inputs(sgu_w, sgu_b, ln_g, ln_b):
    w2d = sgu_w.reshape(NHEAD * HEAD, HEAD)
    bias = jnp.broadcast_to(sgu_b[:, :, None], (NHEAD, HEAD, HEAD)).reshape(NHEAD * HEAD, HEAD)
    return [w2d, bias, ln_g.reshape(1, BW), ln_b.reshape(1, BW)]


def _sgu_fwd(proj, sgu_w, sgu_b, ln_g, ln_b):
    S = proj.shape[0]

    def fn(i, j, rv, bv):
        cu, cv, cg = rv
        w2d, bias, lg, lb = bv
        wm, _ = _sgu_masked_w(w2d)
        u, _ = _gelu_and_grad(cu)
        v, _ = _gelu_and_grad(cv)
        vn, _, _ = _ln_fwd(v, lg, lb)
        z = _sgu_mix(wm, vn, bias)
        gate, _ = _silu_and_grad(cg)
        return [u * z * gate], []

    rows = [(proj, BW, _c(C_CU), 0), (proj, BW, _c(C_CV), 0), (proj, BW, _c(C_CG), 0)]
    return _rowwise("sgu_fwd", fn, rows, _sgu_inputs(sgu_w, sgu_b, ln_g, ln_b), [(BW, BW, _c(0), BF16)], [],
                    nrows=S, tr=min(SGU_TR, S))[0]


def _sgu_bwd(proj, dy, sgu_w, sgu_b, ln_g, ln_b):
    S = proj.shape[0]

    def fn(i, j, rv, bv):
        cu, cv, cg, dyv = rv
        w2d, bias, lg, lb = bv
        wm, tril = _sgu_masked_w(w2d)
        u, du_dcu = _gelu_and_grad(cu)
        v, dv_dcv = _gelu_and_grad(cv)
        vn, xh, rstd = _ln_fwd(v, lg, lb)
        z = _sgu_mix(wm, vn, bias)
        gate, dgate = _silu_and_grad(cg)
        dz = dyv * u * gate
        dcu = dyv * z * gate * du_dcu
        dcg = dyv * u * z * dgate
        tr = dz.shape[0]
        dvn_rows = []
        dw = [jnp.zeros((HEAD, HEAD), F32) for _ in range(NHEAD)]
        dbias = [jnp.zeros((HEAD, HEAD), F32) for _ in range(NHEAD)]
        for n in range(tr // HEAD):
            cols = []
            for g in range(NHEAD):
                dzb = dz[n * HEAD:(n + 1) * HEAD, g * HEAD:(g + 1) * HEAD]
                vnb = vn[n * HEAD:(n + 1) * HEAD, g * HEAD:(g + 1) * HEAD]
                cols.append(_dot(wm[g], dzb, TN))
                dw[g] = dw[g] + _dot(dzb, vnb, NT)
                dbias[g] = dbias[g] + dzb
            dvn_rows.append(jnp.concatenate(cols, axis=1))
        dvn = jnp.concatenate(dvn_rows, axis=0)
        dv, dlg, dlb = _ln_bwd(dvn, xh, rstd, lg)
        dw2d = jnp.concatenate([jnp.where(tril, d, 0.0) for d in dw], axis=0)
        return [dcu, dv * dv_dcv, dcg], [dw2d, jnp.concatenate(dbias, axis=0), dlg, dlb]

    rows = [(proj, BW, _c(C_CU), 0), (proj, BW, _c(C_CV), 0), (proj, BW, _c(C_CG), 0), (dy, BW, _c(2), 0)]
    return _rowwise("sgu_bwd", fn, rows, _sgu_inputs(sgu_w, sgu_b, ln_g, ln_b),
                    [(BW, BW, _c(0), BF16)] * 3,
                    [(NHEAD * HEAD, HEAD, HEAD, _c(0)), (NHEAD * HEAD, HEAD, HEAD, _c(0)), (1, BW, BW, _c(0)),
                     (1, BW, BW, _c(0))], nrows=S, tr=min(SGU_TR, S))


NEG = -1e30
NCOL = IN_WIDTH // BW


def _band_masks():
    a = lax.broadcasted_iota(jnp.int32, (HEAD, HEAD), 0)
    c = lax.broadcasted_iota(jnp.int32, (HEAD, HEAD), 1)
    return a >= c, a <= c


def _dil_view(proj, grp):
    S = proj.shape[0]
    dil = DIL_PATTERNS[grp][1]
    if dil == 1:
        return proj, NCOL, C_DQ + grp, C_DK + grp, C_DV
    cols = [proj[:, c * BW:(c + 1) * BW] for c in (C_DQ + grp, C_DK + grp, C_DV)]
    return jnp.concatenate(cols, axis=1).reshape(S // dil, dil * 3 * BW), 3, 0, 1, 2


def _dil_fwd(proj, grp):
    S = proj.shape[0]
    dil = DIL_PATTERNS[grp][1]
    L = S // dil
    nb = L // HEAD
    proj2, ncol, cq, ck, cv = _dil_view(proj, grp)
    scale = HEAD ** -0.5

    def body(q_ref, kc_ref, kp_ref, vc_ref, vp_ref, o_ref, l_ref):
        b = pl.program_id(1)
        m_cur, m_prev = _band_masks()
        m_prev = m_prev & (b > 0)
        hs = range(NHEAD)
        sl = [slice(h * HEAD, (h + 1) * HEAD) for h in hs]
        q = [q_ref[:, sl[h]].astype(BF16) for h in hs]
        s_c = [jnp.where(m_cur, _dot(q[h], kc_ref[:, sl[h]], NT) * scale, NEG) for h in hs]
        s_p = [jnp.where(m_prev, _dot(q[h], kp_ref[:, sl[h]], NT) * scale, NEG) for h in hs]
        m = [jnp.maximum(jnp.max(s_c[h], axis=1, keepdims=True), jnp.max(s_p[h], axis=1, keepdims=True)) for h in hs]
        p_c = [jnp.exp(s_c[h] - m[h]) for h in hs]
        p_p = [jnp.exp(s_p[h] - m[h]) for h in hs]
        den = [jnp.sum(p_c[h], axis=1, keepdims=True) + jnp.sum(p_p[h], axis=1, keepdims=True) for h in hs]
        o = [(_dot(p_c[h], vc_ref[:, sl[h]], NN) + _dot(p_p[h], vp_ref[:, sl[h]], NN)) / den[h] for h in hs]
        for h in hs:
            o_ref[:, sl[h]] = o[h]
            l_ref[:, sl[h]] = jnp.broadcast_to(m[h] + jnp.log(den[h]), (HEAD, HEAD))

    cur = lambda col: (lambda r, b: (b, r * ncol + col))
    prv = lambda col: (lambda r, b: (jnp.maximum(b - 1, 0), r * ncol + col))
    out_map = lambda r, b: (b, r)
    o, lse = pl.pallas_call(
        body, name=f"dil_fwd{grp}", grid=(dil, nb),
        in_specs=[pl.BlockSpec((HEAD, BW), cur(cq)),
                  pl.BlockSpec((HEAD, BW), cur(ck)), pl.BlockSpec((HEAD, BW), prv(ck)),
                  pl.BlockSpec((HEAD, BW), cur(cv)), pl.BlockSpec((HEAD, BW), prv(cv))],
        out_specs=[pl.BlockSpec((HEAD, BW), out_map), pl.BlockSpec((HEAD, BW), out_map)],
        out_shape=[jax.ShapeDtypeStruct((L, dil * BW), F32), jax.ShapeDtypeStruct((L, dil * BW), F32)],
        compiler_params=_cparams(("arbitrary", "arbitrary")),
    )(proj2, proj2, proj2, proj2, proj2)
    return o.reshape(S, BW), lse.reshape(S, BW)


def _dil_combine(proj, os_, lses):
    S = proj.shape[0]

    def fn(i, j, rv, bv):
        o1, o2, o3, l1, l2, l3, dg = rv
        m = jnp.maximum(jnp.maximum(l1, l2), l3)
        e1, e2, e3 = jnp.exp(l1 - m), jnp.exp(l2 - m), jnp.exp(l3 - m)
        den = e1 + e2 + e3
        out = (e1 * o1 + e2 * o2 + e3 * o3) / den
        gate, _ = _silu_and_grad(dg)
        return [out, m + jnp.log(den), out * gate], []

    rows = [(a, BW, _c(0), 0) for a in (*os_, *lses)] + [(proj, BW, _c(C_DG), 0)]
    return _rowwise("dil_combine", fn, rows, [], [(BW, BW, _c(0), F32), (BW, BW, _c(0), F32), (BW, BW, _c(0), BF16)],
                    [], nrows=S, tr=256)


def _dil_gate_bwd(proj, att, dy):
    S = proj.shape[0]

    def fn(i, j, rv, bv):
        dg, av, dyv = rv
        gate, dgate = _silu_and_grad(dg)
        dout = dyv * gate
        prod = dout * av
        tr = prod.shape[0]
        delta = jnp.concatenate(
            [jnp.broadcast_to(jnp.sum(prod[:, h * HEAD:(h + 1) * HEAD], axis=1, keepdims=True), (tr, HEAD))
             for h in range(NHEAD)], axis=1)
        return [dout, dyv * av * dgate, delta], []

    rows = [(proj, BW, _c(C_DG), 0), (att, BW, _c(0), 0), (dy, BW, _c(3), 0)]
    return _rowwise("dil_gate_bwd", fn, rows, [], [(BW, BW, _c(0), F32), (BW, BW, _c(0), BF16), (BW, BW, _c(0), F32)],
                    [], nrows=S, tr=256)


def _dil_bwd(proj, dout, lse, delta, grp):
    S = proj.shape[0]
    dil = DIL_PATTERNS[grp][1]
    L = S // dil
    nb = L // HEAD
    proj2, ncol, cq, ck, cv = _dil_view(proj, grp)
    view = lambda a: a.reshape(L, dil * BW)
    scale = HEAD ** -0.5

    def body(q_ref, qn_ref, kc_ref, kp_ref, vc_ref, vp_ref, do_ref, don_ref, l_ref, ln_ref, d_ref, dn_ref,
             dq_ref, dk_ref, dv_ref):
        b = pl.program_id(1)
        m_cur, m_band = _band_masks()
        m_prev = m_band & (b > 0)
        m_next = m_band & (b < nb - 1)
        hs = range(NHEAD)
        sl = [slice(h * HEAD, (h + 1) * HEAD) for h in hs]
        q, qn = [q_ref[:, s].astype(BF16) for s in sl], [qn_ref[:, s].astype(BF16) for s in sl]
        kc, kp = [kc_ref[:, s].astype(BF16) for s in sl], [kp_ref[:, s].astype(BF16) for s in sl]
        vc, vp = [vc_ref[:, s].astype(BF16) for s in sl], [vp_ref[:, s].astype(BF16) for s in sl]
        do, don = [do_ref[:, s].astype(BF16) for s in sl], [don_ref[:, s].astype(BF16) for s in sl]
        p_cc = [jnp.exp(jnp.where(m_cur, _dot(q[h], kc[h], NT) * scale - l_ref[:, sl[h]], NEG)) for h in hs]
        p_cp = [jnp.exp(jnp.where(m_prev, _dot(q[h], kp[h], NT) * scale - l_ref[:, sl[h]], NEG)) for h in hs]
        p_nc = [jnp.exp(jnp.where(m_next, _dot(qn[h], kc[h], NT) * scale - ln_ref[:, sl[h]], NEG)) for h in hs]
        ds_cc = [p_cc[h] * (_dot(do[h], vc[h], NT) - d_ref[:, sl[h]]) for h in hs]
        ds_cp = [p_cp[h] * (_dot(do[h], vp[h], NT) - d_ref[:, sl[h]]) for h in hs]
        ds_nc = [p_nc[h] * (_dot(don[h], vc[h], NT) - dn_ref[:, sl[h]]) for h in hs]
        dq = [((_dot(ds_cc[h], kc[h], NN) + _dot(ds_cp[h], kp[h], NN)) * scale).astype(BF16) for h in hs]
        dk = [((_dot(ds_cc[h], q[h], TN) + _dot(ds_nc[h], qn[h], TN)) * scale).astype(BF16) for h in hs]
        dv = [_dot(p_cc[h], do[h], TN) + _dot(p_nc[h], don[h], TN) for h in hs]
        for h in hs:
            dq_ref[:, sl[h]] = dq[h]
            dk_ref[:, sl[h]] = dk[h]
            dv_ref[:, sl[h]] = dv[h]

    cur = lambda col: (lambda r, b: (b, r * ncol + col))
    prv = lambda col: (lambda r, b: (jnp.maximum(b - 1, 0), r * ncol + col))
    nxt = lambda col: (lambda r, b: (jnp.minimum(b + 1, nb - 1), r * ncol + col))
    o_cur = lambda r, b: (b, r)
    o_nxt = lambda r, b: (jnp.minimum(b + 1, nb - 1), r)
    blk = (HEAD, BW)
    dq, dk, dv = pl.pallas_call(
        body, name=f"dil_bwd{grp}", grid=(dil, nb),
        in_specs=[pl.BlockSpec(blk, cur(cq)), pl.BlockSpec(blk, nxt(cq)),
                  pl.BlockSpec(blk, cur(ck)), pl.BlockSpec(blk, prv(ck)),
                  pl.BlockSpec(blk, cur(cv)), pl.BlockSpec(blk, prv(cv)),
                  pl.BlockSpec(blk, o_cur), pl.BlockSpec(blk, o_nxt),
                  pl.BlockSpec(blk, o_cur), pl.BlockSpec(blk, o_nxt),
                  pl.BlockSpec(blk, o_cur), pl.BlockSpec(blk, o_nxt)],
        out_specs=[pl.BlockSpec(blk, o_cur)] * 3,
        out_shape=[jax.ShapeDtypeStruct((L, dil * BW), BF16), jax.ShapeDtypeStruct((L, dil * BW), BF16),
                   jax.ShapeDtypeStruct((L, dil * BW), F32)],
        compiler_params=_cparams(("arbitrary", "arbitrary")),
    )(proj2, proj2, proj2, proj2, proj2, proj2, view(dout), view(dout), view(lse), view(lse), view(delta), view(delta))
    return dq.reshape(S, BW), dk.reshape(S, BW), dv.reshape(S, BW)


def _add3(a, b, c):
    S, W = a.shape

    def fn(i, j, rv, bv):
        return [rv[0] + rv[1] + rv[2]], []

    return _rowwise("add3", fn, [(a, W, _c(0), 0), (b, W, _c(0), 0), (c, W, _c(0), 0)], [], [(W, W, _c(0), BF16)], [],
                    nrows=S, tr=512)[0]


def _layer_fwd(x, p, cargo=None):
    h = _rms_fwd(x, p["norm_g"])
    proj = _mm_nn("in_proj", h, p["w_in"], F32)
    gpre = _mm_nn("gate_proj", h, p["w_gate"], F32, bias=p["b_gate"].reshape(1, GATE_W))
    (att_a, ya, runs_a), carried = _sb_fwd(proj, cargo)
    cpre, yb = _conv_fwd(proj, p["conv_w"], p["conv_b"], p["conv_ln_g"], p["conv_ln_b"])
    yc = _sgu_fwd(proj, p["sgu_w"], p["sgu_b"], p["sgu_ln_g"], p["sgu_ln_b"])
    dil = [_dil_fwd(proj, g) for g in range(3)]
    att_d, lse_d, yd = _dil_combine(proj, [d[0] for d in dil], [d[1] for d in dil])
    y = jnp.concatenate([ya, yb, yc, yd], axis=1)
    yproj = _mm_nn("branch_proj", y, p["w_branch"], F32, groups=4)
    merged = _merge_fwd(gpre, yproj)
    x_next = _mm_nn("out_proj", merged, p["w_out"], F32, res=x)
    saved = dict(x=x, h=h, proj=proj, gpre=gpre, att_a=att_a, runs_a=runs_a, cpre=cpre, att_d=att_d, lse_d=lse_d, y=y,
                 yproj=yproj, merged=merged)
    return x_next, saved, carried


def _layer_bwd(dout, s, p, pending=None):
    proj = s["proj"]
    dmerged = _mm_nt("out_proj_dx", dout, p["w_out"], F32)
    g_w_out = _mm_tn("out_proj_dw", s["merged"], dout)
    if pending is None:
        dyproj, dgpre, g_b_gate = _merge_bwd(dmerged, s["gpre"], s["yproj"])
        cargo = None
    else:
        grads_above, axes, scalars = pending
        (dyproj, dgpre, g_b_gate), got = _merge_bwd(dmerged, s["gpre"], s["yproj"], _swap_cargo(grads_above, axes))
        sums = _reduce_sums(grads_above, got, axes, scalars)
        cargo = _scatter_cargo(sums, axes)
    dy = _mm_nt("branch_proj_dx", dyproj, p["w_branch"], F32, groups=4)
    g_w_branch = _mm_tn("branch_proj_dw", s["y"], dyproj, groups=4)
    (d_aq, d_ak, d_av, d_ag), parts = _sb_bwd(proj, s["att_a"], s["runs_a"], dy, cargo)
    dc, d_bg, g_cln_g, g_cln_b, g_conv_b = _conv_bwd_norm(proj, s["cpre"], dy, p["conv_ln_g"], p["conv_ln_b"])
    d_ba, d_bb, g_conv_w = _conv_bwd_taps(proj, dc, p["conv_w"])
    d_cu, d_cv, d_cg, g_sgu_w, g_sgu_bias, g_sln_g, g_sln_b = _sgu_bwd(
        proj, dy, p["sgu_w"], p["sgu_b"], p["sgu_ln_g"], p["sgu_ln_b"])
    dout_d, d_dg, delta = _dil_gate_bwd(proj, s["att_d"], dy)
    dil = [_dil_bwd(proj, dout_d, s["lse_d"], delta, g) for g in range(3)]
    d_dv = _add3(dil[0][2], dil[1][2], dil[2][2])
    pieces = [d_aq, d_ak, d_av, d_ag, d_ba, d_bb, d_bg, d_cu, d_cv, d_cg,
              dil[0][0], dil[1][0], dil[2][0], dil[0][1], dil[1][1], dil[2][1], d_dv, d_dg]
    dproj = jnp.concatenate([t.astype(BF16) for t in pieces], axis=1)
    dh_gate = _mm_nt("gate_proj_dx", dgpre, p["w_gate"], F32)
    dh = _mm_nt("in_proj_dx", dproj, p["w_in"], F32, res=dh_gate)
    g_w_in = _mm_tn("in_proj_dw", s["h"], dproj)
    g_w_gate = _mm_tn("gate_proj_dw", s["h"], dgpre)
    dx, g_norm_g = _rms_bwd(s["x"], dh, dout, p["norm_g"])
    grads = dict(
        norm_g=g_norm_g.reshape(D_MODEL), w_in=g_w_in, conv_w=g_conv_w[:CONV_K], conv_b=g_conv_b.reshape(BW),
        conv_ln_g=g_cln_g.reshape(BW), conv_ln_b=g_cln_b.reshape(BW), sgu_ln_g=g_sln_g.reshape(BW),
        sgu_ln_b=g_sln_b.reshape(BW), sgu_w=g_sgu_w.reshape(NHEAD, HEAD, HEAD),
        sgu_b=jnp.sum(g_sgu_bias.reshape(NHEAD, HEAD, HEAD), axis=-1), w_branch=g_w_branch,
        w_gate=g_w_gate, b_gate=g_b_gate.reshape(GATE_W), w_out=g_w_out)
    return dx, grads, (None if pending is None else (sums, parts))


CHIP_FLIPS =((1, 0), (0, 1), (1, 1))


def _at(ref, axis, start, size):
    idx = [slice(None)] * len(ref.shape)
    idx[axis] = pl.ds(start, size)
    return ref.at[tuple(idx)]


def _position():
    return lax.axis_index("x"), lax.axis_index("y"), lax.axis_index("c")


def _gather(name, w, axis):
    n = w.shape[axis]
    full = list(w.shape)
    full[axis] = 4 * n

    def body(w_ref, out_ref, send_sems, recv_sems, local_sem):
        x, y, c = _position()
        mine = pltpu.make_async_copy(w_ref, _at(out_ref, axis, (2 * x + y) * n, n), local_sem)
        mine.start()
        sends = []
        for k, (fx, fy) in enumerate(CHIP_FLIPS):
            cp = pltpu.make_async_remote_copy(
                src_ref=w_ref, dst_ref=_at(out_ref, axis, (2 * x + y) * n, n), send_sem=send_sems.at[k],
                recv_sem=recv_sems.at[k], device_id=(x ^ fx, y ^ fy, c), device_id_type=MESH)
            cp.start()
            sends.append(cp)
        for k, (fx, fy) in enumerate(CHIP_FLIPS):
            px, py = x ^ fx, y ^ fy
            pltpu.make_async_remote_copy(
                src_ref=w_ref, dst_ref=_at(out_ref, axis, (2 * px + py) * n, n), send_sem=send_sems.at[k],
                recv_sem=recv_sems.at[k], device_id=(px, py, c), device_id_type=MESH).wait_recv()
        for cp in sends:
            cp.wait_send()
        mine.wait()

    return pl.pallas_call(
        body, name=name, in_specs=[ANY], out_specs=ANY, out_shape=jax.ShapeDtypeStruct(tuple(full), w.dtype),
        scratch_shapes=[pltpu.SemaphoreType.DMA((3,)), pltpu.SemaphoreType.DMA((3,)), pltpu.SemaphoreType.DMA],
    )(w)


def _block(ref, ax, shard, half):
    if shard is not None:
        n = ref.shape[ax] // 4
        ref = _at(ref, ax, shard * n, n)
    if half is not None:
        nh = ref.shape[1 - ax] // 2
        ref = _at(ref, 1 - ax, half * nh, nh)
    return ref


def _remote(src, dst, send_sems, recv_sems, k, device):
    return pltpu.make_async_remote_copy(src_ref=src, dst_ref=dst, send_sem=send_sems.at[k], recv_sem=recv_sems.at[k],
                                        device_id=device, device_id_type=MESH)


def _gather_cargo(fulls, axes):
    na = len(fulls)

    def start(ins, outs, send, recv):
        x, y, c = _position()
        for a in range(na):
            mine = _block(outs[a], axes[a], 2 * x + y, c)
            for k, (fx, fy) in enumerate(CHIP_FLIPS):
                _remote(mine, mine, send, recv, 3 * a + k, (x ^ fx, y ^ fy, c)).start()

    def finish(ins, outs, send, recv):
        x, y, c = _position()
        sib = (x, y, 1 - c)
        for a in range(na):
            for k, (fx, fy) in enumerate(CHIP_FLIPS):
                got = _block(outs[a], axes[a], 2 * (x ^ fx) + (y ^ fy), c)
                _remote(got, got, send, recv, 3 * a + k, sib).wait_recv()
                _remote(got, got, send, recv, 3 * (na + a) + k, sib).start()
        for a in range(na):
            mine = _block(outs[a], axes[a], 2 * x + y, c)
            for k, (fx, fy) in enumerate(CHIP_FLIPS):
                passed = _block(outs[a], axes[a], 2 * (x ^ fx) + (y ^ fy), c)
                theirs = _block(outs[a], axes[a], 2 * (x ^ fx) + (y ^ fy), 1 - c)
                _remote(theirs, theirs, send, recv, 3 * (na + a) + k, sib).wait_recv()
                _remote(passed, passed, send, recv, 3 * (na + a) + k, sib).wait_send()
                _remote(mine, mine, send, recv, 3 * a + k, sib).wait_send()

    shapes = [jax.ShapeDtypeStruct(f.shape, f.dtype) for f in fulls]
    return _Cargo(fulls, shapes, {a: a for a in range(na)}, 6 * na, start, finish)


def _scatter_cargo(sums, axes):
    na = len(sums)
    shapes = []
    for s, ax in zip(sums, axes):
        piece = list(s.shape)
        piece[ax] //= 4
        shapes.append(jax.ShapeDtypeStruct((3, *piece), s.dtype))

    def copies(ins, outs, send, recv):
        x, y, c = _position()
        return [_remote(_block(ins[a], axes[a], 2 * (x ^ fx) + (y ^ fy), None), outs[a].at[k], send, recv, 3 * a + k,
                        (x ^ fx, y ^ fy, c))
                for a in range(na) for k, (fx, fy) in enumerate(CHIP_FLIPS)]

    def start(ins, outs, send, recv):
        for cp in copies(ins, outs, send, recv):
            cp.start()

    def finish(ins, outs, send, recv):
        for cp in copies(ins, outs, send, recv):
            cp.wait()

    return _Cargo(sums, shapes, {}, 3 * na, start, finish)


def _swap_cargo(grads, axes):
    na = len(grads)
    shapes = []
    for g, ax in zip(grads, axes):
        half = list(g.shape)
        half[1 - ax] //= 2
        shapes.append(jax.ShapeDtypeStruct(tuple(half), g.dtype))

    def copies(ins, outs, send, recv):
        x, y, c = _position()
        return [_remote(_block(ins[a], axes[a], None, 1 - c), outs[a], send, recv, a, (x, y, 1 - c)) for a in range(na)]

    def start(ins, outs, send, recv):
        for cp in copies(ins, outs, send, recv):
            cp.start()

    def finish(ins, outs, send, recv):
        for cp in copies(ins, outs, send, recv):
            cp.wait()

    return _Cargo(grads, shapes, {}, na, start, finish)


def _join_halves(name, shards, axes):
    na = len(shards)

    def body(*refs):
        outs, (send, recv) = refs[na:2 * na], refs[2 * na:]
        x, y, c = _position()
        sib = (x, y, 1 - c)
        for a in range(na):
            mine = _block(outs[a], axes[a], None, c)
            _remote(mine, mine, send, recv, a, sib).start()
        for a in range(na):
            mine = _block(outs[a], axes[a], None, c)
            theirs = _block(outs[a], axes[a], None, 1 - c)
            _remote(theirs, theirs, send, recv, a, sib).wait_recv()
            _remote(mine, mine, send, recv, a, sib).wait_send()

    return pl.pallas_call(
        body, name=name, in_specs=[ANY] * na, out_specs=[ANY] * na,
        out_shape=[jax.ShapeDtypeStruct(s.shape, s.dtype) for s in shards],
        scratch_shapes=[pltpu.SemaphoreType.DMA((na,)), pltpu.SemaphoreType.DMA((na,))],
        input_output_aliases={a: a for a in range(na)},
    )(*shards)


def _sum_terms(name, terms, out_shape, out_dtype, blk, nblocks, out_map, scalars):
    tr, tc = blk

    def body(s_ref, *refs):
        acc = refs[0][...].astype(F32)
        for r in refs[1:-1]:
            acc = acc + r[...].astype(F32)
        refs[-1][...] = acc.astype(refs[-1].dtype)

    in_specs = [pl.BlockSpec((None, tr, tc) if a.ndim == 3 else (tr, tc), m) for a, m in terms]
    return pl.pallas_call(
        body, name=name,
        grid_spec=pltpu.PrefetchScalarGridSpec(
            num_scalar_prefetch=1, grid=(nblocks,), in_specs=in_specs, out_specs=pl.BlockSpec((tr, tc), out_map)),
        out_shape=jax.ShapeDtypeStruct(out_shape, out_dtype),
        compiler_params=_cparams(("arbitrary",)),
    )(scalars, *[a for a, _ in terms])


def _rows_for(cols, rows, budget=1 << 19):
    if rows % SUBLANES:
        return rows
    tr = 8
    while tr * 2 * cols <= budget and tr * 2 <= min(rows, 256) and rows % (tr * 2) == 0:
        tr *= 2
    assert rows % tr == 0
    return tr


def _place_shard(name, w, layer, ax, scalars):
    _, R, C = w.shape
    tr = _rows_for(C, R)
    nb = R // tr
    if ax == 0:
        shape, out_map = (4 * R, C), (lambda i, s: (s[1] * nb + i, 0))
    else:
        shape, out_map = (R, 4 * C), (lambda i, s: (i, s[1]))
    return _sum_terms(name, [(w, lambda i, s: (layer, i, 0))], shape, BF16, (tr, C), nb, out_map, scalars)


def _reduce_sums(grads, got, axes, scalars):
    sums = []
    for a, (g, h, ax) in enumerate(zip(grads, got, axes)):
        Rh, Ch = h.shape
        tr = _rows_for(Ch, Rh)
        nb = Rh // tr
        g_map = (lambda i, s, nb=nb: (s[0] * nb + i, 0)) if ax == 1 else (lambda i, s: (i, s[0]))
        sums.append(_sum_terms(f"rs_sum2_{a}", [(g, g_map), (h, lambda i, s: (i, 0))], (Rh, Ch), BF16, (tr, Ch), nb,
                               lambda i, s: (i, 0), scalars))
    return sums


def _reduce_end(sums, parts, axes, scalars):
    shards = []
    for a, (s1, pt, ax) in enumerate(zip(sums, parts, axes)):
        _, Rs, Cs = pt.shape
        tr = _rows_for(Cs, Rs)
        nb = Rs // tr
        if ax == 1:
            shape, s_map, o_map = (2 * Rs, Cs), (lambda i, s: (i, s[1])), (lambda i, s, nb=nb: (s[0] * nb + i, 0))
        else:
            shape, s_map, o_map = (Rs, 2 * Cs), (lambda i, s, nb=nb: (s[1] * nb + i, 0)), (lambda i, s: (i, s[0]))
        terms = [(s1, s_map)] + [(pt, (lambda i, s, k=k: (k, i, 0))) for k in range(3)]
        shards.append(_sum_terms(f"rs_sum4_{a}", terms, shape, F32, (tr, Cs), nb, o_map, scalars))
    return _join_halves("rs_join", shards, axes)


def _all_to_all_small(buf):
    nr = buf.shape[0]

    def body(b_ref, out_ref, send_sems, recv_sems, local_sem):
        x, y, c = _position()
        me = 4 * x + 2 * y + c
        mine = pltpu.make_async_copy(b_ref, out_ref.at[me], local_sem)
        mine.start()
        sends = []
        for r in range(1, 8):
            peer = (x ^ (r >> 2), y ^ ((r >> 1) & 1), c ^ (r & 1))
            cp = pltpu.make_async_remote_copy(
                src_ref=b_ref, dst_ref=out_ref.at[me], send_sem=send_sems.at[r - 1], recv_sem=recv_sems.at[r - 1],
                device_id=peer, device_id_type=MESH)
            cp.start()
            sends.append(cp)
        for r in range(1, 8):
            px, py, pc = x ^ (r >> 2), y ^ ((r >> 1) & 1), c ^ (r & 1)
            pltpu.make_async_remote_copy(
                src_ref=b_ref, dst_ref=out_ref.at[4 * px + 2 * py + pc], send_sem=send_sems.at[r - 1],
                recv_sem=recv_sems.at[r - 1], device_id=(px, py, pc), device_id_type=MESH).wait_recv()
        for cp in sends:
            cp.wait_send()
        mine.wait()

    return pl.pallas_call(
        body, name="small_exchange", in_specs=[ANY], out_specs=ANY,
        out_shape=jax.ShapeDtypeStruct((8, nr, 128), buf.dtype),
        scratch_shapes=[pltpu.SemaphoreType.DMA((7,)), pltpu.SemaphoreType.DMA((7,)), pltpu.SemaphoreType.DMA],
    )(buf)


SMALL_TR = 256


def _all_reduce_small(buf):
    nr = buf.shape[0]
    slots = _all_to_all_small(buf).reshape(8 * nr, 128)
    nblk = nr // SMALL_TR

    def fn(i, j, rv, bv):
        acc = rv[0]
        for v in rv[1:]:
            acc = acc + v
        return [acc], []

    rows = [(slots, 128, _c(0), (lambda i, d=d: d * nblk + i)) for d in range(8)]
    return _rowwise("small_sum", fn, rows, [], [(128, 128, _c(0), F32)], [], nrows=nr, tr=SMALL_TR)[0]


def _adamw(name, w, g, m, v):
    rows, cols = w.shape
    tr = _rows_for(cols, rows, budget=1 << 18)

    def fn(i, j, rv, bv):
        wv, gv, mv, vv = rv
        m2 = ADAM_B1 * mv + (1.0 - ADAM_B1) * gv
        v2 = ADAM_B2 * vv + (1.0 - ADAM_B2) * (gv * gv)
        m_hat = m2 / (1.0 - ADAM_B1 ** ADAM_STEP)
        v_hat = v2 / (1.0 - ADAM_B2 ** ADAM_STEP)
        delta = -ADAM_LR * (m_hat / (jnp.sqrt(v_hat) + ADAM_EPS) + ADAM_WD * wv)
        return [delta, m2, v2], []

    return _rowwise(name, fn, [(a, cols, _c(0), 0) for a in (w, g, m, v)], [], [(cols, cols, _c(0), F32)] * 3, [],
                    nrows=rows, tr=tr)


WEIGHTS = ("norm_g", "w_in", "conv_w", "conv_b", "conv_ln_g", "conv_ln_b", "sgu_ln_g", "sgu_ln_b", "sgu_w", "sgu_b",
           "w_branch", "w_gate", "b_gate", "w_out", "final_g")
BIG = ("w_in", "w_branch", "w_gate", "w_out")
BIG_AXIS = dict(w_in=1, w_branch=1, w_gate=1, w_out=0)
SMALL = tuple(n for n in WEIGHTS if n not in BIG)


def _pack(arrays, pad_rows):
    flat = jnp.concatenate([a.reshape(-1).astype(F32) for a in arrays])
    unit = 128 * pad_rows
    total = -(-flat.shape[0] // unit) * unit
    return jnp.pad(flat, (0, total - flat.shape[0])).reshape(total // 128, 128)


def _unpack(buf, shapes):
    flat = buf.reshape(-1)
    out, off = [], 0
    for shp in shapes:
        size = math.prod(shp)
        out.append(flat[off:off + size].reshape(shp))
        off += size
    return out


def kernel(x, norm_g, w_in, conv_w, conv_b, conv_ln_g, conv_ln_b, sgu_ln_g, sgu_ln_b, sgu_w, sgu_b, w_branch, w_gate, b_gate, w_out, final_g, loss_target, m_norm_g, m_w_in, m_conv_w, m_conv_b, m_conv_ln_g, m_conv_ln_b, m_sgu_ln_g, m_sgu_ln_b, m_sgu_w, m_sgu_b, m_w_branch, m_w_gate, m_b_gate, m_w_out, m_final_g, v_norm_g, v_w_in, v_conv_w, v_conv_b, v_conv_ln_g, v_conv_ln_b, v_sgu_ln_g, v_sgu_ln_b, v_sgu_w, v_sgu_b, v_w_branch, v_w_gate, v_b_gate, v_w_out, v_final_g):
    w = dict(norm_g=norm_g, w_in=w_in, conv_w=conv_w, conv_b=conv_b, conv_ln_g=conv_ln_g, conv_ln_b=conv_ln_b,
             sgu_ln_g=sgu_ln_g, sgu_ln_b=sgu_ln_b, sgu_w=sgu_w, sgu_b=sgu_b, w_branch=w_branch, w_gate=w_gate,
             b_gate=b_gate, w_out=w_out, final_g=final_g)
    m = dict(norm_g=m_norm_g, w_in=m_w_in, conv_w=m_conv_w, conv_b=m_conv_b, conv_ln_g=m_conv_ln_g,
             conv_ln_b=m_conv_ln_b, sgu_ln_g=m_sgu_ln_g, sgu_ln_b=m_sgu_ln_b, sgu_w=m_sgu_w, sgu_b=m_sgu_b,
             w_branch=m_w_branch, w_gate=m_w_gate, b_gate=m_b_gate, w_out=m_w_out, final_g=m_final_g)
    v = dict(norm_g=v_norm_g, w_in=v_w_in, conv_w=v_conv_w, conv_b=v_conv_b, conv_ln_g=v_conv_ln_g,
             conv_ln_b=v_conv_ln_b, sgu_ln_g=v_sgu_ln_g, sgu_ln_b=v_sgu_ln_b, sgu_w=v_sgu_w, sgu_b=v_sgu_b,
             w_branch=v_w_branch, w_gate=v_w_gate, b_gate=v_b_gate, w_out=v_w_out, final_g=v_final_g)
    depth = w_in.shape[0]
    chip = 2 * lax.axis_index("x") + lax.axis_index("y")
    scalars = jnp.stack([lax.axis_index("c"), chip]).astype(jnp.int32)
    axes = [BIG_AXIS[n] for n in BIG]

    stacked = dict(w_in=w_in, w_branch=w_branch.reshape(depth, 4 * BW, -1), w_gate=w_gate, w_out=w_out)
    buffers = [[_place_shard(f"place_{n}", stacked[n], l, BIG_AXIS[n], scalars) for n in BIG] for l in range(depth)]
    conv_w_full = _gather("gather_conv_w", conv_w, 2)
    gathered = _run_cargo("gather_first", _gather_cargo(buffers[0], axes))
    h = x[0]
    saved, layer_params = [], []
    for l in range(depth):
        p = {n: w[n][l] for n in SMALL if n not in ("final_g", "conv_w")}
        p["conv_w"] = conv_w_full[l]
        p.update(dict(zip(BIG, gathered)))
        cargo = _gather_cargo(buffers[l + 1], axes) if l + 1 < depth else None
        h, s, gathered = _layer_fwd(h, p, cargo)
        saved.append(s)
        layer_params.append(p)
    dh, g_final, loss = _final_loss(h, loss_target[0], final_g)

    layer_grads, reduced_big = [None] * depth, [None] * depth
    pending = None
    for l in reversed(range(depth)):
        dh, layer_grads[l], exchanged = _layer_bwd(dh, saved[l], layer_params[l], pending)
        if exchanged is not None:
            reduced_big[l + 1] = _reduce_end(*exchanged, axes, scalars)
        pending = ([layer_grads[l][n] for n in BIG], axes, scalars)
    sums = _reduce_sums(pending[0], _run_cargo("rs_halves_last", _swap_cargo(pending[0], axes)), axes, scalars)
    reduced_big[0] = _reduce_end(sums, _run_cargo("rs_shards_last", _scatter_cargo(sums, axes)), axes, scalars)
    grad_x = dh[None]
    local = {n: jnp.stack([layer_grads[l][n] for l in range(depth)]) for n in SMALL if n != "final_g"}
    local["final_g"] = g_final.reshape(-1)
    grads = {n: jnp.stack([reduced_big[l][a] for l in range(depth)]).reshape(w[n].shape) for a, n in enumerate(BIG)}
    small_shapes = [local[n].shape for n in SMALL] + [(128,)]
    reduced = _unpack(_all_reduce_small(_pack([local[n] for n in SMALL] + [loss.reshape(128)], SMALL_TR)), small_shapes)
    for n, r in zip(SMALL, reduced[:-1]):
        grads[n] = r
    loss_out = reduced[-1][0]
    ncw = conv_w.shape[2]
    grads["conv_w"] = lax.dynamic_slice_in_dim(grads["conv_w"], chip * ncw, ncw, axis=2)

    delta, new_m, new_v = {}, {}, {}
    for n in BIG:
        cols = w[n].shape[-1]
        d2, m2, v2 = _adamw(f"adamw_{n}", *[a.reshape(-1, cols) for a in (w[n], grads[n], m[n], v[n])])
        delta[n], new_m[n], new_v[n] = d2.reshape(w[n].shape), m2.reshape(w[n].shape), v2.reshape(w[n].shape)
    for n in SMALL:
        cols = w[n].shape[-1]
        outs = _adamw(f"adamw_{n}", *[a.reshape(-1, cols) for a in (w[n], grads[n], m[n], v[n])])
        delta[n], new_m[n], new_v[n] = [o.reshape(w[n].shape) for o in outs]
    return (loss_out, grad_x, *[grads[n] for n in WEIGHTS], *[delta[n] for n in WEIGHTS],
            *[new_m[n] for n in WEIGHTS], *[new_v[n] for n in WEIGHTS])
```

```python
import functools
import math

import jax
import jax.numpy as jnp
from jax import lax
from jax.experimental import pallas as pl
from jax.experimental.pallas import tpu as pltpu

F32 = jnp.float32
BF16 = jnp.bfloat16
MESH = pl.DeviceIdType.MESH

DEPTH = 4
D_MODEL = 2048
HEAD = 128
NHEAD = 8
BW = 1024
IN_WIDTH = 18432
GATE_W = 4 * D_MODEL
NORM_EPS = 1e-6
CONV_K = 31
HALO = 32
DIL_PATTERNS = ((128, 1), (512, 4), (2048, 16))
C_AQ, C_AK, C_AV, C_AG, C_BA, C_BB, C_BG, C_CU, C_CV, C_CG, C_DQ, C_DK, C_DV, C_DG = (
    0, 1, 2, 3, 4, 5, 6, 7, 8, 9, 10, 13, 16, 17)
ADAM_LR, ADAM_B1, ADAM_B2, ADAM_EPS, ADAM_WD, ADAM_STEP = 0.001, 0.9, 0.999, 1e-08, 0.01, 10
VMEM_LIMIT = 56 * 1024 * 1024

NN = (((1,), (0,)), ((), ()))
NT = (((1,), (1,)), ((), ()))
TN = (((0,), (0,)), ((), ()))


def _cparams(sem):
    return pltpu.CompilerParams(dimension_semantics=sem, vmem_limit_bytes=VMEM_LIMIT)


def _dot(a, b, dims):
    return lax.dot_general(a.astype(BF16), b.astype(BF16), dims, preferred_element_type=F32)


def _sigmoid(x):
    return 1.0 / (1.0 + jnp.exp(-x))


def _silu_and_grad(x):
    s = _sigmoid(x)
    return x * s, s * (1.0 + x * (1.0 - s))


_GELU_C = math.sqrt(2.0 / math.pi)


def _gelu_and_grad(x):
    t = jnp.tanh(_GELU_C * (x + 0.044715 * x * x * x))
    y = 0.5 * x * (1.0 + t)
    dy = 0.5 * (1.0 + t) + 0.5 * x * (1.0 - t * t) * _GELU_C * (1.0 + 3.0 * 0.044715 * x * x)
    return y, dy


def _ln_fwd(x, g, b):
    mu = jnp.mean(x, axis=-1, keepdims=True)
    xc = x - mu
    var = jnp.mean(xc * xc, axis=-1, keepdims=True)
    rstd = lax.rsqrt(var + NORM_EPS)
    xh = xc * rstd
    return xh * g + b, xh, rstd


def _ln_bwd(dy, xh, rstd, g):
    dxh = dy * g
    dx = rstd * (dxh - jnp.mean(dxh, axis=-1, keepdims=True) - xh * jnp.mean(dxh * xh, axis=-1, keepdims=True))
    return dx, jnp.sum(dy * xh, axis=0, keepdims=True), jnp.sum(dy, axis=0, keepdims=True)


def _rowwise(name, fn, rows, bcast, outs, accs, *, nrows, tr, nj=1, cargo=None):
    ni = nrows // tr
    nr, nb, no = len(rows), len(bcast), len(outs)

    def rmap(colfn, shift):
        if callable(shift):
            return lambda j, i: (shift(i), colfn(j))
        if shift == 0:
            return lambda j, i: (i, colfn(j))
        return lambda j, i: (jnp.clip(i + shift, 0, ni - 1), colfn(j))

    in_specs = [pl.BlockSpec((tr, w), rmap(cf, sh)) for (_, w, cf, sh) in rows]
    in_specs += [pl.BlockSpec(b.shape, lambda j, i, nd=b.ndim: (0,) * nd) for b in bcast]
    out_specs = [pl.BlockSpec((tr, w), rmap(cf, 0)) for (_, w, cf, _) in outs]
    out_specs += [pl.BlockSpec((r, w), lambda j, i, cf=cf: (0, cf(j))) for (r, _, w, cf) in accs]
    out_shape = [jax.ShapeDtypeStruct((nrows, nc), dt) for (nc, _, _, dt) in outs]
    out_shape += [jax.ShapeDtypeStruct((r, nc), F32) for (r, nc, _, _) in accs]

    def body(*refs):
        j = pl.program_id(0)
        i = pl.program_id(1)
        rv = [r[...] for r in refs[:nr]]
        bv = [r[...] for r in refs[nr:nr + nb]]
        o_refs = refs[nr + nb:nr + nb + no]
        a_refs = refs[nr + nb + no:]
        o_vals, a_vals = fn(i, j, rv, bv)
        for ref, val in zip(o_refs, o_vals):
            ref[...] = val.astype(ref.dtype)
        if a_refs:
            @pl.when(i == 0)
            def _():
                for ref in a_refs:
                    ref[...] = jnp.zeros(ref.shape, F32)
            for ref, val in zip(a_refs, a_vals):
                ref[...] += val

    res, carried = _call_with_cargo(body, cargo, name=name, grid=(nj, ni), in_specs=in_specs, out_specs=out_specs,
                                    out_shape=out_shape, scratch_shapes=[], operands=(*[r[0] for r in rows], *bcast))
    return res if cargo is None else (res, carried)


def _c(k):
    return lambda j: k


def _mm(name, a, b, dims, *, a_blk, a_map, b_blk, b_map, o_shape, o_blk, o_map, grid, o_dtype,
        bias=None, bias_blk=None, bias_map=None, res=None):
    nk = grid[3]
    extra, extra_specs = [], []
    if bias is not None:
        extra.append(bias)
        extra_specs.append(pl.BlockSpec(bias_blk, bias_map))
    if res is not None:
        extra.append(res)
        extra_specs.append(pl.BlockSpec(o_blk, o_map))
    nbias = bias is not None
    nres = res is not None

    def body(*refs):
        a_ref, b_ref = refs[0], refs[1]
        idx = 2
        bias_ref = refs[idx] if nbias else None
        idx += nbias
        res_ref = refs[idx] if nres else None
        idx += nres
        o_ref = refs[idx]
        acc_ref = refs[idx + 1]
        k = pl.program_id(3)
        part = _dot(a_ref[...], b_ref[...], dims)

        def finish(r):
            if nbias:
                r = r + bias_ref[...]
            if nres:
                r = r + res_ref[...]
            o_ref[...] = r.astype(o_ref.dtype)

        if nk == 1:
            finish(part)
        else:
            @pl.when(k == 0)
            def _():
                acc_ref[...] = part

            @pl.when(k > 0)
            def _():
                acc_ref[...] += part

            @pl.when(k == nk - 1)
            def _():
                finish(acc_ref[...])

    acc_shape = o_blk if nk > 1 else (8, 128)
    return pl.pallas_call(
        body, name=name, grid=grid,
        in_specs=[pl.BlockSpec(a_blk, a_map), pl.BlockSpec(b_blk, b_map)] + extra_specs,
        out_specs=pl.BlockSpec(o_blk, o_map),
        out_shape=jax.ShapeDtypeStruct(o_shape, o_dtype),
        scratch_shapes=[pltpu.VMEM(acc_shape, F32)],
        compiler_params=_cparams(("arbitrary", "arbitrary", "arbitrary", "arbitrary")),
    )(a, b, *extra)


def _mm_nn(name, a, b, o_dtype, *, bias=None, res=None, groups=1, tm=1024, tn=1024):
    M = a.shape[0]
    K = a.shape[1] // groups
    N = b.shape[1]
    tm, tn = min(tm, M), min(tn, N)
    njn = N // tn
    return _mm(name, a, b, NN,
               a_blk=(tm, K), a_map=lambda g, j, i, k: (i, g),
               b_blk=(K, tn), b_map=lambda g, j, i, k: (g, j),
               o_shape=(M, groups * N), o_blk=(tm, tn), o_map=lambda g, j, i, k: (i, g * njn + j),
               grid=(groups, njn, M // tm, 1), o_dtype=o_dtype,
               bias=bias, bias_blk=(1, tn), bias_map=lambda g, j, i, k: (0, g * njn + j), res=res)


def _mm_nt(name, a, b, o_dtype, *, res=None, groups=1, tm=512, tk=2048):
    M = a.shape[0]
    K = a.shape[1] // groups
    N = b.shape[0] // groups
    tm, tk = min(tm, M), min(tk, K)
    nk = K // tk
    return _mm(name, a, b, NT,
               a_blk=(tm, tk), a_map=lambda g, j, i, k: (i, g * nk + k),
               b_blk=(N, tk), b_map=lambda g, j, i, k: (g, k),
               o_shape=(M, groups * N), o_blk=(tm, N), o_map=lambda g, j, i, k: (i, g),
               grid=(groups, 1, M // tm, nk), o_dtype=o_dtype, res=res)


def _mm_tn(name, a, b, *, groups=1, tm=1024, tn=1024, tk=2048):
    K = a.shape[0]
    M = a.shape[1] // groups
    N = b.shape[1] // groups
    tm, tn, tk = min(tm, M), min(tn, N), min(tk, K)
    nim, njn = M // tm, N // tn
    return _mm(name, a, b, TN,
               a_blk=(tk, tm), a_map=lambda g, j, i, k: (k, g * nim + i),
               b_blk=(tk, tn), b_map=lambda g, j, i, k: (k, g * njn + j),
               o_shape=(groups * M, N), o_blk=(tm, tn), o_map=lambda g, j, i, k: (g * nim + i, j),
               grid=(groups, njn, nim, K // tk), o_dtype=F32)


def _rms_fwd(x, g):
    S, D = x.shape

    def fn(i, j, rv, bv):
        xv, gv = rv[0], bv[0]
        r = lax.rsqrt(jnp.mean(xv * xv, axis=-1, keepdims=True) + NORM_EPS)
        return [xv * r * gv], []

    return _rowwise("rms_fwd", fn, [(x, D, _c(0), 0)], [g.reshape(1, D)], [(D, D, _c(0), BF16)], [],
                    nrows=S, tr=512)[0]


def _rms_bwd_math(xv, dyv, gv):
    r = lax.rsqrt(jnp.mean(xv * xv, axis=-1, keepdims=True) + NORM_EPS)
    xh = xv * r
    dg = jnp.sum(dyv * xh, axis=0, keepdims=True)
    dxh = dyv * gv
    dx = r * (dxh - xh * jnp.mean(dxh * xh, axis=-1, keepdims=True))
    return dx, dg


def _rms_bwd(x, dh, dout, g):
    S, D = x.shape

    def fn(i, j, rv, bv):
        dx, dg = _rms_bwd_math(rv[0], rv[1], bv[0])
        return [rv[2] + dx], [dg]

    return _rowwise("rms_bwd", fn, [(x, D, _c(0), 0), (dh, D, _c(0), 0), (dout, D, _c(0), 0)], [g.reshape(1, D)],
                    [(D, D, _c(0), F32)], [(1, D, D, _c(0))], nrows=S, tr=256)


def _final_loss(x, target, g):
    S, D = x.shape

    def fn(i, j, rv, bv):
        xv, tv, gv = rv[0], rv[1], bv[0]
        r = lax.rsqrt(jnp.mean(xv * xv, axis=-1, keepdims=True) + NORM_EPS)
        err = xv * r * gv - tv
        loss = 0.5 * jnp.sum(jnp.mean(err * err, axis=-1, keepdims=True), axis=0, keepdims=True)
        dx, dg = _rms_bwd_math(xv, err * (1.0 / D), gv)
        return [dx], [dg, jnp.broadcast_to(loss, (1, 128))]

    return _rowwise("final_loss", fn, [(x, D, _c(0), 0), (target, D, _c(0), 0)], [g.reshape(1, D)],
                    [(D, D, _c(0), F32)], [(1, D, D, _c(0)), (1, 128, 128, _c(0))], nrows=S, tr=256)


MERGE_W = 512


def _merge_fwd(gpre, yproj):
    S = gpre.shape[0]
    nj = D_MODEL // MERGE_W

    def fn(i, j, rv, bv):
        acc = _sigmoid(rv[0]) * rv[4]
        for n in range(1, 4):
            acc = acc + _sigmoid(rv[n]) * rv[4 + n]
        return [acc], []

    rows = [(gpre, MERGE_W, (lambda j, n=n: n * nj + j), 0) for n in range(4)]
    rows += [(yproj, MERGE_W, (lambda j, n=n: n * nj + j), 0) for n in range(4)]
    return _rowwise("merge_fwd", fn, rows, [], [(D_MODEL, MERGE_W, lambda j: j, BF16)], [],
                    nrows=S, tr=512, nj=nj)[0]


def _merge_bwd(dmerged, gpre, yproj, cargo=None):
    S = gpre.shape[0]
    nj4 = D_MODEL // MERGE_W

    def fn(i, j, rv, bv):
        dm, gp, yp = rv
        sg = _sigmoid(gp)
        dgp = dm * yp * sg * (1.0 - sg)
        return [dm * sg, dgp], [jnp.sum(dgp, axis=0, keepdims=True)]

    rows = [(dmerged, MERGE_W, lambda j: j % nj4, 0), (gpre, MERGE_W, lambda j: j, 0), (yproj, MERGE_W, lambda j: j, 0)]
    return _rowwise("merge_bwd", fn, rows, [],
                    [(GATE_W, MERGE_W, lambda j: j, BF16), (GATE_W, MERGE_W, lambda j: j, BF16)],
                    [(1, GATE_W, MERGE_W, lambda j: j)], nrows=S, tr=512, nj=4 * nj4, cargo=cargo)


SB_TQ = 1024
SB_TQ_BWD = 512
SB_TK = 512
SB_TRI = 256


def _softplus(z):
    bits, sign = (jnp.uint32, 0x80000000) if z.dtype == F32 else (jnp.uint16, 0x8000)
    neg_abs = lax.bitcast_convert_type(lax.bitcast_convert_type(z, bits) | bits(sign), z.dtype)
    return jnp.maximum(z, 0.0) + jnp.log(1.0 + jnp.exp(neg_abs))


def _sb_sum_matrices():
    r = lax.broadcasted_iota(jnp.int32, (SB_TRI, SB_TRI), 0)
    c = lax.broadcasted_iota(jnp.int32, (SB_TRI, SB_TRI), 1)
    return (r > c).astype(BF16), (r < c).astype(BF16)


ANY =pl.BlockSpec(memory_space=pl.ANY)


class _Cargo:
    def __init__(self, arrays, out_shapes, aliases, nsem, start, finish):
        self.arrays, self.out_shapes, self.aliases, self.nsem = list(arrays), list(out_shapes), dict(aliases), nsem
        self.start, self.finish = start, finish


def _call_with_cargo(body, cargo, *, name, grid, in_specs, out_specs, out_shape, scratch_shapes, operands):
    sem = ("arbitrary",) * len(grid)
    if cargo is None:
        res = pl.pallas_call(body, name=name, grid=grid, in_specs=in_specs, out_specs=out_specs, out_shape=out_shape,
                             scratch_shapes=scratch_shapes, compiler_params=_cparams(sem))(*operands)
        return res, []
    n_in, n_out, n_scr = len(in_specs), len(out_specs), len(scratch_shapes)
    nci, nco = len(cargo.arrays), len(cargo.out_shapes)

    def wrapped(*refs):
        ins, refs = refs[:n_in], refs[n_in:]
        cin, refs = refs[:nci], refs[nci:]
        outs, refs = refs[:n_out], refs[n_out:]
        cout, refs = refs[:nco], refs[nco:]
        scr, (send, recv) = refs[:n_scr], refs[n_scr:]
        ids = [pl.program_id(a) for a in range(len(grid))]
        first = functools.reduce(lambda p, q: p & q, [g == 0 for g in ids])
        last = functools.reduce(lambda p, q: p & q, [g == n - 1 for g, n in zip(ids, grid)])

        @pl.when(first)
        def _():
            cargo.start(cin, cout, send, recv)

        body(*ins, *outs, *scr)

        @pl.when(last)
        def _():
            cargo.finish(cin, cout, send, recv)

    res = pl.pallas_call(
        wrapped, name=name, grid=grid, in_specs=list(in_specs) + [ANY] * nci, out_specs=list(out_specs) + [ANY] * nco,
        out_shape=list(out_shape) + cargo.out_shapes,
        scratch_shapes=list(scratch_shapes) + [pltpu.SemaphoreType.DMA((cargo.nsem,)), pltpu.SemaphoreType.DMA((cargo.nsem,))],
        input_output_aliases={n_in + k: n_out + v for k, v in cargo.aliases.items()},
        compiler_params=_cparams(sem))(*operands, *cargo.arrays)
    return res[:n_out], res[n_out:]


def _run_cargo(name, cargo):
    nci = len(cargo.arrays)

    def body(*refs):
        cin, cout, (send, recv) = refs[:nci], refs[nci:-2], refs[-2:]
        cargo.start(cin, cout, send, recv)
        cargo.finish(cin, cout, send, recv)

    return pl.pallas_call(
        body, name=name, in_specs=[ANY] * nci, out_specs=[ANY] * len(cargo.out_shapes), out_shape=cargo.out_shapes,
        scratch_shapes=[pltpu.SemaphoreType.DMA((cargo.nsem,)), pltpu.SemaphoreType.DMA((cargo.nsem,))],
        input_output_aliases=cargo.aliases)(*cargo.arrays)


def _sb_fwd(proj, cargo=None):
    S = proj.shape[0]
    tq, tk = min(SB_TQ, S), SB_TK
    assert S // SB_TRI <= HEAD
    nq = S // tq
    nkb = tq // tk
    nsub = tk // SB_TRI
    scale = HEAD ** -0.5

    def body(q_ref, k_ref, v_ref, g_ref, o_ref, y_ref, r_ref, kb_ref, vb_ref):
        i = pl.program_id(1)

        @pl.when(i == 0)
        def _():
            kb_ref[...] = k_ref[...].astype(BF16)
            vb_ref[...] = v_ref[...].astype(BF16)

        qb = (q_ref[...] * scale).astype(BF16)
        qpos = i * tq + lax.broadcasted_iota(jnp.int32, (tq, tk), 0)
        kidx = lax.broadcasted_iota(jnp.int32, (tq, tk), 1)
        lane = lax.broadcasted_iota(jnp.int32, (tq, HEAD), 1)
        upper, _ = _sb_sum_matrices()

        def block(kblock, masked, carry):
            acc, run, runs = carry
            off = pl.multiple_of(kblock * tk, tk)
            zz = _dot(qb, kb_ref[pl.ds(off, tk), :], NT).astype(BF16)
            sp = _softplus(zz)
            if masked:
                keep = (off + kidx) < qpos
                sp_sum = jnp.where(keep, sp, 0.0)
            else:
                sp_sum = sp
            right, after = [None] * nsub, [None] * nsub
            for j in reversed(range(nsub)):
                cols = slice(j * SB_TRI, (j + 1) * SB_TRI)
                right[j] = run
                suffix = _dot(sp_sum[:, cols], upper, NN)
                after[j] = (suffix + run).astype(BF16)
                run = run + suffix[:, :1] + sp_sum[:, j * SB_TRI:j * SB_TRI + 1].astype(F32)
                runs = jnp.where(lane == kblock * nsub + j, right[j], runs)
            w = jnp.exp((zz - sp) - jnp.concatenate(after, axis=1))
            if masked:
                w = jnp.where(keep, w, 0.0)
            return acc + _dot(w, vb_ref[pl.ds(off, tk), :], NN), run, runs

        runs0 = jnp.where(qpos[:, :HEAD] + lane < 0, 1.0, 0.0)
        carry = (jnp.zeros((tq, HEAD), F32), jnp.zeros((tq, 1), F32), runs0)
        for kbl in reversed(range(nkb)):
            carry = block(i * nkb + kbl, True, carry)
        acc, _, runs = lax.fori_loop(0, i * nkb, lambda n, c: block(i * nkb - 1 - n, False, c), carry)
        o_ref[...] = acc
        r_ref[...] = runs
        gate, _ = _silu_and_grad(g_ref[...])
        y_ref[...] = (acc * gate).astype(BF16)

    return _call_with_cargo(
        body, cargo, name="sb_fwd", grid=(NHEAD, nq),
        in_specs=[pl.BlockSpec((tq, HEAD), lambda h, i: (i, C_AQ * NHEAD + h)),
                  pl.BlockSpec((S, HEAD), lambda h, i: (0, C_AK * NHEAD + h)),
                  pl.BlockSpec((S, HEAD), lambda h, i: (0, C_AV * NHEAD + h)),
                  pl.BlockSpec((tq, HEAD), lambda h, i: (i, C_AG * NHEAD + h))],
        out_specs=[pl.BlockSpec((tq, HEAD), lambda h, i: (i, h))] * 3,
        out_shape=[jax.ShapeDtypeStruct((S, BW), F32), jax.ShapeDtypeStruct((S, BW), BF16),
                   jax.ShapeDtypeStruct((S, BW), F32)],
        scratch_shapes=[pltpu.VMEM((S, HEAD), BF16), pltpu.VMEM((S, HEAD), BF16)],
        operands=(proj, proj, proj, proj))


def _sb_bwd(proj, att, runs, dy, cargo=None):
    S = proj.shape[0]
    tq, tk = min(SB_TQ_BWD, S), SB_TK
    nq = S // tq
    nkb = tq // tk
    nsub = tk // SB_TRI
    scale = HEAD ** -0.5

    def body(q_ref, k_ref, v_ref, g_ref, o_ref, r_ref, dy_ref, dq_ref, dk_ref, dv_ref, dg_ref, kb_ref, vb_ref):
        i = pl.program_id(1)

        @pl.when(i == 0)
        def _():
            kb_ref[...] = k_ref[...].astype(BF16)
            vb_ref[...] = v_ref[...].astype(BF16)
            dk_ref[...] = jnp.zeros(dk_ref.shape, F32)
            dv_ref[...] = jnp.zeros(dv_ref.shape, F32)

        gate, dgate = _silu_and_grad(g_ref[...])
        dyv = dy_ref[...]
        dg_ref[...] = dyv * o_ref[...] * dgate
        dob = (dyv * gate).astype(BF16)
        qb = (q_ref[...] * scale).astype(BF16)
        runs = r_ref[...]
        qpos = i * tq + lax.broadcasted_iota(jnp.int32, (tq, tk), 0)
        kidx = lax.broadcasted_iota(jnp.int32, (tq, tk), 1)
        lane = lax.broadcasted_iota(jnp.int32, (tq, HEAD), 1)
        upper, lower = _sb_sum_matrices()

        def block(kblock, masked, carry):
            dq, grun = carry
            off = pl.multiple_of(kblock * tk, tk)
            kblk = kb_ref[pl.ds(off, tk), :]
            zz = _dot(qb, kblk, NT).astype(BF16)
            sp = _softplus(zz)
            lb = zz - sp
            if masked:
                keep = (off + kidx) < qpos
                sp_sum = jnp.where(keep, sp, 0.0)
            else:
                sp_sum = sp
            after = []
            for j in range(nsub):
                cols = slice(j * SB_TRI, (j + 1) * SB_TRI)
                run = jnp.sum(jnp.where(lane == kblock * nsub + j, runs, 0.0), axis=1, keepdims=True)
                after.append((_dot(sp_sum[:, cols], upper, NN) + run).astype(BF16))
            w = jnp.exp(lb - jnp.concatenate(after, axis=1))
            if masked:
                w = jnp.where(keep, w, 0.0)
            gw = _dot(dob, vb_ref[pl.ds(off, tk), :], NT).astype(BF16) * w
            gsum = []
            for j in range(nsub):
                cols = slice(j * SB_TRI, (j + 1) * SB_TRI)
                prefix = _dot(gw[:, cols], lower, NN)
                gsum.append((prefix + grun).astype(BF16))
                last = (j + 1) * SB_TRI - 1
                grun = grun + prefix[:, SB_TRI - 1:] + gw[:, last:last + 1].astype(F32)
            dz = gw - (gw + jnp.concatenate(gsum, axis=1)) * jnp.exp(lb)
            if masked:
                dz = jnp.where(keep, dz, 0.0)
            dk_ref[pl.ds(off, tk), :] += _dot(dz, qb, TN)
            dv_ref[pl.ds(off, tk), :] += _dot(w, dob, TN)
            return dq + _dot(dz, kblk, NN), grun

        carry = lax.fori_loop(0, i * nkb, lambda n, c: block(n, False, c),
                              (jnp.zeros((tq, HEAD), F32), jnp.zeros((tq, 1), F32)))
        for kbl in range(nkb):
            carry = block(i * nkb + kbl, True, carry)
        dq_ref[...] = carry[0] * scale

    blk = lambda h, i: (i, h)
    head = lambda h, i: (0, h)
    return _call_with_cargo(
        body, cargo, name="sb_bwd", grid=(NHEAD, nq),
        in_specs=[pl.BlockSpec((tq, HEAD), lambda h, i: (i, C_AQ * NHEAD + h)),
                  pl.BlockSpec((S, HEAD), lambda h, i: (0, C_AK * NHEAD + h)),
                  pl.BlockSpec((S, HEAD), lambda h, i: (0, C_AV * NHEAD + h)),
                  pl.BlockSpec((tq, HEAD), lambda h, i: (i, C_AG * NHEAD + h)),
                  pl.BlockSpec((tq, HEAD), blk), pl.BlockSpec((tq, HEAD), blk), pl.BlockSpec((tq, HEAD), blk)],
        out_specs=[pl.BlockSpec((tq, HEAD), blk), pl.BlockSpec((S, HEAD), head), pl.BlockSpec((S, HEAD), head),
                   pl.BlockSpec((tq, HEAD), blk)],
        out_shape=[jax.ShapeDtypeStruct((S, BW), F32)] * 4,
        scratch_shapes=[pltpu.VMEM((S, HEAD), BF16), pltpu.VMEM((S, HEAD), BF16)],
        operands=(proj, proj, proj, proj, att, runs, dy))


CONV_TR = 256


CONV_ROWS = 64


SUBLANES = 8


def _fill_shifted(shift_ref, buf_ref):
    n = buf_ref.shape[0] - SUBLANES
    buf_ref[n:, :] = jnp.zeros((SUBLANES, buf_ref.shape[1]), F32)
    for b in range(SUBLANES):
        shift_ref[b, :, :] = buf_ref[pl.ds(b, n), :]


def _shifted(shift_ref, row, nrows, cols):
    return shift_ref[row % SUBLANES, pl.ds(row - row % SUBLANES, nrows), cols]


def _conv_taps(shift_ref, w_ref, base, tr, emit):
    for c0 in range(0, BW, HEAD):
        cols = slice(c0, c0 + HEAD)
        taps = [w_ref[k:k + 1, cols] for k in range(CONV_K)]
        for r0 in range(0, tr, CONV_ROWS):
            acc = taps[0] * _shifted(shift_ref, base + r0, CONV_ROWS, cols)
            for k in range(1, CONV_K):
                acc = acc + taps[k] * _shifted(shift_ref, base + r0 + k, CONV_ROWS, cols)
            emit(slice(r0, r0 + CONV_ROWS), cols, acc)


def _conv_fwd(proj, conv_w, conv_b, ln_g, ln_b):
    S = proj.shape[0]
    tr = min(CONV_TR, S)
    ni = S // tr

    def body(a_ref, b_ref, ap_ref, bp_ref, g_ref, w_ref, cb_ref, lg_ref, lb_ref, c_ref, y_ref, buf_ref, shift_ref):
        i = pl.program_id(0)
        prev = ap_ref[tr - HALO:, :] * _sigmoid(bp_ref[tr - HALO:, :])
        buf_ref[0:HALO, :] = jnp.where(i > 0, prev, 0.0)
        buf_ref[HALO:HALO + tr, :] = a_ref[...] * _sigmoid(b_ref[...])
        _fill_shifted(shift_ref, buf_ref)

        def emit(rows, cols, acc):
            c_ref[rows, cols] = acc + cb_ref[:, cols]

        _conv_taps(shift_ref, w_ref, HALO - (CONV_K - 1), tr, emit)
        yn, _, _ = _ln_fwd(c_ref[...], lg_ref[...], lb_ref[...])
        act, _ = _silu_and_grad(yn)
        gate, _ = _silu_and_grad(g_ref[...])
        y_ref[...] = (act * gate).astype(BF16)

    cur = lambda k: (lambda i: (i, k))
    prv = lambda k: (lambda i: (jnp.maximum(i - 1, 0), k))
    full = lambda a: pl.BlockSpec(a.shape, lambda i: (0, 0))
    wpad = jnp.pad(conv_w, ((0, HALO - CONV_K), (0, 0)))
    small = [wpad, conv_b.reshape(1, BW), ln_g.reshape(1, BW), ln_b.reshape(1, BW)]
    return pl.pallas_call(
        body, name="conv_fwd", grid=(ni,),
        in_specs=[pl.BlockSpec((tr, BW), cur(C_BA)), pl.BlockSpec((tr, BW), cur(C_BB)),
                  pl.BlockSpec((tr, BW), prv(C_BA)), pl.BlockSpec((tr, BW), prv(C_BB)),
                  pl.BlockSpec((tr, BW), cur(C_BG))] + [full(a) for a in small],
        out_specs=[pl.BlockSpec((tr, BW), lambda i: (i, 0)), pl.BlockSpec((tr, BW), lambda i: (i, 0))],
        out_shape=[jax.ShapeDtypeStruct((S, BW), F32), jax.ShapeDtypeStruct((S, BW), BF16)],
        scratch_shapes=[pltpu.VMEM((tr + HALO + SUBLANES, BW), F32), pltpu.VMEM((SUBLANES, tr + HALO, BW), F32)],
        compiler_params=_cparams(("arbitrary",)),
    )(proj, proj, proj, proj, proj, *small)


def _conv_bwd_norm(proj, cpre, dy, ln_g, ln_b):
    S = proj.shape[0]

    def fn(i, j, rv, bv):
        c, bg, dyv = rv
        lg, lb = bv
        yn, xh, rstd = _ln_fwd(c, lg, lb)
        act, dact = _silu_and_grad(yn)
        gate, dgate = _silu_and_grad(bg)
        dyn = dyv * gate * dact
        dc, dlg, dlb = _ln_bwd(dyn, xh, rstd, lg)
        return [dc, dyv * act * dgate], [dlg, dlb, jnp.sum(dc, axis=0, keepdims=True)]

    rows = [(cpre, BW, _c(0), 0), (proj, BW, _c(C_BG), 0), (dy, BW, _c(1), 0)]
    return _rowwise("conv_bwd_norm", fn, rows, [ln_g.reshape(1, BW), ln_b.reshape(1, BW)],
                    [(BW, BW, _c(0), F32), (BW, BW, _c(0), BF16)],
                    [(1, BW, BW, _c(0))] * 3, nrows=S, tr=256)


def _conv_bwd_taps(proj, dc, conv_w):
    S = proj.shape[0]
    tr = min(CONV_TR, S)
    ni = S // tr

    def body(a_ref, b_ref, ap_ref, bp_ref, dc_ref, dcn_ref, w_ref, da_ref, db_ref, dw_ref, gbuf, dbuf, sbuf, wrev,
             shift_ref):
        i = pl.program_id(0)

        @pl.when(i == 0)
        def _():
            dw_ref[...] = jnp.zeros(dw_ref.shape, F32)
            for k in range(CONV_K):
                wrev[k:k + 1, :] = w_ref[CONV_K - 1 - k:CONV_K - k, :]

        sb = _sigmoid(b_ref[...])
        sbuf[...] = sb
        prev = ap_ref[tr - HALO:, :] * _sigmoid(bp_ref[tr - HALO:, :])
        gbuf[0:HALO, :] = jnp.where(i > 0, prev, 0.0)
        gbuf[HALO:HALO + tr, :] = a_ref[...] * sb
        dbuf[0:tr, :] = dc_ref[...]
        dbuf[tr:tr + HALO, :] = jnp.where(i < ni - 1, dcn_ref[0:HALO, :], 0.0)

        def emit(rows, cols, dg):
            s = sbuf[rows, cols]
            da_ref[rows, cols] = (dg * s).astype(BF16)
            db_ref[rows, cols] = (dg * a_ref[rows, cols] * s * (1.0 - s)).astype(BF16)

        _fill_shifted(shift_ref, dbuf)
        _conv_taps(shift_ref, wrev, 0, tr, emit)
        _fill_shifted(shift_ref, gbuf)
        base = HALO - (CONV_K - 1)
        for c0 in range(0, BW, HEAD):
            cols = slice(c0, c0 + HEAD)
            dcs = [dc_ref[r0:r0 + CONV_ROWS, cols] for r0 in range(0, tr, CONV_ROWS)]
            for k in range(CONV_K):
                prod = dcs[0] * _shifted(shift_ref, base + k, CONV_ROWS, cols)
                for n in range(1, len(dcs)):
                    prod = prod + dcs[n] * _shifted(shift_ref, base + n * CONV_ROWS + k, CONV_ROWS, cols)
                dw_ref[k:k + 1, cols] += jnp.sum(prod, axis=0, keepdims=True)

    cur = lambda k: (lambda i: (i, k))
    prv = lambda k: (lambda i: (jnp.maximum(i - 1, 0), k))
    wpad = jnp.pad(conv_w, ((0, HALO - CONV_K), (0, 0)))
    return pl.pallas_call(
        body, name="conv_bwd_taps", grid=(ni,),
        in_specs=[pl.BlockSpec((tr, BW), cur(C_BA)), pl.BlockSpec((tr, BW), cur(C_BB)),
                  pl.BlockSpec((tr, BW), prv(C_BA)), pl.BlockSpec((tr, BW), prv(C_BB)),
                  pl.BlockSpec((tr, BW), lambda i: (i, 0)),
                  pl.BlockSpec((tr, BW), lambda i: (jnp.minimum(i + 1, ni - 1), 0)),
                  pl.BlockSpec((HALO, BW), lambda i: (0, 0))],
        out_specs=[pl.BlockSpec((tr, BW), lambda i: (i, 0)), pl.BlockSpec((tr, BW), lambda i: (i, 0)),
                   pl.BlockSpec((HALO, BW), lambda i: (0, 0))],
        out_shape=[jax.ShapeDtypeStruct((S, BW), BF16), jax.ShapeDtypeStruct((S, BW), BF16),
                   jax.ShapeDtypeStruct((HALO, BW), F32)],
        scratch_shapes=[pltpu.VMEM((tr + HALO + SUBLANES, BW), F32), pltpu.VMEM((tr + HALO + SUBLANES, BW), F32),
                        pltpu.VMEM((tr, BW), F32), pltpu.VMEM((HALO, BW), F32),
                        pltpu.VMEM((SUBLANES, tr + HALO, BW), F32)],
        compiler_params=_cparams(("arbitrary",)),
    )(proj, proj, proj, proj, dc, dc, wpad)


SGU_TR = 256


def _sgu_mix(wm, vn, bias):
    tr = vn.shape[0]
    rows = []
    for n in range(tr // HEAD):
        cols = []
        for g in range(NHEAD):
            blk = vn[n * HEAD:(n + 1) * HEAD, g * HEAD:(g + 1) * HEAD]
            cols.append(_dot(wm[g], blk, NN) + bias[g * HEAD:(g + 1) * HEAD, :])
        rows.append(jnp.concatenate(cols, axis=1))
    return jnp.concatenate(rows, axis=0)


def _sgu_masked_w(w2d):
    rr = lax.broadcasted_iota(jnp.int32, (HEAD, HEAD), 0)
    cc = lax.broadcasted_iota(jnp.int32, (HEAD, HEAD), 1)
    tril = rr >= cc
    return [jnp.where(tril, w2d[g * HEAD:(g + 1) * HEAD, :], 0.0).astype(BF16) for g in range(NHEAD)], tril


def _sgu_inputs(sgu_w, sgu_b, ln_g, ln_b):
    w2d = sgu_w.reshape(NHEAD * HEAD, HEAD)
    bias = jnp.broadcast_to(sgu_b[:, :, None], (NHEAD, HEAD, HEAD)).reshape(NHEAD * HEAD, HEAD)
    return [w2d, bias, ln_g.reshape(1, BW), ln_b.reshape(1, BW)]


def _sgu_fwd(proj, sgu_w, sgu_b, ln_g, ln_b):
    S = proj.shape[0]

    def fn(i, j, rv, bv):
        cu, cv, cg = rv
        w2d, bias, lg, lb = bv
        wm, _ = _sgu_masked_w(w2d)
        u, _ = _gelu_and_grad(cu)
        v, _ = _gelu_and_grad(cv)
        vn, _, _ = _ln_fwd(v, lg, lb)
        z = _sgu_mix(wm, vn, bias)
        gate, _ = _silu_and_grad(cg)
        return [u * z * gate], []

    rows = [(proj, BW, _c(C_CU), 0), (proj, BW, _c(C_CV), 0), (proj, BW, _c(C_CG), 0)]
    return _rowwise("sgu_fwd", fn, rows, _sgu_inputs(sgu_w, sgu_b, ln_g, ln_b), [(BW, BW, _c(0), BF16)], [],
                    nrows=S, tr=min(SGU_TR, S))[0]


def _sgu_bwd(proj, dy, sgu_w, sgu_b, ln_g, ln_b):
    S = proj.shape[0]

    def fn(i, j, rv, bv):
        cu, cv, cg, dyv = rv
        w2d, bias, lg, lb = bv
        wm, tril = _sgu_masked_w(w2d)
        u, du_dcu = _gelu_and_grad(cu)
        v, dv_dcv = _gelu_and_grad(cv)
        vn, xh, rstd = _ln_fwd(v, lg, lb)
        z = _sgu_mix(wm, vn, bias)
        gate, dgate = _silu_and_grad(cg)
        dz = dyv * u * gate
        dcu = dyv * z * gate * du_dcu
        dcg = dyv * u * z * dgate
        tr = dz.shape[0]
        dvn_rows = []
        dw = [jnp.zeros((HEAD, HEAD), F32) for _ in range(NHEAD)]
        dbias = [jnp.zeros((HEAD, HEAD), F32) for _ in range(NHEAD)]
        for n in range(tr // HEAD):
            cols = []
            for g in range(NHEAD):
                dzb = dz[n * HEAD:(n + 1) * HEAD, g * HEAD:(g + 1) * HEAD]
                vnb = vn[n * HEAD:(n + 1) * HEAD, g * HEAD:(g + 1) * HEAD]
                cols.append(_dot(wm[g], dzb, TN))
                dw[g] = dw[g] + _dot(dzb, vnb, NT)
                dbias[g] = dbias[g] + dzb
            dvn_rows.append(jnp.concatenate(cols, axis=1))
        dvn = jnp.concatenate(dvn_rows, axis=0)
        dv, dlg, dlb = _ln_bwd(dvn, xh, rstd, lg)
        dw2d = jnp.concatenate([jnp.where(tril, d, 0.0) for d in dw], axis=0)
        return [dcu, dv * dv_dcv, dcg], [dw2d, jnp.concatenate(dbias, axis=0), dlg, dlb]

    rows = [(proj, BW, _c(C_CU), 0), (proj, BW, _c(C_CV), 0), (proj, BW, _c(C_CG), 0), (dy, BW, _c(2), 0)]
    return _rowwise("sgu_bwd", fn, rows, _sgu_inputs(sgu_w, sgu_b, ln_g, ln_b),
                    [(BW, BW, _c(0), BF16)] * 3,
                    [(NHEAD * HEAD, HEAD, HEAD, _c(0)), (NHEAD * HEAD, HEAD, HEAD, _c(0)), (1, BW, BW, _c(0)),
                     (1, BW, BW, _c(0))], nrows=S, tr=min(SGU_TR, S))


NEG = -1e30
NCOL = IN_WIDTH // BW


def _band_masks():
    a = lax.broadcasted_iota(jnp.int32, (HEAD, HEAD), 0)
    c = lax.broadcasted_iota(jnp.int32, (HEAD, HEAD), 1)
    return a >= c, a <= c


def _dil_view(proj, grp):
    S = proj.shape[0]
    dil = DIL_PATTERNS[grp][1]
    if dil == 1:
        return proj, NCOL, C_DQ + grp, C_DK + grp, C_DV
    cols = [proj[:, c * BW:(c + 1) * BW] for c in (C_DQ + grp, C_DK + grp, C_DV)]
    return jnp.concatenate(cols, axis=1).reshape(S // dil, dil * 3 * BW), 3, 0, 1, 2


def _dil_fwd(proj, grp):
    S = proj.shape[0]
    dil = DIL_PATTERNS[grp][1]
    L = S // dil
    nb = L // HEAD
    proj2, ncol, cq, ck, cv = _dil_view(proj, grp)
    scale = HEAD ** -0.5

    def body(q_ref, kc_ref, kp_ref, vc_ref, vp_ref, o_ref, l_ref):
        b = pl.program_id(1)
        m_cur, m_prev = _band_masks()
        m_prev = m_prev & (b > 0)
        hs = range(NHEAD)
        sl = [slice(h * HEAD, (h + 1) * HEAD) for h in hs]
        q = [q_ref[:, sl[h]].astype(BF16) for h in hs]
        s_c = [jnp.where(m_cur, _dot(q[h], kc_ref[:, sl[h]], NT) * scale, NEG) for h in hs]
        s_p = [jnp.where(m_prev, _dot(q[h], kp_ref[:, sl[h]], NT) * scale, NEG) for h in hs]
        m = [jnp.maximum(jnp.max(s_c[h], axis=1, keepdims=True), jnp.max(s_p[h], axis=1, keepdims=True)) for h in hs]
        p_c = [jnp.exp(s_c[h] - m[h]) for h in hs]
        p_p = [jnp.exp(s_p[h] - m[h]) for h in hs]
        den = [jnp.sum(p_c[h], axis=1, keepdims=True) + jnp.sum(p_p[h], axis=1, keepdims=True) for h in hs]
        o = [(_dot(p_c[h], vc_ref[:, sl[h]], NN) + _dot(p_p[h], vp_ref[:, sl[h]], NN)) / den[h] for h in hs]
        for h in hs:
            o_ref[:, sl[h]] = o[h]
            l_ref[:, sl[h]] = jnp.broadcast_to(m[h] + jnp.log(den[h]), (HEAD, HEAD))

    cur = lambda col: (lambda r, b: (b, r * ncol + col))
    prv = lambda col: (lambda r, b: (jnp.maximum(b - 1, 0), r * ncol + col))
    out_map = lambda r, b: (b, r)
    o, lse = pl.pallas_call(
        body, name=f"dil_fwd{grp}", grid=(dil, nb),
        in_specs=[pl.BlockSpec((HEAD, BW), cur(cq)),
                  pl.BlockSpec((HEAD, BW), cur(ck)), pl.BlockSpec((HEAD, BW), prv(ck)),
                  pl.BlockSpec((HEAD, BW), cur(cv)), pl.BlockSpec((HEAD, BW), prv(cv))],
        out_specs=[pl.BlockSpec((HEAD, BW), out_map), pl.BlockSpec((HEAD, BW), out_map)],
        out_shape=[jax.ShapeDtypeStruct((L, dil * BW), F32), jax.ShapeDtypeStruct((L, dil * BW), F32)],
        compiler_params=_cparams(("arbitrary", "arbitrary")),
    )(proj2, proj2, proj2, proj2, proj2)
    return o.reshape(S, BW), lse.reshape(S, BW)


def _dil_combine(proj, os_, lses):
    S = proj.shape[0]

    def fn(i, j, rv, bv):
        o1, o2, o3, l1, l2, l3, dg = rv
        m = jnp.maximum(jnp.maximum(l1, l2), l3)
        e1, e2, e3 = jnp.exp(l1 - m), jnp.exp(l2 - m), jnp.exp(l3 - m)
        den = e1 + e2 + e3
        out = (e1 * o1 + e2 * o2 + e3 * o3) / den
        gate, _ = _silu_and_grad(dg)
        return [out, m + jnp.log(den), out * gate], []

    rows = [(a, BW, _c(0), 0) for a in (*os_, *lses)] + [(proj, BW, _c(C_DG), 0)]
    return _rowwise("dil_combine", fn, rows, [], [(BW, BW, _c(0), F32), (BW, BW, _c(0), F32), (BW, BW, _c(0), BF16)],
                    [], nrows=S, tr=256)


def _dil_gate_bwd(proj, att, dy):
    S = proj.shape[0]

    def fn(i, j, rv, bv):
        dg, av, dyv = rv
        gate, dgate = _silu_and_grad(dg)
        dout = dyv * gate
        prod = dout * av
        tr = prod.shape[0]
        delta = jnp.concatenate(
            [jnp.broadcast_to(jnp.sum(prod[:, h * HEAD:(h + 1) * HEAD], axis=1, keepdims=True), (tr, HEAD))
             for h in range(NHEAD)], axis=1)
        return [dout, dyv * av * dgate, delta], []

    rows = [(proj, BW, _c(C_DG), 0), (att, BW, _c(0), 0), (dy, BW, _c(3), 0)]
    return _rowwise("dil_gate_bwd", fn, rows, [], [(BW, BW, _c(0), F32), (BW, BW, _c(0), BF16), (BW, BW, _c(0), F32)],
                    [], nrows=S, tr=256)


def _dil_bwd(proj, dout, lse, delta, grp):
    S = proj.shape[0]
    dil = DIL_PATTERNS[grp][1]
    L = S // dil
    nb = L // HEAD
    proj2, ncol, cq, ck, cv = _dil_view(proj, grp)
    view = lambda a: a.reshape(L, dil * BW)
    scale = HEAD ** -0.5

    def body(q_ref, qn_ref, kc_ref, kp_ref, vc_ref, vp_ref, do_ref, don_ref, l_ref, ln_ref, d_ref, dn_ref,
             dq_ref, dk_ref, dv_ref):
        b = pl.program_id(1)
        m_cur, m_band = _band_masks()
        m_prev = m_band & (b > 0)
        m_next = m_band & (b < nb - 1)
        hs = range(NHEAD)
        sl = [slice(h * HEAD, (h + 1) * HEAD) for h in hs]
        q, qn = [q_ref[:, s].astype(BF16) for s in sl], [qn_ref[:, s].astype(BF16) for s in sl]
        kc, kp = [kc_ref[:, s].astype(BF16) for s in sl], [kp_ref[:, s].astype(BF16) for s in sl]
        vc, vp = [vc_ref[:, s].astype(BF16) for s in sl], [vp_ref[:, s].astype(BF16) for s in sl]
        do, don = [do_ref[:, s].astype(BF16) for s in sl], [don_ref[:, s].astype(BF16) for s in sl]
        p_cc = [jnp.exp(jnp.where(m_cur, _dot(q[h], kc[h], NT) * scale - l_ref[:, sl[h]], NEG)) for h in hs]
        p_cp = [jnp.exp(jnp.where(m_prev, _dot(q[h], kp[h], NT) * scale - l_ref[:, sl[h]], NEG)) for h in hs]
        p_nc = [jnp.exp(jnp.where(m_next, _dot(qn[h], kc[h], NT) * scale - ln_ref[:, sl[h]], NEG)) for h in hs]
        ds_cc = [p_cc[h] * (_dot(do[h], vc[h], NT) - d_ref[:, sl[h]]) for h in hs]
        ds_cp = [p_cp[h] * (_dot(do[h], vp[h], NT) - d_ref[:, sl[h]]) for h in hs]
        ds_nc = [p_nc[h] * (_dot(don[h], vc[h], NT) - dn_ref[:, sl[h]]) for h in hs]
        dq = [((_dot(ds_cc[h], kc[h], NN) + _dot(ds_cp[h], kp[h], NN)) * scale).astype(BF16) for h in hs]
        dk = [((_dot(ds_cc[h], q[h], TN) + _dot(ds_nc[h], qn[h], TN)) * scale).astype(BF16) for h in hs]
        dv = [_dot(p_cc[h], do[h], TN) + _dot(p_nc[h], don[h], TN) for h in hs]
        for h in hs:
            dq_ref[:, sl[h]] = dq[h]
            dk_ref[:, sl[h]] = dk[h]
            dv_ref[:, sl[h]] = dv[h]

    cur = lambda col: (lambda r, b: (b, r * ncol + col))
    prv = lambda col: (lambda r, b: (jnp.maximum(b - 1, 0), r * ncol + col))
    nxt = lambda col: (lambda r, b: (jnp.minimum(b + 1, nb - 1), r * ncol + col))
    o_cur = lambda r, b: (b, r)
    o_nxt = lambda r, b: (jnp.minimum(b + 1, nb - 1), r)
    blk = (HEAD, BW)
    dq, dk, dv = pl.pallas_call(
        body, name=f"dil_bwd{grp}", grid=(dil, nb),
        in_specs=[pl.BlockSpec(blk, cur(cq)), pl.BlockSpec(blk, nxt(cq)),
                  pl.BlockSpec(blk, cur(ck)), pl.BlockSpec(blk, prv(ck)),
                  pl.BlockSpec(blk, cur(cv)), pl.BlockSpec(blk, prv(cv)),
                  pl.BlockSpec(blk, o_cur), pl.BlockSpec(blk, o_nxt),
                  pl.BlockSpec(blk, o_cur), pl.BlockSpec(blk, o_nxt),
                  pl.BlockSpec(blk, o_cur), pl.BlockSpec(blk, o_nxt)],
        out_specs=[pl.BlockSpec(blk, o_cur)] * 3,
        out_shape=[jax.ShapeDtypeStruct((L, dil * BW), BF16), jax.ShapeDtypeStruct((L, dil * BW), BF16),
                   jax.ShapeDtypeStruct((L, dil * BW), F32)],
        compiler_params=_cparams(("arbitrary", "arbitrary")),
    )(proj2, proj2, proj2, proj2, proj2, proj2, view(dout), view(dout), view(lse), view(lse), view(delta), view(delta))
    return dq.reshape(S, BW), dk.reshape(S, BW), dv.reshape(S, BW)


def _add3(a, b, c):
    S, W = a.shape

    def fn(i, j, rv, bv):
        return [rv[0] + rv[1] + rv[2]], []

    return _rowwise("add3", fn, [(a, W, _c(0), 0), (b, W, _c(0), 0), (c, W, _c(0), 0)], [], [(W, W, _c(0), BF16)], [],
                    nrows=S, tr=512)[0]


def _layer_fwd(x, p, cargo=None):
    h = _rms_fwd(x, p["norm_g"])
    proj = _mm_nn("in_proj", h, p["w_in"], F32)
    gpre = _mm_nn("gate_proj", h, p["w_gate"], F32, bias=p["b_gate"].reshape(1, GATE_W))
    (att_a, ya, runs_a), carried = _sb_fwd(proj, cargo)
    cpre, yb = _conv_fwd(proj, p["conv_w"], p["conv_b"], p["conv_ln_g"], p["conv_ln_b"])
    yc = _sgu_fwd(proj, p["sgu_w"], p["sgu_b"], p["sgu_ln_g"], p["sgu_ln_b"])
    dil = [_dil_fwd(proj, g) for g in range(3)]
    att_d, lse_d, yd = _dil_combine(proj, [d[0] for d in dil], [d[1] for d in dil])
    y = jnp.concatenate([ya, yb, yc, yd], axis=1)
    yproj = _mm_nn("branch_proj", y, p["w_branch"], F32, groups=4)
    merged = _merge_fwd(gpre, yproj)
    x_next = _mm_nn("out_proj", merged, p["w_out"], F32, res=x)
    saved = dict(x=x, h=h, proj=proj, gpre=gpre, att_a=att_a, runs_a=runs_a, cpre=cpre, att_d=att_d, lse_d=lse_d, y=y,
                 yproj=yproj, merged=merged)
    return x_next, saved, carried


def _layer_bwd(dout, s, p, pending=None):
    proj = s["proj"]
    dmerged = _mm_nt("out_proj_dx", dout, p["w_out"], F32, tm=1024)
    g_w_out = _mm_tn("out_proj_dw", s["merged"], dout)
    if pending is None:
        dyproj, dgpre, g_b_gate = _merge_bwd(dmerged, s["gpre"], s["yproj"])
        cargo = None
    else:
        grads_above, axes, scalars = pending
        (dyproj, dgpre, g_b_gate), got = _merge_bwd(dmerged, s["gpre"], s["yproj"], _swap_cargo(grads_above, axes))
        sums = _reduce_sums(grads_above, got, axes, scalars)
        cargo = _scatter_cargo(sums, axes)
    dy = _mm_nt("branch_proj_dx", dyproj, p["w_branch"], F32, groups=4, tm=1024)
    g_w_branch = _mm_tn("branch_proj_dw", s["y"], dyproj, groups=4)
    (d_aq, d_ak, d_av, d_ag), parts = _sb_bwd(proj, s["att_a"], s["runs_a"], dy, cargo)
    dc, d_bg, g_cln_g, g_cln_b, g_conv_b = _conv_bwd_norm(proj, s["cpre"], dy, p["conv_ln_g"], p["conv_ln_b"])
    d_ba, d_bb, g_conv_w = _conv_bwd_taps(proj, dc, p["conv_w"])
    d_cu, d_cv, d_cg, g_sgu_w, g_sgu_bias, g_sln_g, g_sln_b = _sgu_bwd(
        proj, dy, p["sgu_w"], p["sgu_b"], p["sgu_ln_g"], p["sgu_ln_b"])
    dout_d, d_dg, delta = _dil_gate_bwd(proj, s["att_d"], dy)
    dil = [_dil_bwd(proj, dout_d, s["lse_d"], delta, g) for g in range(3)]
    d_dv = _add3(dil[0][2], dil[1][2], dil[2][2])
    pieces = [d_aq, d_ak, d_av, d_ag, d_ba, d_bb, d_bg, d_cu, d_cv, d_cg,
              dil[0][0], dil[1][0], dil[2][0], dil[0][1], dil[1][1], dil[2][1], d_dv, d_dg]
    dproj = jnp.concatenate([t.astype(BF16) for t in pieces], axis=1)
    dh_gate = _mm_nt("gate_proj_dx", dgpre, p["w_gate"], F32)
    dh = _mm_nt("in_proj_dx", dproj, p["w_in"], F32, res=dh_gate)
    g_w_in = _mm_tn("in_proj_dw", s["h"], dproj)
    g_w_gate = _mm_tn("gate_proj_dw", s["h"], dgpre)
    dx, g_norm_g = _rms_bwd(s["x"], dh, dout, p["norm_g"])
    grads = dict(
        norm_g=g_norm_g.reshape(D_MODEL), w_in=g_w_in, conv_w=g_conv_w[:CONV_K], conv_b=g_conv_b.reshape(BW),
        conv_ln_g=g_cln_g.reshape(BW), conv_ln_b=g_cln_b.reshape(BW), sgu_ln_g=g_sln_g.reshape(BW),
        sgu_ln_b=g_sln_b.reshape(BW), sgu_w=g_sgu_w.reshape(NHEAD, HEAD, HEAD),
        sgu_b=jnp.sum(g_sgu_bias.reshape(NHEAD, HEAD, HEAD), axis=-1), w_branch=g_w_branch,
        w_gate=g_w_gate, b_gate=g_b_gate.reshape(GATE_W), w_out=g_w_out)
    return dx, grads, (None if pending is None else (sums, parts))


CHIP_FLIPS =((1, 0), (0, 1), (1, 1))


def _at(ref, axis, start, size):
    idx = [slice(None)] * len(ref.shape)
    idx[axis] = pl.ds(start, size)
    return ref.at[tuple(idx)]


def _position():
    return lax.axis_index("x"), lax.axis_index("y"), lax.axis_index("c")


def _gather(name, w, axis):
    n = w.shape[axis]
    full = list(w.shape)
    full[axis] = 4 * n

    def body(w_ref, out_ref, send_sems, recv_sems, local_sem):
        x, y, c = _position()
        mine = pltpu.make_async_copy(w_ref, _at(out_ref, axis, (2 * x + y) * n, n), local_sem)
        mine.start()
        sends = []
        for k, (fx, fy) in enumerate(CHIP_FLIPS):
            cp = pltpu.make_async_remote_copy(
                src_ref=w_ref, dst_ref=_at(out_ref, axis, (2 * x + y) * n, n), send_sem=send_sems.at[k],
                recv_sem=recv_sems.at[k], device_id=(x ^ fx, y ^ fy, c), device_id_type=MESH)
            cp.start()
            sends.append(cp)
        for k, (fx, fy) in enumerate(CHIP_FLIPS):
            px, py = x ^ fx, y ^ fy
            pltpu.make_async_remote_copy(
                src_ref=w_ref, dst_ref=_at(out_ref, axis, (2 * px + py) * n, n), send_sem=send_sems.at[k],
                recv_sem=recv_sems.at[k], device_id=(px, py, c), device_id_type=MESH).wait_recv()
        for cp in sends:
            cp.wait_send()
        mine.wait()

    return pl.pallas_call(
        body, name=name, in_specs=[ANY], out_specs=ANY, out_shape=jax.ShapeDtypeStruct(tuple(full), w.dtype),
        scratch_shapes=[pltpu.SemaphoreType.DMA((3,)), pltpu.SemaphoreType.DMA((3,)), pltpu.SemaphoreType.DMA],
    )(w)


def _block(ref, ax, shard, half):
    if shard is not None:
        n = ref.shape[ax] // 4
        ref = _at(ref, ax, shard * n, n)
    if half is not None:
        nh = ref.shape[1 - ax] // 2
        ref = _at(ref, 1 - ax, half * nh, nh)
    return ref


def _remote(src, dst, send_sems, recv_sems, k, device):
    return pltpu.make_async_remote_copy(src_ref=src, dst_ref=dst, send_sem=send_sems.at[k], recv_sem=recv_sems.at[k],
                                        device_id=device, device_id_type=MESH)


def _gather_cargo(fulls, axes):
    na = len(fulls)

    def start(ins, outs, send, recv):
        x, y, c = _position()
        for a in range(na):
            mine = _block(outs[a], axes[a], 2 * x + y, c)
            for k, (fx, fy) in enumerate(CHIP_FLIPS):
                _remote(mine, mine, send, recv, 3 * a + k, (x ^ fx, y ^ fy, c)).start()

    def finish(ins, outs, send, recv):
        x, y, c = _position()
        sib = (x, y, 1 - c)
        for a in range(na):
            for k, (fx, fy) in enumerate(CHIP_FLIPS):
                got = _block(outs[a], axes[a], 2 * (x ^ fx) + (y ^ fy), c)
                _remote(got, got, send, recv, 3 * a + k, sib).wait_recv()
                _remote(got, got, send, recv, 3 * (na + a) + k, sib).start()
        for a in range(na):
            mine = _block(outs[a], axes[a], 2 * x + y, c)
            for k, (fx, fy) in enumerate(CHIP_FLIPS):
                passed = _block(outs[a], axes[a], 2 * (x ^ fx) + (y ^ fy), c)
                theirs = _block(outs[a], axes[a], 2 * (x ^ fx) + (y ^ fy), 1 - c)
                _remote(theirs, theirs, send, recv, 3 * (na + a) + k, sib).wait_recv()
                _remote(passed, passed, send, recv, 3 * (na + a) + k, sib).wait_send()
                _remote(mine, mine, send, recv, 3 * a + k, sib).wait_send()

    shapes = [jax.ShapeDtypeStruct(f.shape, f.dtype) for f in fulls]
    return _Cargo(fulls, shapes, {a: a for a in range(na)}, 6 * na, start, finish)


def _scatter_cargo(sums, axes):
    na = len(sums)
    shapes = []
    for s, ax in zip(sums, axes):
        piece = list(s.shape)
        piece[ax] //= 4
        shapes.append(jax.ShapeDtypeStruct((3, *piece), s.dtype))

    def copies(ins, outs, send, recv):
        x, y, c = _position()
        return [_remote(_block(ins[a], axes[a], 2 * (x ^ fx) + (y ^ fy), None), outs[a].at[k], send, recv, 3 * a + k,
                        (x ^ fx, y ^ fy, c))
                for a in range(na) for k, (fx, fy) in enumerate(CHIP_FLIPS)]

    def start(ins, outs, send, recv):
        for cp in copies(ins, outs, send, recv):
            cp.start()

    def finish(ins, outs, send, recv):
        for cp in copies(ins, outs, send, recv):
            cp.wait()

    return _Cargo(sums, shapes, {}, 3 * na, start, finish)


def _swap_cargo(grads, axes):
    na = len(grads)
    shapes = []
    for g, ax in zip(grads, axes):
        half = list(g.shape)
        half[1 - ax] //= 2
        shapes.append(jax.ShapeDtypeStruct(tuple(half), g.dtype))

    def copies(ins, outs, send, recv):
        x, y, c = _position()
        return [_remote(_block(ins[a], axes[a], None, 1 - c), outs[a], send, recv, a, (x, y, 1 - c)) for a in range(na)]

    def start(ins, outs, send, recv):
        for cp in copies(ins, outs, send, recv):
            cp.start()

    def finish(ins, outs, send, recv):
        for cp in copies(ins, outs, send, recv):
            cp.wait()

    return _Cargo(grads, shapes, {}, na, start, finish)


def _join_halves(name, shards, axes):
    na = len(shards)

    def body(*refs):
        outs, (send, recv) = refs[na:2 * na], refs[2 * na:]
        x, y, c = _position()
        sib = (x, y, 1 - c)
        for a in range(na):
            mine = _block(outs[a], axes[a], None, c)
            _remote(mine, mine, send, recv, a, sib).start()
        for a in range(na):
            mine = _block(outs[a], axes[a], None, c)
            theirs = _block(outs[a], axes[a], None, 1 - c)
            _remote(theirs, theirs, send, recv, a, sib).wait_recv()
            _remote(mine, mine, send, recv, a, sib).wait_send()

    return pl.pallas_call(
        body, name=name, in_specs=[ANY] * na, out_specs=[ANY] * na,
        out_shape=[jax.ShapeDtypeStruct(s.shape, s.dtype) for s in shards],
        scratch_shapes=[pltpu.SemaphoreType.DMA((na,)), pltpu.SemaphoreType.DMA((na,))],
        input_output_aliases={a: a for a in range(na)},
    )(*shards)


def _sum_terms(name, terms, out_shape, out_dtype, blk, nblocks, out_map, scalars):
    tr, tc = blk

    def body(s_ref, *refs):
        acc = refs[0][...].astype(F32)
        for r in refs[1:-1]:
            acc = acc + r[...].astype(F32)
        refs[-1][...] = acc.astype(refs[-1].dtype)

    in_specs = [pl.BlockSpec((None, tr, tc) if a.ndim == 3 else (tr, tc), m) for a, m in terms]
    return pl.pallas_call(
        body, name=name,
        grid_spec=pltpu.PrefetchScalarGridSpec(
            num_scalar_prefetch=1, grid=(nblocks,), in_specs=in_specs, out_specs=pl.BlockSpec((tr, tc), out_map)),
        out_shape=jax.ShapeDtypeStruct(out_shape, out_dtype),
        compiler_params=_cparams(("arbitrary",)),
    )(scalars, *[a for a, _ in terms])


def _rows_for(cols, rows, budget=1 << 19):
    if rows % SUBLANES:
        return rows
    tr = 8
    while tr * 2 * cols <= budget and tr * 2 <= min(rows, 256) and rows % (tr * 2) == 0:
        tr *= 2
    assert rows % tr == 0
    return tr


def _place_shard(name, w, layer, ax, scalars):
    _, R, C = w.shape
    tr = _rows_for(C, R)
    nb = R // tr
    if ax == 0:
        shape, out_map = (4 * R, C), (lambda i, s: (s[1] * nb + i, 0))
    else:
        shape, out_map = (R, 4 * C), (lambda i, s: (i, s[1]))
    return _sum_terms(name, [(w, lambda i, s: (layer, i, 0))], shape, BF16, (tr, C), nb, out_map, scalars)


def _reduce_sums(grads, got, axes, scalars):
    sums = []
    for a, (g, h, ax) in enumerate(zip(grads, got, axes)):
        Rh, Ch = h.shape
        tr = _rows_for(Ch, Rh)
        nb = Rh // tr
        g_map = (lambda i, s, nb=nb: (s[0] * nb + i, 0)) if ax == 1 else (lambda i, s: (i, s[0]))
        sums.append(_sum_terms(f"rs_sum2_{a}", [(g, g_map), (h, lambda i, s: (i, 0))], (Rh, Ch), BF16, (tr, Ch), nb,
                               lambda i, s: (i, 0), scalars))
    return sums


def _reduce_end(sums, parts, axes, scalars):
    shards = []
    for a, (s1, pt, ax) in enumerate(zip(sums, parts, axes)):
        _, Rs, Cs = pt.shape
        tr = _rows_for(Cs, Rs)
        nb = Rs // tr
        if ax == 1:
            shape, s_map, o_map = (2 * Rs, Cs), (lambda i, s: (i, s[1])), (lambda i, s, nb=nb: (s[0] * nb + i, 0))
        else:
            shape, s_map, o_map = (Rs, 2 * Cs), (lambda i, s, nb=nb: (s[1] * nb + i, 0)), (lambda i, s: (i, s[0]))
        terms = [(s1, s_map)] + [(pt, (lambda i, s, k=k: (k, i, 0))) for k in range(3)]
        shards.append(_sum_terms(f"rs_sum4_{a}", terms, shape, F32, (tr, Cs), nb, o_map, scalars))
    return _join_halves("rs_join", shards, axes)


def _all_to_all_small(buf):
    nr = buf.shape[0]

    def body(b_ref, out_ref, send_sems, recv_sems, local_sem):
        x, y, c = _position()
        me = 4 * x + 2 * y + c
        mine = pltpu.make_async_copy(b_ref, out_ref.at[me], local_sem)
        mine.start()
        sends = []
        for r in range(1, 8):
            peer = (x ^ (r >> 2), y ^ ((r >> 1) & 1), c ^ (r & 1))
            cp = pltpu.make_async_remote_copy(
                src_ref=b_ref, dst_ref=out_ref.at[me], send_sem=send_sems.at[r - 1], recv_sem=recv_sems.at[r - 1],
                device_id=peer, device_id_type=MESH)
            cp.start()
            sends.append(cp)
        for r in range(1, 8):
            px, py, pc = x ^ (r >> 2), y ^ ((r >> 1) & 1), c ^ (r & 1)
            pltpu.make_async_remote_copy(
                src_ref=b_ref, dst_ref=out_ref.at[4 * px + 2 * py + pc], send_sem=send_sems.at[r - 1],
                recv_sem=recv_sems.at[r - 1], device_id=(px, py, pc), device_id_type=MESH).wait_recv()
        for cp in sends:
            cp.wait_send()
        mine.wait()

    return pl.pallas_call(
        body, name="small_exchange", in_specs=[ANY], out_specs=ANY,
        out_shape=jax.ShapeDtypeStruct((8, nr, 128), buf.dtype),
        scratch_shapes=[pltpu.SemaphoreType.DMA((7,)), pltpu.SemaphoreType.DMA((7,)), pltpu.SemaphoreType.DMA],
    )(buf)


SMALL_TR = 256


def _all_reduce_small(buf):
    nr = buf.shape[0]
    slots = _all_to_all_small(buf).reshape(8 * nr, 128)
    nblk = nr // SMALL_TR

    def fn(i, j, rv, bv):
        acc = rv[0]
        for v in rv[1:]:
            acc = acc + v
        return [acc], []

    rows = [(slots, 128, _c(0), (lambda i, d=d: d * nblk + i)) for d in range(8)]
    return _rowwise("small_sum", fn, rows, [], [(128, 128, _c(0), F32)], [], nrows=nr, tr=SMALL_TR)[0]


def _adamw(name, w, g, m, v):
    rows, cols = w.shape
    tr = _rows_for(cols, rows, budget=1 << 18)

    def fn(i, j, rv, bv):
        wv, gv, mv, vv = rv
        m2 = ADAM_B1 * mv + (1.0 - ADAM_B1) * gv
        v2 = ADAM_B2 * vv + (1.0 - ADAM_B2) * (gv * gv)
        m_hat = m2 / (1.0 - ADAM_B1 ** ADAM_STEP)
        v_hat = v2 / (1.0 - ADAM_B2 ** ADAM_STEP)
        delta = -ADAM_LR * (m_hat / (jnp.sqrt(v_hat) + ADAM_EPS) + ADAM_WD * wv)
        return [delta, m2, v2], []

    return _rowwise(name, fn, [(a, cols, _c(0), 0) for a in (w, g, m, v)], [], [(cols, cols, _c(0), F32)] * 3, [],
                    nrows=rows, tr=tr)


WEIGHTS = ("norm_g", "w_in", "conv_w", "conv_b", "conv_ln_g", "conv_ln_b", "sgu_ln_g", "sgu_ln_b", "sgu_w", "sgu_b",
           "w_branch", "w_gate", "b_gate", "w_out", "final_g")
BIG = ("w_in", "w_branch", "w_gate", "w_out")
BIG_AXIS = dict(w_in=1, w_branch=1, w_gate=1, w_out=0)
SMALL = tuple(n for n in WEIGHTS if n not in BIG)


def _pack(arrays, pad_rows):
    flat = jnp.concatenate([a.reshape(-1).astype(F32) for a in arrays])
    unit = 128 * pad_rows
    total = -(-flat.shape[0] // unit) * unit
    return jnp.pad(flat, (0, total - flat.shape[0])).reshape(total // 128, 128)


def _unpack(buf, shapes):
    flat = buf.reshape(-1)
    out, off = [], 0
    for shp in shapes:
        size = math.prod(shp)
        out.append(flat[off:off + size].reshape(shp))
        off += size
    return out


def kernel(x, norm_g, w_in, conv_w, conv_b, conv_ln_g, conv_ln_b, sgu_ln_g, sgu_ln_b, sgu_w, sgu_b, w_branch, w_gate, b_gate, w_out, final_g, loss_target, m_norm_g, m_w_in, m_conv_w, m_conv_b, m_conv_ln_g, m_conv_ln_b, m_sgu_ln_g, m_sgu_ln_b, m_sgu_w, m_sgu_b, m_w_branch, m_w_gate, m_b_gate, m_w_out, m_final_g, v_norm_g, v_w_in, v_conv_w, v_conv_b, v_conv_ln_g, v_conv_ln_b, v_sgu_ln_g, v_sgu_ln_b, v_sgu_w, v_sgu_b, v_w_branch, v_w_gate, v_b_gate, v_w_out, v_final_g):
    w = dict(norm_g=norm_g, w_in=w_in, conv_w=conv_w, conv_b=conv_b, conv_ln_g=conv_ln_g, conv_ln_b=conv_ln_b,
             sgu_ln_g=sgu_ln_g, sgu_ln_b=sgu_ln_b, sgu_w=sgu_w, sgu_b=sgu_b, w_branch=w_branch, w_gate=w_gate,
             b_gate=b_gate, w_out=w_out, final_g=final_g)
    m = dict(norm_g=m_norm_g, w_in=m_w_in, conv_w=m_conv_w, conv_b=m_conv_b, conv_ln_g=m_conv_ln_g,
             conv_ln_b=m_conv_ln_b, sgu_ln_g=m_sgu_ln_g, sgu_ln_b=m_sgu_ln_b, sgu_w=m_sgu_w, sgu_b=m_sgu_b,
             w_branch=m_w_branch, w_gate=m_w_gate, b_gate=m_b_gate, w_out=m_w_out, final_g=m_final_g)
    v = dict(norm_g=v_norm_g, w_in=v_w_in, conv_w=v_conv_w, conv_b=v_conv_b, conv_ln_g=v_conv_ln_g,
             conv_ln_b=v_conv_ln_b, sgu_ln_g=v_sgu_ln_g, sgu_ln_b=v_sgu_ln_b, sgu_w=v_sgu_w, sgu_b=v_sgu_b,
             w_branch=v_w_branch, w_gate=v_w_gate, b_gate=v_b_gate, w_out=v_w_out, final_g=v_final_g)
    depth = w_in.shape[0]
    chip = 2 * lax.axis_index("x") + lax.axis_index("y")
    scalars = jnp.stack([lax.axis_index("c"), chip]).astype(jnp.int32)
    axes = [BIG_AXIS[n] for n in BIG]

    stacked = dict(w_in=w_in, w_branch=w_branch.reshape(depth, 4 * BW, -1), w_gate=w_gate, w_out=w_out)
    buffers = [[_place_shard(f"place_{n}", stacked[n], l, BIG_AXIS[n], scalars) for n in BIG] for l in range(depth)]
    conv_w_full = _gather("gather_conv_w", conv_w, 2)
    gathered = _run_cargo("gather_first", _gather_cargo(buffers[0], axes))
    h = x[0]
    saved, layer_params = [], []
    for l in range(depth):
        p = {n: w[n][l] for n in SMALL if n not in ("final_g", "conv_w")}
        p["conv_w"] = conv_w_full[l]
        p.update(dict(zip(BIG, gathered)))
        cargo = _gather_cargo(buffers[l + 1], axes) if l + 1 < depth else None
        h, s, gathered = _layer_fwd(h, p, cargo)
        saved.append(s)
        layer_params.append(p)
    dh, g_final, loss = _final_loss(h, loss_target[0], final_g)

    layer_grads, reduced_big = [None] * depth, [None] * depth
    pending = None
    for l in reversed(range(depth)):
        dh, layer_grads[l], exchanged = _layer_bwd(dh, saved[l], layer_params[l], pending)
        if exchanged is not None:
            reduced_big[l + 1] = _reduce_end(*exchanged, axes, scalars)
        pending = ([layer_grads[l][n] for n in BIG], axes, scalars)
    sums = _reduce_sums(pending[0], _run_cargo("rs_halves_last", _swap_cargo(pending[0], axes)), axes, scalars)
    reduced_big[0] = _reduce_end(sums, _run_cargo("rs_shards_last", _scatter_cargo(sums, axes)), axes, scalars)
    grad_x = dh[None]
    local = {n: jnp.stack([layer_grads[l][n] for l in range(depth)]) for n in SMALL if n != "final_g"}
    local["final_g"] = g_final.reshape(-1)
    grads = {n: jnp.stack([reduced_big[l][a] for l in range(depth)]).reshape(w[n].shape) for a, n in enumerate(BIG)}
    small_shapes = [local[n].shape for n in SMALL] + [(128,)]
    reduced = _unpack(_all_reduce_small(_pack([local[n] for n in SMALL] + [loss.reshape(128)], SMALL_TR)), small_shapes)
    for n, r in zip(SMALL, reduced[:-1]):
        grads[n] = r
    loss_out = reduced[-1][0]
    ncw = conv_w.shape[2]
    grads["conv_w"] = lax.dynamic_slice_in_dim(grads["conv_w"], chip * ncw, ncw, axis=2)

    delta, new_m, new_v = {}, {}, {}
    for n in BIG:
        cols = w[n].shape[-1]
        d2, m2, v2 = _adamw(f"adamw_{n}", *[a.reshape(-1, cols) for a in (w[n], grads[n], m[n], v[n])])
        delta[n], new_m[n], new_v[n] = d2.reshape(w[n].shape), m2.reshape(w[n].shape), v2.reshape(w[n].shape)
    for n in SMALL:
        cols = w[n].shape[-1]
        outs = _adamw(f"adamw_{n}", *[a.reshape(-1, cols) for a in (w[n], grads[n], m[n], v[n])])
        delta[n], new_m[n], new_v[n] = [o.reshape(w[n].shape) for o in outs]
    return (loss_out, grad_x, *[grads[n] for n in WEIGHTS], *[delta[n] for n in WEIGHTS],
            *[new_m[n] for n in WEIGHTS], *[new_v[n] for n in WEIGHTS])
```
